```python
import math
import jax
import jax.numpy as jnp
from jax import lax
import numpy as np

D_MODEL = 1024
BATCH = 4
SEQ = 8192
DEPTH = 4

GRID_W = 64
CTX_LEN = 256
HEAD_DIM = 64
Q_BLOCK = 128
ROPE_THETA = 10000.0
EPS = 1e-6
MIX_HALF = D_MODEL // 2
NA_HEADS = MIX_HALF // HEAD_DIM
NA_WIN_R = 8
NA_WIN_C = 16
DA_VDIM = 2 * HEAD_DIM
DA_HEADS = MIX_HALF // DA_VDIM
GQA_Q_HEADS = MIX_HALF // HEAD_DIM
GQA_KV_HEADS = GQA_Q_HEADS // 4
HY_WIDTH = MIX_HALF
HY_EMB_DIM = 33
HY_FILTER_ORDER = 64
HY_FAST_DECAY = 0.3
HY_SLOW_DECAY = 1.5
HY_DECAY_TARGET = 1e-2
FFN_HIDDEN = ((8 * D_MODEL // 3 + 255) // 256) * 256

NA_WIDTH = NA_HEADS * HEAD_DIM
DA_QK_WIDTH = DA_HEADS * 2 * HEAD_DIM
DA_V_WIDTH = DA_HEADS * DA_VDIM
EVEN_IN = 3 * NA_WIDTH + 2 * DA_QK_WIDTH + DA_V_WIDTH
EVEN_MIX = NA_WIDTH + DA_V_WIDTH
GQA_Q_WIDTH = GQA_Q_HEADS * HEAD_DIM
GQA_KV_WIDTH = GQA_KV_HEADS * HEAD_DIM
ODD_IN = GQA_Q_WIDTH + 2 * GQA_KV_WIDTH + 3 * HY_WIDTH
ODD_MIX = GQA_Q_WIDTH + HY_WIDTH
N_EVEN = (DEPTH + 1) // 2
N_ODD = DEPTH // 2

kernel_name = 'hybrid_na_diff_gqa_hyena_dit'

F32 = jnp.float32


def rms_norm(x, gain=None):
    xf = x.astype(F32)
    y = xf * lax.rsqrt(jnp.mean(xf * xf, axis=-1, keepdims=True) + EPS)
    if gain is not None:
        y = y * gain.astype(F32)
    return y.astype(x.dtype)


def modulate(x, shift, scale):
    return rms_norm(x) * (1 + scale) + shift


def split_heads(x, n, d):
    b, l = x.shape[:2]
    return x.reshape(b, l, n, d).transpose(0, 2, 1, 3)


def merge_heads(x):
    b, h, l, d = x.shape
    return x.transpose(0, 2, 1, 3).reshape(b, l, h * d)


def dwconv3(x, w, b):
    xp = jnp.pad(x, ((0, 0), (1, 1), (0, 0)))
    return xp[:, :-2] * w[0] + xp[:, 1:-1] * w[1] + xp[:, 2:] * w[2] + b


def axial_rope(length, head_dim):
    t = jnp.arange(length, dtype=jnp.int32)
    row = (t // GRID_W).astype(F32)
    col = (t % GRID_W).astype(F32)
    n_pairs = head_dim // 4
    inv_freq = ROPE_THETA ** (-jnp.arange(n_pairs, dtype=F32) / n_pairs)
    ang = jnp.concatenate([row[:, None] * inv_freq, col[:, None] * inv_freq], axis=-1)
    return jnp.cos(ang), jnp.sin(ang)


def apply_rope(x, cos, sin):
    xf = x.astype(F32).reshape(x.shape[:-1] + (x.shape[-1] // 2, 2))
    x0, x1 = xf[..., 0], xf[..., 1]
    out = jnp.stack([x0 * cos - x1 * sin, x0 * sin + x1 * cos], axis=-1)
    return out.reshape(x.shape).astype(x.dtype)


def sweep_query_blocks(fn, *qs):
    b, h, l = qs[0].shape[:3]
    n = l // Q_BLOCK
    blocks = tuple(jnp.moveaxis(q.reshape(b, h, n, Q_BLOCK, q.shape[-1]), 2, 0) for q in qs)
    out = lax.map(lambda qb: fn(*qb), blocks)
    return jnp.moveaxis(out, 0, 2).reshape(b, h, l, out.shape[-1])


def attend(q, k, v):
    scale = q.shape[-1] ** -0.5
    s = jnp.einsum('bhqd,bhsd->bhqs', q, k).astype(F32) * scale
    p = jax.nn.softmax(s, axis=-1).astype(v.dtype)
    return jnp.einsum('bhqs,bhsd->bhqd', p, v)


def diff_attend(q1, q2, k1, k2, v, lam):
    scale = q1.shape[-1] ** -0.5
    p1 = jax.nn.softmax(jnp.einsum('bhqd,bhsd->bhqs', q1, k1).astype(F32) * scale, axis=-1)
    p2 = jax.nn.softmax(jnp.einsum('bhqd,bhsd->bhqs', q2, k2).astype(F32) * scale, axis=-1)
    return jnp.einsum('bhqs,bhsd->bhqd', (p1 - lam * p2).astype(v.dtype), v)


def gqa_attend(q, k, v):
    b, hq, nq, d = q.shape
    hkv = k.shape[1]
    qg = q.reshape(b, hkv, hq // hkv, nq, d)
    s = jnp.einsum('bkgqd,bksd->bkgqs', qg, k).astype(F32) * (d ** -0.5)
    p = jax.nn.softmax(s, axis=-1).astype(v.dtype)
    return jnp.einsum('bkgqs,bksd->bkgqd', p, v).reshape(b, hq, nq, d)


def neighbourhood_attention(q, k, v, k_ctx, v_ctx, rpb):
    b, h, l, d = q.shape
    rows = l // GRID_W
    win_r = min(NA_WIN_R, rows)
    n_win = win_r * NA_WIN_C
    scale = d ** -0.5
    qg = q.reshape(b, h, rows, GRID_W, d)
    kg = k.reshape(b, h, rows, GRID_W, d)
    vg = v.reshape(b, h, rows, GRID_W, d)
    row_start = jnp.clip(jnp.arange(rows) - win_r // 2, 0, rows - win_r)
    col_ids = jnp.arange(GRID_W)
    col_start = jnp.clip(col_ids - NA_WIN_C // 2, 0, GRID_W - NA_WIN_C)
    col_idx = col_start[:, None] + jnp.arange(NA_WIN_C)
    col_off = col_idx - col_ids[:, None] + NA_WIN_C - 1
    rpb32 = rpb.astype(F32)

    def one_row(args):
        q_row, r, r0 = args
        k_rows = lax.dynamic_slice_in_dim(kg, r0, win_r, axis=2)
        v_rows = lax.dynamic_slice_in_dim(vg, r0, win_r, axis=2)
        k_win = k_rows[:, :, :, col_idx]
        v_win = v_rows[:, :, :, col_idx]
        row_off = r0 + jnp.arange(win_r) - r + NA_WIN_R - 1
        bias = rpb32[:, row_off[None, :, None], col_off[:, None, :]]
        s_win = jnp.einsum('bhwd,bhrwcd->bhwrc', q_row, k_win).astype(F32) * scale + bias
        s_ctx = jnp.einsum('bhwd,bhsd->bhws', q_row, k_ctx).astype(F32) * scale
        s = jnp.concatenate([s_win.reshape(b, h, GRID_W, n_win), s_ctx], axis=-1)
        p = jax.nn.softmax(s, axis=-1).astype(v.dtype)
        p_win = p[..., :n_win].reshape(b, h, GRID_W, win_r, NA_WIN_C)
        return (jnp.einsum('bhwrc,bhrwcd->bhwd', p_win, v_win)
                + jnp.einsum('bhws,bhsd->bhwd', p[..., n_win:], v_ctx))

    out = lax.map(one_row, (jnp.moveaxis(qg, 2, 0), jnp.arange(rows), row_start))
    return jnp.moveaxis(out, 0, 2).reshape(b, h, l, d)


def hyena_filter(length, w1, b1, w2, b2, w3, b3, w4, freq):
    t = jnp.linspace(0.0, 1.0, length, dtype=F32)[:, None]
    bands = (HY_EMB_DIM - 1) // 2
    w = 2.0 * math.pi * jnp.arange(length, dtype=F32)[:, None] / length
    f = jnp.linspace(1e-4, bands - 1, bands, dtype=F32)[None, :]
    z = jnp.concatenate([t, jnp.cos(f * w), -jnp.sin(f * w)], axis=-1)
    fr = freq.astype(F32)
    hdn = jnp.sin(fr * (z @ w1.astype(F32) + b1.astype(F32)))
    hdn = jnp.sin(fr * (hdn @ w2.astype(F32) + b2.astype(F32)))
    hdn = jnp.sin(fr * (hdn @ w3.astype(F32) + b3.astype(F32)))
    hdn = hdn @ w4.astype(F32)
    max_decay = math.log(HY_DECAY_TARGET) / HY_FAST_DECAY
    min_decay = math.log(HY_DECAY_TARGET) / HY_SLOW_DECAY
    deltas = jnp.linspace(min_decay, max_decay, HY_WIDTH, dtype=F32)
    decay = jnp.exp(-t * jnp.abs(deltas))
    h_fwd = hdn[:, :HY_WIDTH] * decay
    h_bwd = hdn[:, HY_WIDTH:] * decay
    k_two = jnp.concatenate([h_fwd, jnp.zeros((1, HY_WIDTH), F32), h_bwd[:0:-1]], axis=0)
    return k_two / jnp.sum(jnp.abs(k_two), axis=0, keepdims=True)


def hyena(u, conv_w, conv_b, w1, b1, w2, b2, w3, b3, w4, freq, skip):
    length = u.shape[1]
    u = dwconv3(u, conv_w, conv_b)
    x0, x1, val = jnp.split(u, 3, axis=-1)
    z = (val * x1).astype(F32)
    kf = jnp.fft.rfft(hyena_filter(length, w1, b1, w2, b2, w3, b3, w4, freq), n=2 * length, axis=0)
    zf = jnp.fft.rfft(z, n=2 * length, axis=1)
    y = jnp.fft.irfft(zf * kf, n=2 * length, axis=1)[:, :length]
    return (x0.astype(F32) * (y + skip.astype(F32) * z)).astype(u.dtype)


def even_mixer(h_l, h_c, w_in, w_out, na_qg, na_kg, rpb, da_qg, da_kg, lq1, lk1, lq2, lk2, subln,
               lam_init, cos, sin, with_ctx):
    offs = [NA_WIDTH, 2 * NA_WIDTH, 3 * NA_WIDTH, 3 * NA_WIDTH + DA_QK_WIDTH, 3 * NA_WIDTH + 2 * DA_QK_WIDTH]
    qa_l, ka_l, va_l, qb_l, kb_l, vb_l = jnp.split(h_l @ w_in, offs, axis=-1)
    qa_c, ka_c, va_c, qb_c, kb_c, vb_c = jnp.split(h_c @ w_in, offs, axis=-1)
    qa_l = rms_norm(split_heads(qa_l, NA_HEADS, HEAD_DIM), na_qg)
    ka_l = rms_norm(split_heads(ka_l, NA_HEADS, HEAD_DIM), na_kg)
    va_l = split_heads(va_l, NA_HEADS, HEAD_DIM)
    ka_c = rms_norm(split_heads(ka_c, NA_HEADS, HEAD_DIM), na_kg)
    va_c = split_heads(va_c, NA_HEADS, HEAD_DIM)
    ya_l = neighbourhood_attention(qa_l, ka_l, va_l, ka_c, va_c, rpb)
    lam = (jnp.exp(jnp.sum(lq1.astype(F32) * lk1.astype(F32)))
           - jnp.exp(jnp.sum(lq2.astype(F32) * lk2.astype(F32))) + lam_init)
    qb_l = apply_rope(rms_norm(split_heads(qb_l, 2 * DA_HEADS, HEAD_DIM), da_qg), cos, sin)
    kb_l = apply_rope(rms_norm(split_heads(kb_l, 2 * DA_HEADS, HEAD_DIM), da_kg), cos, sin)
    vb_l = split_heads(vb_l, DA_HEADS, DA_VDIM)
    kb_c = rms_norm(split_heads(kb_c, 2 * DA_HEADS, HEAD_DIM), da_kg)
    vb_c = split_heads(vb_c, DA_HEADS, DA_VDIM)
    k1_all = jnp.concatenate([kb_l[:, 0::2], kb_c[:, 0::2]], axis=2)
    k2_all = jnp.concatenate([kb_l[:, 1::2], kb_c[:, 1::2]], axis=2)
    v_all = jnp.concatenate([vb_l, vb_c], axis=2)
    yb_l = sweep_query_blocks(lambda q1, q2: diff_attend(q1, q2, k1_all, k2_all, v_all, lam),
                              qb_l[:, 0::2], qb_l[:, 1::2])
    y_l = jnp.concatenate([merge_heads(ya_l), merge_heads(rms_norm(yb_l, subln) * (1.0 - lam_init))],
                          axis=-1) @ w_out
    if not with_ctx:
        return y_l, None
    qa_c = rms_norm(split_heads(qa_c, NA_HEADS, HEAD_DIM), na_qg)
    ya_c = attend(qa_c, ka_c, va_c)
    qb_c = rms_norm(split_heads(qb_c, 2 * DA_HEADS, HEAD_DIM), da_qg)
    yb_c = diff_attend(qb_c[:, 0::2], qb_c[:, 1::2], kb_c[:, 0::2], kb_c[:, 1::2], vb_c, lam)
    y_c = jnp.concatenate([merge_heads(ya_c), merge_heads(rms_norm(yb_c, subln) * (1.0 - lam_init))],
                          axis=-1) @ w_out
    return y_l, y_c


def odd_mixer(h_l, h_c, w_in, w_out, qg, kg, conv_w, conv_b, w1, b1, w2, b2, w3, b3, w4, freq, skip,
              cos, sin, with_ctx):
    offs = [GQA_Q_WIDTH, GQA_Q_WIDTH + GQA_KV_WIDTH, GQA_Q_WIDTH + 2 * GQA_KV_WIDTH]
    q_l, k_l, v_l, hy_l = jnp.split(h_l @ w_in, offs, axis=-1)
    q_c, k_c, v_c, hy_c = jnp.split(h_c @ w_in, offs, axis=-1)
    q_l = apply_rope(rms_norm(split_heads(q_l, GQA_Q_HEADS, HEAD_DIM), qg), cos, sin)
    k_l = apply_rope(rms_norm(split_heads(k_l, GQA_KV_HEADS, HEAD_DIM), kg), cos, sin)
    v_l = split_heads(v_l, GQA_KV_HEADS, HEAD_DIM)
    k_c = rms_norm(split_heads(k_c, GQA_KV_HEADS, HEAD_DIM), kg)
    v_c = split_heads(v_c, GQA_KV_HEADS, HEAD_DIM)
    k_all = jnp.concatenate([k_l, k_c], axis=2)
    v_all = jnp.concatenate([v_l, v_c], axis=2)
    yc_l = sweep_query_blocks(lambda qb: gqa_attend(qb, k_all, v_all), q_l)
    yd_l = hyena(hy_l, conv_w, conv_b, w1, b1, w2, b2, w3, b3, w4, freq, skip)
    y_l = jnp.concatenate([merge_heads(yc_l), yd_l], axis=-1) @ w_out
    if not with_ctx:
        return y_l, None
    q_c = rms_norm(split_heads(q_c, GQA_Q_HEADS, HEAD_DIM), qg)
    yc_c = gqa_attend(q_c, k_c, v_c)
    yd_c = hyena(hy_c, conv_w, conv_b, w1, b1, w2, b2, w3, b3, w4, freq, skip)
    y_c = jnp.concatenate([merge_heads(yc_c), yd_c], axis=-1) @ w_out
    return y_l, y_c


def conv_ffn(h, w_up, conv_w, conv_b, w_down):
    g, v = jnp.split(h @ w_up, 2, axis=-1)
    g = dwconv3(g, conv_w, conv_b)
    return (jax.nn.silu(g) * v) @ w_down


def setup_inputs(seed: int = 0) -> dict:
    key = jax.random.key(seed)
    D = D_MODEL
    specs = [
        ('x', (BATCH, SEQ, D), 1.0, 0.0),
        ('c', (BATCH, D), 1.0, 0.0),
        ('ctx', (BATCH, CTX_LEN, D), 1.0, 0.0),
        ('c_ctx', (D,), 1.0, 0.0),
        ('w_ada', (DEPTH, D, 6 * D), D ** -0.5, 0.0),
        ('b_ada', (DEPTH, 6 * D), 0.02, 0.0),
        ('w_up', (DEPTH, D, 2 * FFN_HIDDEN), D ** -0.5, 0.0),
        ('ffn_conv_w', (DEPTH, 3, FFN_HIDDEN), 3 ** -0.5, 0.0),
        ('ffn_conv_b', (DEPTH, FFN_HIDDEN), 0.02, 0.0),
        ('w_down', (DEPTH, FFN_HIDDEN, D), FFN_HIDDEN ** -0.5, 0.0),
        ('w_in_e', (N_EVEN, D, EVEN_IN), D ** -0.5, 0.0),
        ('w_out_e', (N_EVEN, EVEN_MIX, D), EVEN_MIX ** -0.5, 0.0),
        ('na_q_gain', (N_EVEN, HEAD_DIM), 0.05, 1.0),
        ('na_k_gain', (N_EVEN, HEAD_DIM), 0.05, 1.0),
        ('na_rpb', (N_EVEN, NA_HEADS, 2 * NA_WIN_R - 1, 2 * NA_WIN_C - 1), 0.02, 0.0),
        ('da_q_gain', (N_EVEN, HEAD_DIM), 0.05, 1.0),
        ('da_k_gain', (N_EVEN, HEAD_DIM), 0.05, 1.0),
        ('da_lambda_q1', (N_EVEN, HEAD_DIM), 0.1, 0.0),
        ('da_lambda_k1', (N_EVEN, HEAD_DIM), 0.1, 0.0),
        ('da_lambda_q2', (N_EVEN, HEAD_DIM), 0.1, 0.0),
        ('da_lambda_k2', (N_EVEN, HEAD_DIM), 0.1, 0.0),
        ('da_subln_gain', (N_EVEN, DA_VDIM), 0.05, 1.0),
        ('w_in_o', (N_ODD, D, ODD_IN), D ** -0.5, 0.0),
        ('w_out_o', (N_ODD, ODD_MIX, D), ODD_MIX ** -0.5, 0.0),
        ('gqa_q_gain', (N_ODD, HEAD_DIM), 0.05, 1.0),
        ('gqa_k_gain', (N_ODD, HEAD_DIM), 0.05, 1.0),
        ('hy_conv_w', (N_ODD, 3, 3 * HY_WIDTH), 3 ** -0.5, 0.0),
        ('hy_conv_b', (N_ODD, 3 * HY_WIDTH), 0.02, 0.0),
        ('hy_w1', (N_ODD, HY_EMB_DIM, HY_FILTER_ORDER), HY_EMB_DIM ** -0.5, 0.0),
        ('hy_b1', (N_ODD, HY_FILTER_ORDER), HY_EMB_DIM ** -0.5, 0.0),
        ('hy_w2', (N_ODD, HY_FILTER_ORDER, HY_FILTER_ORDER), HY_FILTER_ORDER ** -0.5, 0.0),
        ('hy_b2', (N_ODD, HY_FILTER_ORDER), HY_FILTER_ORDER ** -0.5, 0.0),
        ('hy_w3', (N_ODD, HY_FILTER_ORDER, HY_FILTER_ORDER), HY_FILTER_ORDER ** -0.5, 0.0),
        ('hy_b3', (N_ODD, HY_FILTER_ORDER), HY_FILTER_ORDER ** -0.5, 0.0),
        ('hy_w4', (N_ODD, HY_FILTER_ORDER, 2 * HY_WIDTH), HY_FILTER_ORDER ** -0.5, 0.0),
        ('hy_freq', (N_ODD, HY_FILTER_ORDER), 0.05, 1.0),
        ('hy_skip', (N_ODD, HY_WIDTH), 1.0, 0.0),
    ]
    keys = jax.random.split(key, len(specs))
    return {name: off + scale * jax.random.normal(keys[i], shape, F32)
            for i, (name, shape, scale, off) in enumerate(specs)}


def reference(x, c, ctx, c_ctx, w_ada, b_ada, w_up, ffn_conv_w, ffn_conv_b, w_down, w_in_e, w_out_e,
              na_q_gain, na_k_gain, na_rpb, da_q_gain, da_k_gain, da_lambda_q1, da_lambda_k1, da_lambda_q2,
              da_lambda_k2, da_subln_gain, w_in_o, w_out_o, gqa_q_gain, gqa_k_gain, hy_conv_w, hy_conv_b,
              hy_w1, hy_b1, hy_w2, hy_b2, hy_w3, hy_b3, hy_w4, hy_freq, hy_skip):
    cos, sin = axial_rope(x.shape[1], HEAD_DIM)
    x_l, x_c = x, ctx
    for layer in range(DEPTH):
        with_ctx = layer < DEPTH - 1
        mod_l = (jax.nn.silu(c) @ w_ada[layer] + b_ada[layer])[:, None, :]
        mod_c = jax.nn.silu(c_ctx) @ w_ada[layer] + b_ada[layer]
        sh1_l, sc1_l, g1_l, sh2_l, sc2_l, g2_l = jnp.split(mod_l, 6, axis=-1)
        sh1_c, sc1_c, g1_c, sh2_c, sc2_c, g2_c = jnp.split(mod_c, 6, axis=-1)
        h_l = modulate(x_l, sh1_l, sc1_l)
        h_c = modulate(x_c, sh1_c, sc1_c)
        i = layer // 2
        if layer % 2 == 0:
            lam_init = 0.8 - 0.6 * math.exp(-0.3 * layer)
            y_l, y_c = even_mixer(h_l, h_c, w_in_e[i], w_out_e[i], na_q_gain[i], na_k_gain[i], na_rpb[i],
                                  da_q_gain[i], da_k_gain[i], da_lambda_q1[i], da_lambda_k1[i],
                                  da_lambda_q2[i], da_lambda_k2[i], da_subln_gain[i], lam_init, cos, sin,
                                  with_ctx)
        else:
            y_l, y_c = odd_mixer(h_l, h_c, w_in_o[i], w_out_o[i], gqa_q_gain[i], gqa_k_gain[i],
                                 hy_conv_w[i], hy_conv_b[i], hy_w1[i], hy_b1[i], hy_w2[i], hy_b2[i],
                                 hy_w3[i], hy_b3[i], hy_w4[i], hy_freq[i], hy_skip[i], cos, sin, with_ctx)
        x_l = x_l + g1_l * y_l
        x_l = x_l + g2_l * conv_ffn(modulate(x_l, sh2_l, sc2_l), w_up[layer], ffn_conv_w[layer],
                                    ffn_conv_b[layer], w_down[layer])
        if with_ctx:
            x_c = x_c + g1_c * y_c
            x_c = x_c + g2_c * conv_ffn(modulate(x_c, sh2_c, sc2_c), w_up[layer], ffn_conv_w[layer],
                                        ffn_conv_b[layer], w_down[layer])
    return x_l
```

```python
import functools
import math

import numpy as np
import jax
import jax.numpy as jnp
from jax import lax
from jax.experimental import pallas as pl
from jax.experimental.pallas import tpu as pltpu

F32 = jnp.float32
BF16 = jnp.bfloat16

HEAD_DIM = 64
GRID_W = 64
ROPE_THETA = 10000.0
EPS = 1e-6
NA_WIN_R = 8
NA_WIN_C = 16
HY_EMB_DIM = 33
HY_FAST_DECAY = 0.3
HY_SLOW_DECAY = 1.5
HY_DECAY_TARGET = 1e-2

LANES = 128
TM = 256
TK = 512
NA_QROWS = TM // GRID_W
NA_WROWS = NA_QROWS + NA_WIN_R
NEG = -1e30
VMEM_LIMIT = 56 * 1024 * 1024


def _cparams(n_axes):
    return pltpu.CompilerParams(dimension_semantics=("arbitrary",) * n_axes,
                                vmem_limit_bytes=VMEM_LIMIT)


def _modulate(x, sh, sc):
    ms = jnp.mean(x * x, axis=-1, keepdims=True)
    return x * lax.rsqrt(ms + EPS) * (1.0 + sc) + sh


def _seg_norm(y, bd, gain):
    ss = jnp.dot((y * y).astype(BF16), bd, preferred_element_type=F32)
    return y * lax.rsqrt(ss * (1.0 / HEAD_DIM) + EPS) * gain


def _rope(y, c, se, so):
    return y * c + pltpu.roll(y, LANES - 1, 1) * se + pltpu.roll(y, 1, 1) * so


def _dot_nt(a, b):
    return lax.dot_general(a, b, (((1,), (1,)), ((), ())), preferred_element_type=F32)


def _shift_rows(g, prev_row, next_row):
    tm = g.shape[0]
    row = lax.broadcasted_iota(jnp.int32, g.shape, 0)
    dn = jnp.where(row == 0, prev_row, pltpu.roll(g, 1, 0))
    up = jnp.where(row == tm - 1, next_row, pltpu.roll(g, tm - 1, 0))
    return dn, up


def _ada_body(c_ref, w_ref, b_ref, o_ref):
    c = c_ref[...]
    a = (c / (1.0 + jnp.exp(-c))).astype(BF16)
    o_ref[0] = jnp.dot(a, w_ref[0].astype(BF16), preferred_element_type=F32) + b_ref[0]


def _ada_call(cs, w_ada, b_ada):
    depth, d, n6 = w_ada.shape
    rows = cs.shape[0]
    tn = 1536
    return pl.pallas_call(
        _ada_body, grid=(depth, n6 // tn),
        in_specs=[pl.BlockSpec((rows, d), lambda l, n: (0, 0)),
                  pl.BlockSpec((1, d, tn), lambda l, n: (l, 0, n)),
                  pl.BlockSpec((1, 1, tn), lambda l, n: (l, 0, n))],
        out_specs=pl.BlockSpec((1, rows, tn), lambda l, n: (l, 0, n)),
        out_shape=jax.ShapeDtypeStruct((depth, rows, n6), F32),
        compiler_params=_cparams(2), name="ada",
    )(cs, w_ada, b_ada.reshape(depth, 1, n6))


def _mod0_call(x, ctx, modarr):
    b, l, d = x.shape
    c = ctx.shape[1]
    t = l + c
    nl = l // TM

    def body(x_ref, c_ref, m_ref, xo_ref, h_ref):
        j = pl.program_id(1)
        xv = jnp.where(j < nl, x_ref[0], c_ref[0])
        xo_ref[0] = xv
        h_ref[0] = _modulate(xv, m_ref[0, 0, 0:1, :], m_ref[0, 0, 1:2, :]).astype(BF16)

    return pl.pallas_call(
        body, grid=(b, t // TM),
        in_specs=[pl.BlockSpec((1, TM, d), lambda i, j: (i, jnp.minimum(j, nl - 1), 0)),
                  pl.BlockSpec((1, TM, d), lambda i, j: (i, 0, 0)),
                  pl.BlockSpec((1, 1, 6, d), lambda i, j: (i, j // nl, 0, 0))],
        out_specs=[pl.BlockSpec((1, TM, d), lambda i, j: (i, j, 0)),
                   pl.BlockSpec((1, TM, d), lambda i, j: (i, j, 0))],
        out_shape=[jax.ShapeDtypeStruct((b, t, d), F32), jax.ShapeDtypeStruct((b, t, d), BF16)],
        compiler_params=_cparams(2), name="mod0",
    )(x, ctx, modarr)


def _inproj_even_body(a_ref, w_ref, bd_ref, g_ref, rc_ref, rse_ref, rso_ref, o_ref):
    a = a_ref[0]
    bd = bd_ref[...]
    rc, rse, rso = rc_ref[...], rse_ref[...], rso_ref[...]
    plan = ((0, False), (1, False), (None, False), (2, True), (3, True), (None, False))
    for seg, (gain_row, rope) in enumerate(plan):
        y = jnp.dot(a, w_ref[:, seg * 512:(seg + 1) * 512], preferred_element_type=F32)
        for half in range(2):
            yy = y[:, half * 256:(half + 1) * 256]
            if gain_row is not None:
                yy = _seg_norm(yy, bd, g_ref[gain_row:gain_row + 1, :])
            for blk in range(2):
                z = yy[:, blk * LANES:(blk + 1) * LANES]
                if rope:
                    z = _rope(z, rc, rse, rso)
                col = seg * 512 + half * 256 + blk * LANES
                o_ref[0, :, col:col + LANES] = z.astype(BF16)


def _inproj_odd_body(a_ref, w_ref, bd_ref, g_ref, rc_ref, rse_ref, rso_ref, o_ref):
    a = a_ref[0]
    bd = bd_ref[...]
    rc, rse, rso = rc_ref[...], rse_ref[...], rso_ref[...]
    y = jnp.dot(a, w_ref[:, 0:512], preferred_element_type=F32)
    for half in range(2):
        yy = _seg_norm(y[:, half * 256:(half + 1) * 256], bd, g_ref[0:1, :])
        for blk in range(2):
            z = _rope(yy[:, blk * LANES:(blk + 1) * LANES], rc, rse, rso)
            col = 1536 + half * 256 + blk * LANES
            o_ref[0, :, col:col + LANES] = z.astype(BF16)
    y = jnp.dot(a, w_ref[:, 512:768], preferred_element_type=F32)
    k = _seg_norm(y[:, :LANES], bd_ref[0:LANES, 0:LANES], g_ref[1:2, 0:LANES])
    k = _rope(k, rc, rse, rso)
    kr = pltpu.roll(k, HEAD_DIM, 1)
    lo = lax.broadcasted_iota(jnp.int32, k.shape, 1) < HEAD_DIM
    o_ref[0, :, 2048:2176] = jnp.where(lo, k, kr).astype(BF16)
    o_ref[0, :, 2176:2304] = jnp.where(lo, kr, k).astype(BF16)
    o_ref[0, :, 2304:2432] = y[:, LANES:].astype(BF16)
    for seg in range(3):
        y = jnp.dot(a, w_ref[:, 768 + seg * 512:768 + (seg + 1) * 512], preferred_element_type=F32)
        o_ref[0, :, seg * 512:(seg + 1) * 512] = y.astype(BF16)


def _inproj_call(body, name, h, w, bd, gains, ropes, n_out):
    b, t, d = h.shape
    n_in = w.shape[1]
    rc, rse, rso = ropes
    return pl.pallas_call(
        body, grid=(b, t // TM),
        in_specs=[pl.BlockSpec((1, TM, d), lambda i, j: (i, j, 0)),
                  pl.BlockSpec((d, n_in), lambda i, j: (0, 0)),
                  pl.BlockSpec(bd.shape, lambda i, j: (0, 0)),
                  pl.BlockSpec(gains.shape, lambda i, j: (0, 0)),
                  pl.BlockSpec((TM, LANES), lambda i, j: (j, 0)),
                  pl.BlockSpec((TM, LANES), lambda i, j: (j, 0)),
                  pl.BlockSpec((TM, LANES), lambda i, j: (j, 0))],
        out_specs=pl.BlockSpec((1, TM, n_out), lambda i, j: (i, j, 0)),
        out_shape=jax.ShapeDtypeStruct((b, t, n_out), BF16),
        compiler_params=_cparams(2), name=name,
    )(h, w, bd, gains, rc, rse, rso)


def _matmul_call(a, w, tn):
    m, k = a.shape
    n = w.shape[1]
    cw = 256

    def body(a_ref, w_ref, o_ref):
        av = a_ref[...]
        for c0 in range(0, tn, cw):
            o_ref[:, c0:c0 + cw] = jnp.dot(av, w_ref[:, c0:c0 + cw],
                                           preferred_element_type=F32).astype(BF16)

    return pl.pallas_call(
        body, grid=(n // tn, m // TM),
        in_specs=[pl.BlockSpec((TM, k), lambda j, i: (i, 0)),
                  pl.BlockSpec((k, tn), lambda j, i: (0, j))],
        out_specs=pl.BlockSpec((TM, tn), lambda j, i: (i, j)),
        out_shape=jax.ShapeDtypeStruct((m, n), BF16),
        compiler_params=_cparams(2), name="ffn_up",
    )(a, w)


def _outproj_call(ya, yb, wa, wb, x, modarr):
    b, t, d = x.shape
    nl = (t - TM) // TM
    ka, kb = ya.shape[2], yb.shape[2]

    def body(ya_ref, yb_ref, wa_ref, wb_ref, x_ref, m_ref, xo_ref, h_ref):
        y = (jnp.dot(ya_ref[0], wa_ref[...], preferred_element_type=F32)
             + jnp.dot(yb_ref[0], wb_ref[...], preferred_element_type=F32))
        x1 = x_ref[0] + m_ref[0, 0, 2:3, :] * y
        xo_ref[0] = x1
        h_ref[0] = _modulate(x1, m_ref[0, 0, 3:4, :], m_ref[0, 0, 4:5, :]).astype(BF16)

    return pl.pallas_call(
        body, grid=(b, t // TM),
        in_specs=[pl.BlockSpec((1, TM, ka), lambda i, j: (i, j, 0)),
                  pl.BlockSpec((1, TM, kb), lambda i, j: (i, j, 0)),
                  pl.BlockSpec((ka, d), lambda i, j: (0, 0)),
                  pl.BlockSpec((kb, d), lambda i, j: (0, 0)),
                  pl.BlockSpec((1, TM, d), lambda i, j: (i, j, 0)),
                  pl.BlockSpec((1, 1, 6, d), lambda i, j: (i, j // nl, 0, 0))],
        out_specs=[pl.BlockSpec((1, TM, d), lambda i, j: (i, j, 0)),
                   pl.BlockSpec((1, TM, d), lambda i, j: (i, j, 0))],
        out_shape=[jax.ShapeDtypeStruct((b, t, d), F32), jax.ShapeDtypeStruct((b, t, d), BF16)],
        compiler_params=_cparams(2), name="outproj",
    )(ya, yb, wa, wb, x, modarr)


def _ffn_down_call(gv, conv_w, conv_b, w_down, x, modarr, modarr_next):
    b, t, d = x.shape
    f = w_down.shape[0]
    nl = (t - TM) // TM
    nt = t // TM
    hb = 16
    r = TM // hb
    cw = 256

    def body(g_ref, v_ref, gp_ref, gn_ref, cw_ref, cb_ref, wd_ref, x_ref, m_ref, mn_ref,
             xo_ref, h_ref):
        j = pl.program_id(1)
        prev_ok = jnp.where((j == 0) | (j == nl), 0.0, 1.0)
        next_ok = jnp.where((j == nl - 1) | (j == nt - 1), 0.0, 1.0)
        acc = jnp.zeros((TM, d), F32)
        for c0 in range(0, f, cw):
            g = g_ref[0, :, c0:c0 + cw].astype(F32)
            gp = gp_ref[0, hb - 1:hb, c0:c0 + cw].astype(F32) * prev_ok
            gn = gn_ref[0, 0:1, c0:c0 + cw].astype(F32) * next_ok
            dn, up = _shift_rows(g, gp, gn)
            u = (dn * cw_ref[0:1, c0:c0 + cw] + g * cw_ref[1:2, c0:c0 + cw]
                 + up * cw_ref[2:3, c0:c0 + cw] + cb_ref[0:1, c0:c0 + cw])
            act = (u / (1.0 + jnp.exp(-u))) * v_ref[0, :, c0:c0 + cw].astype(F32)
            acc = acc + jnp.dot(act.astype(BF16), wd_ref[c0:c0 + cw, :], preferred_element_type=F32)
        x2 = x_ref[0] + m_ref[0, 0, 5:6, :] * acc
        xo_ref[0] = x2
        h_ref[0] = _modulate(x2, mn_ref[0, 0, 0:1, :], mn_ref[0, 0, 1:2, :]).astype(BF16)

    return pl.pallas_call(
        body, grid=(b, nt),
        in_specs=[pl.BlockSpec((1, TM, f), lambda i, j: (i, j, 0)),
                  pl.BlockSpec((1, TM, f), lambda i, j: (i, j, 1)),
                  pl.BlockSpec((1, hb, f), lambda i, j: (i, jnp.maximum(j * r - 1, 0), 0)),
                  pl.BlockSpec((1, hb, f), lambda i, j: (i, jnp.minimum((j + 1) * r, nt * r - 1), 0)),
                  pl.BlockSpec((3, f), lambda i, j: (0, 0)),
                  pl.BlockSpec((1, f), lambda i, j: (0, 0)),
                  pl.BlockSpec((f, d), lambda i, j: (0, 0)),
                  pl.BlockSpec((1, TM, d), lambda i, j: (i, j, 0)),
                  pl.BlockSpec((1, 1, 6, d), lambda i, j: (i, j // nl, 0, 0)),
                  pl.BlockSpec((1, 1, 6, d), lambda i, j: (i, j // nl, 0, 0))],
        out_specs=[pl.BlockSpec((1, TM, d), lambda i, j: (i, j, 0)),
                   pl.BlockSpec((1, TM, d), lambda i, j: (i, j, 0))],
        out_shape=[jax.ShapeDtypeStruct((b, t, d), F32), jax.ShapeDtypeStruct((b, t, d), BF16)],
        compiler_params=_cparams(2), name="ffn_down",
    )(gv, gv, gv, gv, conv_w, conv_b.reshape(1, f), w_down, x, modarr, modarr_next)


def _split_heads(q):
    lane = lax.broadcasted_iota(jnp.int32, q.shape, 1)
    zero = jnp.zeros_like(q)
    return jnp.where(lane < HEAD_DIM, q, zero), jnp.where(lane >= HEAD_DIM, q, zero)


def _flash_call(name, qkv, vt, n_blocks, qcol, kcol, kmap, vrows, finish, extra, out_cols, l):
    b, t, _ = qkv.shape
    c = t - l
    nl = l // TM
    n_chunks = l // TK

    def body(q_ref, k_ref, vt_ref, *rest):
        extra_refs, o_ref, acc_ref = rest[:-2], rest[-2], rest[-1]
        i = pl.program_id(2)
        qs = _split_heads(q_ref[0])

        def chunk(off, size, carry):
            kc = k_ref[0, pl.ds(off, size), :]
            vc = vt_ref[0, :, pl.ds(off, size)]
            new = []
            for s in range(2):
                m, lsum = carry[2 * s], carry[2 * s + 1]
                st = _dot_nt(kc, qs[s])
                mnew = jnp.maximum(m, jnp.max(st, axis=0, keepdims=True))
                alpha = jnp.exp(m - mnew)
                p = jnp.exp(st - mnew)
                lsum = alpha * lsum + jnp.sum(p, axis=0, keepdims=True)
                acc_ref[s] = alpha * acc_ref[s] + jnp.dot(vc, p.astype(BF16),
                                                          preferred_element_type=F32)
                new += [mnew, lsum]
            return tuple(new)

        acc_ref[...] = jnp.zeros(acc_ref.shape, F32)
        init = (jnp.full((1, TM), NEG, F32), jnp.zeros((1, TM), F32)) * 2
        trips = jnp.where(i < nl, n_chunks, 0)
        carry = lax.fori_loop(0, trips, lambda ci, cr: chunk(pl.multiple_of(ci * TK, TK), TK, cr), init)
        m1, l1, m2, l2 = chunk(l, c, carry)
        o_ref[0] = finish(acc_ref[0] / l1, acc_ref[1] / l2, *extra_refs)

    extra_specs = [pl.BlockSpec(e.shape, lambda bi, h, i: (0,) * e.ndim) for e in extra]
    return pl.pallas_call(
        body, grid=(b, n_blocks, t // TM),
        in_specs=[pl.BlockSpec((1, TM, LANES), lambda bi, h, i: (bi, i, qcol + h)),
                  pl.BlockSpec((1, t, LANES), lambda bi, h, i: (bi, 0, kcol + kmap(h))),
                  pl.BlockSpec((1, vrows, t), lambda bi, h, i: (bi, kmap(h), 0))] + extra_specs,
        out_specs=pl.BlockSpec((1, TM, LANES), lambda bi, h, i: (bi, i, h)),
        out_shape=jax.ShapeDtypeStruct((b, t, out_cols), BF16),
        scratch_shapes=[pltpu.VMEM((2, vrows, TM), F32)],
        compiler_params=_cparams(3), name=name,
    )(qkv, qkv, vt, *extra)


def _gqa_finish(o_lo, o_hi):
    return jnp.concatenate([o_lo, o_hi], axis=0).T.astype(BF16)


def _make_da_finish(lam_init):
    def finish(o1, o2, lam_ref, subln_ref):
        lv = lam_ref[...]
        lam = (jnp.exp(jnp.sum(lv[0:1] * lv[1:2], keepdims=True))
               - jnp.exp(jnp.sum(lv[2:3] * lv[3:4], keepdims=True)) + lam_init)
        o = (o1 - lam * o2).T
        o = o * lax.rsqrt(jnp.mean(o * o, axis=-1, keepdims=True) + EPS)
        return (o * subln_ref[...] * (1.0 - lam_init)).astype(BF16)
    return finish


def _na_call(qkv, vt, bias, l):
    b, t, _ = qkv.shape
    c = t - l
    nl = l // TM
    rows = l // GRID_W
    win = NA_WROWS * GRID_W
    n_pairs = vt.shape[1] // LANES

    def body(q_ref, k_ref, vt_ref, b_ref, o_ref):
        g = pl.program_id(2)
        qs = _split_heads(q_ref[0])
        kc = k_ref[0, l:l + c, :]

        @pl.when(g < nl)
        def _():
            w0 = pl.multiple_of(jnp.clip(g * NA_QROWS - NA_WIN_R // 2, 0, rows - NA_WROWS) * GRID_W, 256)
            kw = k_ref[0, pl.ds(w0, win), :]
            outs = []
            for s in range(2):
                sw = _dot_nt(kw, qs[s]) + b_ref[0, s]
                sc = _dot_nt(kc, qs[s])
                m = jnp.maximum(jnp.max(sw, axis=0, keepdims=True), jnp.max(sc, axis=0, keepdims=True))
                pw = jnp.exp(sw - m)
                pc = jnp.exp(sc - m)
                lsum = jnp.sum(pw, axis=0, keepdims=True) + jnp.sum(pc, axis=0, keepdims=True)
                r0 = s * HEAD_DIM
                o = (jnp.dot(vt_ref[0, r0:r0 + HEAD_DIM, pl.ds(w0, win)], pw.astype(BF16),
                             preferred_element_type=F32)
                     + jnp.dot(vt_ref[0, r0:r0 + HEAD_DIM, l:l + c], pc.astype(BF16),
                               preferred_element_type=F32))
                outs.append(o / lsum)
            o_ref[0] = jnp.concatenate(outs, axis=0).T.astype(BF16)

        @pl.when(g >= nl)
        def _():
            outs = []
            for s in range(2):
                sc = _dot_nt(kc, qs[s])
                m = jnp.max(sc, axis=0, keepdims=True)
                pc = jnp.exp(sc - m)
                lsum = jnp.sum(pc, axis=0, keepdims=True)
                r0 = s * HEAD_DIM
                o = jnp.dot(vt_ref[0, r0:r0 + HEAD_DIM, l:l + c], pc.astype(BF16),
                            preferred_element_type=F32)
                outs.append(o / lsum)
            o_ref[0] = jnp.concatenate(outs, axis=0).T.astype(BF16)

    def bias_idx(bi, h, g):
        case = jnp.where(g == 0, 0, jnp.where(g >= nl - 1, 2, 1))
        return (case * n_pairs + h, 0, 0, 0)

    return pl.pallas_call(
        body, grid=(b, n_pairs, t // TM),
        in_specs=[pl.BlockSpec((1, TM, LANES), lambda bi, h, g: (bi, g, h)),
                  pl.BlockSpec((1, t, LANES), lambda bi, h, g: (bi, 0, n_pairs + h)),
                  pl.BlockSpec((1, LANES, t), lambda bi, h, g: (bi, h, 0)),
                  pl.BlockSpec((1, 2, win, TM), bias_idx)],
        out_specs=pl.BlockSpec((1, TM, LANES), lambda bi, h, g: (bi, g, h)),
        out_shape=jax.ShapeDtypeStruct((b, t, n_pairs * LANES), BF16),
        compiler_params=_cparams(3), name="na_attn",
    )(qkv, qkv, vt, bias)


def _na_bias_table(rpb, rows):
    h = rpb.shape[0]
    qi = np.arange(TM)
    ki = np.arange(NA_WROWS * GRID_W)
    qr, qc = qi // GRID_W, qi % GRID_W
    kr, kc = ki // GRID_W, ki % GRID_W
    cs = np.clip(qc - NA_WIN_C // 2, 0, GRID_W - NA_WIN_C)
    col_ok = (kc[:, None] >= cs[None, :]) & (kc[:, None] < cs[None, :] + NA_WIN_C)
    co = np.clip(kc[:, None] - qc[None, :] + NA_WIN_C - 1, 0, 2 * NA_WIN_C - 2)
    tables = []
    for d, rel in ((0, np.zeros_like(qr)), (NA_WIN_R // 2, qr), (NA_WIN_R, np.full_like(qr, NA_WIN_R // 2))):
        row_ok = (kr[:, None] >= rel[None, :]) & (kr[:, None] < rel[None, :] + NA_WIN_R)
        ro = np.clip(kr[:, None] - qr[None, :] - d + NA_WIN_R - 1, 0, 2 * NA_WIN_R - 2)
        tables.append(jnp.where(jnp.asarray(row_ok & col_ok)[None], rpb[:, ro, co], NEG))
    tbl = jnp.stack(tables, axis=0)
    return tbl.reshape(3 * (h // 2), 2, NA_WROWS * GRID_W, TM).astype(F32)


def _hy_pre_call(qkv, conv_w, conv_b, l):
    b, t, _ = qkv.shape
    w3 = conv_w.shape[1]
    w = w3 // 3
    nl = l // TM
    nt = t // TM
    hb = 16
    r = TM // hb

    def body(u_ref, up_ref, un_ref, cw_ref, cb_ref, x0_ref, z_ref):
        j = pl.program_id(1)
        prev_ok = jnp.where((j == 0) | (j == nl), 0.0, 1.0)
        next_ok = jnp.where((j == nl - 1) | (j == nt - 1), 0.0, 1.0)
        parts = []
        for p in range(3):
            sl = slice(p * w, (p + 1) * w)
            g = u_ref[0, :, sl].astype(F32)
            gp = up_ref[0, hb - 1:hb, sl].astype(F32) * prev_ok
            gn = un_ref[0, 0:1, sl].astype(F32) * next_ok
            dn, up = _shift_rows(g, gp, gn)
            parts.append(dn * cw_ref[0:1, sl] + g * cw_ref[1:2, sl] + up * cw_ref[2:3, sl]
                         + cb_ref[0:1, sl])
        x0_ref[0] = parts[0].astype(BF16)
        z_ref[0] = (parts[2] * parts[1]).astype(BF16)

    return pl.pallas_call(
        body, grid=(b, nt),
        in_specs=[pl.BlockSpec((1, TM, w3), lambda i, j: (i, j, 0)),
                  pl.BlockSpec((1, hb, w3), lambda i, j: (i, jnp.maximum(j * r - 1, 0), 0)),
                  pl.BlockSpec((1, hb, w3), lambda i, j: (i, jnp.minimum((j + 1) * r, nt * r - 1), 0)),
                  pl.BlockSpec((3, w3), lambda i, j: (0, 0)),
                  pl.BlockSpec((1, w3), lambda i, j: (0, 0))],
        out_specs=[pl.BlockSpec((1, TM, w), lambda i, j: (i, j, 0)),
                   pl.BlockSpec((1, TM, w), lambda i, j: (i, j, 0))],
        out_shape=[jax.ShapeDtypeStruct((b, t, w), BF16), jax.ShapeDtypeStruct((b, t, w), BF16)],
        compiler_params=_cparams(2), name="hy_pre",
    )(qkv, qkv, qkv, conv_w, conv_b.reshape(1, w3))


def _hy_filter_call(length, w1, b1, w2, b2, w3, b3, w4, freq):
    order = w2.shape[0]
    w = w4.shape[1] // 2
    tl = min(length, 512)
    hi = lax.Precision.HIGHEST
    t = np.linspace(0.0, 1.0, length, dtype=np.float64)[:, None]
    bands = (HY_EMB_DIM - 1) // 2
    ang = 2.0 * math.pi * np.arange(length, dtype=np.float64)[:, None] / length
    fq = np.linspace(1e-4, bands - 1, bands, dtype=np.float64)[None, :]
    emb = np.concatenate([t, np.cos(fq * ang), -np.sin(fq * ang)], axis=-1).astype(np.float32)
    emb = np.pad(emb, ((0, 0), (0, LANES - HY_EMB_DIM)))
    max_decay = math.log(HY_DECAY_TARGET) / HY_FAST_DECAY
    min_decay = math.log(HY_DECAY_TARGET) / HY_SLOW_DECAY
    deltas = np.linspace(min_decay, max_decay, w, dtype=np.float64)
    decay = np.exp(-t * np.abs(deltas)[None, :]).astype(np.float32)
    decay2 = np.concatenate([decay, decay], axis=1)
    w1p = jnp.pad(w1, ((0, LANES - HY_EMB_DIM), (0, 0)))

    def body(e_ref, d_ref, w1_ref, b1_ref, w2_ref, b2_ref, w3_ref, b3_ref, w4_ref, f_ref,
             h_ref, s_ref):
        i = pl.program_id(0)
        fr = f_ref[...]
        hdn = jnp.sin(fr * (jnp.dot(e_ref[...], w1_ref[...], precision=hi,
                                    preferred_element_type=F32) + b1_ref[...]))
        hdn = jnp.sin(fr * (jnp.dot(hdn, w2_ref[...], precision=hi,
                                    preferred_element_type=F32) + b2_ref[...]))
        hdn = jnp.sin(fr * (jnp.dot(hdn, w3_ref[...], precision=hi,
                                    preferred_element_type=F32) + b3_ref[...]))
        taps = jnp.dot(hdn, w4_ref[...], precision=hi, preferred_element_type=F32) * d_ref[...]
        row = lax.broadcasted_iota(jnp.int32, taps.shape, 0) + i * tl
        col = lax.broadcasted_iota(jnp.int32, taps.shape, 1)
        taps = jnp.where((row == 0) & (col >= w), 0.0, taps)
        h_ref[...] = taps

        @pl.when(i == 0)
        def _():
            s_ref[...] = jnp.zeros(s_ref.shape, F32)
        s_ref[...] += jnp.sum(jnp.abs(taps), axis=0, keepdims=True)

    full = lambda a: pl.BlockSpec(a.shape, lambda i: (0,) * a.ndim)
    ops = (w1p, b1.reshape(1, order), w2, b2.reshape(1, order), w3, b3.reshape(1, order), w4,
           freq.reshape(1, order))
    return pl.pallas_call(
        body, grid=(length // tl,),
        in_specs=[pl.BlockSpec((tl, LANES), lambda i: (i, 0)),
                  pl.BlockSpec((tl, 2 * w), lambda i: (i, 0))] + [full(a) for a in ops],
        out_specs=[pl.BlockSpec((tl, 2 * w), lambda i: (i, 0)),
                   pl.BlockSpec((1, 2 * w), lambda i: (0, 0))],
        out_shape=[jax.ShapeDtypeStruct((length, 2 * w), F32), jax.ShapeDtypeStruct((1, 2 * w), F32)],
        compiler_params=_cparams(1), name="hy_filter",
    )(jnp.asarray(emb), jnp.asarray(decay2), *ops)


def _dft_tables(l1):
    n1 = 2 * l1
    n = n1 * LANES
    f1 = np.arange(n1, dtype=np.float64)
    t1 = np.arange(l1, dtype=np.float64)
    th1 = 2.0 * np.pi * np.outer(f1, t1) / n1
    fwd1 = np.concatenate([np.cos(th1), -np.sin(th1)], axis=0)
    inv1 = np.concatenate([np.cos(th1).T, -np.sin(th1).T], axis=1) / n
    f2 = np.arange(LANES, dtype=np.float64)
    t2 = np.arange(LANES, dtype=np.float64)
    fr = f1[:, None, None] + n1 * f2[None, :, None]
    th2 = 2.0 * np.pi * fr * t2[None, None, :] / n
    gr, gi = np.cos(th2), -np.sin(th2)
    gb = np.concatenate([np.concatenate([gr, -gi], axis=2),
                         np.concatenate([gi, gr], axis=2)], axis=1)
    hb = np.transpose(gb, (0, 2, 1))
    as_bf = lambda a: jnp.asarray(a.astype(np.float32)).astype(BF16)
    return as_bf(fwd1), as_bf(inv1), as_bf(gb), as_bf(hb)


def _hy_stage1_call(xv, fwd1):
    bx, l1, cols = xv.shape
    n2 = fwd1.shape[0]
    tn = min(cols, 4096)

    def body(f_ref, x_ref, o_ref):
        o_ref[0] = jnp.dot(f_ref[...], x_ref[0], preferred_element_type=F32).astype(BF16)

    return pl.pallas_call(
        body, grid=(bx, cols // tn),
        in_specs=[pl.BlockSpec((n2, l1), lambda i, j: (0, 0)),
                  pl.BlockSpec((1, l1, tn), lambda i, j: (i, 0, j))],
        out_specs=pl.BlockSpec((1, n2, tn), lambda i, j: (i, 0, j)),
        out_shape=jax.ShapeDtypeStruct((bx, n2, cols), BF16),
        compiler_params=_cparams(2), name="hy_dft1",
    )(fwd1, xv)


def _hy_filter_spec_call(a5, gb, sums, fb):
    n1, w2 = a5.shape[2], a5.shape[4]
    w = w2 // 2

    def body(a_ref, g_ref, s_ref, o_ref):
        sv = s_ref[...]
        inv = 1.0 / (sv[:, :w] + sv[:, w:])
        for k in range(fb):
            a = jnp.concatenate([a_ref[0, 0, k], a_ref[0, 1, k]], axis=0)
            z = jnp.dot(g_ref[k], a, preferred_element_type=F32)
            o_ref[k, 0] = (z[:LANES, :w] + z[:LANES, w:]) * inv
            o_ref[k, 1] = (z[LANES:, :w] - z[LANES:, w:]) * inv

    return pl.pallas_call(
        body, grid=(n1 // fb,),
        in_specs=[pl.BlockSpec((1, 2, fb, LANES, w2), lambda i: (0, 0, i, 0, 0)),
                  pl.BlockSpec((fb, 2 * LANES, 2 * LANES), lambda i: (i, 0, 0)),
                  pl.BlockSpec((1, w2), lambda i: (0, 0))],
        out_specs=pl.BlockSpec((fb, 2, LANES, w), lambda i: (i, 0, 0, 0)),
        out_shape=jax.ShapeDtypeStruct((n1, 2, LANES, w), F32),
        compiler_params=_cparams(1), name="hy_fspec",
    )(a5, gb, sums)


def _hy_stage23_call(a5, gb, hb, kf, fb):
    b, _, n1, _, w = a5.shape

    def body(a_ref, g_ref, h_ref, k_ref, o_ref):
        for k in range(fb):
            a = jnp.concatenate([a_ref[0, 0, k], a_ref[0, 1, k]], axis=0)
            z = jnp.dot(g_ref[k], a, preferred_element_type=F32)
            zr, zi = z[:LANES], z[LANES:]
            kr, ki = k_ref[k, 0], k_ref[k, 1]
            y = jnp.concatenate([zr * kr - zi * ki, zr * ki + zi * kr], axis=0).astype(BF16)
            cc = jnp.dot(h_ref[k], y, preferred_element_type=F32)
            o_ref[0, 0, k] = cc[:LANES].astype(BF16)
            o_ref[0, 1, k] = cc[LANES:].astype(BF16)

    return pl.pallas_call(
        body, grid=(n1 // fb, b),
        in_specs=[pl.BlockSpec((1, 2, fb, LANES, w), lambda i, j: (j, 0, i, 0, 0)),
                  pl.BlockSpec((fb, 2 * LANES, 2 * LANES), lambda i, j: (i, 0, 0)),
                  pl.BlockSpec((fb, 2 * LANES, 2 * LANES), lambda i, j: (i, 0, 0)),
                  pl.BlockSpec((fb, 2, LANES, w), lambda i, j: (i, 0, 0, 0))],
        out_specs=pl.BlockSpec((1, 2, fb, LANES, w), lambda i, j: (j, 0, i, 0, 0)),
        out_shape=jax.ShapeDtypeStruct(a5.shape, BF16),
        compiler_params=_cparams(2), name="hy_dft23",
    )(a5, gb, hb, kf)


def _hy_stage4_call(cv, inv1, x0v, zv, skip_t):
    b, n2, cols = cv.shape
    l1 = inv1.shape[0]
    tn = skip_t.shape[1]

    def body(f_ref, c_ref, x0_ref, z_ref, s_ref, o_ref):
        y = jnp.dot(f_ref[...], c_ref[0], preferred_element_type=F32)
        z = z_ref[0].astype(F32)
        o_ref[0] = (x0_ref[0].astype(F32) * (y + s_ref[...] * z)).astype(BF16)

    return pl.pallas_call(
        body, grid=(b, cols // tn),
        in_specs=[pl.BlockSpec((l1, n2), lambda i, j: (0, 0)),
                  pl.BlockSpec((1, n2, tn), lambda i, j: (i, 0, j)),
                  pl.BlockSpec((1, l1, tn), lambda i, j: (i, 0, j)),
                  pl.BlockSpec((1, l1, tn), lambda i, j: (i, 0, j)),
                  pl.BlockSpec((1, tn), lambda i, j: (0, 0))],
        out_specs=pl.BlockSpec((1, l1, tn), lambda i, j: (i, 0, j)),
        out_shape=jax.ShapeDtypeStruct((b, l1, cols), BF16),
        compiler_params=_cparams(2), name="hy_dft4",
    )(inv1, cv, x0v, zv, skip_t)


def _hy_dense_call(x0, z, taps, sums, skip):
    b, c, w = z.shape
    n = 2 * c
    th = 2.0 * np.pi * np.outer(np.arange(n, dtype=np.float64), np.arange(c, dtype=np.float64)) / n
    fwd = jnp.asarray(np.concatenate([np.cos(th), -np.sin(th)], axis=0).astype(np.float32)).astype(BF16)
    inv = jnp.asarray((np.concatenate([np.cos(th).T, -np.sin(th).T], axis=1) / n)
                      .astype(np.float32)).astype(BF16)

    def body(f_ref, i_ref, x0_ref, z_ref, t_ref, s_ref, k_ref, o_ref):
        sv = s_ref[...]
        nrm = 1.0 / (sv[:, :w] + sv[:, w:])
        tf = jnp.dot(f_ref[...], t_ref[...].astype(BF16), preferred_element_type=F32)
        kr = (tf[:n, :w] + tf[:n, w:]) * nrm
        ki = (tf[n:, :w] - tf[n:, w:]) * nrm
        zf = jnp.dot(f_ref[...], z_ref[0], preferred_element_type=F32)
        zr, zi = zf[:n], zf[n:]
        y = jnp.concatenate([zr * kr - zi * ki, zr * ki + zi * kr], axis=0).astype(BF16)
        yt = jnp.dot(i_ref[...], y, preferred_element_type=F32)
        o_ref[0] = (x0_ref[0].astype(F32) * (yt + k_ref[...] * z_ref[0].astype(F32))).astype(BF16)

    return pl.pallas_call(
        body, grid=(b,),
        in_specs=[pl.BlockSpec((2 * n, c), lambda i: (0, 0)),
                  pl.BlockSpec((c, 2 * n), lambda i: (0, 0)),
                  pl.BlockSpec((1, c, w), lambda i: (i, 0, 0)),
                  pl.BlockSpec((1, c, w), lambda i: (i, 0, 0)),
                  pl.BlockSpec((c, 2 * w), lambda i: (0, 0)),
                  pl.BlockSpec((1, 2 * w), lambda i: (0, 0)),
                  pl.BlockSpec((1, w), lambda i: (0, 0))],
        out_specs=pl.BlockSpec((1, c, w), lambda i: (i, 0, 0)),
        out_shape=jax.ShapeDtypeStruct((b, c, w), BF16),
        compiler_params=_cparams(1), name="hy_dense",
    )(fwd, inv, x0, z, taps, sums, skip.reshape(1, w))


def _hyena_long(x0, z, fparams, skip):
    b, l, w = z.shape
    l1 = l // LANES
    n1 = 2 * l1
    fb = min(8, n1)
    fwd1, inv1, gb, hb = _dft_tables(l1)
    taps, sums = _hy_filter_call(l, *fparams)
    ta = _hy_stage1_call(taps.astype(BF16).reshape(1, l1, LANES * 2 * w), fwd1)
    kf = _hy_filter_spec_call(ta.reshape(1, 2, n1, LANES, 2 * w), gb, sums, fb)
    za = _hy_stage1_call(z.reshape(b, l1, LANES * w), fwd1)
    cc = _hy_stage23_call(za.reshape(b, 2, n1, LANES, w), gb, hb, kf, fb)
    tn = 8 * w
    skip_t = jnp.tile(skip.reshape(1, w), (1, tn // w))
    y = _hy_stage4_call(cc.reshape(b, 2 * n1, LANES * w), inv1, x0.reshape(b, l1, LANES * w),
                        z.reshape(b, l1, LANES * w), skip_t)
    return y.reshape(b, l, w)


def _hyena_short(x0, z, fparams, skip):
    taps, sums = _hy_filter_call(z.shape[1], *fparams)
    return _hy_dense_call(x0, z, taps, sums, skip)


def _rope_tables(l, c):
    t = np.arange(l)
    row = (t // GRID_W).astype(np.float64)
    col = (t % GRID_W).astype(np.float64)
    n_pairs = HEAD_DIM // 4
    inv_freq = ROPE_THETA ** (-np.arange(n_pairs, dtype=np.float64) / n_pairs)
    ang = np.concatenate([row[:, None] * inv_freq, col[:, None] * inv_freq], axis=-1)
    cos = np.repeat(np.cos(ang), 2, axis=1)
    sin = np.repeat(np.sin(ang), 2, axis=1)
    even = (np.arange(HEAD_DIM) % 2 == 0)[None, :]
    se = np.where(even, -sin, 0.0)
    so = np.where(even, 0.0, sin)
    pad = lambda a, v: np.concatenate([a, np.full((c, HEAD_DIM), v)], axis=0)
    two = lambda a: jnp.asarray(np.concatenate([a, a], axis=1).astype(np.float32))
    return two(pad(cos, 1.0)), two(pad(se, 0.0)), two(pad(so, 0.0))


def _block_diag_ones():
    i = np.arange(2 * LANES)
    return jnp.asarray((i[:, None] // HEAD_DIM == i[None, :] // HEAD_DIM).astype(np.float32)).astype(BF16)


def _gain_rows(gains, scales):
    rows = [jnp.tile(g.astype(F32) * s, 2 * LANES // HEAD_DIM) for g, s in zip(gains, scales)]
    rows += [jnp.zeros((2 * LANES,), F32)] * (8 - len(rows))
    return jnp.stack(rows, axis=0)


def kernel(x, c, ctx, c_ctx, w_ada, b_ada, w_up, ffn_conv_w, ffn_conv_b, w_down, w_in_e, w_out_e, na_q_gain, na_k_gain, na_rpb, da_q_gain, da_k_gain, da_lambda_q1, da_lambda_k1, da_lambda_q2, da_lambda_k2, da_subln_gain, w_in_o, w_out_o, gqa_q_gain, gqa_k_gain, hy_conv_w, hy_conv_b, hy_w1, hy_b1, hy_w2, hy_b2, hy_w3, hy_b3, hy_w4, hy_freq, hy_skip):
    b, l, d = x.shape
    cl = ctx.shape[1]
    t = l + cl
    depth = w_ada.shape[0]
    f = w_down.shape[1]
    assert cl == TM and l % TK == 0 and (l // GRID_W) >= NA_WROWS + NA_QROWS
    assert w_in_e.shape[2] == 3072 and w_in_o.shape[2] == 2304 and d % LANES == 0
    scale = HEAD_DIM ** -0.5

    rows = -(-(b + 1) // 8) * 8
    cs = jnp.zeros((rows, d), F32).at[:b].set(c).at[b].set(c_ctx)
    mods = _ada_call(cs, w_ada, b_ada)
    modarrs = []
    for layer in range(depth):
        lat = mods[layer, :b].reshape(b, 1, 6, d)
        cx = jnp.broadcast_to(mods[layer, b].reshape(1, 1, 6, d), (b, 1, 6, d))
        modarrs.append(jnp.concatenate([lat, cx], axis=1))

    ropes = _rope_tables(l, cl)
    bd = _block_diag_ones()
    x_all, h = _mod0_call(x, ctx, modarrs[0])

    for layer in range(depth):
        i = layer // 2
        if layer % 2 == 0:
            lam_init = 0.8 - 0.6 * math.exp(-0.3 * layer)
            gains = _gain_rows((na_q_gain[i], na_k_gain[i], da_q_gain[i], da_k_gain[i]),
                               (scale, 1.0, scale, 1.0))
            qkv = _inproj_call(_inproj_even_body, "inproj_even", h, w_in_e[i].astype(BF16), bd, gains,
                               ropes, 3072)
            vat = jnp.swapaxes(qkv[:, :, 1024:1536], 1, 2)
            vbt = jnp.swapaxes(qkv[:, :, 2560:3072], 1, 2)
            bias = _na_bias_table(na_rpb[i], l // GRID_W)
            ya = _na_call(qkv, vat, bias, l)
            lamv = jnp.stack([da_lambda_q1[i], da_lambda_k1[i], da_lambda_q2[i], da_lambda_k2[i]]).astype(F32)
            yb = _flash_call("da_attn", qkv, vbt, 4, 12, 16, lambda hh: hh, LANES,
                             _make_da_finish(lam_init),
                             (lamv, da_subln_gain[i].reshape(1, LANES).astype(F32)), 512, l)
            w_out = w_out_e[i].astype(BF16)
        else:
            gains = _gain_rows((gqa_q_gain[i], gqa_k_gain[i]), (scale, 1.0))
            qkv = _inproj_call(_inproj_odd_body, "inproj_odd", h, w_in_o[i].astype(BF16), bd, gains,
                               ropes, 2432)
            vt = jnp.swapaxes(qkv[:, :, 2304:2432], 1, 2)
            ya = _flash_call("gqa_attn", qkv, vt, 4, 12, 16, lambda hh: hh // 2, HEAD_DIM,
                             _gqa_finish, (), 512, l)
            x0, z = _hy_pre_call(qkv, hy_conv_w[i], hy_conv_b[i], l)
            fparams = (hy_w1[i], hy_b1[i], hy_w2[i], hy_b2[i], hy_w3[i], hy_b3[i], hy_w4[i], hy_freq[i])
            yd_l = _hyena_long(x0[:, :l], z[:, :l], fparams, hy_skip[i])
            if layer < depth - 1:
                yd_c = _hyena_short(x0[:, l:], z[:, l:], fparams, hy_skip[i])
            else:
                yd_c = jnp.zeros((b, cl, x0.shape[2]), BF16)
            yb = jnp.concatenate([yd_l, yd_c], axis=1)
            w_out = w_out_o[i].astype(BF16)
        ka = ya.shape[2]
        x_all, h2 = _outproj_call(ya, yb, w_out[:ka], w_out[ka:], x_all, modarrs[layer])
        gv = _matmul_call(h2.reshape(b * t, d), w_up[layer].astype(BF16), f).reshape(b, t, 2 * f)
        x_all, h = _ffn_down_call(gv, ffn_conv_w[layer], ffn_conv_b[layer], w_down[layer].astype(BF16),
                                  x_all, modarrs[layer], modarrs[min(layer + 1, depth - 1)])
    return x_all[:, :l]
```

```python
import functools
import math

import numpy as np
import jax
import jax.numpy as jnp
from jax import lax
from jax.experimental import pallas as pl
from jax.experimental.pallas import tpu as pltpu

F32 = jnp.float32
BF16 = jnp.bfloat16

HEAD_DIM = 64
GRID_W = 64
ROPE_THETA = 10000.0
EPS = 1e-6
NA_WIN_R = 8
NA_WIN_C = 16
HY_EMB_DIM = 33
HY_FAST_DECAY = 0.3
HY_SLOW_DECAY = 1.5
HY_DECAY_TARGET = 1e-2

LANES = 128
TM = 256
TK = 512
NA_QROWS = TM // GRID_W
NA_WROWS = NA_QROWS + NA_WIN_R
NEG = -1e30
LOG2E = 1.4426950408889634
VMEM_LIMIT = 56 * 1024 * 1024


def _cparams(n_axes):
    return pltpu.CompilerParams(dimension_semantics=("arbitrary",) * n_axes,
                                vmem_limit_bytes=VMEM_LIMIT)


def _modulate(x, sh, sc):
    ms = jnp.mean(x * x, axis=-1, keepdims=True)
    return x * lax.rsqrt(ms + EPS) * (1.0 + sc) + sh


def _seg_norm(y, bd, gain):
    ss = jnp.dot((y * y).astype(BF16), bd, preferred_element_type=F32)
    return y * lax.rsqrt(ss * (1.0 / HEAD_DIM) + EPS) * gain


def _rope(y, c, se, so):
    return y * c + pltpu.roll(y, LANES - 1, 1) * se + pltpu.roll(y, 1, 1) * so


def _dot_nt(a, b):
    return lax.dot_general(a, b, (((1,), (1,)), ((), ())), preferred_element_type=F32)


def _shift_rows(g, prev_row, next_row):
    tm = g.shape[0]
    row = lax.broadcasted_iota(jnp.int32, g.shape, 0)
    dn = jnp.where(row == 0, prev_row, pltpu.roll(g, 1, 0))
    up = jnp.where(row == tm - 1, next_row, pltpu.roll(g, tm - 1, 0))
    return dn, up


def _ada_body(c_ref, w_ref, b_ref, o_ref):
    c = c_ref[...]
    a = (c / (1.0 + jnp.exp(-c))).astype(BF16)
    o_ref[0] = jnp.dot(a, w_ref[0].astype(BF16), preferred_element_type=F32) + b_ref[0]


def _ada_call(cs, w_ada, b_ada):
    depth, d, n6 = w_ada.shape
    rows = cs.shape[0]
    tn = 1536
    return pl.pallas_call(
        _ada_body, grid=(depth, n6 // tn),
        in_specs=[pl.BlockSpec((rows, d), lambda l, n: (0, 0)),
                  pl.BlockSpec((1, d, tn), lambda l, n: (l, 0, n)),
                  pl.BlockSpec((1, 1, tn), lambda l, n: (l, 0, n))],
        out_specs=pl.BlockSpec((1, rows, tn), lambda l, n: (l, 0, n)),
        out_shape=jax.ShapeDtypeStruct((depth, rows, n6), F32),
        compiler_params=_cparams(2), name="ada",
    )(cs, w_ada, b_ada.reshape(depth, 1, n6))


def _mod0_call(x, ctx, modarr):
    b, l, d = x.shape
    c = ctx.shape[1]
    t = l + c
    nl = l // TM

    def body(x_ref, c_ref, m_ref, xo_ref, h_ref):
        j = pl.program_id(1)
        xv = jnp.where(j < nl, x_ref[0], c_ref[0])
        xo_ref[0] = xv
        h_ref[0] = _modulate(xv, m_ref[0, 0, 0:1, :], m_ref[0, 0, 1:2, :]).astype(BF16)

    return pl.pallas_call(
        body, grid=(b, t // TM),
        in_specs=[pl.BlockSpec((1, TM, d), lambda i, j: (i, jnp.minimum(j, nl - 1), 0)),
                  pl.BlockSpec((1, TM, d), lambda i, j: (i, 0, 0)),
                  pl.BlockSpec((1, 1, 6, d), lambda i, j: (i, j // nl, 0, 0))],
        out_specs=[pl.BlockSpec((1, TM, d), lambda i, j: (i, j, 0)),
                   pl.BlockSpec((1, TM, d), lambda i, j: (i, j, 0))],
        out_shape=[jax.ShapeDtypeStruct((b, t, d), F32), jax.ShapeDtypeStruct((b, t, d), BF16)],
        compiler_params=_cparams(2), name="mod0",
    )(x, ctx, modarr)


def _inproj_even_body(a_ref, w_ref, bd_ref, g_ref, rc_ref, rse_ref, rso_ref, o_ref):
    a = a_ref[0]
    bd = bd_ref[...]
    rc, rse, rso = rc_ref[...], rse_ref[...], rso_ref[...]
    plan = ((0, False), (1, False), (None, False), (2, True), (3, True), (None, False))
    for seg, (gain_row, rope) in enumerate(plan):
        y = jnp.dot(a, w_ref[:, seg * 512:(seg + 1) * 512], preferred_element_type=F32)
        for half in range(2):
            yy = y[:, half * 256:(half + 1) * 256]
            if gain_row is not None:
                yy = _seg_norm(yy, bd, g_ref[gain_row:gain_row + 1, :])
            for blk in range(2):
                z = yy[:, blk * LANES:(blk + 1) * LANES]
                if rope:
                    z = _rope(z, rc, rse, rso)
                col = seg * 512 + half * 256 + blk * LANES
                o_ref[0, :, col:col + LANES] = z.astype(BF16)


def _inproj_odd_body(a_ref, w_ref, bd_ref, g_ref, rc_ref, rse_ref, rso_ref, o_ref):
    a = a_ref[0]
    bd = bd_ref[...]
    rc, rse, rso = rc_ref[...], rse_ref[...], rso_ref[...]
    y = jnp.dot(a, w_ref[:, 0:512], preferred_element_type=F32)
    for half in range(2):
        yy = _seg_norm(y[:, half * 256:(half + 1) * 256], bd, g_ref[0:1, :])
        for blk in range(2):
            z = _rope(yy[:, blk * LANES:(blk + 1) * LANES], rc, rse, rso)
            col = 1536 + half * 256 + blk * LANES
            o_ref[0, :, col:col + LANES] = z.astype(BF16)
    y = jnp.dot(a, w_ref[:, 512:768], preferred_element_type=F32)
    k = _seg_norm(y[:, :LANES], bd_ref[0:LANES, 0:LANES], g_ref[1:2, 0:LANES])
    k = _rope(k, rc, rse, rso)
    kr = pltpu.roll(k, HEAD_DIM, 1)
    lo = lax.broadcasted_iota(jnp.int32, k.shape, 1) < HEAD_DIM
    o_ref[0, :, 2048:2176] = jnp.where(lo, k, kr).astype(BF16)
    o_ref[0, :, 2176:2304] = jnp.where(lo, kr, k).astype(BF16)
    o_ref[0, :, 2304:2432] = y[:, LANES:].astype(BF16)
    for seg in range(3):
        y = jnp.dot(a, w_ref[:, 768 + seg * 512:768 + (seg + 1) * 512], preferred_element_type=F32)
        o_ref[0, :, seg * 512:(seg + 1) * 512] = y.astype(BF16)


def _inproj_call(body, name, h, w, bd, gains, ropes, n_out):
    b, t, d = h.shape
    n_in = w.shape[1]
    rc, rse, rso = ropes
    return pl.pallas_call(
        body, grid=(b, t // TM),
        in_specs=[pl.BlockSpec((1, TM, d), lambda i, j: (i, j, 0)),
                  pl.BlockSpec((d, n_in), lambda i, j: (0, 0)),
                  pl.BlockSpec(bd.shape, lambda i, j: (0, 0)),
                  pl.BlockSpec(gains.shape, lambda i, j: (0, 0)),
                  pl.BlockSpec((TM, LANES), lambda i, j: (j, 0)),
                  pl.BlockSpec((TM, LANES), lambda i, j: (j, 0)),
                  pl.BlockSpec((TM, LANES), lambda i, j: (j, 0))],
        out_specs=pl.BlockSpec((1, TM, n_out), lambda i, j: (i, j, 0)),
        out_shape=jax.ShapeDtypeStruct((b, t, n_out), BF16),
        compiler_params=_cparams(2), name=name,
    )(h, w, bd, gains, rc, rse, rso)


def _matmul_call(a, w, tn):
    m, k = a.shape
    n = w.shape[1]
    cw = 256

    def body(a_ref, w_ref, o_ref):
        av = a_ref[...]
        for c0 in range(0, tn, cw):
            o_ref[:, c0:c0 + cw] = jnp.dot(av, w_ref[:, c0:c0 + cw],
                                           preferred_element_type=F32).astype(BF16)

    return pl.pallas_call(
        body, grid=(n // tn, m // TM),
        in_specs=[pl.BlockSpec((TM, k), lambda j, i: (i, 0)),
                  pl.BlockSpec((k, tn), lambda j, i: (0, j))],
        out_specs=pl.BlockSpec((TM, tn), lambda j, i: (i, j)),
        out_shape=jax.ShapeDtypeStruct((m, n), BF16),
        compiler_params=_cparams(2), name="ffn_up",
    )(a, w)


def _outproj_call(ya, yb, wa, wb, x, modarr):
    b, t, d = x.shape
    nl = (t - TM) // TM
    ka, kb = ya.shape[2], yb.shape[2]

    def body(ya_ref, yb_ref, wa_ref, wb_ref, x_ref, m_ref, xo_ref, h_ref):
        y = (jnp.dot(ya_ref[0], wa_ref[...], preferred_element_type=F32)
             + jnp.dot(yb_ref[0], wb_ref[...], preferred_element_type=F32))
        x1 = x_ref[0] + m_ref[0, 0, 2:3, :] * y
        xo_ref[0] = x1
        h_ref[0] = _modulate(x1, m_ref[0, 0, 3:4, :], m_ref[0, 0, 4:5, :]).astype(BF16)

    return pl.pallas_call(
        body, grid=(b, t // TM),
        in_specs=[pl.BlockSpec((1, TM, ka), lambda i, j: (i, j, 0)),
                  pl.BlockSpec((1, TM, kb), lambda i, j: (i, j, 0)),
                  pl.BlockSpec((ka, d), lambda i, j: (0, 0)),
                  pl.BlockSpec((kb, d), lambda i, j: (0, 0)),
                  pl.BlockSpec((1, TM, d), lambda i, j: (i, j, 0)),
                  pl.BlockSpec((1, 1, 6, d), lambda i, j: (i, j // nl, 0, 0))],
        out_specs=[pl.BlockSpec((1, TM, d), lambda i, j: (i, j, 0)),
                   pl.BlockSpec((1, TM, d), lambda i, j: (i, j, 0))],
        out_shape=[jax.ShapeDtypeStruct((b, t, d), F32), jax.ShapeDtypeStruct((b, t, d), BF16)],
        compiler_params=_cparams(2), name="outproj",
    )(ya, yb, wa, wb, x, modarr)


def _ffn_down_call(gv, conv_w, conv_b, w_down, x, modarr, modarr_next):
    b, t, d = x.shape
    f = w_down.shape[0]
    nl = (t - TM) // TM
    nt = t // TM
    hb = 16
    r = TM // hb
    cw = 256

    def body(g_ref, v_ref, gp_ref, gn_ref, cw_ref, cb_ref, wd_ref, x_ref, m_ref, mn_ref,
             xo_ref, h_ref):
        j = pl.program_id(1)
        prev_ok = jnp.where((j == 0) | (j == nl), 0.0, 1.0)
        next_ok = jnp.where((j == nl - 1) | (j == nt - 1), 0.0, 1.0)
        acc = jnp.zeros((TM, d), F32)
        for c0 in range(0, f, cw):
            g = g_ref[0, :, c0:c0 + cw].astype(F32)
            gp = gp_ref[0, hb - 1:hb, c0:c0 + cw].astype(F32) * prev_ok
            gn = gn_ref[0, 0:1, c0:c0 + cw].astype(F32) * next_ok
            dn, up = _shift_rows(g, gp, gn)
            u = (dn * cw_ref[0:1, c0:c0 + cw] + g * cw_ref[1:2, c0:c0 + cw]
                 + up * cw_ref[2:3, c0:c0 + cw] + cb_ref[0:1, c0:c0 + cw])
            act = (u / (1.0 + jnp.exp(-u))) * v_ref[0, :, c0:c0 + cw].astype(F32)
            acc = acc + jnp.dot(act.astype(BF16), wd_ref[c0:c0 + cw, :], preferred_element_type=F32)
        x2 = x_ref[0] + m_ref[0, 0, 5:6, :] * acc
        xo_ref[0] = x2
        h_ref[0] = _modulate(x2, mn_ref[0, 0, 0:1, :], mn_ref[0, 0, 1:2, :]).astype(BF16)

    return pl.pallas_call(
        body, grid=(b, nt),
        in_specs=[pl.BlockSpec((1, TM, f), lambda i, j: (i, j, 0)),
                  pl.BlockSpec((1, TM, f), lambda i, j: (i, j, 1)),
                  pl.BlockSpec((1, hb, f), lambda i, j: (i, jnp.maximum(j * r - 1, 0), 0)),
                  pl.BlockSpec((1, hb, f), lambda i, j: (i, jnp.minimum((j + 1) * r, nt * r - 1), 0)),
                  pl.BlockSpec((3, f), lambda i, j: (0, 0)),
                  pl.BlockSpec((1, f), lambda i, j: (0, 0)),
                  pl.BlockSpec((f, d), lambda i, j: (0, 0)),
                  pl.BlockSpec((1, TM, d), lambda i, j: (i, j, 0)),
                  pl.BlockSpec((1, 1, 6, d), lambda i, j: (i, j // nl, 0, 0)),
                  pl.BlockSpec((1, 1, 6, d), lambda i, j: (i, j // nl, 0, 0))],
        out_specs=[pl.BlockSpec((1, TM, d), lambda i, j: (i, j, 0)),
                   pl.BlockSpec((1, TM, d), lambda i, j: (i, j, 0))],
        out_shape=[jax.ShapeDtypeStruct((b, t, d), F32), jax.ShapeDtypeStruct((b, t, d), BF16)],
        compiler_params=_cparams(2), name="ffn_down",
    )(gv, gv, gv, gv, conv_w, conv_b.reshape(1, f), w_down, x, modarr, modarr_next)


def _split_heads(q):
    lane = lax.broadcasted_iota(jnp.int32, q.shape, 1)
    zero = jnp.zeros_like(q)
    return jnp.where(lane < HEAD_DIM, q, zero), jnp.where(lane >= HEAD_DIM, q, zero)


def _flash_call(name, qkv, vt, n_blocks, qcol, kcol, kmap, vrows, finish, extra, out_cols, l):
    b, t, _ = qkv.shape
    c = t - l
    nl = l // TM
    n_pairs = l // (2 * TK)
    assert l % (2 * TK) == 0

    def body(q_ref, k_ref, vt_ref, *rest):
        extra_refs = rest[:-6]
        o_ref, s_ref, sc_ref, acc_ref, m_ref, l_ref = rest[-6:]
        is_lat = pl.program_id(2) < nl
        qs = _split_heads(q_ref[0])

        def scores(off, size, dst):
            kc = k_ref[0, pl.ds(off, size), :]
            for s in range(2):
                dst(s)[...] = _dot_nt(kc, qs[s])

        def absorb(src, off, size):
            vc = vt_ref[0, :, pl.ds(off, size)]
            for s in range(2):
                st = src(s)[...]
                m = m_ref[s]
                mnew = jnp.maximum(m, jnp.max(st, axis=0, keepdims=True))
                alpha = jnp.exp2(m - mnew)
                p = jnp.exp2(st - mnew)
                l_ref[s] = alpha * l_ref[s] + jnp.sum(p, axis=0, keepdims=True)
                m_ref[s] = mnew
                acc_ref[s] = alpha * acc_ref[s] + jnp.dot(vc, p.astype(BF16),
                                                          preferred_element_type=F32)

        ctx_buf = lambda s: sc_ref.at[s]
        buf0 = lambda s: s_ref.at[0, s]
        buf1 = lambda s: s_ref.at[1, s]

        acc_ref[...] = jnp.zeros(acc_ref.shape, F32)
        m_ref[...] = jnp.full(m_ref.shape, NEG, F32)
        l_ref[...] = jnp.zeros(l_ref.shape, F32)
        scores(l, c, ctx_buf)

        @pl.when(is_lat)
        def _():
            scores(0, TK, buf0)

        absorb(ctx_buf, l, c)

        @pl.when(is_lat)
        def _():
            def pair(pi, carry):
                base = pl.multiple_of(pi * (2 * TK), 2 * TK)
                scores(base + TK, TK, buf1)
                absorb(buf0, base, TK)
                scores(base + 2 * TK, TK, buf0)
                absorb(buf1, base + TK, TK)
                return carry

            lax.fori_loop(0, n_pairs - 1, pair, 0)
            base = (n_pairs - 1) * 2 * TK
            scores(base + TK, TK, buf1)
            absorb(buf0, base, TK)
            absorb(buf1, base + TK, TK)

        o_ref[0] = finish(acc_ref[0] / l_ref[0], acc_ref[1] / l_ref[1], *extra_refs)

    extra_specs = [pl.BlockSpec(e.shape, lambda bi, h, i: (0,) * e.ndim) for e in extra]
    return pl.pallas_call(
        body, grid=(b, n_blocks, t // TM),
        in_specs=[pl.BlockSpec((1, TM, LANES), lambda bi, h, i: (bi, i, qcol + h)),
                  pl.BlockSpec((1, t, LANES), lambda bi, h, i: (bi, 0, kcol + kmap(h))),
                  pl.BlockSpec((1, vrows, t), lambda bi, h, i: (bi, kmap(h), 0))] + extra_specs,
        out_specs=pl.BlockSpec((1, TM, LANES), lambda bi, h, i: (bi, i, h)),
        out_shape=jax.ShapeDtypeStruct((b, t, out_cols), BF16),
        scratch_shapes=[pltpu.VMEM((2, 2, TK, TM), F32), pltpu.VMEM((2, c, TM), F32),
                        pltpu.VMEM((2, vrows, TM), F32), pltpu.VMEM((2, 1, TM), F32),
                        pltpu.VMEM((2, 1, TM), F32)],
        compiler_params=_cparams(3), name=name,
    )(qkv, qkv, vt, *extra)


def _gqa_finish(o_lo, o_hi):
    return jnp.concatenate([o_lo, o_hi], axis=0).T.astype(BF16)


def _make_da_finish(lam_init):
    def finish(o1, o2, lam_ref, subln_ref):
        lv = lam_ref[...]
        lam = (jnp.exp(jnp.sum(lv[0:1] * lv[1:2], keepdims=True))
               - jnp.exp(jnp.sum(lv[2:3] * lv[3:4], keepdims=True)) + lam_init)
        o = (o1 - lam * o2).T
        o = o * lax.rsqrt(jnp.mean(o * o, axis=-1, keepdims=True) + EPS)
        return (o * subln_ref[...] * (1.0 - lam_init)).astype(BF16)
    return finish


def _na_call(qkv, vt, bias, l):
    b, t, _ = qkv.shape
    c = t - l
    nl = l // TM
    rows = l // GRID_W
    win = NA_WROWS * GRID_W
    n_pairs = vt.shape[1] // LANES

    def body(q_ref, k_ref, vt_ref, b_ref, o_ref):
        g = pl.program_id(2)
        qs = _split_heads(q_ref[0])
        kc = k_ref[0, l:l + c, :]

        @pl.when(g < nl)
        def _():
            w0 = pl.multiple_of(jnp.clip(g * NA_QROWS - NA_WIN_R // 2, 0, rows - NA_WROWS) * GRID_W, 256)
            kw = k_ref[0, pl.ds(w0, win), :]
            outs = []
            for s in range(2):
                sw = _dot_nt(kw, qs[s]) + b_ref[0, s]
                sc = _dot_nt(kc, qs[s])
                m = jnp.maximum(jnp.max(sw, axis=0, keepdims=True), jnp.max(sc, axis=0, keepdims=True))
                pw = jnp.exp2(sw - m)
                pc = jnp.exp2(sc - m)
                lsum = jnp.sum(pw, axis=0, keepdims=True) + jnp.sum(pc, axis=0, keepdims=True)
                r0 = s * HEAD_DIM
                o = (jnp.dot(vt_ref[0, r0:r0 + HEAD_DIM, pl.ds(w0, win)], pw.astype(BF16),
                             preferred_element_type=F32)
                     + jnp.dot(vt_ref[0, r0:r0 + HEAD_DIM, l:l + c], pc.astype(BF16),
                               preferred_element_type=F32))
                outs.append(o / lsum)
            o_ref[0] = jnp.concatenate(outs, axis=0).T.astype(BF16)

        @pl.when(g >= nl)
        def _():
            outs = []
            for s in range(2):
                sc = _dot_nt(kc, qs[s])
                m = jnp.max(sc, axis=0, keepdims=True)
                pc = jnp.exp2(sc - m)
                lsum = jnp.sum(pc, axis=0, keepdims=True)
                r0 = s * HEAD_DIM
                o = jnp.dot(vt_ref[0, r0:r0 + HEAD_DIM, l:l + c], pc.astype(BF16),
                            preferred_element_type=F32)
                outs.append(o / lsum)
            o_ref[0] = jnp.concatenate(outs, axis=0).T.astype(BF16)

    def bias_idx(bi, h, g):
        case = jnp.where(g == 0, 0, jnp.where(g >= nl - 1, 2, 1))
        return (case * n_pairs + h, 0, 0, 0)

    return pl.pallas_call(
        body, grid=(b, n_pairs, t // TM),
        in_specs=[pl.BlockSpec((1, TM, LANES), lambda bi, h, g: (bi, g, h)),
                  pl.BlockSpec((1, t, LANES), lambda bi, h, g: (bi, 0, n_pairs + h)),
                  pl.BlockSpec((1, LANES, t), lambda bi, h, g: (bi, h, 0)),
                  pl.BlockSpec((1, 2, win, TM), bias_idx)],
        out_specs=pl.BlockSpec((1, TM, LANES), lambda bi, h, g: (bi, g, h)),
        out_shape=jax.ShapeDtypeStruct((b, t, n_pairs * LANES), BF16),
        compiler_params=_cparams(3), name="na_attn",
    )(qkv, qkv, vt, bias)


def _na_bias_table(rpb):
    h, n_ro, n_co = rpb.shape
    kr, qr = np.arange(NA_WROWS), np.arange(NA_QROWS)
    kc, qc = np.arange(GRID_W), np.arange(GRID_W)
    cs = np.clip(qc - NA_WIN_C // 2, 0, GRID_W - NA_WIN_C)
    col_ok = (kc[:, None] >= cs[None, :]) & (kc[:, None] < cs[None, :] + NA_WIN_C)
    co = kc[:, None] - qc[None, :] + NA_WIN_C - 1
    col_sel = (co[None] == np.arange(n_co)[:, None, None]).astype(np.float32)
    tables = []
    for d, rel in ((0, np.zeros_like(qr)), (NA_WIN_R // 2, qr), (NA_WIN_R, np.full_like(qr, NA_WIN_R // 2))):
        row_ok = (kr[:, None] >= rel[None, :]) & (kr[:, None] < rel[None, :] + NA_WIN_R)
        ro = kr[:, None] - qr[None, :] - d + NA_WIN_R - 1
        row_sel = (ro[:, :, None] == np.arange(n_ro)[None, None, :]).astype(np.float32)
        tb = jnp.einsum('kqa,hab,bcd->hkcqd', jnp.asarray(row_sel), rpb.astype(F32) * LOG2E,
                        jnp.asarray(col_sel), precision=lax.Precision.HIGHEST)
        ok = row_ok[:, None, :, None] & col_ok[None, :, None, :]
        tables.append(jnp.where(jnp.asarray(ok)[None], tb, NEG))
    tbl = jnp.stack(tables, axis=0)
    return tbl.reshape(3 * (h // 2), 2, NA_WROWS * GRID_W, TM).astype(F32)


def _hy_pre_call(qkv, conv_w, conv_b, l):
    b, t, _ = qkv.shape
    w3 = conv_w.shape[1]
    w = w3 // 3
    nl = l // TM
    nt = t // TM
    hb = 16
    r = TM // hb

    def body(u_ref, up_ref, un_ref, cw_ref, cb_ref, x0_ref, z_ref):
        j = pl.program_id(1)
        prev_ok = jnp.where((j == 0) | (j == nl), 0.0, 1.0)
        next_ok = jnp.where((j == nl - 1) | (j == nt - 1), 0.0, 1.0)
        parts = []
        for p in range(3):
            sl = slice(p * w, (p + 1) * w)
            g = u_ref[0, :, sl].astype(F32)
            gp = up_ref[0, hb - 1:hb, sl].astype(F32) * prev_ok
            gn = un_ref[0, 0:1, sl].astype(F32) * next_ok
            dn, up = _shift_rows(g, gp, gn)
            parts.append(dn * cw_ref[0:1, sl] + g * cw_ref[1:2, sl] + up * cw_ref[2:3, sl]
                         + cb_ref[0:1, sl])
        x0_ref[0] = parts[0].astype(BF16)
        z_ref[0] = (parts[2] * parts[1]).astype(BF16)

    return pl.pallas_call(
        body, grid=(b, nt),
        in_specs=[pl.BlockSpec((1, TM, w3), lambda i, j: (i, j, 0)),
                  pl.BlockSpec((1, hb, w3), lambda i, j: (i, jnp.maximum(j * r - 1, 0), 0)),
                  pl.BlockSpec((1, hb, w3), lambda i, j: (i, jnp.minimum((j + 1) * r, nt * r - 1), 0)),
                  pl.BlockSpec((3, w3), lambda i, j: (0, 0)),
                  pl.BlockSpec((1, w3), lambda i, j: (0, 0))],
        out_specs=[pl.BlockSpec((1, TM, w), lambda i, j: (i, j, 0)),
                   pl.BlockSpec((1, TM, w), lambda i, j: (i, j, 0))],
        out_shape=[jax.ShapeDtypeStruct((b, t, w), BF16), jax.ShapeDtypeStruct((b, t, w), BF16)],
        compiler_params=_cparams(2), name="hy_pre",
    )(qkv, qkv, qkv, conv_w, conv_b.reshape(1, w3))


def _hy_filter_call(length, w1, b1, w2, b2, w3, b3, w4, freq):
    order = w2.shape[0]
    w = w4.shape[1] // 2
    tl = min(length, 512)
    hi = lax.Precision.HIGHEST
    t = np.linspace(0.0, 1.0, length, dtype=np.float64)[:, None]
    bands = (HY_EMB_DIM - 1) // 2
    ang = 2.0 * math.pi * np.arange(length, dtype=np.float64)[:, None] / length
    fq = np.linspace(1e-4, bands - 1, bands, dtype=np.float64)[None, :]
    emb = np.concatenate([t, np.cos(fq * ang), -np.sin(fq * ang)], axis=-1).astype(np.float32)
    emb = np.pad(emb, ((0, 0), (0, LANES - HY_EMB_DIM)))
    max_decay = math.log(HY_DECAY_TARGET) / HY_FAST_DECAY
    min_decay = math.log(HY_DECAY_TARGET) / HY_SLOW_DECAY
    deltas = np.linspace(min_decay, max_decay, w, dtype=np.float64)
    decay = np.exp(-t * np.abs(deltas)[None, :]).astype(np.float32)
    decay2 = np.concatenate([decay, decay], axis=1)
    w1p = jnp.pad(w1, ((0, LANES - HY_EMB_DIM), (0, 0)))

    def body(e_ref, d_ref, w1_ref, b1_ref, w2_ref, b2_ref, w3_ref, b3_ref, w4_ref, f_ref,
             h_ref, s_ref):
        i = pl.program_id(0)
        fr = f_ref[...]
        hdn = jnp.sin(fr * (jnp.dot(e_ref[...], w1_ref[...], precision=hi,
                                    preferred_element_type=F32) + b1_ref[...]))
        hdn = jnp.sin(fr * (jnp.dot(hdn, w2_ref[...], precision=hi,
                                    preferred_element_type=F32) + b2_ref[...]))
        hdn = jnp.sin(fr * (jnp.dot(hdn, w3_ref[...], precision=hi,
                                    preferred_element_type=F32) + b3_ref[...]))
        taps = jnp.dot(hdn, w4_ref[...], precision=hi, preferred_element_type=F32) * d_ref[...]
        row = lax.broadcasted_iota(jnp.int32, taps.shape, 0) + i * tl
        col = lax.broadcasted_iota(jnp.int32, taps.shape, 1)
        taps = jnp.where((row == 0) & (col >= w), 0.0, taps)
        h_ref[...] = taps

        @pl.when(i == 0)
        def _():
            s_ref[...] = jnp.zeros(s_ref.shape, F32)
        s_ref[...] += jnp.sum(jnp.abs(taps), axis=0, keepdims=True)

    full = lambda a: pl.BlockSpec(a.shape, lambda i: (0,) * a.ndim)
    ops = (w1p, b1.reshape(1, order), w2, b2.reshape(1, order), w3, b3.reshape(1, order), w4,
           freq.reshape(1, order))
    return pl.pallas_call(
        body, grid=(length // tl,),
        in_specs=[pl.BlockSpec((tl, LANES), lambda i: (i, 0)),
                  pl.BlockSpec((tl, 2 * w), lambda i: (i, 0))] + [full(a) for a in ops],
        out_specs=[pl.BlockSpec((tl, 2 * w), lambda i: (i, 0)),
                   pl.BlockSpec((1, 2 * w), lambda i: (0, 0))],
        out_shape=[jax.ShapeDtypeStruct((length, 2 * w), F32), jax.ShapeDtypeStruct((1, 2 * w), F32)],
        compiler_params=_cparams(1), name="hy_filter",
    )(jnp.asarray(emb), jnp.asarray(decay2), *ops)


def _dft_tables(l1):
    n1 = 2 * l1
    n = n1 * LANES
    f1 = np.arange(n1, dtype=np.float64)
    t1 = np.arange(l1, dtype=np.float64)
    th1 = 2.0 * np.pi * np.outer(f1, t1) / n1
    fwd1 = np.concatenate([np.cos(th1), -np.sin(th1)], axis=0)
    inv1 = np.concatenate([np.cos(th1).T, -np.sin(th1).T], axis=1) / n
    f2 = np.arange(LANES, dtype=np.float64)
    t2 = np.arange(LANES, dtype=np.float64)
    fr = f1[:, None, None] + n1 * f2[None, :, None]
    th2 = 2.0 * np.pi * fr * t2[None, None, :] / n
    gr, gi = np.cos(th2), -np.sin(th2)
    gb = np.concatenate([np.concatenate([gr, -gi], axis=2),
                         np.concatenate([gi, gr], axis=2)], axis=1)
    hb = np.transpose(gb, (0, 2, 1))
    as_bf = lambda a: jnp.asarray(a.astype(np.float32)).astype(BF16)
    return as_bf(fwd1), as_bf(inv1), as_bf(gb), as_bf(hb)


def _hy_stage1_call(xv, fwd1):
    bx, l1, cols = xv.shape
    n2 = fwd1.shape[0]
    tn = min(cols, 4096)

    def body(f_ref, x_ref, o_ref):
        o_ref[0] = jnp.dot(f_ref[...], x_ref[0], preferred_element_type=F32).astype(BF16)

    return pl.pallas_call(
        body, grid=(bx, cols // tn),
        in_specs=[pl.BlockSpec((n2, l1), lambda i, j: (0, 0)),
                  pl.BlockSpec((1, l1, tn), lambda i, j: (i, 0, j))],
        out_specs=pl.BlockSpec((1, n2, tn), lambda i, j: (i, 0, j)),
        out_shape=jax.ShapeDtypeStruct((bx, n2, cols), BF16),
        compiler_params=_cparams(2), name="hy_dft1",
    )(fwd1, xv)


def _hy_filter_spec_call(a5, gb, sums, fb):
    n1, w2 = a5.shape[2], a5.shape[4]
    w = w2 // 2

    def body(a_ref, g_ref, s_ref, o_ref):
        sv = s_ref[...]
        inv = 1.0 / (sv[:, :w] + sv[:, w:])
        for k in range(fb):
            a = jnp.concatenate([a_ref[0, 0, k], a_ref[0, 1, k]], axis=0)
            z = jnp.dot(g_ref[k], a, preferred_element_type=F32)
            o_ref[k, 0] = (z[:LANES, :w] + z[:LANES, w:]) * inv
            o_ref[k, 1] = (z[LANES:, :w] - z[LANES:, w:]) * inv

    return pl.pallas_call(
        body, grid=(n1 // fb,),
        in_specs=[pl.BlockSpec((1, 2, fb, LANES, w2), lambda i: (0, 0, i, 0, 0)),
                  pl.BlockSpec((fb, 2 * LANES, 2 * LANES), lambda i: (i, 0, 0)),
                  pl.BlockSpec((1, w2), lambda i: (0, 0))],
        out_specs=pl.BlockSpec((fb, 2, LANES, w), lambda i: (i, 0, 0, 0)),
        out_shape=jax.ShapeDtypeStruct((n1, 2, LANES, w), F32),
        compiler_params=_cparams(1), name="hy_fspec",
    )(a5, gb, sums)


def _hy_stage23_call(a5, gb, hb, kf, fb):
    b, _, n1, _, w = a5.shape

    def body(a_ref, g_ref, h_ref, k_ref, o_ref):
        for k in range(fb):
            a = jnp.concatenate([a_ref[0, 0, k], a_ref[0, 1, k]], axis=0)
            z = jnp.dot(g_ref[k], a, preferred_element_type=F32)
            zr, zi = z[:LANES], z[LANES:]
            kr, ki = k_ref[k, 0], k_ref[k, 1]
            y = jnp.concatenate([zr * kr - zi * ki, zr * ki + zi * kr], axis=0).astype(BF16)
            cc = jnp.dot(h_ref[k], y, preferred_element_type=F32)
            o_ref[0, 0, k] = cc[:LANES].astype(BF16)
            o_ref[0, 1, k] = cc[LANES:].astype(BF16)

    return pl.pallas_call(
        body, grid=(n1 // fb, b),
        in_specs=[pl.BlockSpec((1, 2, fb, LANES, w), lambda i, j: (j, 0, i, 0, 0)),
                  pl.BlockSpec((fb, 2 * LANES, 2 * LANES), lambda i, j: (i, 0, 0)),
                  pl.BlockSpec((fb, 2 * LANES, 2 * LANES), lambda i, j: (i, 0, 0)),
                  pl.BlockSpec((fb, 2, LANES, w), lambda i, j: (i, 0, 0, 0))],
        out_specs=pl.BlockSpec((1, 2, fb, LANES, w), lambda i, j: (j, 0, i, 0, 0)),
        out_shape=jax.ShapeDtypeStruct(a5.shape, BF16),
        compiler_params=_cparams(2), name="hy_dft23",
    )(a5, gb, hb, kf)


def _hy_stage4_call(cv, inv1, x0v, zv, skip_t):
    b, n2, cols = cv.shape
    l1 = inv1.shape[0]
    tn = skip_t.shape[1]

    def body(f_ref, c_ref, x0_ref, z_ref, s_ref, o_ref):
        y = jnp.dot(f_ref[...], c_ref[0], preferred_element_type=F32)
        z = z_ref[0].astype(F32)
        o_ref[0] = (x0_ref[0].astype(F32) * (y + s_ref[...] * z)).astype(BF16)

    return pl.pallas_call(
        body, grid=(b, cols // tn),
        in_specs=[pl.BlockSpec((l1, n2), lambda i, j: (0, 0)),
                  pl.BlockSpec((1, n2, tn), lambda i, j: (i, 0, j)),
                  pl.BlockSpec((1, l1, tn), lambda i, j: (i, 0, j)),
                  pl.BlockSpec((1, l1, tn), lambda i, j: (i, 0, j)),
                  pl.BlockSpec((1, tn), lambda i, j: (0, 0))],
        out_specs=pl.BlockSpec((1, l1, tn), lambda i, j: (i, 0, j)),
        out_shape=jax.ShapeDtypeStruct((b, l1, cols), BF16),
        compiler_params=_cparams(2), name="hy_dft4",
    )(inv1, cv, x0v, zv, skip_t)


def _hy_dense_call(x0, z, taps, sums, skip):
    b, c, w = z.shape
    n = 2 * c
    th = 2.0 * np.pi * np.outer(np.arange(n, dtype=np.float64), np.arange(c, dtype=np.float64)) / n
    fwd = jnp.asarray(np.concatenate([np.cos(th), -np.sin(th)], axis=0).astype(np.float32)).astype(BF16)
    inv = jnp.asarray((np.concatenate([np.cos(th).T, -np.sin(th).T], axis=1) / n)
                      .astype(np.float32)).astype(BF16)

    def body(f_ref, i_ref, x0_ref, z_ref, t_ref, s_ref, k_ref, o_ref):
        sv = s_ref[...]
        nrm = 1.0 / (sv[:, :w] + sv[:, w:])
        tf = jnp.dot(f_ref[...], t_ref[...].astype(BF16), preferred_element_type=F32)
        kr = (tf[:n, :w] + tf[:n, w:]) * nrm
        ki = (tf[n:, :w] - tf[n:, w:]) * nrm
        zf = jnp.dot(f_ref[...], z_ref[0], preferred_element_type=F32)
        zr, zi = zf[:n], zf[n:]
        y = jnp.concatenate([zr * kr - zi * ki, zr * ki + zi * kr], axis=0).astype(BF16)
        yt = jnp.dot(i_ref[...], y, preferred_element_type=F32)
        o_ref[0] = (x0_ref[0].astype(F32) * (yt + k_ref[...] * z_ref[0].astype(F32))).astype(BF16)

    return pl.pallas_call(
        body, grid=(b,),
        in_specs=[pl.BlockSpec((2 * n, c), lambda i: (0, 0)),
                  pl.BlockSpec((c, 2 * n), lambda i: (0, 0)),
                  pl.BlockSpec((1, c, w), lambda i: (i, 0, 0)),
                  pl.BlockSpec((1, c, w), lambda i: (i, 0, 0)),
                  pl.BlockSpec((c, 2 * w), lambda i: (0, 0)),
                  pl.BlockSpec((1, 2 * w), lambda i: (0, 0)),
                  pl.BlockSpec((1, w), lambda i: (0, 0))],
        out_specs=pl.BlockSpec((1, c, w), lambda i: (i, 0, 0)),
        out_shape=jax.ShapeDtypeStruct((b, c, w), BF16),
        compiler_params=_cparams(1), name="hy_dense",
    )(fwd, inv, x0, z, taps, sums, skip.reshape(1, w))


def _hyena_long(x0, z, fparams, skip):
    b, l, w = z.shape
    l1 = l // LANES
    n1 = 2 * l1
    fb = min(8, n1)
    fwd1, inv1, gb, hb = _dft_tables(l1)
    taps, sums = _hy_filter_call(l, *fparams)
    ta = _hy_stage1_call(taps.astype(BF16).reshape(1, l1, LANES * 2 * w), fwd1)
    kf = _hy_filter_spec_call(ta.reshape(1, 2, n1, LANES, 2 * w), gb, sums, fb)
    za = _hy_stage1_call(z.reshape(b, l1, LANES * w), fwd1)
    cc = _hy_stage23_call(za.reshape(b, 2, n1, LANES, w), gb, hb, kf, fb)
    tn = 8 * w
    skip_t = jnp.tile(skip.reshape(1, w), (1, tn // w))
    y = _hy_stage4_call(cc.reshape(b, 2 * n1, LANES * w), inv1, x0.reshape(b, l1, LANES * w),
                        z.reshape(b, l1, LANES * w), skip_t)
    return y.reshape(b, l, w)


def _hyena_short(x0, z, fparams, skip):
    taps, sums = _hy_filter_call(z.shape[1], *fparams)
    return _hy_dense_call(x0, z, taps, sums, skip)


def _rope_tables(l, c):
    t = np.arange(l)
    row = (t // GRID_W).astype(np.float64)
    col = (t % GRID_W).astype(np.float64)
    n_pairs = HEAD_DIM // 4
    inv_freq = ROPE_THETA ** (-np.arange(n_pairs, dtype=np.float64) / n_pairs)
    ang = np.concatenate([row[:, None] * inv_freq, col[:, None] * inv_freq], axis=-1)
    cos = np.repeat(np.cos(ang), 2, axis=1)
    sin = np.repeat(np.sin(ang), 2, axis=1)
    even = (np.arange(HEAD_DIM) % 2 == 0)[None, :]
    se = np.where(even, -sin, 0.0)
    so = np.where(even, 0.0, sin)
    pad = lambda a, v: np.concatenate([a, np.full((c, HEAD_DIM), v)], axis=0)
    two = lambda a: jnp.asarray(np.concatenate([a, a], axis=1).astype(np.float32))
    return two(pad(cos, 1.0)), two(pad(se, 0.0)), two(pad(so, 0.0))


def _block_diag_ones():
    i = np.arange(2 * LANES)
    return jnp.asarray((i[:, None] // HEAD_DIM == i[None, :] // HEAD_DIM).astype(np.float32)).astype(BF16)


def _gain_rows(gains, scales):
    rows = [jnp.tile(g.astype(F32) * s, 2 * LANES // HEAD_DIM) for g, s in zip(gains, scales)]
    rows += [jnp.zeros((2 * LANES,), F32)] * (8 - len(rows))
    return jnp.stack(rows, axis=0)


def kernel(x, c, ctx, c_ctx, w_ada, b_ada, w_up, ffn_conv_w, ffn_conv_b, w_down, w_in_e, w_out_e, na_q_gain, na_k_gain, na_rpb, da_q_gain, da_k_gain, da_lambda_q1, da_lambda_k1, da_lambda_q2, da_lambda_k2, da_subln_gain, w_in_o, w_out_o, gqa_q_gain, gqa_k_gain, hy_conv_w, hy_conv_b, hy_w1, hy_b1, hy_w2, hy_b2, hy_w3, hy_b3, hy_w4, hy_freq, hy_skip):
    b, l, d = x.shape
    cl = ctx.shape[1]
    t = l + cl
    depth = w_ada.shape[0]
    f = w_down.shape[1]
    assert cl == TM and l % TK == 0 and (l // GRID_W) >= NA_WROWS + NA_QROWS
    assert w_in_e.shape[2] == 3072 and w_in_o.shape[2] == 2304 and d % LANES == 0
    scale = HEAD_DIM ** -0.5 * LOG2E

    rows = -(-(b + 1) // 8) * 8
    cs = jnp.zeros((rows, d), F32).at[:b].set(c).at[b].set(c_ctx)
    mods = _ada_call(cs, w_ada, b_ada)
    modarrs = []
    for layer in range(depth):
        lat = mods[layer, :b].reshape(b, 1, 6, d)
        cx = jnp.broadcast_to(mods[layer, b].reshape(1, 1, 6, d), (b, 1, 6, d))
        modarrs.append(jnp.concatenate([lat, cx], axis=1))

    ropes = _rope_tables(l, cl)
    bd = _block_diag_ones()
    x_all, h = _mod0_call(x, ctx, modarrs[0])

    for layer in range(depth):
        i = layer // 2
        if layer % 2 == 0:
            lam_init = 0.8 - 0.6 * math.exp(-0.3 * layer)
            gains = _gain_rows((na_q_gain[i], na_k_gain[i], da_q_gain[i], da_k_gain[i]),
                               (scale, 1.0, scale, 1.0))
            qkv = _inproj_call(_inproj_even_body, "inproj_even", h, w_in_e[i].astype(BF16), bd, gains,
                               ropes, 3072)
            vat = jnp.swapaxes(qkv[:, :, 1024:1536], 1, 2)
            vbt = jnp.swapaxes(qkv[:, :, 2560:3072], 1, 2)
            bias = _na_bias_table(na_rpb[i])
            ya = _na_call(qkv, vat, bias, l)
            lamv = jnp.stack([da_lambda_q1[i], da_lambda_k1[i], da_lambda_q2[i], da_lambda_k2[i]]).astype(F32)
            yb = _flash_call("da_attn", qkv, vbt, 4, 12, 16, lambda hh: hh, LANES,
                             _make_da_finish(lam_init),
                             (lamv, da_subln_gain[i].reshape(1, LANES).astype(F32)), 512, l)
            w_out = w_out_e[i].astype(BF16)
        else:
            gains = _gain_rows((gqa_q_gain[i], gqa_k_gain[i]), (scale, 1.0))
            qkv = _inproj_call(_inproj_odd_body, "inproj_odd", h, w_in_o[i].astype(BF16), bd, gains,
                               ropes, 2432)
            vt = jnp.swapaxes(qkv[:, :, 2304:2432], 1, 2)
            ya = _flash_call("gqa_attn", qkv, vt, 4, 12, 16, lambda hh: hh // 2, HEAD_DIM,
                             _gqa_finish, (), 512, l)
            x0, z = _hy_pre_call(qkv, hy_conv_w[i], hy_conv_b[i], l)
            fparams = (hy_w1[i], hy_b1[i], hy_w2[i], hy_b2[i], hy_w3[i], hy_b3[i], hy_w4[i], hy_freq[i])
            yd_l = _hyena_long(x0[:, :l], z[:, :l], fparams, hy_skip[i])
            if layer < depth - 1:
                yd_c = _hyena_short(x0[:, l:], z[:, l:], fparams, hy_skip[i])
            else:
                yd_c = jnp.zeros((b, cl, x0.shape[2]), BF16)
            yb = jnp.concatenate([yd_l, yd_c], axis=1)
            w_out = w_out_o[i].astype(BF16)
        ka = ya.shape[2]
        x_all, h2 = _outproj_call(ya, yb, w_out[:ka], w_out[ka:], x_all, modarrs[layer])
        gv = _matmul_call(h2.reshape(b * t, d), w_up[layer].astype(BF16), f).reshape(b, t, 2 * f)
        x_all, h = _ffn_down_call(gv, ffn_conv_w[layer], ffn_conv_b[layer], w_down[layer].astype(BF16),
                                  x_all, modarrs[layer], modarrs[min(layer + 1, depth - 1)])
    return x_all[:, :l]
```

```python
import functools
import math

import numpy as np
import jax
import jax.numpy as jnp
from jax import lax
from jax.experimental import pallas as pl
from jax.experimental.pallas import tpu as pltpu

F32 = jnp.float32
BF16 = jnp.bfloat16

HEAD_DIM = 64
GRID_W = 64
ROPE_THETA = 10000.0
EPS = 1e-6
NA_WIN_R = 8
NA_WIN_C = 16
HY_EMB_DIM = 33
HY_FAST_DECAY = 0.3
HY_SLOW_DECAY = 1.5
HY_DECAY_TARGET = 1e-2

LANES = 128
TM = 256
TK = 512
NA_QROWS = TM // GRID_W
NA_WROWS = NA_QROWS + NA_WIN_R
ONES_ROWS = 16
NEG = -1e30
LOG2E = 1.4426950408889634
VMEM_LIMIT = 56 * 1024 * 1024


def _cparams(n_axes):
    return pltpu.CompilerParams(dimension_semantics=("arbitrary",) * n_axes,
                                vmem_limit_bytes=VMEM_LIMIT)


def _modulate(x, sh, sc):
    ms = jnp.mean(x * x, axis=-1, keepdims=True)
    return x * lax.rsqrt(ms + EPS) * (1.0 + sc) + sh


def _seg_norm(y, bd, gain):
    ss = jnp.dot((y * y).astype(BF16), bd, preferred_element_type=F32)
    return y * lax.rsqrt(ss * (1.0 / HEAD_DIM) + EPS) * gain


def _rope(y, c, se, so):
    return y * c + pltpu.roll(y, LANES - 1, 1) * se + pltpu.roll(y, 1, 1) * so


def _dot_nt(a, b):
    return lax.dot_general(a, b, (((1,), (1,)), ((), ())), preferred_element_type=F32)


def _shift_rows(g, prev_row, next_row):
    tm = g.shape[0]
    row = lax.broadcasted_iota(jnp.int32, g.shape, 0)
    dn = jnp.where(row == 0, prev_row, pltpu.roll(g, 1, 0))
    up = jnp.where(row == tm - 1, next_row, pltpu.roll(g, tm - 1, 0))
    return dn, up


def _ada_body(c_ref, w_ref, b_ref, o_ref):
    c = c_ref[...]
    a = (c / (1.0 + jnp.exp(-c))).astype(BF16)
    o_ref[0] = jnp.dot(a, w_ref[0].astype(BF16), preferred_element_type=F32) + b_ref[0]


def _ada_call(cs, w_ada, b_ada):
    depth, d, n6 = w_ada.shape
    rows = cs.shape[0]
    tn = 1536
    return pl.pallas_call(
        _ada_body, grid=(depth, n6 // tn),
        in_specs=[pl.BlockSpec((rows, d), lambda l, n: (0, 0)),
                  pl.BlockSpec((1, d, tn), lambda l, n: (l, 0, n)),
                  pl.BlockSpec((1, 1, tn), lambda l, n: (l, 0, n))],
        out_specs=pl.BlockSpec((1, rows, tn), lambda l, n: (l, 0, n)),
        out_shape=jax.ShapeDtypeStruct((depth, rows, n6), F32),
        compiler_params=_cparams(2), name="ada",
    )(cs, w_ada, b_ada.reshape(depth, 1, n6))


def _mod0_call(x, ctx, modarr):
    b, l, d = x.shape
    c = ctx.shape[1]
    t = l + c
    nl = l // TM

    def body(x_ref, c_ref, m_ref, xo_ref, h_ref):
        j = pl.program_id(1)
        xv = jnp.where(j < nl, x_ref[0], c_ref[0])
        xo_ref[0] = xv
        h_ref[0] = _modulate(xv, m_ref[0, 0, 0:1, :], m_ref[0, 0, 1:2, :]).astype(BF16)

    return pl.pallas_call(
        body, grid=(b, t // TM),
        in_specs=[pl.BlockSpec((1, TM, d), lambda i, j: (i, jnp.minimum(j, nl - 1), 0)),
                  pl.BlockSpec((1, TM, d), lambda i, j: (i, 0, 0)),
                  pl.BlockSpec((1, 1, 6, d), lambda i, j: (i, j // nl, 0, 0))],
        out_specs=[pl.BlockSpec((1, TM, d), lambda i, j: (i, j, 0)),
                   pl.BlockSpec((1, TM, d), lambda i, j: (i, j, 0))],
        out_shape=[jax.ShapeDtypeStruct((b, t, d), F32), jax.ShapeDtypeStruct((b, t, d), BF16)],
        compiler_params=_cparams(2), name="mod0",
    )(x, ctx, modarr)


def _inproj_even_body(a_ref, w_ref, bd_ref, g_ref, rc_ref, rse_ref, rso_ref, o_ref):
    a = a_ref[0]
    bd = bd_ref[...]
    rc, rse, rso = rc_ref[...], rse_ref[...], rso_ref[...]
    plan = ((0, False), (1, False), (None, False), (2, True), (3, True), (None, False))
    for seg, (gain_row, rope) in enumerate(plan):
        y = jnp.dot(a, w_ref[:, seg * 512:(seg + 1) * 512], preferred_element_type=F32)
        for half in range(2):
            yy = y[:, half * 256:(half + 1) * 256]
            if gain_row is not None:
                yy = _seg_norm(yy, bd, g_ref[gain_row:gain_row + 1, :])
            for blk in range(2):
                z = yy[:, blk * LANES:(blk + 1) * LANES]
                if rope:
                    z = _rope(z, rc, rse, rso)
                col = seg * 512 + half * 256 + blk * LANES
                o_ref[0, :, col:col + LANES] = z.astype(BF16)


def _inproj_odd_body(a_ref, w_ref, bd_ref, g_ref, rc_ref, rse_ref, rso_ref, o_ref):
    a = a_ref[0]
    bd = bd_ref[...]
    rc, rse, rso = rc_ref[...], rse_ref[...], rso_ref[...]
    y = jnp.dot(a, w_ref[:, 0:512], preferred_element_type=F32)
    for half in range(2):
        yy = _seg_norm(y[:, half * 256:(half + 1) * 256], bd, g_ref[0:1, :])
        for blk in range(2):
            z = _rope(yy[:, blk * LANES:(blk + 1) * LANES], rc, rse, rso)
            col = 1536 + half * 256 + blk * LANES
            o_ref[0, :, col:col + LANES] = z.astype(BF16)
    y = jnp.dot(a, w_ref[:, 512:768], preferred_element_type=F32)
    k = _seg_norm(y[:, :LANES], bd_ref[0:LANES, 0:LANES], g_ref[1:2, 0:LANES])
    k = _rope(k, rc, rse, rso)
    kr = pltpu.roll(k, HEAD_DIM, 1)
    lo = lax.broadcasted_iota(jnp.int32, k.shape, 1) < HEAD_DIM
    o_ref[0, :, 2048:2176] = jnp.where(lo, k, kr).astype(BF16)
    o_ref[0, :, 2176:2304] = jnp.where(lo, kr, k).astype(BF16)
    o_ref[0, :, 2304:2432] = y[:, LANES:].astype(BF16)
    for seg in range(3):
        y = jnp.dot(a, w_ref[:, 768 + seg * 512:768 + (seg + 1) * 512], preferred_element_type=F32)
        o_ref[0, :, seg * 512:(seg + 1) * 512] = y.astype(BF16)


def _inproj_call(body, name, h, w, bd, gains, ropes, n_out):
    b, t, d = h.shape
    n_in = w.shape[1]
    rc, rse, rso = ropes
    return pl.pallas_call(
        body, grid=(b, t // TM),
        in_specs=[pl.BlockSpec((1, TM, d), lambda i, j: (i, j, 0)),
                  pl.BlockSpec((d, n_in), lambda i, j: (0, 0)),
                  pl.BlockSpec(bd.shape, lambda i, j: (0, 0)),
                  pl.BlockSpec(gains.shape, lambda i, j: (0, 0)),
                  pl.BlockSpec((TM, LANES), lambda i, j: (j, 0)),
                  pl.BlockSpec((TM, LANES), lambda i, j: (j, 0)),
                  pl.BlockSpec((TM, LANES), lambda i, j: (j, 0))],
        out_specs=pl.BlockSpec((1, TM, n_out), lambda i, j: (i, j, 0)),
        out_shape=jax.ShapeDtypeStruct((b, t, n_out), BF16),
        compiler_params=_cparams(2), name=name,
    )(h, w, bd, gains, rc, rse, rso)


def _matmul_call(a, w, tn):
    m, k = a.shape
    n = w.shape[1]
    cw = 256

    def body(a_ref, w_ref, o_ref):
        av = a_ref[...]
        for c0 in range(0, tn, cw):
            o_ref[:, c0:c0 + cw] = jnp.dot(av, w_ref[:, c0:c0 + cw],
                                           preferred_element_type=F32).astype(BF16)

    return pl.pallas_call(
        body, grid=(n // tn, m // TM),
        in_specs=[pl.BlockSpec((TM, k), lambda j, i: (i, 0)),
                  pl.BlockSpec((k, tn), lambda j, i: (0, j))],
        out_specs=pl.BlockSpec((TM, tn), lambda j, i: (i, j)),
        out_shape=jax.ShapeDtypeStruct((m, n), BF16),
        compiler_params=_cparams(2), name="ffn_up",
    )(a, w)


def _outproj_call(ya, yb, wa, wb, x, modarr):
    b, t, d = x.shape
    nl = (t - TM) // TM
    ka, kb = ya.shape[2], yb.shape[2]

    def body(ya_ref, yb_ref, wa_ref, wb_ref, x_ref, m_ref, xo_ref, h_ref):
        y = (jnp.dot(ya_ref[0], wa_ref[...], preferred_element_type=F32)
             + jnp.dot(yb_ref[0], wb_ref[...], preferred_element_type=F32))
        x1 = x_ref[0] + m_ref[0, 0, 2:3, :] * y
        xo_ref[0] = x1
        h_ref[0] = _modulate(x1, m_ref[0, 0, 3:4, :], m_ref[0, 0, 4:5, :]).astype(BF16)

    return pl.pallas_call(
        body, grid=(b, t // TM),
        in_specs=[pl.BlockSpec((1, TM, ka), lambda i, j: (i, j, 0)),
                  pl.BlockSpec((1, TM, kb), lambda i, j: (i, j, 0)),
                  pl.BlockSpec((ka, d), lambda i, j: (0, 0)),
                  pl.BlockSpec((kb, d), lambda i, j: (0, 0)),
                  pl.BlockSpec((1, TM, d), lambda i, j: (i, j, 0)),
                  pl.BlockSpec((1, 1, 6, d), lambda i, j: (i, j // nl, 0, 0))],
        out_specs=[pl.BlockSpec((1, TM, d), lambda i, j: (i, j, 0)),
                   pl.BlockSpec((1, TM, d), lambda i, j: (i, j, 0))],
        out_shape=[jax.ShapeDtypeStruct((b, t, d), F32), jax.ShapeDtypeStruct((b, t, d), BF16)],
        compiler_params=_cparams(2), name="outproj",
    )(ya, yb, wa, wb, x, modarr)


def _ffn_down_call(gv, conv_w, conv_b, w_down, x, modarr, modarr_next):
    b, t, d = x.shape
    f = w_down.shape[0]
    nl = (t - TM) // TM
    nt = t // TM
    hb = 16
    r = TM // hb
    cw = 256

    def body(g_ref, v_ref, gp_ref, gn_ref, cw_ref, cb_ref, wd_ref, x_ref, m_ref, mn_ref,
             xo_ref, h_ref):
        j = pl.program_id(1)
        prev_ok = jnp.where((j == 0) | (j == nl), 0.0, 1.0)
        next_ok = jnp.where((j == nl - 1) | (j == nt - 1), 0.0, 1.0)
        acc = jnp.zeros((TM, d), F32)
        for c0 in range(0, f, cw):
            g = g_ref[0, :, c0:c0 + cw].astype(F32)
            gp = gp_ref[0, hb - 1:hb, c0:c0 + cw].astype(F32) * prev_ok
            gn = gn_ref[0, 0:1, c0:c0 + cw].astype(F32) * next_ok
            dn, up = _shift_rows(g, gp, gn)
            u = (dn * cw_ref[0:1, c0:c0 + cw] + g * cw_ref[1:2, c0:c0 + cw]
                 + up * cw_ref[2:3, c0:c0 + cw] + cb_ref[0:1, c0:c0 + cw])
            act = (u / (1.0 + jnp.exp(-u))) * v_ref[0, :, c0:c0 + cw].astype(F32)
            acc = acc + jnp.dot(act.astype(BF16), wd_ref[c0:c0 + cw, :], preferred_element_type=F32)
        x2 = x_ref[0] + m_ref[0, 0, 5:6, :] * acc
        xo_ref[0] = x2
        h_ref[0] = _modulate(x2, mn_ref[0, 0, 0:1, :], mn_ref[0, 0, 1:2, :]).astype(BF16)

    return pl.pallas_call(
        body, grid=(b, nt),
        in_specs=[pl.BlockSpec((1, TM, f), lambda i, j: (i, j, 0)),
                  pl.BlockSpec((1, TM, f), lambda i, j: (i, j, 1)),
                  pl.BlockSpec((1, hb, f), lambda i, j: (i, jnp.maximum(j * r - 1, 0), 0)),
                  pl.BlockSpec((1, hb, f), lambda i, j: (i, jnp.minimum((j + 1) * r, nt * r - 1), 0)),
                  pl.BlockSpec((3, f), lambda i, j: (0, 0)),
                  pl.BlockSpec((1, f), lambda i, j: (0, 0)),
                  pl.BlockSpec((f, d), lambda i, j: (0, 0)),
                  pl.BlockSpec((1, TM, d), lambda i, j: (i, j, 0)),
                  pl.BlockSpec((1, 1, 6, d), lambda i, j: (i, j // nl, 0, 0)),
                  pl.BlockSpec((1, 1, 6, d), lambda i, j: (i, j // nl, 0, 0))],
        out_specs=[pl.BlockSpec((1, TM, d), lambda i, j: (i, j, 0)),
                   pl.BlockSpec((1, TM, d), lambda i, j: (i, j, 0))],
        out_shape=[jax.ShapeDtypeStruct((b, t, d), F32), jax.ShapeDtypeStruct((b, t, d), BF16)],
        compiler_params=_cparams(2), name="ffn_down",
    )(gv, gv, gv, gv, conv_w, conv_b.reshape(1, f), w_down, x, modarr, modarr_next)


def _split_heads(q):
    lane = lax.broadcasted_iota(jnp.int32, q.shape, 1)
    zero = jnp.zeros_like(q)
    return jnp.where(lane < HEAD_DIM, q, zero), jnp.where(lane >= HEAD_DIM, q, zero)


def _flash_call(name, qkv, vt, n_blocks, qcol, kcol, kmap, vrows, finish, extra, out_cols, l):
    b, t, _ = qkv.shape
    c = t - l
    nl = l // TM
    n_pairs = l // (2 * TK)
    assert l % (2 * TK) == 0

    def body(q_ref, k_ref, vt_ref, *rest):
        extra_refs = rest[:-8]
        o_ref, s_ref, p_ref, sc_ref, pc_ref, acc_ref, m_ref, a_ref = rest[-8:]
        is_lat = pl.program_id(2) < nl
        qs = _split_heads(q_ref[0])

        def scores(off, size, dst):
            kc = k_ref[0, pl.ds(off, size), :]
            for s in range(2):
                dst(s)[...] = _dot_nt(kc, qs[s])

        def softmax(src, dst, slot):
            for s in range(2):
                st = src(s)[...]
                m = m_ref[s]
                mnew = jnp.maximum(m, jnp.max(st, axis=0, keepdims=True))
                a_ref[slot, s] = jnp.exp2(m - mnew)
                dst(s)[...] = jnp.exp2((st - mnew).astype(BF16))
                m_ref[s] = mnew

        def values(src, slot, off, size):
            vc = vt_ref[0, 0, :, pl.ds(off, size)]
            for s in range(2):
                acc_ref[s] = a_ref[slot, s] * acc_ref[s] + jnp.dot(vc, src(s)[...],
                                                                  preferred_element_type=F32)

        s_ctx, p_ctx = (lambda s: sc_ref.at[s]), (lambda s: pc_ref.at[s])
        s_buf = [(lambda s, i=i: s_ref.at[i, s]) for i in range(2)]
        p_buf = [(lambda s, i=i: p_ref.at[i, s]) for i in range(2)]

        def start():
            acc_ref[...] = jnp.zeros(acc_ref.shape, F32)
            m_ref[...] = jnp.full(m_ref.shape, NEG, F32)
            scores(l, c, s_ctx)

        def finalize():
            outs = [acc_ref[s, 0:vrows, :] / acc_ref[s, vrows:vrows + 1, :] for s in range(2)]
            o_ref[0] = finish(outs[0], outs[1], *extra_refs)

        @pl.when(is_lat)
        def _():
            start()
            scores(0, TK, s_buf[0])
            softmax(s_ctx, p_ctx, 2)
            values(p_ctx, 2, l, c)
            scores(TK, TK, s_buf[1])
            softmax(s_buf[0], p_buf[0], 0)

            for pi in range(n_pairs - 1):
                base = pi * (2 * TK)
                values(p_buf[0], 0, base, TK)
                scores(base + 2 * TK, TK, s_buf[0])
                softmax(s_buf[1], p_buf[1], 1)
                values(p_buf[1], 1, base + TK, TK)
                scores(base + 3 * TK, TK, s_buf[1])
                softmax(s_buf[0], p_buf[0], 0)
            base = (n_pairs - 1) * 2 * TK
            values(p_buf[0], 0, base, TK)
            softmax(s_buf[1], p_buf[1], 1)
            values(p_buf[1], 1, base + TK, TK)
            finalize()

        @pl.when(jnp.logical_not(is_lat))
        def _():
            start()
            softmax(s_ctx, p_ctx, 2)
            values(p_ctx, 2, l, c)
            finalize()

    extra_specs = [pl.BlockSpec(e.shape, lambda bi, h, i: (0,) * e.ndim) for e in extra]
    return pl.pallas_call(
        body, grid=(b, n_blocks, t // TM),
        in_specs=[pl.BlockSpec((1, TM, LANES), lambda bi, h, i: (bi, i, qcol + h)),
                  pl.BlockSpec((1, t, LANES), lambda bi, h, i: (bi, 0, kcol + kmap(h))),
                  pl.BlockSpec((1, 1, vrows + ONES_ROWS, t), lambda bi, h, i: (bi, kmap(h), 0, 0))]
        + extra_specs,
        out_specs=pl.BlockSpec((1, TM, LANES), lambda bi, h, i: (bi, i, h)),
        out_shape=jax.ShapeDtypeStruct((b, t, out_cols), BF16),
        scratch_shapes=[pltpu.VMEM((2, 2, TK, TM), F32), pltpu.VMEM((2, 2, TK, TM), BF16),
                        pltpu.VMEM((2, c, TM), F32), pltpu.VMEM((2, c, TM), BF16),
                        pltpu.VMEM((2, vrows + ONES_ROWS, TM), F32), pltpu.VMEM((2, 1, TM), F32),
                        pltpu.VMEM((3, 2, 1, TM), F32)],
        compiler_params=_cparams(3), name=name,
    )(qkv, qkv, vt, *extra)


def _gqa_finish(o_lo, o_hi):
    return jnp.concatenate([o_lo, o_hi], axis=0).T.astype(BF16)


def _make_da_finish(lam_init):
    def finish(o1, o2, lam_ref, subln_ref):
        lv = lam_ref[...]
        lam = (jnp.exp(jnp.sum(lv[0:1] * lv[1:2], keepdims=True))
               - jnp.exp(jnp.sum(lv[2:3] * lv[3:4], keepdims=True)) + lam_init)
        o = (o1 - lam * o2).T
        o = o * lax.rsqrt(jnp.mean(o * o, axis=-1, keepdims=True) + EPS)
        return (o * subln_ref[...] * (1.0 - lam_init)).astype(BF16)
    return finish


def _na_call(qkv, vt, bias, l):
    b, t, _ = qkv.shape
    c = t - l
    nl = l // TM
    rows = l // GRID_W
    win = NA_WROWS * GRID_W
    n_pairs = vt.shape[1] // 2

    def body(q_ref, k_ref, vt_ref, b_ref, o_ref):
        g = pl.program_id(2)
        qs = _split_heads(q_ref[0])
        kc = k_ref[0, l:l + c, :]

        def soft(parts):
            m = functools.reduce(jnp.maximum, [jnp.max(sc, axis=0, keepdims=True) for sc, _ in parts])
            acc = sum(jnp.dot(v, jnp.exp2((sc - m).astype(BF16)), preferred_element_type=F32)
                      for sc, v in parts)
            return acc[:HEAD_DIM] / acc[HEAD_DIM:HEAD_DIM + 1]

        @pl.when(g < nl)
        def _():
            w0 = pl.multiple_of(jnp.clip(g * NA_QROWS - NA_WIN_R // 2, 0, rows - NA_WROWS) * GRID_W, 256)
            kw = k_ref[0, pl.ds(w0, win), :]
            sws = [_dot_nt(kw, qs[s]) for s in range(2)]
            scs = [_dot_nt(kc, qs[s]) for s in range(2)]
            outs = [soft([(sws[s] + b_ref[0, s], vt_ref[0, s, :, pl.ds(w0, win)]),
                          (scs[s], vt_ref[0, s, :, l:l + c])]) for s in range(2)]
            o_ref[0] = jnp.concatenate(outs, axis=0).T.astype(BF16)

        @pl.when(g >= nl)
        def _():
            scs = [_dot_nt(kc, qs[s]) for s in range(2)]
            outs = [soft([(scs[s], vt_ref[0, s, :, l:l + c])]) for s in range(2)]
            o_ref[0] = jnp.concatenate(outs, axis=0).T.astype(BF16)

    def bias_idx(bi, h, g):
        case = jnp.where(g == 0, 0, jnp.where(g >= nl - 1, 2, 1))
        return (case * n_pairs + h, 0, 0, 0)

    return pl.pallas_call(
        body, grid=(b, n_pairs, t // TM),
        in_specs=[pl.BlockSpec((1, TM, LANES), lambda bi, h, g: (bi, g, h)),
                  pl.BlockSpec((1, t, LANES), lambda bi, h, g: (bi, 0, n_pairs + h)),
                  pl.BlockSpec((1, 2, HEAD_DIM + ONES_ROWS, t), lambda bi, h, g: (bi, h, 0, 0)),
                  pl.BlockSpec((1, 2, win, TM), bias_idx)],
        out_specs=pl.BlockSpec((1, TM, LANES), lambda bi, h, g: (bi, g, h)),
        out_shape=jax.ShapeDtypeStruct((b, t, n_pairs * LANES), BF16),
        compiler_params=_cparams(3), name="na_attn",
    )(qkv, qkv, vt, bias)


def _na_bias_table(rpb):
    h, n_ro, n_co = rpb.shape
    kr, qr = np.arange(NA_WROWS), np.arange(NA_QROWS)
    kc, qc = np.arange(GRID_W), np.arange(GRID_W)
    cs = np.clip(qc - NA_WIN_C // 2, 0, GRID_W - NA_WIN_C)
    col_ok = (kc[:, None] >= cs[None, :]) & (kc[:, None] < cs[None, :] + NA_WIN_C)
    co = kc[:, None] - qc[None, :] + NA_WIN_C - 1
    col_sel = (co[None] == np.arange(n_co)[:, None, None]).astype(np.float32)
    tables = []
    for d, rel in ((0, np.zeros_like(qr)), (NA_WIN_R // 2, qr), (NA_WIN_R, np.full_like(qr, NA_WIN_R // 2))):
        row_ok = (kr[:, None] >= rel[None, :]) & (kr[:, None] < rel[None, :] + NA_WIN_R)
        ro = kr[:, None] - qr[None, :] - d + NA_WIN_R - 1
        row_sel = (ro[:, :, None] == np.arange(n_ro)[None, None, :]).astype(np.float32)
        tb = jnp.einsum('kqa,hab,bcd->hkcqd', jnp.asarray(row_sel), rpb.astype(F32) * LOG2E,
                        jnp.asarray(col_sel), precision=lax.Precision.HIGHEST)
        ok = row_ok[:, None, :, None] & col_ok[None, :, None, :]
        tables.append(jnp.where(jnp.asarray(ok)[None], tb, NEG))
    tbl = jnp.stack(tables, axis=0)
    return tbl.reshape(3 * (h // 2), 2, NA_WROWS * GRID_W, TM).astype(F32)


def _hy_pre_call(qkv, conv_w, conv_b, l):
    b, t, _ = qkv.shape
    w3 = conv_w.shape[1]
    w = w3 // 3
    nl = l // TM
    nt = t // TM
    hb = 16
    r = TM // hb

    def body(u_ref, up_ref, un_ref, cw_ref, cb_ref, x0_ref, z_ref):
        j = pl.program_id(1)
        prev_ok = jnp.where((j == 0) | (j == nl), 0.0, 1.0)
        next_ok = jnp.where((j == nl - 1) | (j == nt - 1), 0.0, 1.0)
        parts = []
        for p in range(3):
            sl = slice(p * w, (p + 1) * w)
            g = u_ref[0, :, sl].astype(F32)
            gp = up_ref[0, hb - 1:hb, sl].astype(F32) * prev_ok
            gn = un_ref[0, 0:1, sl].astype(F32) * next_ok
            dn, up = _shift_rows(g, gp, gn)
            parts.append(dn * cw_ref[0:1, sl] + g * cw_ref[1:2, sl] + up * cw_ref[2:3, sl]
                         + cb_ref[0:1, sl])
        x0_ref[0] = parts[0].astype(BF16)
        z_ref[0] = (parts[2] * parts[1]).astype(BF16)

    return pl.pallas_call(
        body, grid=(b, nt),
        in_specs=[pl.BlockSpec((1, TM, w3), lambda i, j: (i, j, 0)),
                  pl.BlockSpec((1, hb, w3), lambda i, j: (i, jnp.maximum(j * r - 1, 0), 0)),
                  pl.BlockSpec((1, hb, w3), lambda i, j: (i, jnp.minimum((j + 1) * r, nt * r - 1), 0)),
                  pl.BlockSpec((3, w3), lambda i, j: (0, 0)),
                  pl.BlockSpec((1, w3), lambda i, j: (0, 0))],
        out_specs=[pl.BlockSpec((1, TM, w), lambda i, j: (i, j, 0)),
                   pl.BlockSpec((1, TM, w), lambda i, j: (i, j, 0))],
        out_shape=[jax.ShapeDtypeStruct((b, t, w), BF16), jax.ShapeDtypeStruct((b, t, w), BF16)],
        compiler_params=_cparams(2), name="hy_pre",
    )(qkv, qkv, qkv, conv_w, conv_b.reshape(1, w3))


def _hy_filter_call(length, w1, b1, w2, b2, w3, b3, w4, freq):
    order = w2.shape[0]
    w = w4.shape[1] // 2
    tl = min(length, 512)
    hi = lax.Precision.HIGHEST
    t = np.linspace(0.0, 1.0, length, dtype=np.float64)[:, None]
    bands = (HY_EMB_DIM - 1) // 2
    ang = 2.0 * math.pi * np.arange(length, dtype=np.float64)[:, None] / length
    fq = np.linspace(1e-4, bands - 1, bands, dtype=np.float64)[None, :]
    emb = np.concatenate([t, np.cos(fq * ang), -np.sin(fq * ang)], axis=-1).astype(np.float32)
    emb = np.pad(emb, ((0, 0), (0, LANES - HY_EMB_DIM)))
    max_decay = math.log(HY_DECAY_TARGET) / HY_FAST_DECAY
    min_decay = math.log(HY_DECAY_TARGET) / HY_SLOW_DECAY
    deltas = np.linspace(min_decay, max_decay, w, dtype=np.float64)
    decay = np.exp(-t * np.abs(deltas)[None, :]).astype(np.float32)
    decay2 = np.concatenate([decay, decay], axis=1)
    w1p = jnp.pad(w1, ((0, LANES - HY_EMB_DIM), (0, 0)))

    def body(e_ref, d_ref, w1_ref, b1_ref, w2_ref, b2_ref, w3_ref, b3_ref, w4_ref, f_ref,
             h_ref, s_ref):
        i = pl.program_id(0)
        fr = f_ref[...]
        hdn = jnp.sin(fr * (jnp.dot(e_ref[...], w1_ref[...], precision=hi,
                                    preferred_element_type=F32) + b1_ref[...]))
        hdn = jnp.sin(fr * (jnp.dot(hdn, w2_ref[...], precision=hi,
                                    preferred_element_type=F32) + b2_ref[...]))
        hdn = jnp.sin(fr * (jnp.dot(hdn, w3_ref[...], precision=hi,
                                    preferred_element_type=F32) + b3_ref[...]))
        taps = jnp.dot(hdn, w4_ref[...], precision=hi, preferred_element_type=F32) * d_ref[...]
        row = lax.broadcasted_iota(jnp.int32, taps.shape, 0) + i * tl
        col = lax.broadcasted_iota(jnp.int32, taps.shape, 1)
        taps = jnp.where((row == 0) & (col >= w), 0.0, taps)
        h_ref[...] = taps

        @pl.when(i == 0)
        def _():
            s_ref[...] = jnp.zeros(s_ref.shape, F32)
        s_ref[...] += jnp.sum(jnp.abs(taps), axis=0, keepdims=True)

    full = lambda a: pl.BlockSpec(a.shape, lambda i: (0,) * a.ndim)
    ops = (w1p, b1.reshape(1, order), w2, b2.reshape(1, order), w3, b3.reshape(1, order), w4,
           freq.reshape(1, order))
    return pl.pallas_call(
        body, grid=(length // tl,),
        in_specs=[pl.BlockSpec((tl, LANES), lambda i: (i, 0)),
                  pl.BlockSpec((tl, 2 * w), lambda i: (i, 0))] + [full(a) for a in ops],
        out_specs=[pl.BlockSpec((tl, 2 * w), lambda i: (i, 0)),
                   pl.BlockSpec((1, 2 * w), lambda i: (0, 0))],
        out_shape=[jax.ShapeDtypeStruct((length, 2 * w), F32), jax.ShapeDtypeStruct((1, 2 * w), F32)],
        compiler_params=_cparams(1), name="hy_filter",
    )(jnp.asarray(emb), jnp.asarray(decay2), *ops)


def _dft_tables(l1):
    n1 = 2 * l1
    n = n1 * LANES
    f1 = np.arange(n1, dtype=np.float64)
    t1 = np.arange(l1, dtype=np.float64)
    th1 = 2.0 * np.pi * np.outer(f1, t1) / n1
    fwd1 = np.concatenate([np.cos(th1), -np.sin(th1)], axis=0)
    inv1 = np.concatenate([np.cos(th1).T, -np.sin(th1).T], axis=1) / n
    f2 = np.arange(LANES, dtype=np.float64)
    t2 = np.arange(LANES, dtype=np.float64)
    fr = f1[:, None, None] + n1 * f2[None, :, None]
    th2 = 2.0 * np.pi * fr * t2[None, None, :] / n
    gr, gi = np.cos(th2), -np.sin(th2)
    gb = np.concatenate([np.concatenate([gr, -gi], axis=2),
                         np.concatenate([gi, gr], axis=2)], axis=1)
    hb = np.transpose(gb, (0, 2, 1))
    as_bf = lambda a: jnp.asarray(a.astype(np.float32)).astype(BF16)
    return as_bf(fwd1), as_bf(inv1), as_bf(gb), as_bf(hb)


def _hy_stage1_call(xv, fwd1):
    bx, l1, cols = xv.shape
    n2 = fwd1.shape[0]
    tn = min(cols, 4096)

    def body(f_ref, x_ref, o_ref):
        o_ref[0] = jnp.dot(f_ref[...], x_ref[0], preferred_element_type=F32).astype(BF16)

    return pl.pallas_call(
        body, grid=(bx, cols // tn),
        in_specs=[pl.BlockSpec((n2, l1), lambda i, j: (0, 0)),
                  pl.BlockSpec((1, l1, tn), lambda i, j: (i, 0, j))],
        out_specs=pl.BlockSpec((1, n2, tn), lambda i, j: (i, 0, j)),
        out_shape=jax.ShapeDtypeStruct((bx, n2, cols), BF16),
        compiler_params=_cparams(2), name="hy_dft1",
    )(fwd1, xv)


def _hy_filter_spec_call(a5, gb, sums, fb):
    n1, w2 = a5.shape[2], a5.shape[4]
    w = w2 // 2

    def body(a_ref, g_ref, s_ref, o_ref):
        sv = s_ref[...]
        inv = 1.0 / (sv[:, :w] + sv[:, w:])
        for k in range(fb):
            a = jnp.concatenate([a_ref[0, 0, k], a_ref[0, 1, k]], axis=0)
            z = jnp.dot(g_ref[k], a, preferred_element_type=F32)
            o_ref[k, 0] = (z[:LANES, :w] + z[:LANES, w:]) * inv
            o_ref[k, 1] = (z[LANES:, :w] - z[LANES:, w:]) * inv

    return pl.pallas_call(
        body, grid=(n1 // fb,),
        in_specs=[pl.BlockSpec((1, 2, fb, LANES, w2), lambda i: (0, 0, i, 0, 0)),
                  pl.BlockSpec((fb, 2 * LANES, 2 * LANES), lambda i: (i, 0, 0)),
                  pl.BlockSpec((1, w2), lambda i: (0, 0))],
        out_specs=pl.BlockSpec((fb, 2, LANES, w), lambda i: (i, 0, 0, 0)),
        out_shape=jax.ShapeDtypeStruct((n1, 2, LANES, w), F32),
        compiler_params=_cparams(1), name="hy_fspec",
    )(a5, gb, sums)


def _hy_stage23_call(a5, gb, hb, kf, fb):
    b, _, n1, _, w = a5.shape

    def body(a_ref, g_ref, h_ref, k_ref, o_ref):
        for k in range(fb):
            a = jnp.concatenate([a_ref[0, 0, k], a_ref[0, 1, k]], axis=0)
            z = jnp.dot(g_ref[k], a, preferred_element_type=F32)
            zr, zi = z[:LANES], z[LANES:]
            kr, ki = k_ref[k, 0], k_ref[k, 1]
            y = jnp.concatenate([zr * kr - zi * ki, zr * ki + zi * kr], axis=0).astype(BF16)
            cc = jnp.dot(h_ref[k], y, preferred_element_type=F32)
            o_ref[0, 0, k] = cc[:LANES].astype(BF16)
            o_ref[0, 1, k] = cc[LANES:].astype(BF16)

    return pl.pallas_call(
        body, grid=(n1 // fb, b),
        in_specs=[pl.BlockSpec((1, 2, fb, LANES, w), lambda i, j: (j, 0, i, 0, 0)),
                  pl.BlockSpec((fb, 2 * LANES, 2 * LANES), lambda i, j: (i, 0, 0)),
                  pl.BlockSpec((fb, 2 * LANES, 2 * LANES), lambda i, j: (i, 0, 0)),
                  pl.BlockSpec((fb, 2, LANES, w), lambda i, j: (i, 0, 0, 0))],
        out_specs=pl.BlockSpec((1, 2, fb, LANES, w), lambda i, j: (j, 0, i, 0, 0)),
        out_shape=jax.ShapeDtypeStruct(a5.shape, BF16),
        compiler_params=_cparams(2), name="hy_dft23",
    )(a5, gb, hb, kf)


def _hy_stage4_call(cv, inv1, x0v, zv, skip_t):
    b, n2, cols = cv.shape
    l1 = inv1.shape[0]
    tn = skip_t.shape[1]

    def body(f_ref, c_ref, x0_ref, z_ref, s_ref, o_ref):
        y = jnp.dot(f_ref[...], c_ref[0], preferred_element_type=F32)
        z = z_ref[0].astype(F32)
        o_ref[0] = (x0_ref[0].astype(F32) * (y + s_ref[...] * z)).astype(BF16)

    return pl.pallas_call(
        body, grid=(b, cols // tn),
        in_specs=[pl.BlockSpec((l1, n2), lambda i, j: (0, 0)),
                  pl.BlockSpec((1, n2, tn), lambda i, j: (i, 0, j)),
                  pl.BlockSpec((1, l1, tn), lambda i, j: (i, 0, j)),
                  pl.BlockSpec((1, l1, tn), lambda i, j: (i, 0, j)),
                  pl.BlockSpec((1, tn), lambda i, j: (0, 0))],
        out_specs=pl.BlockSpec((1, l1, tn), lambda i, j: (i, 0, j)),
        out_shape=jax.ShapeDtypeStruct((b, l1, cols), BF16),
        compiler_params=_cparams(2), name="hy_dft4",
    )(inv1, cv, x0v, zv, skip_t)


def _hy_dense_call(x0, z, taps, sums, skip):
    b, c, w = z.shape
    n = 2 * c
    th = 2.0 * np.pi * np.outer(np.arange(n, dtype=np.float64), np.arange(c, dtype=np.float64)) / n
    fwd = jnp.asarray(np.concatenate([np.cos(th), -np.sin(th)], axis=0).astype(np.float32)).astype(BF16)
    inv = jnp.asarray((np.concatenate([np.cos(th).T, -np.sin(th).T], axis=1) / n)
                      .astype(np.float32)).astype(BF16)

    def body(f_ref, i_ref, x0_ref, z_ref, t_ref, s_ref, k_ref, o_ref):
        sv = s_ref[...]
        nrm = 1.0 / (sv[:, :w] + sv[:, w:])
        tf = jnp.dot(f_ref[...], t_ref[...].astype(BF16), preferred_element_type=F32)
        kr = (tf[:n, :w] + tf[:n, w:]) * nrm
        ki = (tf[n:, :w] - tf[n:, w:]) * nrm
        zf = jnp.dot(f_ref[...], z_ref[0], preferred_element_type=F32)
        zr, zi = zf[:n], zf[n:]
        y = jnp.concatenate([zr * kr - zi * ki, zr * ki + zi * kr], axis=0).astype(BF16)
        yt = jnp.dot(i_ref[...], y, preferred_element_type=F32)
        o_ref[0] = (x0_ref[0].astype(F32) * (yt + k_ref[...] * z_ref[0].astype(F32))).astype(BF16)

    return pl.pallas_call(
        body, grid=(b,),
        in_specs=[pl.BlockSpec((2 * n, c), lambda i: (0, 0)),
                  pl.BlockSpec((c, 2 * n), lambda i: (0, 0)),
                  pl.BlockSpec((1, c, w), lambda i: (i, 0, 0)),
                  pl.BlockSpec((1, c, w), lambda i: (i, 0, 0)),
                  pl.BlockSpec((c, 2 * w), lambda i: (0, 0)),
                  pl.BlockSpec((1, 2 * w), lambda i: (0, 0)),
                  pl.BlockSpec((1, w), lambda i: (0, 0))],
        out_specs=pl.BlockSpec((1, c, w), lambda i: (i, 0, 0)),
        out_shape=jax.ShapeDtypeStruct((b, c, w), BF16),
        compiler_params=_cparams(1), name="hy_dense",
    )(fwd, inv, x0, z, taps, sums, skip.reshape(1, w))


def _hyena_long(x0, z, fparams, skip):
    b, l, w = z.shape
    l1 = l // LANES
    n1 = 2 * l1
    fb = min(8, n1)
    fwd1, inv1, gb, hb = _dft_tables(l1)
    taps, sums = _hy_filter_call(l, *fparams)
    ta = _hy_stage1_call(taps.astype(BF16).reshape(1, l1, LANES * 2 * w), fwd1)
    kf = _hy_filter_spec_call(ta.reshape(1, 2, n1, LANES, 2 * w), gb, sums, fb)
    za = _hy_stage1_call(z.reshape(b, l1, LANES * w), fwd1)
    cc = _hy_stage23_call(za.reshape(b, 2, n1, LANES, w), gb, hb, kf, fb)
    tn = 8 * w
    skip_t = jnp.tile(skip.reshape(1, w), (1, tn // w))
    y = _hy_stage4_call(cc.reshape(b, 2 * n1, LANES * w), inv1, x0.reshape(b, l1, LANES * w),
                        z.reshape(b, l1, LANES * w), skip_t)
    return y.reshape(b, l, w)


def _hyena_short(x0, z, fparams, skip):
    taps, sums = _hy_filter_call(z.shape[1], *fparams)
    return _hy_dense_call(x0, z, taps, sums, skip)


def _rope_tables(l, c):
    t = np.arange(l)
    row = (t // GRID_W).astype(np.float64)
    col = (t % GRID_W).astype(np.float64)
    n_pairs = HEAD_DIM // 4
    inv_freq = ROPE_THETA ** (-np.arange(n_pairs, dtype=np.float64) / n_pairs)
    ang = np.concatenate([row[:, None] * inv_freq, col[:, None] * inv_freq], axis=-1)
    cos = np.repeat(np.cos(ang), 2, axis=1)
    sin = np.repeat(np.sin(ang), 2, axis=1)
    even = (np.arange(HEAD_DIM) % 2 == 0)[None, :]
    se = np.where(even, -sin, 0.0)
    so = np.where(even, 0.0, sin)
    pad = lambda a, v: np.concatenate([a, np.full((c, HEAD_DIM), v)], axis=0)
    two = lambda a: jnp.asarray(np.concatenate([a, a], axis=1).astype(np.float32))
    return two(pad(cos, 1.0)), two(pad(se, 0.0)), two(pad(so, 0.0))


def _values_t(v, rows):
    b, t, w = v.shape
    vt = jnp.swapaxes(v, 1, 2).reshape(b, w // rows, rows, t)
    return jnp.concatenate([vt, jnp.ones((b, w // rows, ONES_ROWS, t), v.dtype)], axis=2)


def _block_diag_ones():
    i = np.arange(2 * LANES)
    return jnp.asarray((i[:, None] // HEAD_DIM == i[None, :] // HEAD_DIM).astype(np.float32)).astype(BF16)


def _gain_rows(gains, scales):
    rows = [jnp.tile(g.astype(F32) * s, 2 * LANES // HEAD_DIM) for g, s in zip(gains, scales)]
    rows += [jnp.zeros((2 * LANES,), F32)] * (8 - len(rows))
    return jnp.stack(rows, axis=0)


def kernel(x, c, ctx, c_ctx, w_ada, b_ada, w_up, ffn_conv_w, ffn_conv_b, w_down, w_in_e, w_out_e, na_q_gain, na_k_gain, na_rpb, da_q_gain, da_k_gain, da_lambda_q1, da_lambda_k1, da_lambda_q2, da_lambda_k2, da_subln_gain, w_in_o, w_out_o, gqa_q_gain, gqa_k_gain, hy_conv_w, hy_conv_b, hy_w1, hy_b1, hy_w2, hy_b2, hy_w3, hy_b3, hy_w4, hy_freq, hy_skip):
    b, l, d = x.shape
    cl = ctx.shape[1]
    t = l + cl
    depth = w_ada.shape[0]
    f = w_down.shape[1]
    assert cl == TM and l % TK == 0 and (l // GRID_W) >= NA_WROWS + NA_QROWS
    assert w_in_e.shape[2] == 3072 and w_in_o.shape[2] == 2304 and d % LANES == 0
    scale = HEAD_DIM ** -0.5 * LOG2E

    rows = -(-(b + 1) // 8) * 8
    cs = jnp.zeros((rows, d), F32).at[:b].set(c).at[b].set(c_ctx)
    mods = _ada_call(cs, w_ada, b_ada)
    modarrs = []
    for layer in range(depth):
        lat = mods[layer, :b].reshape(b, 1, 6, d)
        cx = jnp.broadcast_to(mods[layer, b].reshape(1, 1, 6, d), (b, 1, 6, d))
        modarrs.append(jnp.concatenate([lat, cx], axis=1))

    ropes = _rope_tables(l, cl)
    bd = _block_diag_ones()
    x_all, h = _mod0_call(x, ctx, modarrs[0])

    for layer in range(depth):
        i = layer // 2
        if layer % 2 == 0:
            lam_init = 0.8 - 0.6 * math.exp(-0.3 * layer)
            gains = _gain_rows((na_q_gain[i], na_k_gain[i], da_q_gain[i], da_k_gain[i]),
                               (scale, 1.0, scale, 1.0))
            qkv = _inproj_call(_inproj_even_body, "inproj_even", h, w_in_e[i].astype(BF16), bd, gains,
                               ropes, 3072)
            vat = _values_t(qkv[:, :, 1024:1536], HEAD_DIM)
            vbt = _values_t(qkv[:, :, 2560:3072], LANES)
            bias = _na_bias_table(na_rpb[i])
            ya = _na_call(qkv, vat, bias, l)
            lamv = jnp.stack([da_lambda_q1[i], da_lambda_k1[i], da_lambda_q2[i], da_lambda_k2[i]]).astype(F32)
            yb = _flash_call("da_attn", qkv, vbt, 4, 12, 16, lambda hh: hh, LANES,
                             _make_da_finish(lam_init),
                             (lamv, da_subln_gain[i].reshape(1, LANES).astype(F32)), 512, l)
            w_out = w_out_e[i].astype(BF16)
        else:
            gains = _gain_rows((gqa_q_gain[i], gqa_k_gain[i]), (scale, 1.0))
            qkv = _inproj_call(_inproj_odd_body, "inproj_odd", h, w_in_o[i].astype(BF16), bd, gains,
                               ropes, 2432)
            vt = _values_t(qkv[:, :, 2304:2432], HEAD_DIM)
            ya = _flash_call("gqa_attn", qkv, vt, 4, 12, 16, lambda hh: hh // 2, HEAD_DIM,
                             _gqa_finish, (), 512, l)
            x0, z = _hy_pre_call(qkv, hy_conv_w[i], hy_conv_b[i], l)
            fparams = (hy_w1[i], hy_b1[i], hy_w2[i], hy_b2[i], hy_w3[i], hy_b3[i], hy_w4[i], hy_freq[i])
            yd_l = _hyena_long(x0[:, :l], z[:, :l], fparams, hy_skip[i])
            if layer < depth - 1:
                yd_c = _hyena_short(x0[:, l:], z[:, l:], fparams, hy_skip[i])
            else:
                yd_c = jnp.zeros((b, cl, x0.shape[2]), BF16)
            yb = jnp.concatenate([yd_l, yd_c], axis=1)
            w_out = w_out_o[i].astype(BF16)
        ka = ya.shape[2]
        x_all, h2 = _outproj_call(ya, yb, w_out[:ka], w_out[ka:], x_all, modarrs[layer])
        gv = _matmul_call(h2.reshape(b * t, d), w_up[layer].astype(BF16), f).reshape(b, t, 2 * f)
        x_all, h = _ffn_down_call(gv, ffn_conv_w[layer], ffn_conv_b[layer], w_down[layer].astype(BF16),
                                  x_all, modarrs[layer], modarrs[min(layer + 1, depth - 1)])
    return x_all[:, :l]
```

```python
import functools
import math

import numpy as np
import jax
import jax.numpy as jnp
from jax import lax
from jax.experimental import pallas as pl
from jax.experimental.pallas import tpu as pltpu

F32 = jnp.float32
BF16 = jnp.bfloat16

HEAD_DIM = 64
GRID_W = 64
ROPE_THETA = 10000.0
EPS = 1e-6
NA_WIN_R = 8
NA_WIN_C = 16
HY_EMB_DIM = 33
HY_FAST_DECAY = 0.3
HY_SLOW_DECAY = 1.5
HY_DECAY_TARGET = 1e-2

LANES = 128
TM = 256
TK = 512
NA_QROWS = TM // GRID_W
NA_WROWS = NA_QROWS + NA_WIN_R
ONES_ROWS = 16
NEG = -1e30
LOG2E = 1.4426950408889634
VMEM_LIMIT = 56 * 1024 * 1024


def _cparams(n_axes):
    return pltpu.CompilerParams(dimension_semantics=("arbitrary",) * n_axes,
                                vmem_limit_bytes=VMEM_LIMIT)


def _modulate(x, sh, sc):
    ms = jnp.mean(x * x, axis=-1, keepdims=True)
    return x * lax.rsqrt(ms + EPS) * (1.0 + sc) + sh


def _seg_norm(y, bd, gain):
    ss = jnp.dot((y * y).astype(BF16), bd, preferred_element_type=F32)
    return y * lax.rsqrt(ss * (1.0 / HEAD_DIM) + EPS) * gain


def _rope(y, c, se, so):
    return y * c + pltpu.roll(y, LANES - 1, 1) * se + pltpu.roll(y, 1, 1) * so


def _dot_nt(a, b):
    return lax.dot_general(a, b, (((1,), (1,)), ((), ())), preferred_element_type=F32)


def _shift_rows(g, prev_row, next_row):
    tm = g.shape[0]
    row = lax.broadcasted_iota(jnp.int32, g.shape, 0)
    dn = jnp.where(row == 0, prev_row, pltpu.roll(g, 1, 0))
    up = jnp.where(row == tm - 1, next_row, pltpu.roll(g, tm - 1, 0))
    return dn, up


def _ada_body(c_ref, w_ref, b_ref, o_ref):
    c = c_ref[...]
    a = (c / (1.0 + jnp.exp(-c))).astype(BF16)
    o_ref[0] = jnp.dot(a, w_ref[0].astype(BF16), preferred_element_type=F32) + b_ref[0]


def _ada_call(cs, w_ada, b_ada):
    depth, d, n6 = w_ada.shape
    rows = cs.shape[0]
    tn = 1536
    return pl.pallas_call(
        _ada_body, grid=(depth, n6 // tn),
        in_specs=[pl.BlockSpec((rows, d), lambda l, n: (0, 0)),
                  pl.BlockSpec((1, d, tn), lambda l, n: (l, 0, n)),
                  pl.BlockSpec((1, 1, tn), lambda l, n: (l, 0, n))],
        out_specs=pl.BlockSpec((1, rows, tn), lambda l, n: (l, 0, n)),
        out_shape=jax.ShapeDtypeStruct((depth, rows, n6), F32),
        compiler_params=_cparams(2), name="ada",
    )(cs, w_ada, b_ada.reshape(depth, 1, n6))


def _mod0_call(x, ctx, modarr):
    b, l, d = x.shape
    c = ctx.shape[1]
    t = l + c
    nl = l // TM

    def body(x_ref, c_ref, m_ref, xo_ref, h_ref):
        j = pl.program_id(1)
        xv = jnp.where(j < nl, x_ref[0], c_ref[0])
        xo_ref[0] = xv
        h_ref[0] = _modulate(xv, m_ref[0, 0, 0:1, :], m_ref[0, 0, 1:2, :]).astype(BF16)

    return pl.pallas_call(
        body, grid=(b, t // TM),
        in_specs=[pl.BlockSpec((1, TM, d), lambda i, j: (i, jnp.minimum(j, nl - 1), 0)),
                  pl.BlockSpec((1, TM, d), lambda i, j: (i, 0, 0)),
                  pl.BlockSpec((1, 1, 6, d), lambda i, j: (i, j // nl, 0, 0))],
        out_specs=[pl.BlockSpec((1, TM, d), lambda i, j: (i, j, 0)),
                   pl.BlockSpec((1, TM, d), lambda i, j: (i, j, 0))],
        out_shape=[jax.ShapeDtypeStruct((b, t, d), F32), jax.ShapeDtypeStruct((b, t, d), BF16)],
        compiler_params=_cparams(2), name="mod0",
    )(x, ctx, modarr)


def _inproj_even_body(a_ref, w_ref, bd_ref, g_ref, rc_ref, rse_ref, rso_ref, o_ref):
    a = a_ref[0]
    bd = bd_ref[...]
    rc, rse, rso = rc_ref[...], rse_ref[...], rso_ref[...]
    plan = ((0, False), (1, False), (None, False), (2, True), (3, True), (None, False))
    for seg, (gain_row, rope) in enumerate(plan):
        y = jnp.dot(a, w_ref[:, seg * 512:(seg + 1) * 512], preferred_element_type=F32)
        for half in range(2):
            yy = y[:, half * 256:(half + 1) * 256]
            if gain_row is not None:
                yy = _seg_norm(yy, bd, g_ref[gain_row:gain_row + 1, :])
            for blk in range(2):
                z = yy[:, blk * LANES:(blk + 1) * LANES]
                if rope:
                    z = _rope(z, rc, rse, rso)
                col = seg * 512 + half * 256 + blk * LANES
                o_ref[0, :, col:col + LANES] = z.astype(BF16)


def _inproj_odd_body(a_ref, w_ref, bd_ref, g_ref, rc_ref, rse_ref, rso_ref, o_ref):
    a = a_ref[0]
    bd = bd_ref[...]
    rc, rse, rso = rc_ref[...], rse_ref[...], rso_ref[...]
    y = jnp.dot(a, w_ref[:, 0:512], preferred_element_type=F32)
    for half in range(2):
        yy = _seg_norm(y[:, half * 256:(half + 1) * 256], bd, g_ref[0:1, :])
        for blk in range(2):
            z = _rope(yy[:, blk * LANES:(blk + 1) * LANES], rc, rse, rso)
            col = 1536 + half * 256 + blk * LANES
            o_ref[0, :, col:col + LANES] = z.astype(BF16)
    y = jnp.dot(a, w_ref[:, 512:768], preferred_element_type=F32)
    k = _seg_norm(y[:, :LANES], bd_ref[0:LANES, 0:LANES], g_ref[1:2, 0:LANES])
    k = _rope(k, rc, rse, rso)
    kr = pltpu.roll(k, HEAD_DIM, 1)
    lo = lax.broadcasted_iota(jnp.int32, k.shape, 1) < HEAD_DIM
    o_ref[0, :, 2048:2176] = jnp.where(lo, k, kr).astype(BF16)
    o_ref[0, :, 2176:2304] = jnp.where(lo, kr, k).astype(BF16)
    o_ref[0, :, 2304:2432] = y[:, LANES:].astype(BF16)
    for seg in range(3):
        y = jnp.dot(a, w_ref[:, 768 + seg * 512:768 + (seg + 1) * 512], preferred_element_type=F32)
        o_ref[0, :, seg * 512:(seg + 1) * 512] = y.astype(BF16)


def _inproj_call(body, name, h, w, bd, gains, ropes, n_out):
    b, t, d = h.shape
    n_in = w.shape[1]
    rc, rse, rso = ropes
    return pl.pallas_call(
        body, grid=(b, t // TM),
        in_specs=[pl.BlockSpec((1, TM, d), lambda i, j: (i, j, 0)),
                  pl.BlockSpec((d, n_in), lambda i, j: (0, 0)),
                  pl.BlockSpec(bd.shape, lambda i, j: (0, 0)),
                  pl.BlockSpec(gains.shape, lambda i, j: (0, 0)),
                  pl.BlockSpec((TM, LANES), lambda i, j: (j, 0)),
                  pl.BlockSpec((TM, LANES), lambda i, j: (j, 0)),
                  pl.BlockSpec((TM, LANES), lambda i, j: (j, 0))],
        out_specs=pl.BlockSpec((1, TM, n_out), lambda i, j: (i, j, 0)),
        out_shape=jax.ShapeDtypeStruct((b, t, n_out), BF16),
        compiler_params=_cparams(2), name=name,
    )(h, w, bd, gains, rc, rse, rso)


def _ffn_up_call(h, w_up, conv_w, conv_b):
    b, t, d = h.shape
    f = w_up.shape[1] // 2
    nl = (t - TM) // TM
    nt = t // TM
    hb = 16
    r = TM // hb
    n_split = 1
    tn = f // n_split
    assert tn % LANES == 0
    chunks = [(c0, min(2 * LANES, tn - c0)) for c0 in range(0, tn, 2 * LANES)]

    def body(a_ref, ap_ref, an_ref, wg_ref, wv_ref, cw_ref, cb_ref, o_ref):
        j = pl.program_id(2)
        a = a_ref[0]
        zero = jnp.zeros((hb, d), BF16)
        ap = jnp.where((j == 0) | (j == nl), zero, ap_ref[0])
        an = jnp.where((j == nl - 1) | (j == nt - 1), zero, an_ref[0])
        a_ext = jnp.concatenate([ap, a, an], axis=0)
        for c0, cw in chunks:
            sl = slice(c0, c0 + cw)
            g = jnp.dot(a_ext, wg_ref[:, sl], preferred_element_type=F32)
            u = (g[hb - 1:hb - 1 + TM] * cw_ref[0:1, sl] + g[hb:hb + TM] * cw_ref[1:2, sl]
                 + g[hb + 1:hb + 1 + TM] * cw_ref[2:3, sl] + cb_ref[0:1, sl])
            v = jnp.dot(a, wv_ref[:, sl], preferred_element_type=F32)
            o_ref[0, :, sl] = ((u / (1.0 + jnp.exp(-u))) * v).astype(BF16)

    return pl.pallas_call(
        body, grid=(n_split, b, nt),
        in_specs=[pl.BlockSpec((1, TM, d), lambda n, i, j: (i, j, 0)),
                  pl.BlockSpec((1, hb, d), lambda n, i, j: (i, jnp.maximum(j * r - 1, 0), 0)),
                  pl.BlockSpec((1, hb, d), lambda n, i, j: (i, jnp.minimum((j + 1) * r, nt * r - 1), 0)),
                  pl.BlockSpec((d, tn), lambda n, i, j: (0, n)),
                  pl.BlockSpec((d, tn), lambda n, i, j: (0, n_split + n)),
                  pl.BlockSpec((3, tn), lambda n, i, j: (0, n)),
                  pl.BlockSpec((1, tn), lambda n, i, j: (0, n))],
        out_specs=pl.BlockSpec((1, TM, tn), lambda n, i, j: (i, j, n)),
        out_shape=jax.ShapeDtypeStruct((b, t, f), BF16),
        compiler_params=_cparams(3), name="ffn_up",
    )(h, h, h, w_up, w_up, conv_w, conv_b.reshape(1, f))


def _outproj_call(ya, yb, wa, wb, x, modarr):
    b, t, d = x.shape
    nl = (t - TM) // TM
    ka, kb = ya.shape[2], yb.shape[2]

    def body(ya_ref, yb_ref, wa_ref, wb_ref, x_ref, m_ref, xo_ref, h_ref):
        y = (jnp.dot(ya_ref[0], wa_ref[...], preferred_element_type=F32)
             + jnp.dot(yb_ref[0], wb_ref[...], preferred_element_type=F32))
        x1 = x_ref[0] + m_ref[0, 0, 2:3, :] * y
        xo_ref[0] = x1
        h_ref[0] = _modulate(x1, m_ref[0, 0, 3:4, :], m_ref[0, 0, 4:5, :]).astype(BF16)

    return pl.pallas_call(
        body, grid=(b, t // TM),
        in_specs=[pl.BlockSpec((1, TM, ka), lambda i, j: (i, j, 0)),
                  pl.BlockSpec((1, TM, kb), lambda i, j: (i, j, 0)),
                  pl.BlockSpec((ka, d), lambda i, j: (0, 0)),
                  pl.BlockSpec((kb, d), lambda i, j: (0, 0)),
                  pl.BlockSpec((1, TM, d), lambda i, j: (i, j, 0)),
                  pl.BlockSpec((1, 1, 6, d), lambda i, j: (i, j // nl, 0, 0))],
        out_specs=[pl.BlockSpec((1, TM, d), lambda i, j: (i, j, 0)),
                   pl.BlockSpec((1, TM, d), lambda i, j: (i, j, 0))],
        out_shape=[jax.ShapeDtypeStruct((b, t, d), F32), jax.ShapeDtypeStruct((b, t, d), BF16)],
        compiler_params=_cparams(2), name="outproj",
    )(ya, yb, wa, wb, x, modarr)


def _ffn_down_call(act, w_down, x, modarr, modarr_next):
    b, t, d = x.shape
    f = w_down.shape[0]
    nl = (t - TM) // TM
    nt = t // TM

    def body(a_ref, wd_ref, x_ref, m_ref, mn_ref, xo_ref, h_ref):
        y = jnp.dot(a_ref[0], wd_ref[...], preferred_element_type=F32)
        x2 = x_ref[0] + m_ref[0, 0, 5:6, :] * y
        xo_ref[0] = x2
        h_ref[0] = _modulate(x2, mn_ref[0, 0, 0:1, :], mn_ref[0, 0, 1:2, :]).astype(BF16)

    return pl.pallas_call(
        body, grid=(b, nt),
        in_specs=[pl.BlockSpec((1, TM, f), lambda i, j: (i, j, 0)),
                  pl.BlockSpec((f, d), lambda i, j: (0, 0)),
                  pl.BlockSpec((1, TM, d), lambda i, j: (i, j, 0)),
                  pl.BlockSpec((1, 1, 6, d), lambda i, j: (i, j // nl, 0, 0)),
                  pl.BlockSpec((1, 1, 6, d), lambda i, j: (i, j // nl, 0, 0))],
        out_specs=[pl.BlockSpec((1, TM, d), lambda i, j: (i, j, 0)),
                   pl.BlockSpec((1, TM, d), lambda i, j: (i, j, 0))],
        out_shape=[jax.ShapeDtypeStruct((b, t, d), F32), jax.ShapeDtypeStruct((b, t, d), BF16)],
        compiler_params=_cparams(2), name="ffn_down",
    )(act, w_down, x, modarr, modarr_next)


def _split_heads(q):
    lane = lax.broadcasted_iota(jnp.int32, q.shape, 1)
    zero = jnp.zeros_like(q)
    return jnp.where(lane < HEAD_DIM, q, zero), jnp.where(lane >= HEAD_DIM, q, zero)


def _flash_call(name, qt, qkv, vt, kcol, shared_kv, vrows, finish, extra, out_cols, l):
    b, t, _ = qkv.shape
    c = t - l
    nl = l // TM
    n_pairs = l // (2 * TK)
    n_steps = out_cols // (2 * LANES)
    n_kv = 1 if shared_kv else 2
    ns = 4
    assert l % (2 * TK) == 0

    def body(q_ref, k_ref, vt_ref, *rest):
        extra_refs = rest[:-8]
        o_ref, s_ref, p_ref, sc_ref, pc_ref, acc_ref, m_ref, a_ref = rest[-8:]
        is_lat = pl.program_id(2) < nl
        row = lax.broadcasted_iota(jnp.int32, (LANES, TM), 0)
        qs = []
        for blk in range(2):
            qb = q_ref[0, blk * LANES:(blk + 1) * LANES, :]
            qs += [jnp.where(row < HEAD_DIM, qb, jnp.zeros_like(qb)),
                   jnp.where(row >= HEAD_DIM, qb, jnp.zeros_like(qb))]
        kv_of = lambda s: 0 if shared_kv else s // 2

        def scores(off, size, dst):
            kcs = [k_ref[0, pl.ds(off, size), j * LANES:(j + 1) * LANES] for j in range(n_kv)]
            for s in range(ns):
                dst(s)[...] = jnp.dot(kcs[kv_of(s)], qs[s], preferred_element_type=F32)

        def softmax(src, dst, slot):
            for s in range(ns):
                st = src(s)[...]
                m = m_ref[s]
                mnew = jnp.maximum(m, jnp.max(st, axis=0, keepdims=True))
                a_ref[slot, s] = jnp.exp2(m - mnew)
                dst(s)[...] = jnp.exp2((st - mnew).astype(BF16))
                m_ref[s] = mnew

        def values(src, slot, off, size):
            vcs = [vt_ref[0, j, :, pl.ds(off, size)] for j in range(n_kv)]
            for s in range(ns):
                acc_ref[s] = a_ref[slot, s] * acc_ref[s] + jnp.dot(vcs[kv_of(s)], src(s)[...],
                                                                  preferred_element_type=F32)

        s_ctx, p_ctx = (lambda s: sc_ref.at[s]), (lambda s: pc_ref.at[s])
        s_buf = [(lambda s, i=i: s_ref.at[i, s]) for i in range(2)]
        p_buf = [(lambda s, i=i: p_ref.at[i, s]) for i in range(2)]

        def start():
            acc_ref[...] = jnp.zeros(acc_ref.shape, F32)
            m_ref[...] = jnp.full(m_ref.shape, NEG, F32)
            scores(l, c, s_ctx)

        def finalize():
            outs = [acc_ref[s, 0:vrows, :] / acc_ref[s, vrows:vrows + 1, :] for s in range(ns)]
            o_ref[0] = finish(outs, *extra_refs)

        @pl.when(is_lat)
        def _():
            start()
            scores(0, TK, s_buf[0])
            softmax(s_ctx, p_ctx, 2)
            values(p_ctx, 2, l, c)
            scores(TK, TK, s_buf[1])
            softmax(s_buf[0], p_buf[0], 0)

            for pi in range(n_pairs - 1):
                base = pi * (2 * TK)
                values(p_buf[0], 0, base, TK)
                scores(base + 2 * TK, TK, s_buf[0])
                softmax(s_buf[1], p_buf[1], 1)
                values(p_buf[1], 1, base + TK, TK)
                scores(base + 3 * TK, TK, s_buf[1])
                softmax(s_buf[0], p_buf[0], 0)
            base = (n_pairs - 1) * 2 * TK
            values(p_buf[0], 0, base, TK)
            softmax(s_buf[1], p_buf[1], 1)
            values(p_buf[1], 1, base + TK, TK)
            finalize()

        @pl.when(jnp.logical_not(is_lat))
        def _():
            start()
            softmax(s_ctx, p_ctx, 2)
            values(p_ctx, 2, l, c)
            finalize()

    kw = n_kv * LANES
    extra_specs = [pl.BlockSpec(e.shape, lambda bi, h, i: (0,) * e.ndim) for e in extra]
    return pl.pallas_call(
        body, grid=(b, n_steps, t // TM),
        in_specs=[pl.BlockSpec((1, 2 * LANES, TM), lambda bi, h, i: (bi, h, i)),
                  pl.BlockSpec((1, t, kw), lambda bi, h, i: (bi, 0, kcol + h)),
                  pl.BlockSpec((1, n_kv, vrows + ONES_ROWS, t), lambda bi, h, i: (bi, h, 0, 0))]
        + extra_specs,
        out_specs=pl.BlockSpec((1, TM, 2 * LANES), lambda bi, h, i: (bi, i, h)),
        out_shape=jax.ShapeDtypeStruct((b, t, out_cols), BF16),
        scratch_shapes=[pltpu.VMEM((2, ns, TK, TM), F32), pltpu.VMEM((2, ns, TK, TM), BF16),
                        pltpu.VMEM((ns, c, TM), F32), pltpu.VMEM((ns, c, TM), BF16),
                        pltpu.VMEM((ns, vrows + ONES_ROWS, TM), F32), pltpu.VMEM((ns, 1, TM), F32),
                        pltpu.VMEM((3, ns, 1, TM), F32)],
        compiler_params=_cparams(3), name=name,
    )(qt, qkv, vt, *extra)


def _gqa_finish(outs):
    return jnp.concatenate(outs, axis=0).T.astype(BF16)


def _make_da_finish(lam_init):
    def finish(outs, lam_ref, subln_ref):
        lv = lam_ref[...]
        lam = (jnp.exp(jnp.sum(lv[0:1] * lv[1:2], keepdims=True))
               - jnp.exp(jnp.sum(lv[2:3] * lv[3:4], keepdims=True)) + lam_init)
        heads = []
        for hd in range(len(outs) // 2):
            o = (outs[2 * hd] - lam * outs[2 * hd + 1]).T
            o = o * lax.rsqrt(jnp.mean(o * o, axis=-1, keepdims=True) + EPS)
            heads.append((o * subln_ref[...] * (1.0 - lam_init)).astype(BF16))
        return jnp.concatenate(heads, axis=1)
    return finish


def _na_call(qkv, vt, bias, l):
    b, t, _ = qkv.shape
    c = t - l
    nl = l // TM
    rows = l // GRID_W
    win = NA_WROWS * GRID_W
    n_pairs = vt.shape[1] // 2

    nt = t // TM

    def window(tile):
        return pl.multiple_of(jnp.clip(tile * NA_QROWS - NA_WIN_R // 2, 0, rows - NA_WROWS) * GRID_W, 256)

    def body(q0_ref, qn_ref, k_ref, vt_ref, b0_ref, bn_ref, o_ref, sw_ref, sc_ref):
        g = pl.program_id(2)
        kc = k_ref[0, l:l + c, :]

        def scores(q_ref, b_ref, tile, buf, with_win):
            qs = _split_heads(q_ref[0])
            if with_win:
                kw = k_ref[0, pl.ds(window(tile), win), :]
                for s in range(2):
                    sw_ref[buf, s] = _dot_nt(kw, qs[s]) + b_ref[0, s]
            for s in range(2):
                sc_ref[buf, s] = _dot_nt(kc, qs[s])

        def finish(tile, buf, with_win):
            outs = []
            for s in range(2):
                parts = [(sc_ref[buf, s], vt_ref[0, s, :, l:l + c])]
                if with_win:
                    parts.append((sw_ref[buf, s], vt_ref[0, s, :, pl.ds(window(tile), win)]))
                m = functools.reduce(jnp.maximum, [jnp.max(sc, axis=0, keepdims=True) for sc, _ in parts])
                acc = sum(jnp.dot(v, jnp.exp2((sc - m).astype(BF16)), preferred_element_type=F32)
                          for sc, v in parts)
                outs.append(acc[:HEAD_DIM] / acc[HEAD_DIM:HEAD_DIM + 1])
            o_ref[0] = jnp.concatenate(outs, axis=0).T.astype(BF16)

        cur, nxt = g % 2, (g + 1) % 2

        @pl.when(g == 0)
        def _():
            scores(q0_ref, b0_ref, 0, 0, True)

        @pl.when(g + 1 < nl)
        def _():
            scores(qn_ref, bn_ref, g + 1, nxt, True)
            finish(g, cur, True)

        @pl.when(g + 1 == nl)
        def _():
            scores(qn_ref, bn_ref, g + 1, nxt, False)
            finish(g, cur, True)

        @pl.when(g == nl)
        def _():
            finish(g, cur, False)

    def bias_next(bi, h, g):
        tile = g + 1
        case = jnp.where(tile >= nl - 1, 2, 1)
        return (case * n_pairs + h, 0, 0, 0)

    return pl.pallas_call(
        body, grid=(b, n_pairs, nt),
        in_specs=[pl.BlockSpec((1, TM, LANES), lambda bi, h, g: (bi, 0, h)),
                  pl.BlockSpec((1, TM, LANES), lambda bi, h, g: (bi, jnp.minimum(g + 1, nt - 1), h)),
                  pl.BlockSpec((1, t, LANES), lambda bi, h, g: (bi, 0, n_pairs + h)),
                  pl.BlockSpec((1, 2, HEAD_DIM + ONES_ROWS, t), lambda bi, h, g: (bi, h, 0, 0)),
                  pl.BlockSpec((1, 2, win, TM), lambda bi, h, g: (h, 0, 0, 0)),
                  pl.BlockSpec((1, 2, win, TM), bias_next)],
        out_specs=pl.BlockSpec((1, TM, LANES), lambda bi, h, g: (bi, g, h)),
        out_shape=jax.ShapeDtypeStruct((b, t, n_pairs * LANES), BF16),
        scratch_shapes=[pltpu.VMEM((2, 2, win, TM), F32), pltpu.VMEM((2, 2, c, TM), F32)],
        compiler_params=_cparams(3), name="na_attn",
    )(qkv, qkv, qkv, vt, bias, bias)


def _na_bias_table(rpb):
    h, n_ro, n_co = rpb.shape
    kr, qr = np.arange(NA_WROWS), np.arange(NA_QROWS)
    kc, qc = np.arange(GRID_W), np.arange(GRID_W)
    cs = np.clip(qc - NA_WIN_C // 2, 0, GRID_W - NA_WIN_C)
    col_ok = (kc[:, None] >= cs[None, :]) & (kc[:, None] < cs[None, :] + NA_WIN_C)
    co = kc[:, None] - qc[None, :] + NA_WIN_C - 1
    col_sel = (co[None] == np.arange(n_co)[:, None, None]).astype(np.float32)
    tables = []
    for d, rel in ((0, np.zeros_like(qr)), (NA_WIN_R // 2, qr), (NA_WIN_R, np.full_like(qr, NA_WIN_R // 2))):
        row_ok = (kr[:, None] >= rel[None, :]) & (kr[:, None] < rel[None, :] + NA_WIN_R)
        ro = kr[:, None] - qr[None, :] - d + NA_WIN_R - 1
        row_sel = (ro[:, :, None] == np.arange(n_ro)[None, None, :]).astype(np.float32)
        tb = jnp.einsum('kqa,hab,bcd->hkcqd', jnp.asarray(row_sel), rpb.astype(F32) * LOG2E,
                        jnp.asarray(col_sel), precision=lax.Precision.HIGHEST)
        ok = row_ok[:, None, :, None] & col_ok[None, :, None, :]
        tables.append(jnp.where(jnp.asarray(ok)[None], tb, NEG))
    tbl = jnp.stack(tables, axis=0)
    return tbl.reshape(3 * (h // 2), 2, NA_WROWS * GRID_W, TM).astype(F32)


def _hy_pre_call(qkv, conv_w, conv_b, l):
    b, t, _ = qkv.shape
    w3 = conv_w.shape[1]
    w = w3 // 3
    nl = l // TM
    nt = t // TM
    hb = 16
    r = TM // hb

    def body(u_ref, up_ref, un_ref, cw_ref, cb_ref, x0_ref, z_ref):
        j = pl.program_id(1)
        prev_ok = jnp.where((j == 0) | (j == nl), 0.0, 1.0)
        next_ok = jnp.where((j == nl - 1) | (j == nt - 1), 0.0, 1.0)
        parts = []
        for p in range(3):
            sl = slice(p * w, (p + 1) * w)
            g = u_ref[0, :, sl].astype(F32)
            gp = up_ref[0, hb - 1:hb, sl].astype(F32) * prev_ok
            gn = un_ref[0, 0:1, sl].astype(F32) * next_ok
            dn, up = _shift_rows(g, gp, gn)
            parts.append(dn * cw_ref[0:1, sl] + g * cw_ref[1:2, sl] + up * cw_ref[2:3, sl]
                         + cb_ref[0:1, sl])
        x0_ref[0] = parts[0].astype(BF16)
        z_ref[0] = (parts[2] * parts[1]).astype(BF16)

    return pl.pallas_call(
        body, grid=(b, nt),
        in_specs=[pl.BlockSpec((1, TM, w3), lambda i, j: (i, j, 0)),
                  pl.BlockSpec((1, hb, w3), lambda i, j: (i, jnp.maximum(j * r - 1, 0), 0)),
                  pl.BlockSpec((1, hb, w3), lambda i, j: (i, jnp.minimum((j + 1) * r, nt * r - 1), 0)),
                  pl.BlockSpec((3, w3), lambda i, j: (0, 0)),
                  pl.BlockSpec((1, w3), lambda i, j: (0, 0))],
        out_specs=[pl.BlockSpec((1, TM, w), lambda i, j: (i, j, 0)),
                   pl.BlockSpec((1, TM, w), lambda i, j: (i, j, 0))],
        out_shape=[jax.ShapeDtypeStruct((b, t, w), BF16), jax.ShapeDtypeStruct((b, t, w), BF16)],
        compiler_params=_cparams(2), name="hy_pre",
    )(qkv, qkv, qkv, conv_w, conv_b.reshape(1, w3))


def _hy_filter_call(length, w1, b1, w2, b2, w3, b3, w4, freq):
    order = w2.shape[0]
    w = w4.shape[1] // 2
    tl = min(length, 512)
    hi = lax.Precision.HIGHEST
    t = np.linspace(0.0, 1.0, length, dtype=np.float64)[:, None]
    bands = (HY_EMB_DIM - 1) // 2
    ang = 2.0 * math.pi * np.arange(length, dtype=np.float64)[:, None] / length
    fq = np.linspace(1e-4, bands - 1, bands, dtype=np.float64)[None, :]
    emb = np.concatenate([t, np.cos(fq * ang), -np.sin(fq * ang)], axis=-1).astype(np.float32)
    emb = np.pad(emb, ((0, 0), (0, LANES - HY_EMB_DIM)))
    max_decay = math.log(HY_DECAY_TARGET) / HY_FAST_DECAY
    min_decay = math.log(HY_DECAY_TARGET) / HY_SLOW_DECAY
    deltas = np.linspace(min_decay, max_decay, w, dtype=np.float64)
    decay = np.exp(-t * np.abs(deltas)[None, :]).astype(np.float32)
    decay2 = np.concatenate([decay, decay], axis=1)
    w1p = jnp.pad(w1, ((0, LANES - HY_EMB_DIM), (0, 0)))

    def body(e_ref, d_ref, w1_ref, b1_ref, w2_ref, b2_ref, w3_ref, b3_ref, w4_ref, f_ref,
             h_ref, s_ref):
        i = pl.program_id(0)
        fr = f_ref[...]
        hdn = jnp.sin(fr * (jnp.dot(e_ref[...], w1_ref[...], precision=hi,
                                    preferred_element_type=F32) + b1_ref[...]))
        hdn = jnp.sin(fr * (jnp.dot(hdn, w2_ref[...], precision=hi,
                                    preferred_element_type=F32) + b2_ref[...]))
        hdn = jnp.sin(fr * (jnp.dot(hdn, w3_ref[...], precision=hi,
                                    preferred_element_type=F32) + b3_ref[...]))
        taps = jnp.dot(hdn, w4_ref[...], precision=hi, preferred_element_type=F32) * d_ref[...]
        row = lax.broadcasted_iota(jnp.int32, taps.shape, 0) + i * tl
        col = lax.broadcasted_iota(jnp.int32, taps.shape, 1)
        taps = jnp.where((row == 0) & (col >= w), 0.0, taps)
        h_ref[...] = taps

        @pl.when(i == 0)
        def _():
            s_ref[...] = jnp.zeros(s_ref.shape, F32)
        s_ref[...] += jnp.sum(jnp.abs(taps), axis=0, keepdims=True)

    full = lambda a: pl.BlockSpec(a.shape, lambda i: (0,) * a.ndim)
    ops = (w1p, b1.reshape(1, order), w2, b2.reshape(1, order), w3, b3.reshape(1, order), w4,
           freq.reshape(1, order))
    return pl.pallas_call(
        body, grid=(length // tl,),
        in_specs=[pl.BlockSpec((tl, LANES), lambda i: (i, 0)),
                  pl.BlockSpec((tl, 2 * w), lambda i: (i, 0))] + [full(a) for a in ops],
        out_specs=[pl.BlockSpec((tl, 2 * w), lambda i: (i, 0)),
                   pl.BlockSpec((1, 2 * w), lambda i: (0, 0))],
        out_shape=[jax.ShapeDtypeStruct((length, 2 * w), F32), jax.ShapeDtypeStruct((1, 2 * w), F32)],
        compiler_params=_cparams(1), name="hy_filter",
    )(jnp.asarray(emb), jnp.asarray(decay2), *ops)


def _dft_tables(l1):
    n1 = 2 * l1
    n = n1 * LANES
    f1 = np.arange(n1, dtype=np.float64)
    t1 = np.arange(l1, dtype=np.float64)
    th1 = 2.0 * np.pi * np.outer(f1, t1) / n1
    fwd1 = np.concatenate([np.cos(th1), -np.sin(th1)], axis=0)
    inv1 = np.concatenate([np.cos(th1).T, -np.sin(th1).T], axis=1) / n
    f2 = np.arange(LANES, dtype=np.float64)
    t2 = np.arange(LANES, dtype=np.float64)
    fr = f1[:, None, None] + n1 * f2[None, :, None]
    th2 = 2.0 * np.pi * fr * t2[None, None, :] / n
    gr, gi = np.cos(th2), -np.sin(th2)
    gb = np.concatenate([np.concatenate([gr, -gi], axis=2),
                         np.concatenate([gi, gr], axis=2)], axis=1)
    hb = np.transpose(gb, (0, 2, 1))
    as_bf = lambda a: jnp.asarray(a.astype(np.float32)).astype(BF16)
    return as_bf(fwd1), as_bf(inv1), as_bf(gb), as_bf(hb)


def _hy_stage1_call(xv, fwd1):
    bx, l1, cols = xv.shape
    n2 = fwd1.shape[0]
    tn = min(cols, 4096)

    def body(f_ref, x_ref, o_ref):
        o_ref[0] = jnp.dot(f_ref[...], x_ref[0], preferred_element_type=F32).astype(BF16)

    return pl.pallas_call(
        body, grid=(bx, cols // tn),
        in_specs=[pl.BlockSpec((n2, l1), lambda i, j: (0, 0)),
                  pl.BlockSpec((1, l1, tn), lambda i, j: (i, 0, j))],
        out_specs=pl.BlockSpec((1, n2, tn), lambda i, j: (i, 0, j)),
        out_shape=jax.ShapeDtypeStruct((bx, n2, cols), BF16),
        compiler_params=_cparams(2), name="hy_dft1",
    )(fwd1, xv)


def _hy_filter_spec_call(a5, gb, sums, fb):
    n1, w2 = a5.shape[2], a5.shape[4]
    w = w2 // 2

    def body(a_ref, g_ref, s_ref, o_ref):
        sv = s_ref[...]
        inv = 1.0 / (sv[:, :w] + sv[:, w:])
        for k in range(fb):
            a = jnp.concatenate([a_ref[0, 0, k], a_ref[0, 1, k]], axis=0)
            z = jnp.dot(g_ref[k], a, preferred_element_type=F32)
            o_ref[k, 0] = (z[:LANES, :w] + z[:LANES, w:]) * inv
            o_ref[k, 1] = (z[LANES:, :w] - z[LANES:, w:]) * inv

    return pl.pallas_call(
        body, grid=(n1 // fb,),
        in_specs=[pl.BlockSpec((1, 2, fb, LANES, w2), lambda i: (0, 0, i, 0, 0)),
                  pl.BlockSpec((fb, 2 * LANES, 2 * LANES), lambda i: (i, 0, 0)),
                  pl.BlockSpec((1, w2), lambda i: (0, 0))],
        out_specs=pl.BlockSpec((fb, 2, LANES, w), lambda i: (i, 0, 0, 0)),
        out_shape=jax.ShapeDtypeStruct((n1, 2, LANES, w), F32),
        compiler_params=_cparams(1), name="hy_fspec",
    )(a5, gb, sums)


def _hy_stage23_call(a5, gb, hb, kf, fb):
    b, _, n1, _, w = a5.shape

    def body(a_ref, g_ref, h_ref, k_ref, o_ref):
        for k in range(fb):
            a = jnp.concatenate([a_ref[0, 0, k], a_ref[0, 1, k]], axis=0)
            z = jnp.dot(g_ref[k], a, preferred_element_type=F32)
            zr, zi = z[:LANES], z[LANES:]
            kr, ki = k_ref[k, 0], k_ref[k, 1]
            y = jnp.concatenate([zr * kr - zi * ki, zr * ki + zi * kr], axis=0).astype(BF16)
            cc = jnp.dot(h_ref[k], y, preferred_element_type=F32)
            o_ref[0, 0, k] = cc[:LANES].astype(BF16)
            o_ref[0, 1, k] = cc[LANES:].astype(BF16)

    return pl.pallas_call(
        body, grid=(n1 // fb, b),
        in_specs=[pl.BlockSpec((1, 2, fb, LANES, w), lambda i, j: (j, 0, i, 0, 0)),
                  pl.BlockSpec((fb, 2 * LANES, 2 * LANES), lambda i, j: (i, 0, 0)),
                  pl.BlockSpec((fb, 2 * LANES, 2 * LANES), lambda i, j: (i, 0, 0)),
                  pl.BlockSpec((fb, 2, LANES, w), lambda i, j: (i, 0, 0, 0))],
        out_specs=pl.BlockSpec((1, 2, fb, LANES, w), lambda i, j: (j, 0, i, 0, 0)),
        out_shape=jax.ShapeDtypeStruct(a5.shape, BF16),
        compiler_params=_cparams(2), name="hy_dft23",
    )(a5, gb, hb, kf)


def _hy_stage4_call(cv, inv1, x0v, zv, skip_t):
    b, n2, cols = cv.shape
    l1 = inv1.shape[0]
    tn = skip_t.shape[1]

    def body(f_ref, c_ref, x0_ref, z_ref, s_ref, o_ref):
        y = jnp.dot(f_ref[...], c_ref[0], preferred_element_type=F32)
        z = z_ref[0].astype(F32)
        o_ref[0] = (x0_ref[0].astype(F32) * (y + s_ref[...] * z)).astype(BF16)

    return pl.pallas_call(
        body, grid=(b, cols // tn),
        in_specs=[pl.BlockSpec((l1, n2), lambda i, j: (0, 0)),
                  pl.BlockSpec((1, n2, tn), lambda i, j: (i, 0, j)),
                  pl.BlockSpec((1, l1, tn), lambda i, j: (i, 0, j)),
                  pl.BlockSpec((1, l1, tn), lambda i, j: (i, 0, j)),
                  pl.BlockSpec((1, tn), lambda i, j: (0, 0))],
        out_specs=pl.BlockSpec((1, l1, tn), lambda i, j: (i, 0, j)),
        out_shape=jax.ShapeDtypeStruct((b, l1, cols), BF16),
        compiler_params=_cparams(2), name="hy_dft4",
    )(inv1, cv, x0v, zv, skip_t)


def _hy_dense_call(x0, z, taps, sums, skip):
    b, c, w = z.shape
    n = 2 * c
    th = 2.0 * np.pi * np.outer(np.arange(n, dtype=np.float64), np.arange(c, dtype=np.float64)) / n
    fwd = jnp.asarray(np.concatenate([np.cos(th), -np.sin(th)], axis=0).astype(np.float32)).astype(BF16)
    inv = jnp.asarray((np.concatenate([np.cos(th).T, -np.sin(th).T], axis=1) / n)
                      .astype(np.float32)).astype(BF16)

    def body(f_ref, i_ref, x0_ref, z_ref, t_ref, s_ref, k_ref, o_ref):
        sv = s_ref[...]
        nrm = 1.0 / (sv[:, :w] + sv[:, w:])
        tf = jnp.dot(f_ref[...], t_ref[...].astype(BF16), preferred_element_type=F32)
        kr = (tf[:n, :w] + tf[:n, w:]) * nrm
        ki = (tf[n:, :w] - tf[n:, w:]) * nrm
        zf = jnp.dot(f_ref[...], z_ref[0], preferred_element_type=F32)
        zr, zi = zf[:n], zf[n:]
        y = jnp.concatenate([zr * kr - zi * ki, zr * ki + zi * kr], axis=0).astype(BF16)
        yt = jnp.dot(i_ref[...], y, preferred_element_type=F32)
        o_ref[0] = (x0_ref[0].astype(F32) * (yt + k_ref[...] * z_ref[0].astype(F32))).astype(BF16)

    return pl.pallas_call(
        body, grid=(b,),
        in_specs=[pl.BlockSpec((2 * n, c), lambda i: (0, 0)),
                  pl.BlockSpec((c, 2 * n), lambda i: (0, 0)),
                  pl.BlockSpec((1, c, w), lambda i: (i, 0, 0)),
                  pl.BlockSpec((1, c, w), lambda i: (i, 0, 0)),
                  pl.BlockSpec((c, 2 * w), lambda i: (0, 0)),
                  pl.BlockSpec((1, 2 * w), lambda i: (0, 0)),
                  pl.BlockSpec((1, w), lambda i: (0, 0))],
        out_specs=pl.BlockSpec((1, c, w), lambda i: (i, 0, 0)),
        out_shape=jax.ShapeDtypeStruct((b, c, w), BF16),
        compiler_params=_cparams(1), name="hy_dense",
    )(fwd, inv, x0, z, taps, sums, skip.reshape(1, w))


def _hyena_long(x0, z, fparams, skip):
    b, l, w = z.shape
    l1 = l // LANES
    n1 = 2 * l1
    fb = min(8, n1)
    fwd1, inv1, gb, hb = _dft_tables(l1)
    taps, sums = _hy_filter_call(l, *fparams)
    ta = _hy_stage1_call(taps.astype(BF16).reshape(1, l1, LANES * 2 * w), fwd1)
    kf = _hy_filter_spec_call(ta.reshape(1, 2, n1, LANES, 2 * w), gb, sums, fb)
    za = _hy_stage1_call(z.reshape(b, l1, LANES * w), fwd1)
    cc = _hy_stage23_call(za.reshape(b, 2, n1, LANES, w), gb, hb, kf, fb)
    tn = 8 * w
    skip_t = jnp.tile(skip.reshape(1, w), (1, tn // w))
    y = _hy_stage4_call(cc.reshape(b, 2 * n1, LANES * w), inv1, x0.reshape(b, l1, LANES * w),
                        z.reshape(b, l1, LANES * w), skip_t)
    return y.reshape(b, l, w)


def _hyena_short(x0, z, fparams, skip):
    taps, sums = _hy_filter_call(z.shape[1], *fparams)
    return _hy_dense_call(x0, z, taps, sums, skip)


def _rope_tables(l, c):
    t = np.arange(l)
    row = (t // GRID_W).astype(np.float64)
    col = (t % GRID_W).astype(np.float64)
    n_pairs = HEAD_DIM // 4
    inv_freq = ROPE_THETA ** (-np.arange(n_pairs, dtype=np.float64) / n_pairs)
    ang = np.concatenate([row[:, None] * inv_freq, col[:, None] * inv_freq], axis=-1)
    cos = np.repeat(np.cos(ang), 2, axis=1)
    sin = np.repeat(np.sin(ang), 2, axis=1)
    even = (np.arange(HEAD_DIM) % 2 == 0)[None, :]
    se = np.where(even, -sin, 0.0)
    so = np.where(even, 0.0, sin)
    pad = lambda a, v: np.concatenate([a, np.full((c, HEAD_DIM), v)], axis=0)
    two = lambda a: jnp.asarray(np.concatenate([a, a], axis=1).astype(np.float32))
    return two(pad(cos, 1.0)), two(pad(se, 0.0)), two(pad(so, 0.0))


def _values_t(v, rows):
    b, t, w = v.shape
    vt = jnp.swapaxes(v, 1, 2).reshape(b, w // rows, rows, t)
    return jnp.concatenate([vt, jnp.ones((b, w // rows, ONES_ROWS, t), v.dtype)], axis=2)


def _block_diag_ones():
    i = np.arange(2 * LANES)
    return jnp.asarray((i[:, None] // HEAD_DIM == i[None, :] // HEAD_DIM).astype(np.float32)).astype(BF16)


def _gain_rows(gains, scales):
    rows = [jnp.tile(g.astype(F32) * s, 2 * LANES // HEAD_DIM) for g, s in zip(gains, scales)]
    rows += [jnp.zeros((2 * LANES,), F32)] * (8 - len(rows))
    return jnp.stack(rows, axis=0)


def kernel(x, c, ctx, c_ctx, w_ada, b_ada, w_up, ffn_conv_w, ffn_conv_b, w_down, w_in_e, w_out_e, na_q_gain, na_k_gain, na_rpb, da_q_gain, da_k_gain, da_lambda_q1, da_lambda_k1, da_lambda_q2, da_lambda_k2, da_subln_gain, w_in_o, w_out_o, gqa_q_gain, gqa_k_gain, hy_conv_w, hy_conv_b, hy_w1, hy_b1, hy_w2, hy_b2, hy_w3, hy_b3, hy_w4, hy_freq, hy_skip):
    b, l, d = x.shape
    cl = ctx.shape[1]
    t = l + cl
    depth = w_ada.shape[0]
    f = w_down.shape[1]
    assert cl == TM and l % TK == 0 and (l // GRID_W) >= NA_WROWS + NA_QROWS
    assert w_in_e.shape[2] == 3072 and w_in_o.shape[2] == 2304 and d % LANES == 0
    scale = HEAD_DIM ** -0.5 * LOG2E

    rows = -(-(b + 1) // 8) * 8
    cs = jnp.zeros((rows, d), F32).at[:b].set(c).at[b].set(c_ctx)
    mods = _ada_call(cs, w_ada, b_ada)
    modarrs = []
    for layer in range(depth):
        lat = mods[layer, :b].reshape(b, 1, 6, d)
        cx = jnp.broadcast_to(mods[layer, b].reshape(1, 1, 6, d), (b, 1, 6, d))
        modarrs.append(jnp.concatenate([lat, cx], axis=1))

    ropes = _rope_tables(l, cl)
    bd = _block_diag_ones()
    x_all, h = _mod0_call(x, ctx, modarrs[0])

    for layer in range(depth):
        i = layer // 2
        if layer % 2 == 0:
            lam_init = 0.8 - 0.6 * math.exp(-0.3 * layer)
            gains = _gain_rows((na_q_gain[i], na_k_gain[i], da_q_gain[i], da_k_gain[i]),
                               (scale, 1.0, scale, 1.0))
            qkv = _inproj_call(_inproj_even_body, "inproj_even", h, w_in_e[i].astype(BF16), bd, gains,
                               ropes, 3072)
            vat = _values_t(qkv[:, :, 1024:1536], HEAD_DIM)
            vbt = _values_t(qkv[:, :, 2560:3072], LANES)
            bias = _na_bias_table(na_rpb[i])
            ya = _na_call(qkv, vat, bias, l)
            lamv = jnp.stack([da_lambda_q1[i], da_lambda_k1[i], da_lambda_q2[i], da_lambda_k2[i]]).astype(F32)
            qbt = jnp.swapaxes(qkv[:, :, 1536:2048], 1, 2)
            yb = _flash_call("da_attn", qbt, qkv, vbt, 2048 // 256, False, LANES,
                             _make_da_finish(lam_init),
                             (lamv, da_subln_gain[i].reshape(1, LANES).astype(F32)), 512, l)
            w_out = w_out_e[i].astype(BF16)
        else:
            gains = _gain_rows((gqa_q_gain[i], gqa_k_gain[i]), (scale, 1.0))
            qkv = _inproj_call(_inproj_odd_body, "inproj_odd", h, w_in_o[i].astype(BF16), bd, gains,
                               ropes, 2432)
            vt = _values_t(qkv[:, :, 2304:2432], HEAD_DIM)
            qt = jnp.swapaxes(qkv[:, :, 1536:2048], 1, 2)
            ya = _flash_call("gqa_attn", qt, qkv, vt, 2048 // 128, True, HEAD_DIM,
                             _gqa_finish, (), 512, l)
            x0, z = _hy_pre_call(qkv, hy_conv_w[i], hy_conv_b[i], l)
            fparams = (hy_w1[i], hy_b1[i], hy_w2[i], hy_b2[i], hy_w3[i], hy_b3[i], hy_w4[i], hy_freq[i])
            yd_l = _hyena_long(x0[:, :l], z[:, :l], fparams, hy_skip[i])
            if layer < depth - 1:
                yd_c = _hyena_short(x0[:, l:], z[:, l:], fparams, hy_skip[i])
            else:
                yd_c = jnp.zeros((b, cl, x0.shape[2]), BF16)
            yb = jnp.concatenate([yd_l, yd_c], axis=1)
            w_out = w_out_o[i].astype(BF16)
        ka = ya.shape[2]
        x_all, h2 = _outproj_call(ya, yb, w_out[:ka], w_out[ka:], x_all, modarrs[layer])
        act = _ffn_up_call(h2, w_up[layer].astype(BF16), ffn_conv_w[layer], ffn_conv_b[layer])
        x_all, h = _ffn_down_call(act, w_down[layer].astype(BF16), x_all, modarrs[layer],
                                  modarrs[min(layer + 1, depth - 1)])
    return x_all[:, :l]
```

```python
import functools
import math

import numpy as np
import jax
import jax.numpy as jnp
from jax import lax
from jax.experimental import pallas as pl
from jax.experimental.pallas import tpu as pltpu

F32 = jnp.float32
BF16 = jnp.bfloat16

HEAD_DIM = 64
GRID_W = 64
ROPE_THETA = 10000.0
EPS = 1e-6
NA_WIN_R = 8
NA_WIN_C = 16
HY_EMB_DIM = 33
HY_FAST_DECAY = 0.3
HY_SLOW_DECAY = 1.5
HY_DECAY_TARGET = 1e-2

LANES = 128
TM = 256
TK = 512
NA_QROWS = TM // GRID_W
NA_WROWS = NA_QROWS + NA_WIN_R
ONES_ROWS = 16
NEG = -1e30
LOG2E = 1.4426950408889634
VMEM_LIMIT = 56 * 1024 * 1024


def _cparams(n_axes):
    return pltpu.CompilerParams(dimension_semantics=("arbitrary",) * n_axes,
                                vmem_limit_bytes=VMEM_LIMIT)


def _modulate(x, sh, sc):
    ms = jnp.mean(x * x, axis=-1, keepdims=True)
    return x * lax.rsqrt(ms + EPS) * (1.0 + sc) + sh


def _seg_norm(y, bd, gain):
    ss = jnp.dot((y * y).astype(BF16), bd, preferred_element_type=F32)
    return y * lax.rsqrt(ss * (1.0 / HEAD_DIM) + EPS) * gain


def _rope(y, c, se, so):
    return y * c + pltpu.roll(y, LANES - 1, 1) * se + pltpu.roll(y, 1, 1) * so


def _dot_nt(a, b):
    return lax.dot_general(a, b, (((1,), (1,)), ((), ())), preferred_element_type=F32)


def _shift_rows(g, prev_row, next_row):
    tm = g.shape[0]
    row = lax.broadcasted_iota(jnp.int32, g.shape, 0)
    dn = jnp.where(row == 0, prev_row, pltpu.roll(g, 1, 0))
    up = jnp.where(row == tm - 1, next_row, pltpu.roll(g, tm - 1, 0))
    return dn, up


def _ada_body(c_ref, w_ref, b_ref, o_ref):
    c = c_ref[...]
    a = (c / (1.0 + jnp.exp(-c))).astype(BF16)
    o_ref[0] = jnp.dot(a, w_ref[0].astype(BF16), preferred_element_type=F32) + b_ref[0]


def _ada_call(cs, w_ada, b_ada):
    depth, d, n6 = w_ada.shape
    rows = cs.shape[0]
    tn = 1536
    return pl.pallas_call(
        _ada_body, grid=(depth, n6 // tn),
        in_specs=[pl.BlockSpec((rows, d), lambda l, n: (0, 0)),
                  pl.BlockSpec((1, d, tn), lambda l, n: (l, 0, n)),
                  pl.BlockSpec((1, 1, tn), lambda l, n: (l, 0, n))],
        out_specs=pl.BlockSpec((1, rows, tn), lambda l, n: (l, 0, n)),
        out_shape=jax.ShapeDtypeStruct((depth, rows, n6), F32),
        compiler_params=_cparams(2), name="ada",
    )(cs, w_ada, b_ada.reshape(depth, 1, n6))


def _mod0_call(x, ctx, modarr):
    b, l, d = x.shape
    c = ctx.shape[1]
    t = l + c
    nl = l // TM

    def body(x_ref, c_ref, m_ref, xo_ref, h_ref):
        j = pl.program_id(1)
        xv = jnp.where(j < nl, x_ref[0], c_ref[0])
        xo_ref[0] = xv
        h_ref[0] = _modulate(xv, m_ref[0, 0, 0:1, :], m_ref[0, 0, 1:2, :]).astype(BF16)

    return pl.pallas_call(
        body, grid=(b, t // TM),
        in_specs=[pl.BlockSpec((1, TM, d), lambda i, j: (i, jnp.minimum(j, nl - 1), 0)),
                  pl.BlockSpec((1, TM, d), lambda i, j: (i, 0, 0)),
                  pl.BlockSpec((1, 1, 6, d), lambda i, j: (i, j // nl, 0, 0))],
        out_specs=[pl.BlockSpec((1, TM, d), lambda i, j: (i, j, 0)),
                   pl.BlockSpec((1, TM, d), lambda i, j: (i, j, 0))],
        out_shape=[jax.ShapeDtypeStruct((b, t, d), F32), jax.ShapeDtypeStruct((b, t, d), BF16)],
        compiler_params=_cparams(2), name="mod0",
    )(x, ctx, modarr)


def _inproj_even_body(a_ref, w_ref, bd_ref, g_ref, rc_ref, rse_ref, rso_ref,
                      o_ref, vat_ref, qbt_ref, vbt_ref):
    a = a_ref[0]
    bd = bd_ref[...]
    rc, rse, rso = rc_ref[...], rse_ref[...], rso_ref[...]
    plan = ((0, False, 0), (1, False, 512), (None, False, None), (2, True, None), (3, True, 1024),
            (None, False, None))
    for seg, (gain_row, rope, out_col) in enumerate(plan):
        y = jnp.dot(a, w_ref[:, seg * 512:(seg + 1) * 512], preferred_element_type=F32)
        for half in range(2):
            yy = y[:, half * 256:(half + 1) * 256]
            if gain_row is not None:
                yy = _seg_norm(yy, bd, g_ref[gain_row:gain_row + 1, :])
            for blk in range(2):
                z = yy[:, blk * LANES:(blk + 1) * LANES]
                if rope:
                    z = _rope(z, rc, rse, rso)
                m = half * 2 + blk
                if out_col is not None:
                    col = out_col + m * LANES
                    o_ref[0, :, col:col + LANES] = z.astype(BF16)
                    continue
                zt = z.T.astype(BF16)
                if seg == 2:
                    vat_ref[0, 2 * m, 0:HEAD_DIM, :] = zt[:HEAD_DIM]
                    vat_ref[0, 2 * m + 1, 0:HEAD_DIM, :] = zt[HEAD_DIM:]
                elif seg == 3:
                    qbt_ref[0, m * LANES:(m + 1) * LANES, :] = zt
                else:
                    vbt_ref[0, m, 0:LANES, :] = zt
    ones = jnp.ones((ONES_ROWS, TM), BF16)
    for hd in range(vat_ref.shape[1]):
        vat_ref[0, hd, HEAD_DIM:HEAD_DIM + ONES_ROWS, :] = ones
    for hd in range(vbt_ref.shape[1]):
        vbt_ref[0, hd, LANES:LANES + ONES_ROWS, :] = ones


def _inproj_odd_body(a_ref, w_ref, bd_ref, g_ref, rc_ref, rse_ref, rso_ref, o_ref, qt_ref, vt_ref):
    a = a_ref[0]
    bd = bd_ref[...]
    rc, rse, rso = rc_ref[...], rse_ref[...], rso_ref[...]
    y = jnp.dot(a, w_ref[:, 0:512], preferred_element_type=F32)
    for half in range(2):
        yy = _seg_norm(y[:, half * 256:(half + 1) * 256], bd, g_ref[0:1, :])
        for blk in range(2):
            z = _rope(yy[:, blk * LANES:(blk + 1) * LANES], rc, rse, rso)
            m = half * 2 + blk
            qt_ref[0, m * LANES:(m + 1) * LANES, :] = z.T.astype(BF16)
    y = jnp.dot(a, w_ref[:, 512:768], preferred_element_type=F32)
    k = _seg_norm(y[:, :LANES], bd_ref[0:LANES, 0:LANES], g_ref[1:2, 0:LANES])
    k = _rope(k, rc, rse, rso)
    kr = pltpu.roll(k, HEAD_DIM, 1)
    lo = lax.broadcasted_iota(jnp.int32, k.shape, 1) < HEAD_DIM
    o_ref[0, :, 1536:1664] = jnp.where(lo, k, kr).astype(BF16)
    o_ref[0, :, 1664:1792] = jnp.where(lo, kr, k).astype(BF16)
    vt = y[:, LANES:].T.astype(BF16)
    ones = jnp.ones((ONES_ROWS, TM), BF16)
    for hd in range(2):
        vt_ref[0, hd, 0:HEAD_DIM, :] = vt[hd * HEAD_DIM:(hd + 1) * HEAD_DIM]
        vt_ref[0, hd, HEAD_DIM:HEAD_DIM + ONES_ROWS, :] = ones
    for seg in range(3):
        y = jnp.dot(a, w_ref[:, 768 + seg * 512:768 + (seg + 1) * 512], preferred_element_type=F32)
        o_ref[0, :, seg * 512:(seg + 1) * 512] = y.astype(BF16)


def _inproj_call(body, name, h, w, bd, gains, ropes, n_out, t_shapes):
    b, t, d = h.shape
    n_in = w.shape[1]
    rc, rse, rso = ropes

    def t_spec(shape):
        nd = len(shape)
        return pl.BlockSpec((1,) + tuple(shape[1:-1]) + (TM,), lambda i, j: (i,) + (0,) * (nd - 2) + (j,))

    return pl.pallas_call(
        body, grid=(b, t // TM),
        in_specs=[pl.BlockSpec((1, TM, d), lambda i, j: (i, j, 0)),
                  pl.BlockSpec((d, n_in), lambda i, j: (0, 0)),
                  pl.BlockSpec(bd.shape, lambda i, j: (0, 0)),
                  pl.BlockSpec(gains.shape, lambda i, j: (0, 0)),
                  pl.BlockSpec((TM, LANES), lambda i, j: (j, 0)),
                  pl.BlockSpec((TM, LANES), lambda i, j: (j, 0)),
                  pl.BlockSpec((TM, LANES), lambda i, j: (j, 0))],
        out_specs=[pl.BlockSpec((1, TM, n_out), lambda i, j: (i, j, 0))] + [t_spec(s) for s in t_shapes],
        out_shape=[jax.ShapeDtypeStruct((b, t, n_out), BF16)]
        + [jax.ShapeDtypeStruct(s, BF16) for s in t_shapes],
        compiler_params=_cparams(2), name=name,
    )(h, w, bd, gains, rc, rse, rso)


def _ffn_up_call(h, w_up, conv_w, conv_b):
    b, t, d = h.shape
    f = w_up.shape[1] // 2
    nl = (t - TM) // TM
    nt = t // TM
    hb = 16
    r = TM // hb
    n_split = 1
    tn = f // n_split
    assert tn % LANES == 0
    chunks = [(c0, min(2 * LANES, tn - c0)) for c0 in range(0, tn, 2 * LANES)]

    def body(a_ref, ap_ref, an_ref, wg_ref, wv_ref, cw_ref, cb_ref, o_ref):
        j = pl.program_id(2)
        a = a_ref[0]
        zero = jnp.zeros((hb, d), BF16)
        ap = jnp.where((j == 0) | (j == nl), zero, ap_ref[0])
        an = jnp.where((j == nl - 1) | (j == nt - 1), zero, an_ref[0])
        a_ext = jnp.concatenate([ap, a, an], axis=0)
        for c0, cw in chunks:
            sl = slice(c0, c0 + cw)
            g = jnp.dot(a_ext, wg_ref[:, sl], preferred_element_type=F32)
            u = (g[hb - 1:hb - 1 + TM] * cw_ref[0:1, sl] + g[hb:hb + TM] * cw_ref[1:2, sl]
                 + g[hb + 1:hb + 1 + TM] * cw_ref[2:3, sl] + cb_ref[0:1, sl])
            v = jnp.dot(a, wv_ref[:, sl], preferred_element_type=F32)
            o_ref[0, :, sl] = ((u / (1.0 + jnp.exp(-u))) * v).astype(BF16)

    return pl.pallas_call(
        body, grid=(n_split, b, nt),
        in_specs=[pl.BlockSpec((1, TM, d), lambda n, i, j: (i, j, 0)),
                  pl.BlockSpec((1, hb, d), lambda n, i, j: (i, jnp.maximum(j * r - 1, 0), 0)),
                  pl.BlockSpec((1, hb, d), lambda n, i, j: (i, jnp.minimum((j + 1) * r, nt * r - 1), 0)),
                  pl.BlockSpec((d, tn), lambda n, i, j: (0, n)),
                  pl.BlockSpec((d, tn), lambda n, i, j: (0, n_split + n)),
                  pl.BlockSpec((3, tn), lambda n, i, j: (0, n)),
                  pl.BlockSpec((1, tn), lambda n, i, j: (0, n))],
        out_specs=pl.BlockSpec((1, TM, tn), lambda n, i, j: (i, j, n)),
        out_shape=jax.ShapeDtypeStruct((b, t, f), BF16),
        compiler_params=_cparams(3), name="ffn_up",
    )(h, h, h, w_up, w_up, conv_w, conv_b.reshape(1, f))


def _outproj_call(ya, yb, wa, wb, x, modarr):
    b, t, d = x.shape
    nl = (t - TM) // TM
    ka, kb = ya.shape[2], yb.shape[2]

    def body(ya_ref, yb_ref, wa_ref, wb_ref, x_ref, m_ref, xo_ref, h_ref):
        y = (jnp.dot(ya_ref[0], wa_ref[...], preferred_element_type=F32)
             + jnp.dot(yb_ref[0], wb_ref[...], preferred_element_type=F32))
        x1 = x_ref[0] + m_ref[0, 0, 2:3, :] * y
        xo_ref[0] = x1
        h_ref[0] = _modulate(x1, m_ref[0, 0, 3:4, :], m_ref[0, 0, 4:5, :]).astype(BF16)

    return pl.pallas_call(
        body, grid=(b, t // TM),
        in_specs=[pl.BlockSpec((1, TM, ka), lambda i, j: (i, j, 0)),
                  pl.BlockSpec((1, TM, kb), lambda i, j: (i, j, 0)),
                  pl.BlockSpec((ka, d), lambda i, j: (0, 0)),
                  pl.BlockSpec((kb, d), lambda i, j: (0, 0)),
                  pl.BlockSpec((1, TM, d), lambda i, j: (i, j, 0)),
                  pl.BlockSpec((1, 1, 6, d), lambda i, j: (i, j // nl, 0, 0))],
        out_specs=[pl.BlockSpec((1, TM, d), lambda i, j: (i, j, 0)),
                   pl.BlockSpec((1, TM, d), lambda i, j: (i, j, 0))],
        out_shape=[jax.ShapeDtypeStruct((b, t, d), F32), jax.ShapeDtypeStruct((b, t, d), BF16)],
        compiler_params=_cparams(2), name="outproj",
    )(ya, yb, wa, wb, x, modarr)


def _ffn_down_call(act, w_down, x, modarr, modarr_next):
    b, t, d = x.shape
    f = w_down.shape[0]
    nl = (t - TM) // TM
    nt = t // TM

    def body(a_ref, wd_ref, x_ref, m_ref, mn_ref, xo_ref, h_ref):
        y = jnp.dot(a_ref[0], wd_ref[...], preferred_element_type=F32)
        x2 = x_ref[0] + m_ref[0, 0, 5:6, :] * y
        xo_ref[0] = x2
        h_ref[0] = _modulate(x2, mn_ref[0, 0, 0:1, :], mn_ref[0, 0, 1:2, :]).astype(BF16)

    return pl.pallas_call(
        body, grid=(b, nt),
        in_specs=[pl.BlockSpec((1, TM, f), lambda i, j: (i, j, 0)),
                  pl.BlockSpec((f, d), lambda i, j: (0, 0)),
                  pl.BlockSpec((1, TM, d), lambda i, j: (i, j, 0)),
                  pl.BlockSpec((1, 1, 6, d), lambda i, j: (i, j // nl, 0, 0)),
                  pl.BlockSpec((1, 1, 6, d), lambda i, j: (i, j // nl, 0, 0))],
        out_specs=[pl.BlockSpec((1, TM, d), lambda i, j: (i, j, 0)),
                   pl.BlockSpec((1, TM, d), lambda i, j: (i, j, 0))],
        out_shape=[jax.ShapeDtypeStruct((b, t, d), F32), jax.ShapeDtypeStruct((b, t, d), BF16)],
        compiler_params=_cparams(2), name="ffn_down",
    )(act, w_down, x, modarr, modarr_next)


def _ffn_last_call(act, w_down, x, modarr):
    b, t, d = x.shape
    f = w_down.shape[0]
    nl = (t - TM) // TM

    def body(a_ref, wd_ref, x_ref, m_ref, xo_ref):
        y = jnp.dot(a_ref[0], wd_ref[...], preferred_element_type=F32)
        xo_ref[0] = x_ref[0] + m_ref[0, 0, 5:6, :] * y

    return pl.pallas_call(
        body, grid=(b, nl),
        in_specs=[pl.BlockSpec((1, TM, f), lambda i, j: (i, j, 0)),
                  pl.BlockSpec((f, d), lambda i, j: (0, 0)),
                  pl.BlockSpec((1, TM, d), lambda i, j: (i, j, 0)),
                  pl.BlockSpec((1, 1, 6, d), lambda i, j: (i, 0, 0, 0))],
        out_specs=pl.BlockSpec((1, TM, d), lambda i, j: (i, j, 0)),
        out_shape=jax.ShapeDtypeStruct((b, nl * TM, d), F32),
        compiler_params=_cparams(2), name="ffn_last",
    )(act, w_down, x, modarr)


def _split_heads(q):
    lane = lax.broadcasted_iota(jnp.int32, q.shape, 1)
    zero = jnp.zeros_like(q)
    return jnp.where(lane < HEAD_DIM, q, zero), jnp.where(lane >= HEAD_DIM, q, zero)


def _flash_call(name, qt, qkv, vt, kcol, shared_kv, vrows, finish, extra, out_cols, l):
    b, t, _ = qkv.shape
    c = t - l
    nl = l // TM
    n_pairs = l // (2 * TK)
    n_steps = out_cols // (2 * LANES)
    n_kv = 1 if shared_kv else 2
    ns = 4
    assert l % (2 * TK) == 0

    def body(q_ref, k_ref, vt_ref, *rest):
        extra_refs = rest[:-8]
        o_ref, s_ref, p_ref, sc_ref, pc_ref, acc_ref, m_ref, a_ref = rest[-8:]
        is_lat = pl.program_id(2) < nl
        row = lax.broadcasted_iota(jnp.int32, (LANES, TM), 0)
        qs = []
        for blk in range(2):
            qb = q_ref[0, blk * LANES:(blk + 1) * LANES, :]
            qs += [jnp.where(row < HEAD_DIM, qb, jnp.zeros_like(qb)),
                   jnp.where(row >= HEAD_DIM, qb, jnp.zeros_like(qb))]
        kv_of = lambda s: 0 if shared_kv else s // 2

        def scores(off, size, dst):
            kcs = [k_ref[0, pl.ds(off, size), j * LANES:(j + 1) * LANES] for j in range(n_kv)]
            for s in range(ns):
                dst(s)[...] = jnp.dot(kcs[kv_of(s)], qs[s], preferred_element_type=F32)

        def softmax(src, dst, slot):
            for s in range(ns):
                st = src(s)[...]
                m = m_ref[s]
                mnew = jnp.maximum(m, jnp.max(st, axis=0, keepdims=True))
                a_ref[slot, s] = jnp.exp2(m - mnew)
                dst(s)[...] = jnp.exp2((st - mnew).astype(BF16))
                m_ref[s] = mnew

        def values(src, slot, off, size):
            vcs = [vt_ref[0, j, :, pl.ds(off, size)] for j in range(n_kv)]
            for s in range(ns):
                acc_ref[s] = a_ref[slot, s] * acc_ref[s] + jnp.dot(vcs[kv_of(s)], src(s)[...],
                                                                  preferred_element_type=F32)

        s_ctx, p_ctx = (lambda s: sc_ref.at[s]), (lambda s: pc_ref.at[s])
        s_buf = [(lambda s, i=i: s_ref.at[i, s]) for i in range(2)]
        p_buf = [(lambda s, i=i: p_ref.at[i, s]) for i in range(2)]

        def start():
            acc_ref[...] = jnp.zeros(acc_ref.shape, F32)
            m_ref[...] = jnp.full(m_ref.shape, NEG, F32)
            scores(l, c, s_ctx)

        def finalize():
            outs = [acc_ref[s, 0:vrows, :] / acc_ref[s, vrows:vrows + 1, :] for s in range(ns)]
            o_ref[0] = finish(outs, *extra_refs)

        @pl.when(is_lat)
        def _():
            start()
            scores(0, TK, s_buf[0])
            softmax(s_ctx, p_ctx, 2)
            values(p_ctx, 2, l, c)
            scores(TK, TK, s_buf[1])
            softmax(s_buf[0], p_buf[0], 0)

            for pi in range(n_pairs - 1):
                base = pi * (2 * TK)
                values(p_buf[0], 0, base, TK)
                scores(base + 2 * TK, TK, s_buf[0])
                softmax(s_buf[1], p_buf[1], 1)
                values(p_buf[1], 1, base + TK, TK)
                scores(base + 3 * TK, TK, s_buf[1])
                softmax(s_buf[0], p_buf[0], 0)
            base = (n_pairs - 1) * 2 * TK
            values(p_buf[0], 0, base, TK)
            softmax(s_buf[1], p_buf[1], 1)
            values(p_buf[1], 1, base + TK, TK)
            finalize()

        @pl.when(jnp.logical_not(is_lat))
        def _():
            start()
            softmax(s_ctx, p_ctx, 2)
            values(p_ctx, 2, l, c)
            finalize()

    kw = n_kv * LANES
    extra_specs = [pl.BlockSpec(e.shape, lambda bi, h, i: (0,) * e.ndim) for e in extra]
    return pl.pallas_call(
        body, grid=(b, n_steps, t // TM),
        in_specs=[pl.BlockSpec((1, 2 * LANES, TM), lambda bi, h, i: (bi, h, i)),
                  pl.BlockSpec((1, t, kw), lambda bi, h, i: (bi, 0, kcol + h)),
                  pl.BlockSpec((1, n_kv, vrows + ONES_ROWS, t), lambda bi, h, i: (bi, h, 0, 0))]
        + extra_specs,
        out_specs=pl.BlockSpec((1, TM, 2 * LANES), lambda bi, h, i: (bi, i, h)),
        out_shape=jax.ShapeDtypeStruct((b, t, out_cols), BF16),
        scratch_shapes=[pltpu.VMEM((2, ns, TK, TM), F32), pltpu.VMEM((2, ns, TK, TM), BF16),
                        pltpu.VMEM((ns, c, TM), F32), pltpu.VMEM((ns, c, TM), BF16),
                        pltpu.VMEM((ns, vrows + ONES_ROWS, TM), F32), pltpu.VMEM((ns, 1, TM), F32),
                        pltpu.VMEM((3, ns, 1, TM), F32)],
        compiler_params=_cparams(3), name=name,
    )(qt, qkv, vt, *extra)


def _gqa_finish(outs):
    return jnp.concatenate(outs, axis=0).T.astype(BF16)


def _make_da_finish(lam_init):
    def finish(outs, lam_ref, subln_ref):
        lv = lam_ref[...]
        lam = (jnp.exp(jnp.sum(lv[0:1] * lv[1:2], keepdims=True))
               - jnp.exp(jnp.sum(lv[2:3] * lv[3:4], keepdims=True)) + lam_init)
        heads = []
        for hd in range(len(outs) // 2):
            o = (outs[2 * hd] - lam * outs[2 * hd + 1]).T
            o = o * lax.rsqrt(jnp.mean(o * o, axis=-1, keepdims=True) + EPS)
            heads.append((o * subln_ref[...] * (1.0 - lam_init)).astype(BF16))
        return jnp.concatenate(heads, axis=1)
    return finish


def _na_call(qkv, vt, bias, l):
    b, t, _ = qkv.shape
    c = t - l
    nl = l // TM
    rows = l // GRID_W
    win = NA_WROWS * GRID_W
    n_pairs = vt.shape[1] // 2

    nt = t // TM

    def window(tile):
        return pl.multiple_of(jnp.clip(tile * NA_QROWS - NA_WIN_R // 2, 0, rows - NA_WROWS) * GRID_W, 256)

    def body(q0_ref, qn_ref, k_ref, vt_ref, b0_ref, bn_ref, o_ref, sw_ref, sc_ref):
        g = pl.program_id(2)
        kc = k_ref[0, l:l + c, :]

        def scores(q_ref, b_ref, tile, buf, with_win):
            qs = _split_heads(q_ref[0])
            if with_win:
                kw = k_ref[0, pl.ds(window(tile), win), :]
                for s in range(2):
                    sw_ref[buf, s] = _dot_nt(kw, qs[s]) + b_ref[0, s]
            for s in range(2):
                sc_ref[buf, s] = _dot_nt(kc, qs[s])

        def finish(tile, buf, with_win):
            outs = []
            for s in range(2):
                parts = [(sc_ref[buf, s], vt_ref[0, s, :, l:l + c])]
                if with_win:
                    parts.append((sw_ref[buf, s], vt_ref[0, s, :, pl.ds(window(tile), win)]))
                m = functools.reduce(jnp.maximum, [jnp.max(sc, axis=0, keepdims=True) for sc, _ in parts])
                acc = sum(jnp.dot(v, jnp.exp2((sc - m).astype(BF16)), preferred_element_type=F32)
                          for sc, v in parts)
                outs.append(acc[:HEAD_DIM] / acc[HEAD_DIM:HEAD_DIM + 1])
            o_ref[0] = jnp.concatenate(outs, axis=0).T.astype(BF16)

        cur, nxt = g % 2, (g + 1) % 2

        @pl.when(g == 0)
        def _():
            scores(q0_ref, b0_ref, 0, 0, True)

        @pl.when(g + 1 < nl)
        def _():
            scores(qn_ref, bn_ref, g + 1, nxt, True)
            finish(g, cur, True)

        @pl.when(g + 1 == nl)
        def _():
            scores(qn_ref, bn_ref, g + 1, nxt, False)
            finish(g, cur, True)

        @pl.when(g == nl)
        def _():
            finish(g, cur, False)

    def bias_next(bi, h, g):
        tile = g + 1
        case = jnp.where(tile >= nl - 1, 2, 1)
        return (case * n_pairs + h, 0, 0, 0)

    return pl.pallas_call(
        body, grid=(b, n_pairs, nt),
        in_specs=[pl.BlockSpec((1, TM, LANES), lambda bi, h, g: (bi, 0, h)),
                  pl.BlockSpec((1, TM, LANES), lambda bi, h, g: (bi, jnp.minimum(g + 1, nt - 1), h)),
                  pl.BlockSpec((1, t, LANES), lambda bi, h, g: (bi, 0, n_pairs + h)),
                  pl.BlockSpec((1, 2, HEAD_DIM + ONES_ROWS, t), lambda bi, h, g: (bi, h, 0, 0)),
                  pl.BlockSpec((1, 2, win, TM), lambda bi, h, g: (h, 0, 0, 0)),
                  pl.BlockSpec((1, 2, win, TM), bias_next)],
        out_specs=pl.BlockSpec((1, TM, LANES), lambda bi, h, g: (bi, g, h)),
        out_shape=jax.ShapeDtypeStruct((b, t, n_pairs * LANES), BF16),
        scratch_shapes=[pltpu.VMEM((2, 2, win, TM), F32), pltpu.VMEM((2, 2, c, TM), F32)],
        compiler_params=_cparams(3), name="na_attn",
    )(qkv, qkv, qkv, vt, bias, bias)


def _na_bias_table(rpb):
    h, n_ro, n_co = rpb.shape
    kr, qr = np.arange(NA_WROWS), np.arange(NA_QROWS)
    kc, qc = np.arange(GRID_W), np.arange(GRID_W)
    cs = np.clip(qc - NA_WIN_C // 2, 0, GRID_W - NA_WIN_C)
    col_ok = (kc[:, None] >= cs[None, :]) & (kc[:, None] < cs[None, :] + NA_WIN_C)
    co = kc[:, None] - qc[None, :] + NA_WIN_C - 1
    col_sel = (co[None] == np.arange(n_co)[:, None, None]).astype(np.float32)
    tables = []
    for d, rel in ((0, np.zeros_like(qr)), (NA_WIN_R // 2, qr), (NA_WIN_R, np.full_like(qr, NA_WIN_R // 2))):
        row_ok = (kr[:, None] >= rel[None, :]) & (kr[:, None] < rel[None, :] + NA_WIN_R)
        ro = kr[:, None] - qr[None, :] - d + NA_WIN_R - 1
        row_sel = (ro[:, :, None] == np.arange(n_ro)[None, None, :]).astype(np.float32)
        tb = jnp.einsum('kqa,hab,bcd->hkcqd', jnp.asarray(row_sel), rpb.astype(F32) * LOG2E,
                        jnp.asarray(col_sel), precision=lax.Precision.HIGHEST)
        ok = row_ok[:, None, :, None] & col_ok[None, :, None, :]
        tables.append(jnp.where(jnp.asarray(ok)[None], tb, NEG))
    tbl = jnp.stack(tables, axis=0)
    return tbl.reshape(3 * (h // 2), 2, NA_WROWS * GRID_W, TM).astype(F32)


def _hy_pre_call(qkv, conv_w, conv_b, l):
    b, t, _ = qkv.shape
    w3 = conv_w.shape[1]
    w = w3 // 3
    nl = l // TM
    nt = t // TM
    hb = 16
    r = TM // hb

    def body(u_ref, up_ref, un_ref, cw_ref, cb_ref, x0_ref, z_ref):
        j = pl.program_id(1)
        prev_ok = jnp.where((j == 0) | (j == nl), 0.0, 1.0)
        next_ok = jnp.where((j == nl - 1) | (j == nt - 1), 0.0, 1.0)
        parts = []
        for p in range(3):
            sl = slice(p * w, (p + 1) * w)
            g = u_ref[0, :, sl].astype(F32)
            gp = up_ref[0, hb - 1:hb, sl].astype(F32) * prev_ok
            gn = un_ref[0, 0:1, sl].astype(F32) * next_ok
            dn, up = _shift_rows(g, gp, gn)
            parts.append(dn * cw_ref[0:1, sl] + g * cw_ref[1:2, sl] + up * cw_ref[2:3, sl]
                         + cb_ref[0:1, sl])
        x0_ref[0] = parts[0].astype(BF16)
        z_ref[0] = (parts[2] * parts[1]).astype(BF16)

    return pl.pallas_call(
        body, grid=(b, nt),
        in_specs=[pl.BlockSpec((1, TM, w3), lambda i, j: (i, j, 0)),
                  pl.BlockSpec((1, hb, w3), lambda i, j: (i, jnp.maximum(j * r - 1, 0), 0)),
                  pl.BlockSpec((1, hb, w3), lambda i, j: (i, jnp.minimum((j + 1) * r, nt * r - 1), 0)),
                  pl.BlockSpec((3, w3), lambda i, j: (0, 0)),
                  pl.BlockSpec((1, w3), lambda i, j: (0, 0))],
        out_specs=[pl.BlockSpec((1, TM, w), lambda i, j: (i, j, 0)),
                   pl.BlockSpec((1, TM, w), lambda i, j: (i, j, 0))],
        out_shape=[jax.ShapeDtypeStruct((b, t, w), BF16), jax.ShapeDtypeStruct((b, t, w), BF16)],
        compiler_params=_cparams(2), name="hy_pre",
    )(qkv, qkv, qkv, conv_w, conv_b.reshape(1, w3))


def _hy_filter_call(length, w1, b1, w2, b2, w3, b3, w4, freq):
    order = w2.shape[0]
    w = w4.shape[1] // 2
    tl = min(length, 512)
    hi = lax.Precision.HIGHEST
    t = np.linspace(0.0, 1.0, length, dtype=np.float64)[:, None]
    bands = (HY_EMB_DIM - 1) // 2
    ang = 2.0 * math.pi * np.arange(length, dtype=np.float64)[:, None] / length
    fq = np.linspace(1e-4, bands - 1, bands, dtype=np.float64)[None, :]
    emb = np.concatenate([t, np.cos(fq * ang), -np.sin(fq * ang)], axis=-1).astype(np.float32)
    emb = np.pad(emb, ((0, 0), (0, LANES - HY_EMB_DIM)))
    max_decay = math.log(HY_DECAY_TARGET) / HY_FAST_DECAY
    min_decay = math.log(HY_DECAY_TARGET) / HY_SLOW_DECAY
    deltas = np.linspace(min_decay, max_decay, w, dtype=np.float64)
    decay = np.exp(-t * np.abs(deltas)[None, :]).astype(np.float32)
    decay2 = np.concatenate([decay, decay], axis=1)
    w1p = jnp.pad(w1, ((0, LANES - HY_EMB_DIM), (0, 0)))

    def body(e_ref, d_ref, w1_ref, b1_ref, w2_ref, b2_ref, w3_ref, b3_ref, w4_ref, f_ref,
             h_ref, s_ref):
        i = pl.program_id(0)
        fr = f_ref[...]
        hdn = jnp.sin(fr * (jnp.dot(e_ref[...], w1_ref[...], precision=hi,
                                    preferred_element_type=F32) + b1_ref[...]))
        hdn = jnp.sin(fr * (jnp.dot(hdn, w2_ref[...], precision=hi,
                                    preferred_element_type=F32) + b2_ref[...]))
        hdn = jnp.sin(fr * (jnp.dot(hdn, w3_ref[...], precision=hi,
                                    preferred_element_type=F32) + b3_ref[...]))
        taps = jnp.dot(hdn, w4_ref[...], precision=hi, preferred_element_type=F32) * d_ref[...]
        row = lax.broadcasted_iota(jnp.int32, taps.shape, 0) + i * tl
        col = lax.broadcasted_iota(jnp.int32, taps.shape, 1)
        taps = jnp.where((row == 0) & (col >= w), 0.0, taps)
        h_ref[...] = taps

        @pl.when(i == 0)
        def _():
            s_ref[...] = jnp.zeros(s_ref.shape, F32)
        s_ref[...] += jnp.sum(jnp.abs(taps), axis=0, keepdims=True)

    full = lambda a: pl.BlockSpec(a.shape, lambda i: (0,) * a.ndim)
    ops = (w1p, b1.reshape(1, order), w2, b2.reshape(1, order), w3, b3.reshape(1, order), w4,
           freq.reshape(1, order))
    return pl.pallas_call(
        body, grid=(length // tl,),
        in_specs=[pl.BlockSpec((tl, LANES), lambda i: (i, 0)),
                  pl.BlockSpec((tl, 2 * w), lambda i: (i, 0))] + [full(a) for a in ops],
        out_specs=[pl.BlockSpec((tl, 2 * w), lambda i: (i, 0)),
                   pl.BlockSpec((1, 2 * w), lambda i: (0, 0))],
        out_shape=[jax.ShapeDtypeStruct((length, 2 * w), F32), jax.ShapeDtypeStruct((1, 2 * w), F32)],
        compiler_params=_cparams(1), name="hy_filter",
    )(jnp.asarray(emb), jnp.asarray(decay2), *ops)


def _dft_tables(l1, fb):
    n1 = 2 * l1
    n = n1 * LANES
    nf = -(-(n1 // 2 + 1) // fb) * fb
    f1 = np.arange(nf, dtype=np.float64)
    wgt = np.where((f1 == 0) | (f1 == n1 // 2), 1.0, np.where(f1 < n1 // 2, 2.0, 0.0))[None, :]
    t1 = np.arange(l1, dtype=np.float64)
    th1 = 2.0 * np.pi * np.outer(f1, t1) / n1
    fwd1 = np.concatenate([np.cos(th1), -np.sin(th1)], axis=0)
    inv1 = np.concatenate([wgt * np.cos(th1).T, -wgt * np.sin(th1).T], axis=1) / n
    f2 = np.arange(LANES, dtype=np.float64)
    t2 = np.arange(LANES, dtype=np.float64)
    fr = f1[:, None, None] + n1 * f2[None, :, None]
    th2 = 2.0 * np.pi * fr * t2[None, None, :] / n
    gr, gi = np.cos(th2), -np.sin(th2)
    gb = np.concatenate([np.concatenate([gr, -gi], axis=2),
                         np.concatenate([gi, gr], axis=2)], axis=1)
    hb = np.transpose(gb, (0, 2, 1))
    as_bf = lambda a: jnp.asarray(a.astype(np.float32)).astype(BF16)
    return as_bf(fwd1), as_bf(inv1), as_bf(gb), as_bf(hb)


def _hy_stage1_call(xv, fwd1):
    bx, l1, cols = xv.shape
    n2 = fwd1.shape[0]
    tn = min(cols, 4096)

    def body(f_ref, x_ref, o_ref):
        o_ref[0] = jnp.dot(f_ref[...], x_ref[0], preferred_element_type=F32).astype(BF16)

    return pl.pallas_call(
        body, grid=(bx, cols // tn),
        in_specs=[pl.BlockSpec((n2, l1), lambda i, j: (0, 0)),
                  pl.BlockSpec((1, l1, tn), lambda i, j: (i, 0, j))],
        out_specs=pl.BlockSpec((1, n2, tn), lambda i, j: (i, 0, j)),
        out_shape=jax.ShapeDtypeStruct((bx, n2, cols), BF16),
        compiler_params=_cparams(2), name="hy_dft1",
    )(fwd1, xv)


def _hy_filter_spec_call(a5, gb, sums, fb):
    n1, w2 = a5.shape[2], a5.shape[4]
    w = w2 // 2

    def body(a_ref, g_ref, s_ref, o_ref):
        sv = s_ref[...]
        inv = 1.0 / (sv[:, :w] + sv[:, w:])
        for k in range(fb):
            a = jnp.concatenate([a_ref[0, 0, k], a_ref[0, 1, k]], axis=0)
            z = jnp.dot(g_ref[k], a, preferred_element_type=F32)
            o_ref[k, 0] = (z[:LANES, :w] + z[:LANES, w:]) * inv
            o_ref[k, 1] = (z[LANES:, :w] - z[LANES:, w:]) * inv

    return pl.pallas_call(
        body, grid=(n1 // fb,),
        in_specs=[pl.BlockSpec((1, 2, fb, LANES, w2), lambda i: (0, 0, i, 0, 0)),
                  pl.BlockSpec((fb, 2 * LANES, 2 * LANES), lambda i: (i, 0, 0)),
                  pl.BlockSpec((1, w2), lambda i: (0, 0))],
        out_specs=pl.BlockSpec((fb, 2, LANES, w), lambda i: (i, 0, 0, 0)),
        out_shape=jax.ShapeDtypeStruct((n1, 2, LANES, w), F32),
        compiler_params=_cparams(1), name="hy_fspec",
    )(a5, gb, sums)


def _hy_stage23_call(a5, gb, hb, kf, fb):
    b, _, n1, _, w = a5.shape

    def body(a_ref, g_ref, h_ref, k_ref, o_ref):
        for k in range(fb):
            a = jnp.concatenate([a_ref[0, 0, k], a_ref[0, 1, k]], axis=0)
            z = jnp.dot(g_ref[k], a, preferred_element_type=F32)
            zr, zi = z[:LANES], z[LANES:]
            kr, ki = k_ref[k, 0], k_ref[k, 1]
            y = jnp.concatenate([zr * kr - zi * ki, zr * ki + zi * kr], axis=0).astype(BF16)
            cc = jnp.dot(h_ref[k], y, preferred_element_type=F32)
            o_ref[0, 0, k] = cc[:LANES].astype(BF16)
            o_ref[0, 1, k] = cc[LANES:].astype(BF16)

    return pl.pallas_call(
        body, grid=(n1 // fb, b),
        in_specs=[pl.BlockSpec((1, 2, fb, LANES, w), lambda i, j: (j, 0, i, 0, 0)),
                  pl.BlockSpec((fb, 2 * LANES, 2 * LANES), lambda i, j: (i, 0, 0)),
                  pl.BlockSpec((fb, 2 * LANES, 2 * LANES), lambda i, j: (i, 0, 0)),
                  pl.BlockSpec((fb, 2, LANES, w), lambda i, j: (i, 0, 0, 0))],
        out_specs=pl.BlockSpec((1, 2, fb, LANES, w), lambda i, j: (j, 0, i, 0, 0)),
        out_shape=jax.ShapeDtypeStruct(a5.shape, BF16),
        compiler_params=_cparams(2), name="hy_dft23",
    )(a5, gb, hb, kf)


def _hy_stage4_call(cv, inv1, x0v, zv, skip_t):
    b, n2, cols = cv.shape
    l1 = inv1.shape[0]
    tn = skip_t.shape[1]

    def body(f_ref, c_ref, x0_ref, z_ref, s_ref, o_ref):
        y = jnp.dot(f_ref[...], c_ref[0], preferred_element_type=F32)
        z = z_ref[0].astype(F32)
        o_ref[0] = (x0_ref[0].astype(F32) * (y + s_ref[...] * z)).astype(BF16)

    return pl.pallas_call(
        body, grid=(b, cols // tn),
        in_specs=[pl.BlockSpec((l1, n2), lambda i, j: (0, 0)),
                  pl.BlockSpec((1, n2, tn), lambda i, j: (i, 0, j)),
                  pl.BlockSpec((1, l1, tn), lambda i, j: (i, 0, j)),
                  pl.BlockSpec((1, l1, tn), lambda i, j: (i, 0, j)),
                  pl.BlockSpec((1, tn), lambda i, j: (0, 0))],
        out_specs=pl.BlockSpec((1, l1, tn), lambda i, j: (i, 0, j)),
        out_shape=jax.ShapeDtypeStruct((b, l1, cols), BF16),
        compiler_params=_cparams(2), name="hy_dft4",
    )(inv1, cv, x0v, zv, skip_t)


def _hy_dense_call(x0, z, taps, sums, skip):
    b, c, w = z.shape
    n = 2 * c
    th = 2.0 * np.pi * np.outer(np.arange(n, dtype=np.float64), np.arange(c, dtype=np.float64)) / n
    fwd = jnp.asarray(np.concatenate([np.cos(th), -np.sin(th)], axis=0).astype(np.float32)).astype(BF16)
    inv = jnp.asarray((np.concatenate([np.cos(th).T, -np.sin(th).T], axis=1) / n)
                      .astype(np.float32)).astype(BF16)

    def body(f_ref, i_ref, x0_ref, z_ref, t_ref, s_ref, k_ref, o_ref):
        sv = s_ref[...]
        nrm = 1.0 / (sv[:, :w] + sv[:, w:])
        tf = jnp.dot(f_ref[...], t_ref[...].astype(BF16), preferred_element_type=F32)
        kr = (tf[:n, :w] + tf[:n, w:]) * nrm
        ki = (tf[n:, :w] - tf[n:, w:]) * nrm
        zf = jnp.dot(f_ref[...], z_ref[0], preferred_element_type=F32)
        zr, zi = zf[:n], zf[n:]
        y = jnp.concatenate([zr * kr - zi * ki, zr * ki + zi * kr], axis=0).astype(BF16)
        yt = jnp.dot(i_ref[...], y, preferred_element_type=F32)
        o_ref[0] = (x0_ref[0].astype(F32) * (yt + k_ref[...] * z_ref[0].astype(F32))).astype(BF16)

    return pl.pallas_call(
        body, grid=(b,),
        in_specs=[pl.BlockSpec((2 * n, c), lambda i: (0, 0)),
                  pl.BlockSpec((c, 2 * n), lambda i: (0, 0)),
                  pl.BlockSpec((1, c, w), lambda i: (i, 0, 0)),
                  pl.BlockSpec((1, c, w), lambda i: (i, 0, 0)),
                  pl.BlockSpec((c, 2 * w), lambda i: (0, 0)),
                  pl.BlockSpec((1, 2 * w), lambda i: (0, 0)),
                  pl.BlockSpec((1, w), lambda i: (0, 0))],
        out_specs=pl.BlockSpec((1, c, w), lambda i: (i, 0, 0)),
        out_shape=jax.ShapeDtypeStruct((b, c, w), BF16),
        compiler_params=_cparams(1), name="hy_dense",
    )(fwd, inv, x0, z, taps, sums, skip.reshape(1, w))


def _hyena_long(x0, z, fparams, skip):
    b, l, w = z.shape
    l1 = l // LANES
    n1 = 2 * l1
    fb = min(8, n1)
    fwd1, inv1, gb, hb = _dft_tables(l1, fb)
    nf = gb.shape[0]
    taps, sums = _hy_filter_call(l, *fparams)
    ta = _hy_stage1_call(taps.astype(BF16).reshape(1, l1, LANES * 2 * w), fwd1)
    kf = _hy_filter_spec_call(ta.reshape(1, 2, nf, LANES, 2 * w), gb, sums, fb)
    za = _hy_stage1_call(z.reshape(b, l1, LANES * w), fwd1)
    cc = _hy_stage23_call(za.reshape(b, 2, nf, LANES, w), gb, hb, kf, fb)
    tn = 8 * w
    skip_t = jnp.tile(skip.reshape(1, w), (1, tn // w))
    y = _hy_stage4_call(cc.reshape(b, 2 * nf, LANES * w), inv1, x0.reshape(b, l1, LANES * w),
                        z.reshape(b, l1, LANES * w), skip_t)
    return y.reshape(b, l, w)


def _hyena_short(x0, z, fparams, skip):
    taps, sums = _hy_filter_call(z.shape[1], *fparams)
    return _hy_dense_call(x0, z, taps, sums, skip)


def _rope_tables(l, c):
    t = np.arange(l)
    row = (t // GRID_W).astype(np.float64)
    col = (t % GRID_W).astype(np.float64)
    n_pairs = HEAD_DIM // 4
    inv_freq = ROPE_THETA ** (-np.arange(n_pairs, dtype=np.float64) / n_pairs)
    ang = np.concatenate([row[:, None] * inv_freq, col[:, None] * inv_freq], axis=-1)
    cos = np.repeat(np.cos(ang), 2, axis=1)
    sin = np.repeat(np.sin(ang), 2, axis=1)
    even = (np.arange(HEAD_DIM) % 2 == 0)[None, :]
    se = np.where(even, -sin, 0.0)
    so = np.where(even, 0.0, sin)
    pad = lambda a, v: np.concatenate([a, np.full((c, HEAD_DIM), v)], axis=0)
    two = lambda a: jnp.asarray(np.concatenate([a, a], axis=1).astype(np.float32))
    return two(pad(cos, 1.0)), two(pad(se, 0.0)), two(pad(so, 0.0))


def _block_diag_ones():
    i = np.arange(2 * LANES)
    return jnp.asarray((i[:, None] // HEAD_DIM == i[None, :] // HEAD_DIM).astype(np.float32)).astype(BF16)


def _gain_rows(gains, scales):
    rows = [jnp.tile(g.astype(F32) * s, 2 * LANES // HEAD_DIM) for g, s in zip(gains, scales)]
    rows += [jnp.zeros((2 * LANES,), F32)] * (8 - len(rows))
    return jnp.stack(rows, axis=0)


def kernel(x, c, ctx, c_ctx, w_ada, b_ada, w_up, ffn_conv_w, ffn_conv_b, w_down, w_in_e, w_out_e, na_q_gain, na_k_gain, na_rpb, da_q_gain, da_k_gain, da_lambda_q1, da_lambda_k1, da_lambda_q2, da_lambda_k2, da_subln_gain, w_in_o, w_out_o, gqa_q_gain, gqa_k_gain, hy_conv_w, hy_conv_b, hy_w1, hy_b1, hy_w2, hy_b2, hy_w3, hy_b3, hy_w4, hy_freq, hy_skip):
    b, l, d = x.shape
    cl = ctx.shape[1]
    t = l + cl
    depth = w_ada.shape[0]
    f = w_down.shape[1]
    assert cl == TM and l % TK == 0 and (l // GRID_W) >= NA_WROWS + NA_QROWS
    assert w_in_e.shape[2] == 3072 and w_in_o.shape[2] == 2304 and d % LANES == 0
    scale = HEAD_DIM ** -0.5 * LOG2E

    rows = -(-(b + 1) // 8) * 8
    cs = jnp.zeros((rows, d), F32).at[:b].set(c).at[b].set(c_ctx)
    mods = _ada_call(cs, w_ada, b_ada)
    modarrs = []
    for layer in range(depth):
        lat = mods[layer, :b].reshape(b, 1, 6, d)
        cx = jnp.broadcast_to(mods[layer, b].reshape(1, 1, 6, d), (b, 1, 6, d))
        modarrs.append(jnp.concatenate([lat, cx], axis=1))

    ropes = _rope_tables(l, cl)
    bd = _block_diag_ones()
    x_all, h = _mod0_call(x, ctx, modarrs[0])

    for layer in range(depth):
        i = layer // 2
        if layer % 2 == 0:
            lam_init = 0.8 - 0.6 * math.exp(-0.3 * layer)
            gains = _gain_rows((na_q_gain[i], na_k_gain[i], da_q_gain[i], da_k_gain[i]),
                               (scale, 1.0, scale, 1.0))
            qkv, vat, qbt, vbt = _inproj_call(
                _inproj_even_body, "inproj_even", h, w_in_e[i].astype(BF16), bd, gains, ropes, 1536,
                ((b, 8, HEAD_DIM + ONES_ROWS, t), (b, 512, t), (b, 4, LANES + ONES_ROWS, t)))
            bias = _na_bias_table(na_rpb[i])
            ya = _na_call(qkv, vat, bias, l)
            lamv = jnp.stack([da_lambda_q1[i], da_lambda_k1[i], da_lambda_q2[i], da_lambda_k2[i]]).astype(F32)
            yb = _flash_call("da_attn", qbt, qkv, vbt, 1024 // 256, False, LANES,
                             _make_da_finish(lam_init),
                             (lamv, da_subln_gain[i].reshape(1, LANES).astype(F32)), 512, l)
            w_out = w_out_e[i].astype(BF16)
        else:
            gains = _gain_rows((gqa_q_gain[i], gqa_k_gain[i]), (scale, 1.0))
            qkv, qt, vt = _inproj_call(
                _inproj_odd_body, "inproj_odd", h, w_in_o[i].astype(BF16), bd, gains, ropes, 1792,
                ((b, 512, t), (b, 2, HEAD_DIM + ONES_ROWS, t)))
            ya = _flash_call("gqa_attn", qt, qkv, vt, 1536 // 128, True, HEAD_DIM,
                             _gqa_finish, (), 512, l)
            x0, z = _hy_pre_call(qkv, hy_conv_w[i], hy_conv_b[i], l)
            fparams = (hy_w1[i], hy_b1[i], hy_w2[i], hy_b2[i], hy_w3[i], hy_b3[i], hy_w4[i], hy_freq[i])
            yd_l = _hyena_long(x0[:, :l], z[:, :l], fparams, hy_skip[i])
            if layer < depth - 1:
                yd_c = _hyena_short(x0[:, l:], z[:, l:], fparams, hy_skip[i])
            else:
                yd_c = jnp.zeros((b, cl, x0.shape[2]), BF16)
            yb = jnp.concatenate([yd_l, yd_c], axis=1)
            w_out = w_out_o[i].astype(BF16)
        ka = ya.shape[2]
        x_all, h2 = _outproj_call(ya, yb, w_out[:ka], w_out[ka:], x_all, modarrs[layer])
        act = _ffn_up_call(h2, w_up[layer].astype(BF16), ffn_conv_w[layer], ffn_conv_b[layer])
        if layer == depth - 1:
            return _ffn_last_call(act, w_down[layer].astype(BF16), x_all, modarrs[layer])
        x_all, h = _ffn_down_call(act, w_down[layer].astype(BF16), x_all, modarrs[layer],
                                  modarrs[layer + 1])
```

```python
import functools
import math

import numpy as np
import jax
import jax.numpy as jnp
from jax import lax
from jax.experimental import pallas as pl
from jax.experimental.pallas import tpu as pltpu

F32 = jnp.float32
BF16 = jnp.bfloat16

HEAD_DIM = 64
GRID_W = 64
ROPE_THETA = 10000.0
EPS = 1e-6
NA_WIN_R = 8
NA_WIN_C = 16
HY_EMB_DIM = 33
HY_FAST_DECAY = 0.3
HY_SLOW_DECAY = 1.5
HY_DECAY_TARGET = 1e-2

LANES = 128
TM = 256
TK = 512
NA_QROWS = TM // GRID_W
NA_WROWS = NA_QROWS + NA_WIN_R
ONES_ROWS = 16
NEG = -1e30
LOG2E = 1.4426950408889634
VMEM_LIMIT = 56 * 1024 * 1024


def _cparams(n_axes):
    return pltpu.CompilerParams(dimension_semantics=("arbitrary",) * n_axes,
                                vmem_limit_bytes=VMEM_LIMIT)


def _modulate(x, sh, sc):
    ms = jnp.mean(x * x, axis=-1, keepdims=True)
    return x * lax.rsqrt(ms + EPS) * (1.0 + sc) + sh


def _seg_norm(y, bd, gain):
    ss = jnp.dot((y * y).astype(BF16), bd, preferred_element_type=F32)
    return y * lax.rsqrt(ss * (1.0 / HEAD_DIM) + EPS) * gain


def _rope(y, c, se, so):
    return y * c + pltpu.roll(y, LANES - 1, 1) * se + pltpu.roll(y, 1, 1) * so


def _dot_nt(a, b):
    return lax.dot_general(a, b, (((1,), (1,)), ((), ())), preferred_element_type=F32)


def _shift_rows(g, prev_row, next_row):
    tm = g.shape[0]
    row = lax.broadcasted_iota(jnp.int32, g.shape, 0)
    dn = jnp.where(row == 0, prev_row, pltpu.roll(g, 1, 0))
    up = jnp.where(row == tm - 1, next_row, pltpu.roll(g, tm - 1, 0))
    return dn, up


def _ada_body(c_ref, w_ref, b_ref, o_ref):
    c = c_ref[...]
    a = (c / (1.0 + jnp.exp(-c))).astype(BF16)
    o_ref[0] = jnp.dot(a, w_ref[0].astype(BF16), preferred_element_type=F32) + b_ref[0]


def _ada_call(cs, w_ada, b_ada):
    depth, d, n6 = w_ada.shape
    rows = cs.shape[0]
    tn = 1536
    return pl.pallas_call(
        _ada_body, grid=(depth, n6 // tn),
        in_specs=[pl.BlockSpec((rows, d), lambda l, n: (0, 0)),
                  pl.BlockSpec((1, d, tn), lambda l, n: (l, 0, n)),
                  pl.BlockSpec((1, 1, tn), lambda l, n: (l, 0, n))],
        out_specs=pl.BlockSpec((1, rows, tn), lambda l, n: (l, 0, n)),
        out_shape=jax.ShapeDtypeStruct((depth, rows, n6), F32),
        compiler_params=_cparams(2), name="ada",
    )(cs, w_ada, b_ada.reshape(depth, 1, n6))


def _mod0_call(x, ctx, modarr):
    b, l, d = x.shape
    c = ctx.shape[1]
    t = l + c
    nl = l // TM

    def body(x_ref, c_ref, m_ref, xo_ref, h_ref):
        j = pl.program_id(1)
        xv = jnp.where(j < nl, x_ref[0], c_ref[0])
        xo_ref[0] = xv
        h_ref[0] = _modulate(xv, m_ref[0, 0, 0:1, :], m_ref[0, 0, 1:2, :]).astype(BF16)

    return pl.pallas_call(
        body, grid=(b, t // TM),
        in_specs=[pl.BlockSpec((1, TM, d), lambda i, j: (i, jnp.minimum(j, nl - 1), 0)),
                  pl.BlockSpec((1, TM, d), lambda i, j: (i, 0, 0)),
                  pl.BlockSpec((1, 1, 6, d), lambda i, j: (i, j // nl, 0, 0))],
        out_specs=[pl.BlockSpec((1, TM, d), lambda i, j: (i, j, 0)),
                   pl.BlockSpec((1, TM, d), lambda i, j: (i, j, 0))],
        out_shape=[jax.ShapeDtypeStruct((b, t, d), F32), jax.ShapeDtypeStruct((b, t, d), BF16)],
        compiler_params=_cparams(2), name="mod0",
    )(x, ctx, modarr)


def _inproj_even_body(a_ref, w_ref, bd_ref, g_ref, rc_ref, rse_ref, rso_ref,
                      o_ref, vat_ref, qbt_ref, vbt_ref):
    a = a_ref[0]
    bd = bd_ref[...]
    rc, rse, rso = rc_ref[...], rse_ref[...], rso_ref[...]
    plan = ((0, False, 0), (1, False, 512), (None, False, None), (2, True, None), (3, True, 1024),
            (None, False, None))
    for seg, (gain_row, rope, out_col) in enumerate(plan):
        y = jnp.dot(a, w_ref[:, seg * 512:(seg + 1) * 512], preferred_element_type=F32)
        for half in range(2):
            yy = y[:, half * 256:(half + 1) * 256]
            if gain_row is not None:
                yy = _seg_norm(yy, bd, g_ref[gain_row:gain_row + 1, :])
            for blk in range(2):
                z = yy[:, blk * LANES:(blk + 1) * LANES]
                if rope:
                    z = _rope(z, rc, rse, rso)
                m = half * 2 + blk
                if out_col is not None:
                    col = out_col + m * LANES
                    o_ref[0, :, col:col + LANES] = z.astype(BF16)
                    continue
                zt = z.T.astype(BF16)
                if seg == 2:
                    vat_ref[0, 2 * m, 0:HEAD_DIM, :] = zt[:HEAD_DIM]
                    vat_ref[0, 2 * m + 1, 0:HEAD_DIM, :] = zt[HEAD_DIM:]
                elif seg == 3:
                    qbt_ref[0, m * LANES:(m + 1) * LANES, :] = zt
                else:
                    vbt_ref[0, m, 0:LANES, :] = zt
    ones = jnp.ones((ONES_ROWS, TM), BF16)
    for hd in range(vat_ref.shape[1]):
        vat_ref[0, hd, HEAD_DIM:HEAD_DIM + ONES_ROWS, :] = ones
    for hd in range(vbt_ref.shape[1]):
        vbt_ref[0, hd, LANES:LANES + ONES_ROWS, :] = ones


def _inproj_odd_body(a_ref, w_ref, bd_ref, g_ref, rc_ref, rse_ref, rso_ref, o_ref, qt_ref, vt_ref):
    a = a_ref[0]
    bd = bd_ref[...]
    rc, rse, rso = rc_ref[...], rse_ref[...], rso_ref[...]
    y = jnp.dot(a, w_ref[:, 0:512], preferred_element_type=F32)
    for half in range(2):
        yy = _seg_norm(y[:, half * 256:(half + 1) * 256], bd, g_ref[0:1, :])
        for blk in range(2):
            z = _rope(yy[:, blk * LANES:(blk + 1) * LANES], rc, rse, rso)
            m = half * 2 + blk
            qt_ref[0, m * LANES:(m + 1) * LANES, :] = z.T.astype(BF16)
    y = jnp.dot(a, w_ref[:, 512:768], preferred_element_type=F32)
    k = _seg_norm(y[:, :LANES], bd_ref[0:LANES, 0:LANES], g_ref[1:2, 0:LANES])
    k = _rope(k, rc, rse, rso)
    kr = pltpu.roll(k, HEAD_DIM, 1)
    lo = lax.broadcasted_iota(jnp.int32, k.shape, 1) < HEAD_DIM
    o_ref[0, :, 1536:1664] = jnp.where(lo, k, kr).astype(BF16)
    o_ref[0, :, 1664:1792] = jnp.where(lo, kr, k).astype(BF16)
    vt = y[:, LANES:].T.astype(BF16)
    ones = jnp.ones((ONES_ROWS, TM), BF16)
    for hd in range(2):
        vt_ref[0, hd, 0:HEAD_DIM, :] = vt[hd * HEAD_DIM:(hd + 1) * HEAD_DIM]
        vt_ref[0, hd, HEAD_DIM:HEAD_DIM + ONES_ROWS, :] = ones
    for seg in range(3):
        y = jnp.dot(a, w_ref[:, 768 + seg * 512:768 + (seg + 1) * 512], preferred_element_type=F32)
        o_ref[0, :, seg * 512:(seg + 1) * 512] = y.astype(BF16)


def _inproj_call(body, name, h, w, bd, gains, ropes, n_out, t_shapes):
    b, t, d = h.shape
    n_in = w.shape[1]
    rc, rse, rso = ropes

    def t_spec(shape):
        nd = len(shape)
        return pl.BlockSpec((1,) + tuple(shape[1:-1]) + (TM,), lambda i, j: (i,) + (0,) * (nd - 2) + (j,))

    return pl.pallas_call(
        body, grid=(b, t // TM),
        in_specs=[pl.BlockSpec((1, TM, d), lambda i, j: (i, j, 0)),
                  pl.BlockSpec((d, n_in), lambda i, j: (0, 0)),
                  pl.BlockSpec(bd.shape, lambda i, j: (0, 0)),
                  pl.BlockSpec(gains.shape, lambda i, j: (0, 0)),
                  pl.BlockSpec((TM, LANES), lambda i, j: (j, 0)),
                  pl.BlockSpec((TM, LANES), lambda i, j: (j, 0)),
                  pl.BlockSpec((TM, LANES), lambda i, j: (j, 0))],
        out_specs=[pl.BlockSpec((1, TM, n_out), lambda i, j: (i, j, 0))] + [t_spec(s) for s in t_shapes],
        out_shape=[jax.ShapeDtypeStruct((b, t, n_out), BF16)]
        + [jax.ShapeDtypeStruct(s, BF16) for s in t_shapes],
        compiler_params=_cparams(2), name=name,
    )(h, w, bd, gains, rc, rse, rso)


def _ffn_up_call(h, w_up, conv_w, conv_b):
    b, t, d = h.shape
    f = w_up.shape[1] // 2
    nl = (t - TM) // TM
    nt = t // TM
    hb = 16
    r = TM // hb
    n_split = 1
    tn = f // n_split
    assert tn % LANES == 0
    chunks = [(c0, min(2 * LANES, tn - c0)) for c0 in range(0, tn, 2 * LANES)]

    def body(a_ref, ap_ref, an_ref, wg_ref, wv_ref, cw_ref, cb_ref, o_ref):
        j = pl.program_id(2)
        a = a_ref[0]
        zero = jnp.zeros((hb, d), BF16)
        ap = jnp.where((j == 0) | (j == nl), zero, ap_ref[0])
        an = jnp.where((j == nl - 1) | (j == nt - 1), zero, an_ref[0])
        a_ext = jnp.concatenate([ap, a, an], axis=0)
        for c0, cw in chunks:
            sl = slice(c0, c0 + cw)
            g = jnp.dot(a_ext, wg_ref[:, sl], preferred_element_type=F32)
            u = (g[hb - 1:hb - 1 + TM] * cw_ref[0:1, sl] + g[hb:hb + TM] * cw_ref[1:2, sl]
                 + g[hb + 1:hb + 1 + TM] * cw_ref[2:3, sl] + cb_ref[0:1, sl])
            v = jnp.dot(a, wv_ref[:, sl], preferred_element_type=F32)
            o_ref[0, :, sl] = ((u / (1.0 + jnp.exp(-u))) * v).astype(BF16)

    return pl.pallas_call(
        body, grid=(n_split, b, nt),
        in_specs=[pl.BlockSpec((1, TM, d), lambda n, i, j: (i, j, 0)),
                  pl.BlockSpec((1, hb, d), lambda n, i, j: (i, jnp.maximum(j * r - 1, 0), 0)),
                  pl.BlockSpec((1, hb, d), lambda n, i, j: (i, jnp.minimum((j + 1) * r, nt * r - 1), 0)),
                  pl.BlockSpec((d, tn), lambda n, i, j: (0, n)),
                  pl.BlockSpec((d, tn), lambda n, i, j: (0, n_split + n)),
                  pl.BlockSpec((3, tn), lambda n, i, j: (0, n)),
                  pl.BlockSpec((1, tn), lambda n, i, j: (0, n))],
        out_specs=pl.BlockSpec((1, TM, tn), lambda n, i, j: (i, j, n)),
        out_shape=jax.ShapeDtypeStruct((b, t, f), BF16),
        compiler_params=_cparams(3), name="ffn_up",
    )(h, h, h, w_up, w_up, conv_w, conv_b.reshape(1, f))


def _outproj_call(ya, yb, wa, wb, x, modarr):
    b, t, d = x.shape
    nl = (t - TM) // TM
    ka, kb = ya.shape[2], yb.shape[2]

    def body(ya_ref, yb_ref, wa_ref, wb_ref, x_ref, m_ref, xo_ref, h_ref):
        y = (jnp.dot(ya_ref[0], wa_ref[...], preferred_element_type=F32)
             + jnp.dot(yb_ref[0], wb_ref[...], preferred_element_type=F32))
        x1 = x_ref[0] + m_ref[0, 0, 2:3, :] * y
        xo_ref[0] = x1
        h_ref[0] = _modulate(x1, m_ref[0, 0, 3:4, :], m_ref[0, 0, 4:5, :]).astype(BF16)

    return pl.pallas_call(
        body, grid=(b, t // TM),
        in_specs=[pl.BlockSpec((1, TM, ka), lambda i, j: (i, j, 0)),
                  pl.BlockSpec((1, TM, kb), lambda i, j: (i, j, 0)),
                  pl.BlockSpec((ka, d), lambda i, j: (0, 0)),
                  pl.BlockSpec((kb, d), lambda i, j: (0, 0)),
                  pl.BlockSpec((1, TM, d), lambda i, j: (i, j, 0)),
                  pl.BlockSpec((1, 1, 6, d), lambda i, j: (i, j // nl, 0, 0))],
        out_specs=[pl.BlockSpec((1, TM, d), lambda i, j: (i, j, 0)),
                   pl.BlockSpec((1, TM, d), lambda i, j: (i, j, 0))],
        out_shape=[jax.ShapeDtypeStruct((b, t, d), F32), jax.ShapeDtypeStruct((b, t, d), BF16)],
        compiler_params=_cparams(2), name="outproj",
    )(ya, yb, wa, wb, x, modarr)


def _ffn_down_call(act, w_down, x, modarr, modarr_next):
    b, t, d = x.shape
    f = w_down.shape[0]
    nl = (t - TM) // TM
    nt = t // TM

    def body(a_ref, wd_ref, x_ref, m_ref, mn_ref, xo_ref, h_ref):
        y = jnp.dot(a_ref[0], wd_ref[...], preferred_element_type=F32)
        x2 = x_ref[0] + m_ref[0, 0, 5:6, :] * y
        xo_ref[0] = x2
        h_ref[0] = _modulate(x2, mn_ref[0, 0, 0:1, :], mn_ref[0, 0, 1:2, :]).astype(BF16)

    return pl.pallas_call(
        body, grid=(b, nt),
        in_specs=[pl.BlockSpec((1, TM, f), lambda i, j: (i, j, 0)),
                  pl.BlockSpec((f, d), lambda i, j: (0, 0)),
                  pl.BlockSpec((1, TM, d), lambda i, j: (i, j, 0)),
                  pl.BlockSpec((1, 1, 6, d), lambda i, j: (i, j // nl, 0, 0)),
                  pl.BlockSpec((1, 1, 6, d), lambda i, j: (i, j // nl, 0, 0))],
        out_specs=[pl.BlockSpec((1, TM, d), lambda i, j: (i, j, 0)),
                   pl.BlockSpec((1, TM, d), lambda i, j: (i, j, 0))],
        out_shape=[jax.ShapeDtypeStruct((b, t, d), F32), jax.ShapeDtypeStruct((b, t, d), BF16)],
        compiler_params=_cparams(2), name="ffn_down",
    )(act, w_down, x, modarr, modarr_next)


def _ffn_last_call(act, w_down, x, modarr):
    b, t, d = x.shape
    f = w_down.shape[0]
    nl = (t - TM) // TM

    def body(a_ref, wd_ref, x_ref, m_ref, xo_ref):
        y = jnp.dot(a_ref[0], wd_ref[...], preferred_element_type=F32)
        xo_ref[0] = x_ref[0] + m_ref[0, 0, 5:6, :] * y

    return pl.pallas_call(
        body, grid=(b, nl),
        in_specs=[pl.BlockSpec((1, TM, f), lambda i, j: (i, j, 0)),
                  pl.BlockSpec((f, d), lambda i, j: (0, 0)),
                  pl.BlockSpec((1, TM, d), lambda i, j: (i, j, 0)),
                  pl.BlockSpec((1, 1, 6, d), lambda i, j: (i, 0, 0, 0))],
        out_specs=pl.BlockSpec((1, TM, d), lambda i, j: (i, j, 0)),
        out_shape=jax.ShapeDtypeStruct((b, nl * TM, d), F32),
        compiler_params=_cparams(2), name="ffn_last",
    )(act, w_down, x, modarr)


def _split_heads(q):
    lane = lax.broadcasted_iota(jnp.int32, q.shape, 1)
    zero = jnp.zeros_like(q)
    return jnp.where(lane < HEAD_DIM, q, zero), jnp.where(lane >= HEAD_DIM, q, zero)


def _flash_call(name, qt, qkv, vt, kcol, shared_kv, vrows, finish, extra, out_cols, l):
    b, t, _ = qkv.shape
    c = t - l
    nl = l // TM
    n_pairs = l // (2 * TK)
    n_steps = out_cols // (2 * LANES)
    n_kv = 1 if shared_kv else 2
    ns = 4
    assert l % (2 * TK) == 0

    def body(q_ref, qn_ref, k_ref, vt_ref, *rest):
        extra_refs = rest[:-8]
        o_ref, s_ref, p_ref, sc_ref, pc_ref, acc_ref, m_ref, a_ref = rest[-8:]
        tile = pl.program_id(2)
        is_lat = tile < nl
        row = lax.broadcasted_iota(jnp.int32, (LANES, TM), 0)

        def streams(ref):
            out = []
            for blk in range(2):
                qb = ref[0, blk * LANES:(blk + 1) * LANES, :]
                out += [jnp.where(row < HEAD_DIM, qb, jnp.zeros_like(qb)),
                        jnp.where(row >= HEAD_DIM, qb, jnp.zeros_like(qb))]
            return out

        kv_of = lambda s: 0 if shared_kv else s // 2

        def scores(off, size, dst, qs):
            kcs = [k_ref[0, pl.ds(off, size), j * LANES:(j + 1) * LANES] for j in range(n_kv)]
            for s in range(ns):
                dst(s)[...] = jnp.dot(kcs[kv_of(s)], qs[s], preferred_element_type=F32)

        def softmax(src, dst, slot):
            for s in range(ns):
                st = src(s)[...]
                m = m_ref[s]
                mnew = jnp.maximum(m, jnp.max(st, axis=0, keepdims=True))
                a_ref[slot, s] = jnp.exp2(m - mnew)
                dst(s)[...] = jnp.exp2((st - mnew).astype(BF16))
                m_ref[s] = mnew

        def values(src, slot, off, size):
            vcs = [vt_ref[0, j, :, pl.ds(off, size)] for j in range(n_kv)]
            for s in range(ns):
                acc_ref[s] = a_ref[slot, s] * acc_ref[s] + jnp.dot(vcs[kv_of(s)], src(s)[...],
                                                                  preferred_element_type=F32)

        s_ctx, p_ctx = (lambda s: sc_ref.at[s]), (lambda s: pc_ref.at[s])
        s_buf = [(lambda s, i=i: s_ref.at[i, s]) for i in range(2)]
        p_buf = [(lambda s, i=i: p_ref.at[i, s]) for i in range(2)]

        def start():
            acc_ref[...] = jnp.zeros(acc_ref.shape, F32)
            m_ref[...] = jnp.full(m_ref.shape, NEG, F32)

        def finalize():
            outs = [acc_ref[s, 0:vrows, :] / acc_ref[s, vrows:vrows + 1, :] for s in range(ns)]
            o_ref[0] = finish(outs, *extra_refs)

        @pl.when(tile == 0)
        def _():
            qs = streams(q_ref)
            scores(l, c, s_ctx, qs)
            scores(0, TK, s_buf[0], qs)

        @pl.when(is_lat)
        def _():
            qs = streams(q_ref)
            qs_next = streams(qn_ref)
            start()
            softmax(s_ctx, p_ctx, 2)
            values(p_ctx, 2, l, c)
            scores(TK, TK, s_buf[1], qs)
            softmax(s_buf[0], p_buf[0], 0)

            for pi in range(n_pairs - 1):
                base = pi * (2 * TK)
                values(p_buf[0], 0, base, TK)
                scores(base + 2 * TK, TK, s_buf[0], qs)
                softmax(s_buf[1], p_buf[1], 1)
                values(p_buf[1], 1, base + TK, TK)
                scores(base + 3 * TK, TK, s_buf[1], qs)
                softmax(s_buf[0], p_buf[0], 0)
            base = (n_pairs - 1) * 2 * TK
            values(p_buf[0], 0, base, TK)
            scores(l, c, s_ctx, qs_next)
            softmax(s_buf[1], p_buf[1], 1)
            scores(0, TK, s_buf[0], qs_next)
            values(p_buf[1], 1, base + TK, TK)
            finalize()

        @pl.when(jnp.logical_not(is_lat))
        def _():
            start()
            softmax(s_ctx, p_ctx, 2)
            values(p_ctx, 2, l, c)
            finalize()

    kw = n_kv * LANES
    extra_specs = [pl.BlockSpec(e.shape, lambda bi, h, i: (0,) * e.ndim) for e in extra]
    return pl.pallas_call(
        body, grid=(b, n_steps, t // TM),
        in_specs=[pl.BlockSpec((1, 2 * LANES, TM), lambda bi, h, i: (bi, h, i)),
                  pl.BlockSpec((1, 2 * LANES, TM), lambda bi, h, i: (bi, h, jnp.minimum(i + 1, nl))),
                  pl.BlockSpec((1, t, kw), lambda bi, h, i: (bi, 0, kcol + h)),
                  pl.BlockSpec((1, n_kv, vrows + ONES_ROWS, t), lambda bi, h, i: (bi, h, 0, 0))]
        + extra_specs,
        out_specs=pl.BlockSpec((1, TM, 2 * LANES), lambda bi, h, i: (bi, i, h)),
        out_shape=jax.ShapeDtypeStruct((b, t, out_cols), BF16),
        scratch_shapes=[pltpu.VMEM((2, ns, TK, TM), F32), pltpu.VMEM((2, ns, TK, TM), BF16),
                        pltpu.VMEM((ns, c, TM), F32), pltpu.VMEM((ns, c, TM), BF16),
                        pltpu.VMEM((ns, vrows + ONES_ROWS, TM), F32), pltpu.VMEM((ns, 1, TM), F32),
                        pltpu.VMEM((3, ns, 1, TM), F32)],
        compiler_params=_cparams(3), name=name,
    )(qt, qt, qkv, vt, *extra)


def _gqa_finish(outs):
    return jnp.concatenate(outs, axis=0).T.astype(BF16)


def _make_da_finish(lam_init):
    def finish(outs, lam_ref, subln_ref):
        lv = lam_ref[...]
        lam = (jnp.exp(jnp.sum(lv[0:1] * lv[1:2], keepdims=True))
               - jnp.exp(jnp.sum(lv[2:3] * lv[3:4], keepdims=True)) + lam_init)
        heads = []
        for hd in range(len(outs) // 2):
            o = (outs[2 * hd] - lam * outs[2 * hd + 1]).T
            o = o * lax.rsqrt(jnp.mean(o * o, axis=-1, keepdims=True) + EPS)
            heads.append((o * subln_ref[...] * (1.0 - lam_init)).astype(BF16))
        return jnp.concatenate(heads, axis=1)
    return finish


def _na_call(qkv, vt, bias, l):
    b, t, _ = qkv.shape
    c = t - l
    nl = l // TM
    rows = l // GRID_W
    win = NA_WROWS * GRID_W
    n_pairs = vt.shape[1] // 2

    nt = t // TM

    def window(tile):
        return pl.multiple_of(jnp.clip(tile * NA_QROWS - NA_WIN_R // 2, 0, rows - NA_WROWS) * GRID_W, 256)

    def body(q0_ref, qn_ref, k_ref, vt_ref, b0_ref, bn_ref, o_ref, sw_ref, sc_ref):
        g = pl.program_id(2)
        kc = k_ref[0, l:l + c, :]

        def scores(q_ref, b_ref, tile, buf, with_win):
            qs = _split_heads(q_ref[0])
            if with_win:
                kw = k_ref[0, pl.ds(window(tile), win), :]
                for s in range(2):
                    sw_ref[buf, s] = _dot_nt(kw, qs[s]) + b_ref[0, s]
            for s in range(2):
                sc_ref[buf, s] = _dot_nt(kc, qs[s])

        def finish(tile, buf, with_win):
            outs = []
            for s in range(2):
                parts = [(sc_ref[buf, s], vt_ref[0, s, :, l:l + c])]
                if with_win:
                    parts.append((sw_ref[buf, s], vt_ref[0, s, :, pl.ds(window(tile), win)]))
                m = functools.reduce(jnp.maximum, [jnp.max(sc, axis=0, keepdims=True) for sc, _ in parts])
                acc = sum(jnp.dot(v, jnp.exp2((sc - m).astype(BF16)), preferred_element_type=F32)
                          for sc, v in parts)
                outs.append(acc[:HEAD_DIM] / acc[HEAD_DIM:HEAD_DIM + 1])
            o_ref[0] = jnp.concatenate(outs, axis=0).T.astype(BF16)

        cur, nxt = g % 2, (g + 1) % 2

        @pl.when(g == 0)
        def _():
            scores(q0_ref, b0_ref, 0, 0, True)

        @pl.when(g + 1 < nl)
        def _():
            scores(qn_ref, bn_ref, g + 1, nxt, True)
            finish(g, cur, True)

        @pl.when(g + 1 == nl)
        def _():
            scores(qn_ref, bn_ref, g + 1, nxt, False)
            finish(g, cur, True)

        @pl.when(g == nl)
        def _():
            finish(g, cur, False)

    def bias_next(bi, h, g):
        tile = g + 1
        case = jnp.where(tile >= nl - 1, 2, 1)
        return (case * n_pairs + h, 0, 0, 0)

    return pl.pallas_call(
        body, grid=(b, n_pairs, nt),
        in_specs=[pl.BlockSpec((1, TM, LANES), lambda bi, h, g: (bi, 0, h)),
                  pl.BlockSpec((1, TM, LANES), lambda bi, h, g: (bi, jnp.minimum(g + 1, nt - 1), h)),
                  pl.BlockSpec((1, t, LANES), lambda bi, h, g: (bi, 0, n_pairs + h)),
                  pl.BlockSpec((1, 2, HEAD_DIM + ONES_ROWS, t), lambda bi, h, g: (bi, h, 0, 0)),
                  pl.BlockSpec((1, 2, win, TM), lambda bi, h, g: (h, 0, 0, 0)),
                  pl.BlockSpec((1, 2, win, TM), bias_next)],
        out_specs=pl.BlockSpec((1, TM, LANES), lambda bi, h, g: (bi, g, h)),
        out_shape=jax.ShapeDtypeStruct((b, t, n_pairs * LANES), BF16),
        scratch_shapes=[pltpu.VMEM((2, 2, win, TM), F32), pltpu.VMEM((2, 2, c, TM), F32)],
        compiler_params=_cparams(3), name="na_attn",
    )(qkv, qkv, qkv, vt, bias, bias)


def _na_bias_table(rpb):
    h, n_ro, n_co = rpb.shape
    kr, qr = np.arange(NA_WROWS), np.arange(NA_QROWS)
    kc, qc = np.arange(GRID_W), np.arange(GRID_W)
    cs = np.clip(qc - NA_WIN_C // 2, 0, GRID_W - NA_WIN_C)
    col_ok = (kc[:, None] >= cs[None, :]) & (kc[:, None] < cs[None, :] + NA_WIN_C)
    co = kc[:, None] - qc[None, :] + NA_WIN_C - 1
    col_sel = (co[None] == np.arange(n_co)[:, None, None]).astype(np.float32)
    tables = []
    for d, rel in ((0, np.zeros_like(qr)), (NA_WIN_R // 2, qr), (NA_WIN_R, np.full_like(qr, NA_WIN_R // 2))):
        row_ok = (kr[:, None] >= rel[None, :]) & (kr[:, None] < rel[None, :] + NA_WIN_R)
        ro = kr[:, None] - qr[None, :] - d + NA_WIN_R - 1
        row_sel = (ro[:, :, None] == np.arange(n_ro)[None, None, :]).astype(np.float32)
        tb = jnp.einsum('kqa,hab,bcd->hkcqd', jnp.asarray(row_sel), rpb.astype(F32) * LOG2E,
                        jnp.asarray(col_sel), precision=lax.Precision.HIGHEST)
        ok = row_ok[:, None, :, None] & col_ok[None, :, None, :]
        tables.append(jnp.where(jnp.asarray(ok)[None], tb, NEG))
    tbl = jnp.stack(tables, axis=0)
    return tbl.reshape(3 * (h // 2), 2, NA_WROWS * GRID_W, TM).astype(F32)


def _hy_pre_call(qkv, conv_w, conv_b, l):
    b, t, _ = qkv.shape
    w3 = conv_w.shape[1]
    w = w3 // 3
    nl = l // TM
    nt = t // TM
    hb = 16
    r = TM // hb

    def body(u_ref, up_ref, un_ref, cw_ref, cb_ref, x0_ref, z_ref):
        j = pl.program_id(1)
        prev_ok = jnp.where((j == 0) | (j == nl), 0.0, 1.0)
        next_ok = jnp.where((j == nl - 1) | (j == nt - 1), 0.0, 1.0)
        parts = []
        for p in range(3):
            sl = slice(p * w, (p + 1) * w)
            g = u_ref[0, :, sl].astype(F32)
            gp = up_ref[0, hb - 1:hb, sl].astype(F32) * prev_ok
            gn = un_ref[0, 0:1, sl].astype(F32) * next_ok
            dn, up = _shift_rows(g, gp, gn)
            parts.append(dn * cw_ref[0:1, sl] + g * cw_ref[1:2, sl] + up * cw_ref[2:3, sl]
                         + cb_ref[0:1, sl])
        x0_ref[0] = parts[0].astype(BF16)
        z_ref[0] = (parts[2] * parts[1]).astype(BF16)

    return pl.pallas_call(
        body, grid=(b, nt),
        in_specs=[pl.BlockSpec((1, TM, w3), lambda i, j: (i, j, 0)),
                  pl.BlockSpec((1, hb, w3), lambda i, j: (i, jnp.maximum(j * r - 1, 0), 0)),
                  pl.BlockSpec((1, hb, w3), lambda i, j: (i, jnp.minimum((j + 1) * r, nt * r - 1), 0)),
                  pl.BlockSpec((3, w3), lambda i, j: (0, 0)),
                  pl.BlockSpec((1, w3), lambda i, j: (0, 0))],
        out_specs=[pl.BlockSpec((1, TM, w), lambda i, j: (i, j, 0)),
                   pl.BlockSpec((1, TM, w), lambda i, j: (i, j, 0))],
        out_shape=[jax.ShapeDtypeStruct((b, t, w), BF16), jax.ShapeDtypeStruct((b, t, w), BF16)],
        compiler_params=_cparams(2), name="hy_pre",
    )(qkv, qkv, qkv, conv_w, conv_b.reshape(1, w3))


def _hy_filter_call(length, w1, b1, w2, b2, w3, b3, w4, freq):
    order = w2.shape[0]
    w = w4.shape[1] // 2
    tl = min(length, 512)
    hi = lax.Precision.HIGHEST
    t = np.linspace(0.0, 1.0, length, dtype=np.float64)[:, None]
    bands = (HY_EMB_DIM - 1) // 2
    ang = 2.0 * math.pi * np.arange(length, dtype=np.float64)[:, None] / length
    fq = np.linspace(1e-4, bands - 1, bands, dtype=np.float64)[None, :]
    emb = np.concatenate([t, np.cos(fq * ang), -np.sin(fq * ang)], axis=-1).astype(np.float32)
    emb = np.pad(emb, ((0, 0), (0, LANES - HY_EMB_DIM)))
    max_decay = math.log(HY_DECAY_TARGET) / HY_FAST_DECAY
    min_decay = math.log(HY_DECAY_TARGET) / HY_SLOW_DECAY
    deltas = np.linspace(min_decay, max_decay, w, dtype=np.float64)
    decay = np.exp(-t * np.abs(deltas)[None, :]).astype(np.float32)
    decay2 = np.concatenate([decay, decay], axis=1)
    w1p = jnp.pad(w1, ((0, LANES - HY_EMB_DIM), (0, 0)))

    def body(e_ref, d_ref, w1_ref, b1_ref, w2_ref, b2_ref, w3_ref, b3_ref, w4_ref, f_ref,
             h_ref, s_ref):
        i = pl.program_id(0)
        fr = f_ref[...]
        hdn = jnp.sin(fr * (jnp.dot(e_ref[...], w1_ref[...], precision=hi,
                                    preferred_element_type=F32) + b1_ref[...]))
        hdn = jnp.sin(fr * (jnp.dot(hdn, w2_ref[...], precision=hi,
                                    preferred_element_type=F32) + b2_ref[...]))
        hdn = jnp.sin(fr * (jnp.dot(hdn, w3_ref[...], precision=hi,
                                    preferred_element_type=F32) + b3_ref[...]))
        taps = jnp.dot(hdn, w4_ref[...], precision=hi, preferred_element_type=F32) * d_ref[...]
        row = lax.broadcasted_iota(jnp.int32, taps.shape, 0) + i * tl
        col = lax.broadcasted_iota(jnp.int32, taps.shape, 1)
        taps = jnp.where((row == 0) & (col >= w), 0.0, taps)
        h_ref[...] = taps

        @pl.when(i == 0)
        def _():
            s_ref[...] = jnp.zeros(s_ref.shape, F32)
        s_ref[...] += jnp.sum(jnp.abs(taps), axis=0, keepdims=True)

    full = lambda a: pl.BlockSpec(a.shape, lambda i: (0,) * a.ndim)
    ops = (w1p, b1.reshape(1, order), w2, b2.reshape(1, order), w3, b3.reshape(1, order), w4,
           freq.reshape(1, order))
    return pl.pallas_call(
        body, grid=(length // tl,),
        in_specs=[pl.BlockSpec((tl, LANES), lambda i: (i, 0)),
                  pl.BlockSpec((tl, 2 * w), lambda i: (i, 0))] + [full(a) for a in ops],
        out_specs=[pl.BlockSpec((tl, 2 * w), lambda i: (i, 0)),
                   pl.BlockSpec((1, 2 * w), lambda i: (0, 0))],
        out_shape=[jax.ShapeDtypeStruct((length, 2 * w), F32), jax.ShapeDtypeStruct((1, 2 * w), F32)],
        compiler_params=_cparams(1), name="hy_filter",
    )(jnp.asarray(emb), jnp.asarray(decay2), *ops)


def _dft_tables(l1, fb):
    n1 = 2 * l1
    n = n1 * LANES
    nf = -(-(n1 // 2 + 1) // fb) * fb
    f1 = np.arange(nf, dtype=np.float64)
    wgt = np.where((f1 == 0) | (f1 == n1 // 2), 1.0, np.where(f1 < n1 // 2, 2.0, 0.0))[None, :]
    t1 = np.arange(l1, dtype=np.float64)
    th1 = 2.0 * np.pi * np.outer(f1, t1) / n1
    fwd1 = np.concatenate([np.cos(th1), -np.sin(th1)], axis=0)
    inv1 = np.concatenate([wgt * np.cos(th1).T, -wgt * np.sin(th1).T], axis=1) / n
    f2 = np.arange(LANES, dtype=np.float64)
    t2 = np.arange(LANES, dtype=np.float64)
    fr = f1[:, None, None] + n1 * f2[None, :, None]
    th2 = 2.0 * np.pi * fr * t2[None, None, :] / n
    gr, gi = np.cos(th2), -np.sin(th2)
    gb = np.concatenate([np.concatenate([gr, -gi], axis=2),
                         np.concatenate([gi, gr], axis=2)], axis=1)
    hb = np.transpose(gb, (0, 2, 1))
    as_bf = lambda a: jnp.asarray(a.astype(np.float32)).astype(BF16)
    return as_bf(fwd1), as_bf(inv1), as_bf(gb), as_bf(hb)


def _hy_stage1_call(xv, fwd1):
    bx, l1, cols = xv.shape
    n2 = fwd1.shape[0]
    tn = min(cols, 4096)

    def body(f_ref, x_ref, o_ref):
        o_ref[0] = jnp.dot(f_ref[...], x_ref[0], preferred_element_type=F32).astype(BF16)

    return pl.pallas_call(
        body, grid=(bx, cols // tn),
        in_specs=[pl.BlockSpec((n2, l1), lambda i, j: (0, 0)),
                  pl.BlockSpec((1, l1, tn), lambda i, j: (i, 0, j))],
        out_specs=pl.BlockSpec((1, n2, tn), lambda i, j: (i, 0, j)),
        out_shape=jax.ShapeDtypeStruct((bx, n2, cols), BF16),
        compiler_params=_cparams(2), name="hy_dft1",
    )(fwd1, xv)


def _hy_filter_spec_call(a5, gb, sums, fb):
    n1, w2 = a5.shape[2], a5.shape[4]
    w = w2 // 2

    def body(a_ref, g_ref, s_ref, o_ref):
        sv = s_ref[...]
        inv = 1.0 / (sv[:, :w] + sv[:, w:])
        for k in range(fb):
            a = jnp.concatenate([a_ref[0, 0, k], a_ref[0, 1, k]], axis=0)
            z = jnp.dot(g_ref[k], a, preferred_element_type=F32)
            o_ref[k, 0] = (z[:LANES, :w] + z[:LANES, w:]) * inv
            o_ref[k, 1] = (z[LANES:, :w] - z[LANES:, w:]) * inv

    return pl.pallas_call(
        body, grid=(n1 // fb,),
        in_specs=[pl.BlockSpec((1, 2, fb, LANES, w2), lambda i: (0, 0, i, 0, 0)),
                  pl.BlockSpec((fb, 2 * LANES, 2 * LANES), lambda i: (i, 0, 0)),
                  pl.BlockSpec((1, w2), lambda i: (0, 0))],
        out_specs=pl.BlockSpec((fb, 2, LANES, w), lambda i: (i, 0, 0, 0)),
        out_shape=jax.ShapeDtypeStruct((n1, 2, LANES, w), F32),
        compiler_params=_cparams(1), name="hy_fspec",
    )(a5, gb, sums)


def _hy_stage23_call(a5, gb, hb, kf, fb):
    b, _, n1, _, w = a5.shape

    def body(a_ref, g_ref, h_ref, k_ref, o_ref):
        for k in range(fb):
            a = jnp.concatenate([a_ref[0, 0, k], a_ref[0, 1, k]], axis=0)
            z = jnp.dot(g_ref[k], a, preferred_element_type=F32)
            zr, zi = z[:LANES], z[LANES:]
            kr, ki = k_ref[k, 0], k_ref[k, 1]
            y = jnp.concatenate([zr * kr - zi * ki, zr * ki + zi * kr], axis=0).astype(BF16)
            cc = jnp.dot(h_ref[k], y, preferred_element_type=F32)
            o_ref[0, 0, k] = cc[:LANES].astype(BF16)
            o_ref[0, 1, k] = cc[LANES:].astype(BF16)

    return pl.pallas_call(
        body, grid=(n1 // fb, b),
        in_specs=[pl.BlockSpec((1, 2, fb, LANES, w), lambda i, j: (j, 0, i, 0, 0)),
                  pl.BlockSpec((fb, 2 * LANES, 2 * LANES), lambda i, j: (i, 0, 0)),
                  pl.BlockSpec((fb, 2 * LANES, 2 * LANES), lambda i, j: (i, 0, 0)),
                  pl.BlockSpec((fb, 2, LANES, w), lambda i, j: (i, 0, 0, 0))],
        out_specs=pl.BlockSpec((1, 2, fb, LANES, w), lambda i, j: (j, 0, i, 0, 0)),
        out_shape=jax.ShapeDtypeStruct(a5.shape, BF16),
        compiler_params=_cparams(2), name="hy_dft23",
    )(a5, gb, hb, kf)


def _hy_stage4_call(cv, inv1, x0v, zv, skip_t):
    b, n2, cols = cv.shape
    l1 = inv1.shape[0]
    tn = skip_t.shape[1]

    def body(f_ref, c_ref, x0_ref, z_ref, s_ref, o_ref):
        y = jnp.dot(f_ref[...], c_ref[0], preferred_element_type=F32)
        z = z_ref[0].astype(F32)
        o_ref[0] = (x0_ref[0].astype(F32) * (y + s_ref[...] * z)).astype(BF16)

    return pl.pallas_call(
        body, grid=(b, cols // tn),
        in_specs=[pl.BlockSpec((l1, n2), lambda i, j: (0, 0)),
                  pl.BlockSpec((1, n2, tn), lambda i, j: (i, 0, j)),
                  pl.BlockSpec((1, l1, tn), lambda i, j: (i, 0, j)),
                  pl.BlockSpec((1, l1, tn), lambda i, j: (i, 0, j)),
                  pl.BlockSpec((1, tn), lambda i, j: (0, 0))],
        out_specs=pl.BlockSpec((1, l1, tn), lambda i, j: (i, 0, j)),
        out_shape=jax.ShapeDtypeStruct((b, l1, cols), BF16),
        compiler_params=_cparams(2), name="hy_dft4",
    )(inv1, cv, x0v, zv, skip_t)


def _hy_dense_call(x0, z, taps, sums, skip):
    b, c, w = z.shape
    n = 2 * c
    th = 2.0 * np.pi * np.outer(np.arange(n, dtype=np.float64), np.arange(c, dtype=np.float64)) / n
    fwd = jnp.asarray(np.concatenate([np.cos(th), -np.sin(th)], axis=0).astype(np.float32)).astype(BF16)
    inv = jnp.asarray((np.concatenate([np.cos(th).T, -np.sin(th).T], axis=1) / n)
                      .astype(np.float32)).astype(BF16)

    def body(f_ref, i_ref, x0_ref, z_ref, t_ref, s_ref, k_ref, o_ref):
        sv = s_ref[...]
        nrm = 1.0 / (sv[:, :w] + sv[:, w:])
        tf = jnp.dot(f_ref[...], t_ref[...].astype(BF16), preferred_element_type=F32)
        kr = (tf[:n, :w] + tf[:n, w:]) * nrm
        ki = (tf[n:, :w] - tf[n:, w:]) * nrm
        zf = jnp.dot(f_ref[...], z_ref[0], preferred_element_type=F32)
        zr, zi = zf[:n], zf[n:]
        y = jnp.concatenate([zr * kr - zi * ki, zr * ki + zi * kr], axis=0).astype(BF16)
        yt = jnp.dot(i_ref[...], y, preferred_element_type=F32)
        o_ref[0] = (x0_ref[0].astype(F32) * (yt + k_ref[...] * z_ref[0].astype(F32))).astype(BF16)

    return pl.pallas_call(
        body, grid=(b,),
        in_specs=[pl.BlockSpec((2 * n, c), lambda i: (0, 0)),
                  pl.BlockSpec((c, 2 * n), lambda i: (0, 0)),
                  pl.BlockSpec((1, c, w), lambda i: (i, 0, 0)),
                  pl.BlockSpec((1, c, w), lambda i: (i, 0, 0)),
                  pl.BlockSpec((c, 2 * w), lambda i: (0, 0)),
                  pl.BlockSpec((1, 2 * w), lambda i: (0, 0)),
                  pl.BlockSpec((1, w), lambda i: (0, 0))],
        out_specs=pl.BlockSpec((1, c, w), lambda i: (i, 0, 0)),
        out_shape=jax.ShapeDtypeStruct((b, c, w), BF16),
        compiler_params=_cparams(1), name="hy_dense",
    )(fwd, inv, x0, z, taps, sums, skip.reshape(1, w))


def _hyena_long(x0, z, fparams, skip):
    b, l, w = z.shape
    l1 = l // LANES
    n1 = 2 * l1
    fb = min(8, n1)
    fwd1, inv1, gb, hb = _dft_tables(l1, fb)
    nf = gb.shape[0]
    taps, sums = _hy_filter_call(l, *fparams)
    ta = _hy_stage1_call(taps.astype(BF16).reshape(1, l1, LANES * 2 * w), fwd1)
    kf = _hy_filter_spec_call(ta.reshape(1, 2, nf, LANES, 2 * w), gb, sums, fb)
    za = _hy_stage1_call(z.reshape(b, l1, LANES * w), fwd1)
    cc = _hy_stage23_call(za.reshape(b, 2, nf, LANES, w), gb, hb, kf, fb)
    tn = 8 * w
    skip_t = jnp.tile(skip.reshape(1, w), (1, tn // w))
    y = _hy_stage4_call(cc.reshape(b, 2 * nf, LANES * w), inv1, x0.reshape(b, l1, LANES * w),
                        z.reshape(b, l1, LANES * w), skip_t)
    return y.reshape(b, l, w)


def _hyena_short(x0, z, fparams, skip):
    taps, sums = _hy_filter_call(z.shape[1], *fparams)
    return _hy_dense_call(x0, z, taps, sums, skip)


def _rope_tables(l, c):
    t = np.arange(l)
    row = (t // GRID_W).astype(np.float64)
    col = (t % GRID_W).astype(np.float64)
    n_pairs = HEAD_DIM // 4
    inv_freq = ROPE_THETA ** (-np.arange(n_pairs, dtype=np.float64) / n_pairs)
    ang = np.concatenate([row[:, None] * inv_freq, col[:, None] * inv_freq], axis=-1)
    cos = np.repeat(np.cos(ang), 2, axis=1)
    sin = np.repeat(np.sin(ang), 2, axis=1)
    even = (np.arange(HEAD_DIM) % 2 == 0)[None, :]
    se = np.where(even, -sin, 0.0)
    so = np.where(even, 0.0, sin)
    pad = lambda a, v: np.concatenate([a, np.full((c, HEAD_DIM), v)], axis=0)
    two = lambda a: jnp.asarray(np.concatenate([a, a], axis=1).astype(np.float32))
    return two(pad(cos, 1.0)), two(pad(se, 0.0)), two(pad(so, 0.0))


def _block_diag_ones():
    i = np.arange(2 * LANES)
    return jnp.asarray((i[:, None] // HEAD_DIM == i[None, :] // HEAD_DIM).astype(np.float32)).astype(BF16)


def _gain_rows(gains, scales):
    rows = [jnp.tile(g.astype(F32) * s, 2 * LANES // HEAD_DIM) for g, s in zip(gains, scales)]
    rows += [jnp.zeros((2 * LANES,), F32)] * (8 - len(rows))
    return jnp.stack(rows, axis=0)


def kernel(x, c, ctx, c_ctx, w_ada, b_ada, w_up, ffn_conv_w, ffn_conv_b, w_down, w_in_e, w_out_e, na_q_gain, na_k_gain, na_rpb, da_q_gain, da_k_gain, da_lambda_q1, da_lambda_k1, da_lambda_q2, da_lambda_k2, da_subln_gain, w_in_o, w_out_o, gqa_q_gain, gqa_k_gain, hy_conv_w, hy_conv_b, hy_w1, hy_b1, hy_w2, hy_b2, hy_w3, hy_b3, hy_w4, hy_freq, hy_skip):
    b, l, d = x.shape
    cl = ctx.shape[1]
    t = l + cl
    depth = w_ada.shape[0]
    f = w_down.shape[1]
    assert cl == TM and l % TK == 0 and (l // GRID_W) >= NA_WROWS + NA_QROWS
    assert w_in_e.shape[2] == 3072 and w_in_o.shape[2] == 2304 and d % LANES == 0
    scale = HEAD_DIM ** -0.5 * LOG2E

    rows = -(-(b + 1) // 8) * 8
    cs = jnp.zeros((rows, d), F32).at[:b].set(c).at[b].set(c_ctx)
    mods = _ada_call(cs, w_ada, b_ada)
    modarrs = []
    for layer in range(depth):
        lat = mods[layer, :b].reshape(b, 1, 6, d)
        cx = jnp.broadcast_to(mods[layer, b].reshape(1, 1, 6, d), (b, 1, 6, d))
        modarrs.append(jnp.concatenate([lat, cx], axis=1))

    ropes = _rope_tables(l, cl)
    bd = _block_diag_ones()
    x_all, h = _mod0_call(x, ctx, modarrs[0])

    for layer in range(depth):
        i = layer // 2
        if layer % 2 == 0:
            lam_init = 0.8 - 0.6 * math.exp(-0.3 * layer)
            gains = _gain_rows((na_q_gain[i], na_k_gain[i], da_q_gain[i], da_k_gain[i]),
                               (scale, 1.0, scale, 1.0))
            qkv, vat, qbt, vbt = _inproj_call(
                _inproj_even_body, "inproj_even", h, w_in_e[i].astype(BF16), bd, gains, ropes, 1536,
                ((b, 8, HEAD_DIM + ONES_ROWS, t), (b, 512, t), (b, 4, LANES + ONES_ROWS, t)))
            bias = _na_bias_table(na_rpb[i])
            ya = _na_call(qkv, vat, bias, l)
            lamv = jnp.stack([da_lambda_q1[i], da_lambda_k1[i], da_lambda_q2[i], da_lambda_k2[i]]).astype(F32)
            yb = _flash_call("da_attn", qbt, qkv, vbt, 1024 // 256, False, LANES,
                             _make_da_finish(lam_init),
                             (lamv, da_subln_gain[i].reshape(1, LANES).astype(F32)), 512, l)
            w_out = w_out_e[i].astype(BF16)
        else:
            gains = _gain_rows((gqa_q_gain[i], gqa_k_gain[i]), (scale, 1.0))
            qkv, qt, vt = _inproj_call(
                _inproj_odd_body, "inproj_odd", h, w_in_o[i].astype(BF16), bd, gains, ropes, 1792,
                ((b, 512, t), (b, 2, HEAD_DIM + ONES_ROWS, t)))
            ya = _flash_call("gqa_attn", qt, qkv, vt, 1536 // 128, True, HEAD_DIM,
                             _gqa_finish, (), 512, l)
            x0, z = _hy_pre_call(qkv, hy_conv_w[i], hy_conv_b[i], l)
            fparams = (hy_w1[i], hy_b1[i], hy_w2[i], hy_b2[i], hy_w3[i], hy_b3[i], hy_w4[i], hy_freq[i])
            yd_l = _hyena_long(x0[:, :l], z[:, :l], fparams, hy_skip[i])
            if layer < depth - 1:
                yd_c = _hyena_short(x0[:, l:], z[:, l:], fparams, hy_skip[i])
            else:
                yd_c = jnp.zeros((b, cl, x0.shape[2]), BF16)
            yb = jnp.concatenate([yd_l, yd_c], axis=1)
            w_out = w_out_o[i].astype(BF16)
        ka = ya.shape[2]
        x_all, h2 = _outproj_call(ya, yb, w_out[:ka], w_out[ka:], x_all, modarrs[layer])
        act = _ffn_up_call(h2, w_up[layer].astype(BF16), ffn_conv_w[layer], ffn_conv_b[layer])
        if layer == depth - 1:
            return _ffn_last_call(act, w_down[layer].astype(BF16), x_all, modarrs[layer])
        x_all, h = _ffn_down_call(act, w_down[layer].astype(BF16), x_all, modarrs[layer],
                                  modarrs[layer + 1])
```

```python
import functools
import math

import numpy as np
import jax
import jax.numpy as jnp
from jax import lax
from jax.experimental import pallas as pl
from jax.experimental.pallas import tpu as pltpu

F32 = jnp.float32
BF16 = jnp.bfloat16

HEAD_DIM = 64
GRID_W = 64
ROPE_THETA = 10000.0
EPS = 1e-6
NA_WIN_R = 8
NA_WIN_C = 16
HY_EMB_DIM = 33
HY_FAST_DECAY = 0.3
HY_SLOW_DECAY = 1.5
HY_DECAY_TARGET = 1e-2

LANES = 128
TM = 256
TK_DA = 512
TK_GQA = 256
NA_QROWS = TM // GRID_W
NA_WROWS = NA_QROWS + NA_WIN_R
ONES_ROWS = 16
NEG = -1e30
LOG2E = 1.4426950408889634
VMEM_LIMIT = 56 * 1024 * 1024


def _cparams(n_axes):
    return pltpu.CompilerParams(dimension_semantics=("arbitrary",) * n_axes,
                                vmem_limit_bytes=VMEM_LIMIT)


def _modulate(x, sh, sc):
    ms = jnp.mean(x * x, axis=-1, keepdims=True)
    return x * lax.rsqrt(ms + EPS) * (1.0 + sc) + sh


def _seg_norm(y, bd, gain):
    ss = jnp.dot((y * y).astype(BF16), bd, preferred_element_type=F32)
    return y * lax.rsqrt(ss * (1.0 / HEAD_DIM) + EPS) * gain


def _rope(y, c, se, so):
    return y * c + pltpu.roll(y, LANES - 1, 1) * se + pltpu.roll(y, 1, 1) * so


def _dot_nt(a, b):
    return lax.dot_general(a, b, (((1,), (1,)), ((), ())), preferred_element_type=F32)


def _shift_rows(g, prev_row, next_row):
    tm = g.shape[0]
    row = lax.broadcasted_iota(jnp.int32, g.shape, 0)
    dn = jnp.where(row == 0, prev_row, pltpu.roll(g, 1, 0))
    up = jnp.where(row == tm - 1, next_row, pltpu.roll(g, tm - 1, 0))
    return dn, up


def _ada_body(c_ref, w_ref, b_ref, o_ref):
    c = c_ref[...]
    a = (c / (1.0 + jnp.exp(-c))).astype(BF16)
    o_ref[0] = jnp.dot(a, w_ref[0].astype(BF16), preferred_element_type=F32) + b_ref[0]


def _ada_call(cs, w_ada, b_ada):
    depth, d, n6 = w_ada.shape
    rows = cs.shape[0]
    tn = 1536
    return pl.pallas_call(
        _ada_body, grid=(depth, n6 // tn),
        in_specs=[pl.BlockSpec((rows, d), lambda l, n: (0, 0)),
                  pl.BlockSpec((1, d, tn), lambda l, n: (l, 0, n)),
                  pl.BlockSpec((1, 1, tn), lambda l, n: (l, 0, n))],
        out_specs=pl.BlockSpec((1, rows, tn), lambda l, n: (l, 0, n)),
        out_shape=jax.ShapeDtypeStruct((depth, rows, n6), F32),
        compiler_params=_cparams(2), name="ada",
    )(cs, w_ada, b_ada.reshape(depth, 1, n6))


def _mod0_call(x, ctx, modarr):
    b, l, d = x.shape
    c = ctx.shape[1]
    t = l + c
    nl = l // TM

    def body(x_ref, c_ref, m_ref, xo_ref, h_ref):
        j = pl.program_id(1)
        xv = jnp.where(j < nl, x_ref[0], c_ref[0])
        xo_ref[0] = xv
        h_ref[0] = _modulate(xv, m_ref[0, 0, 0:1, :], m_ref[0, 0, 1:2, :]).astype(BF16)

    return pl.pallas_call(
        body, grid=(b, t // TM),
        in_specs=[pl.BlockSpec((1, TM, d), lambda i, j: (i, jnp.minimum(j, nl - 1), 0)),
                  pl.BlockSpec((1, TM, d), lambda i, j: (i, 0, 0)),
                  pl.BlockSpec((1, 1, 6, d), lambda i, j: (i, j // nl, 0, 0))],
        out_specs=[pl.BlockSpec((1, TM, d), lambda i, j: (i, j, 0)),
                   pl.BlockSpec((1, TM, d), lambda i, j: (i, j, 0))],
        out_shape=[jax.ShapeDtypeStruct((b, t, d), F32), jax.ShapeDtypeStruct((b, t, d), BF16)],
        compiler_params=_cparams(2), name="mod0",
    )(x, ctx, modarr)


def _inproj_even_body(a_ref, w_ref, bd_ref, g_ref, rc_ref, rse_ref, rso_ref,
                      o_ref, vat_ref, qbt_ref, vbt_ref):
    a = a_ref[0]
    bd = bd_ref[...]
    rc, rse, rso = rc_ref[...], rse_ref[...], rso_ref[...]
    plan = ((0, False, 0), (1, False, 512), (None, False, None), (2, True, None), (3, True, 1024),
            (None, False, None))
    for seg, (gain_row, rope, out_col) in enumerate(plan):
        y = jnp.dot(a, w_ref[:, seg * 512:(seg + 1) * 512], preferred_element_type=F32)
        for half in range(2):
            yy = y[:, half * 256:(half + 1) * 256]
            if gain_row is not None:
                yy = _seg_norm(yy, bd, g_ref[gain_row:gain_row + 1, :])
            for blk in range(2):
                z = yy[:, blk * LANES:(blk + 1) * LANES]
                if rope:
                    z = _rope(z, rc, rse, rso)
                m = half * 2 + blk
                if out_col is not None:
                    col = out_col + m * LANES
                    o_ref[0, :, col:col + LANES] = z.astype(BF16)
                    continue
                zt = z.T.astype(BF16)
                if seg == 2:
                    vat_ref[0, 2 * m, 0:HEAD_DIM, :] = zt[:HEAD_DIM]
                    vat_ref[0, 2 * m + 1, 0:HEAD_DIM, :] = zt[HEAD_DIM:]
                elif seg == 3:
                    qbt_ref[0, m * LANES:(m + 1) * LANES, :] = zt
                else:
                    vbt_ref[0, m, 0:LANES, :] = zt
    ones = jnp.ones((ONES_ROWS, TM), BF16)
    for hd in range(vat_ref.shape[1]):
        vat_ref[0, hd, HEAD_DIM:HEAD_DIM + ONES_ROWS, :] = ones
    for hd in range(vbt_ref.shape[1]):
        vbt_ref[0, hd, LANES:LANES + ONES_ROWS, :] = ones


def _inproj_odd_body(a_ref, w_ref, bd_ref, g_ref, rc_ref, rse_ref, rso_ref, o_ref, qt_ref, vt_ref):
    a = a_ref[0]
    bd = bd_ref[...]
    rc, rse, rso = rc_ref[...], rse_ref[...], rso_ref[...]
    y = jnp.dot(a, w_ref[:, 0:512], preferred_element_type=F32)
    for half in range(2):
        yy = _seg_norm(y[:, half * 256:(half + 1) * 256], bd, g_ref[0:1, :])
        for blk in range(2):
            z = _rope(yy[:, blk * LANES:(blk + 1) * LANES], rc, rse, rso)
            m = half * 2 + blk
            qt_ref[0, m * LANES:(m + 1) * LANES, :] = z.T.astype(BF16)
    y = jnp.dot(a, w_ref[:, 512:768], preferred_element_type=F32)
    k = _seg_norm(y[:, :LANES], bd_ref[0:LANES, 0:LANES], g_ref[1:2, 0:LANES])
    k = _rope(k, rc, rse, rso)
    kr = pltpu.roll(k, HEAD_DIM, 1)
    lo = lax.broadcasted_iota(jnp.int32, k.shape, 1) < HEAD_DIM
    o_ref[0, :, 1536:1664] = jnp.where(lo, k, kr).astype(BF16)
    o_ref[0, :, 1664:1792] = jnp.where(lo, kr, k).astype(BF16)
    vt = y[:, LANES:].T.astype(BF16)
    ones = jnp.ones((ONES_ROWS, TM), BF16)
    for hd in range(2):
        vt_ref[0, hd, 0:HEAD_DIM, :] = vt[hd * HEAD_DIM:(hd + 1) * HEAD_DIM]
        vt_ref[0, hd, HEAD_DIM:HEAD_DIM + ONES_ROWS, :] = ones
    for seg in range(3):
        y = jnp.dot(a, w_ref[:, 768 + seg * 512:768 + (seg + 1) * 512], preferred_element_type=F32)
        o_ref[0, :, seg * 512:(seg + 1) * 512] = y.astype(BF16)


def _inproj_call(body, name, h, w, bd, gains, ropes, n_out, t_shapes):
    b, t, d = h.shape
    n_in = w.shape[1]
    rc, rse, rso = ropes

    def t_spec(shape):
        nd = len(shape)
        return pl.BlockSpec((1,) + tuple(shape[1:-1]) + (TM,), lambda i, j: (i,) + (0,) * (nd - 2) + (j,))

    return pl.pallas_call(
        body, grid=(b, t // TM),
        in_specs=[pl.BlockSpec((1, TM, d), lambda i, j: (i, j, 0)),
                  pl.BlockSpec((d, n_in), lambda i, j: (0, 0)),
                  pl.BlockSpec(bd.shape, lambda i, j: (0, 0)),
                  pl.BlockSpec(gains.shape, lambda i, j: (0, 0)),
                  pl.BlockSpec((TM, LANES), lambda i, j: (j, 0)),
                  pl.BlockSpec((TM, LANES), lambda i, j: (j, 0)),
                  pl.BlockSpec((TM, LANES), lambda i, j: (j, 0))],
        out_specs=[pl.BlockSpec((1, TM, n_out), lambda i, j: (i, j, 0))] + [t_spec(s) for s in t_shapes],
        out_shape=[jax.ShapeDtypeStruct((b, t, n_out), BF16)]
        + [jax.ShapeDtypeStruct(s, BF16) for s in t_shapes],
        compiler_params=_cparams(2), name=name,
    )(h, w, bd, gains, rc, rse, rso)


def _ffn_up_call(h, w_up, conv_w, conv_b):
    b, t, d = h.shape
    f = w_up.shape[1] // 2
    nl = (t - TM) // TM
    nt = t // TM
    hb = 16
    r = TM // hb
    n_split = 1
    tn = f // n_split
    assert tn % LANES == 0
    chunks = [(c0, min(2 * LANES, tn - c0)) for c0 in range(0, tn, 2 * LANES)]

    def body(a_ref, ap_ref, an_ref, wg_ref, wv_ref, cw_ref, cb_ref, o_ref):
        j = pl.program_id(2)
        a = a_ref[0]
        zero = jnp.zeros((hb, d), BF16)
        ap = jnp.where((j == 0) | (j == nl), zero, ap_ref[0])
        an = jnp.where((j == nl - 1) | (j == nt - 1), zero, an_ref[0])
        a_ext = jnp.concatenate([ap, a, an], axis=0)
        for c0, cw in chunks:
            sl = slice(c0, c0 + cw)
            g = jnp.dot(a_ext, wg_ref[:, sl], preferred_element_type=F32)
            u = (g[hb - 1:hb - 1 + TM] * cw_ref[0:1, sl] + g[hb:hb + TM] * cw_ref[1:2, sl]
                 + g[hb + 1:hb + 1 + TM] * cw_ref[2:3, sl] + cb_ref[0:1, sl])
            v = jnp.dot(a, wv_ref[:, sl], preferred_element_type=F32)
            o_ref[0, :, sl] = ((u / (1.0 + jnp.exp(-u))) * v).astype(BF16)

    return pl.pallas_call(
        body, grid=(n_split, b, nt),
        in_specs=[pl.BlockSpec((1, TM, d), lambda n, i, j: (i, j, 0)),
                  pl.BlockSpec((1, hb, d), lambda n, i, j: (i, jnp.maximum(j * r - 1, 0), 0)),
                  pl.BlockSpec((1, hb, d), lambda n, i, j: (i, jnp.minimum((j + 1) * r, nt * r - 1), 0)),
                  pl.BlockSpec((d, tn), lambda n, i, j: (0, n)),
                  pl.BlockSpec((d, tn), lambda n, i, j: (0, n_split + n)),
                  pl.BlockSpec((3, tn), lambda n, i, j: (0, n)),
                  pl.BlockSpec((1, tn), lambda n, i, j: (0, n))],
        out_specs=pl.BlockSpec((1, TM, tn), lambda n, i, j: (i, j, n)),
        out_shape=jax.ShapeDtypeStruct((b, t, f), BF16),
        compiler_params=_cparams(3), name="ffn_up",
    )(h, h, h, w_up, w_up, conv_w, conv_b.reshape(1, f))


def _outproj_call(ya, yb, wa, wb, x, modarr):
    b, t, d = x.shape
    nl = (t - TM) // TM
    ka, kb = ya.shape[2], yb.shape[2]

    def body(ya_ref, yb_ref, wa_ref, wb_ref, x_ref, m_ref, xo_ref, h_ref):
        y = (jnp.dot(ya_ref[0], wa_ref[...], preferred_element_type=F32)
             + jnp.dot(yb_ref[0], wb_ref[...], preferred_element_type=F32))
        x1 = x_ref[0] + m_ref[0, 0, 2:3, :] * y
        xo_ref[0] = x1
        h_ref[0] = _modulate(x1, m_ref[0, 0, 3:4, :], m_ref[0, 0, 4:5, :]).astype(BF16)

    return pl.pallas_call(
        body, grid=(b, t // TM),
        in_specs=[pl.BlockSpec((1, TM, ka), lambda i, j: (i, j, 0)),
                  pl.BlockSpec((1, TM, kb), lambda i, j: (i, j, 0)),
                  pl.BlockSpec((ka, d), lambda i, j: (0, 0)),
                  pl.BlockSpec((kb, d), lambda i, j: (0, 0)),
                  pl.BlockSpec((1, TM, d), lambda i, j: (i, j, 0)),
                  pl.BlockSpec((1, 1, 6, d), lambda i, j: (i, j // nl, 0, 0))],
        out_specs=[pl.BlockSpec((1, TM, d), lambda i, j: (i, j, 0)),
                   pl.BlockSpec((1, TM, d), lambda i, j: (i, j, 0))],
        out_shape=[jax.ShapeDtypeStruct((b, t, d), F32), jax.ShapeDtypeStruct((b, t, d), BF16)],
        compiler_params=_cparams(2), name="outproj",
    )(ya, yb, wa, wb, x, modarr)


def _mix_ffn_up_call(ya, yb, wa, wb, x, modarr, w_up, conv_w, conv_b):
    b, t, d = x.shape
    f = w_up.shape[1] // 2
    nl = (t - TM) // TM
    nt = t // TM
    hb = 16
    r = TM // hb
    ka, kb = ya.shape[2], yb.shape[2]
    chunks = [(c0, min(2 * LANES, f - c0)) for c0 in range(0, f, 2 * LANES)]

    def body(ya_ref, yap_ref, yan_ref, yb_ref, ybp_ref, ybn_ref, x_ref, xp_ref, xn_ref,
             wa_ref, wb_ref, m_ref, wg_ref, wv_ref, cw_ref, cb_ref, xo_ref, o_ref):
        j = pl.program_id(1)
        ext = lambda p, c, n: jnp.concatenate([p[0], c[0], n[0]], axis=0)
        y = (jnp.dot(ext(yap_ref, ya_ref, yan_ref), wa_ref[...], preferred_element_type=F32)
             + jnp.dot(ext(ybp_ref, yb_ref, ybn_ref), wb_ref[...], preferred_element_type=F32))
        x1 = ext(xp_ref, x_ref, xn_ref) + m_ref[0, 0, 2:3, :] * y
        xo_ref[0] = x1[hb:hb + TM]
        hx = _modulate(x1, m_ref[0, 0, 3:4, :], m_ref[0, 0, 4:5, :]).astype(BF16)
        zero = jnp.zeros((hb, d), BF16)
        a = hx[hb:hb + TM]
        ap = jnp.where((j == 0) | (j == nl), zero, hx[0:hb])
        an = jnp.where((j == nl - 1) | (j == nt - 1), zero, hx[hb + TM:])
        a_ext = jnp.concatenate([ap, a, an], axis=0)
        for c0, cw in chunks:
            sl = slice(c0, c0 + cw)
            g = jnp.dot(a_ext, wg_ref[:, sl], preferred_element_type=F32)
            u = (g[hb - 1:hb - 1 + TM] * cw_ref[0:1, sl] + g[hb:hb + TM] * cw_ref[1:2, sl]
                 + g[hb + 1:hb + 1 + TM] * cw_ref[2:3, sl] + cb_ref[0:1, sl])
            v = jnp.dot(a, wv_ref[:, sl], preferred_element_type=F32)
            o_ref[0, :, sl] = ((u / (1.0 + jnp.exp(-u))) * v).astype(BF16)

    main = lambda w: pl.BlockSpec((1, TM, w), lambda i, j: (i, j, 0))
    prev = lambda w: pl.BlockSpec((1, hb, w), lambda i, j: (i, jnp.maximum(j * r - 1, 0), 0))
    nxt = lambda w: pl.BlockSpec((1, hb, w), lambda i, j: (i, jnp.minimum((j + 1) * r, nt * r - 1), 0))
    full = lambda a: pl.BlockSpec(a.shape, lambda i, j: (0,) * a.ndim)
    return pl.pallas_call(
        body, grid=(b, nt),
        in_specs=[main(ka), prev(ka), nxt(ka), main(kb), prev(kb), nxt(kb), main(d), prev(d), nxt(d),
                  full(wa), full(wb), pl.BlockSpec((1, 1, 6, d), lambda i, j: (i, j // nl, 0, 0)),
                  pl.BlockSpec((d, f), lambda i, j: (0, 0)), pl.BlockSpec((d, f), lambda i, j: (0, 1)),
                  pl.BlockSpec((3, f), lambda i, j: (0, 0)), pl.BlockSpec((1, f), lambda i, j: (0, 0))],
        out_specs=[pl.BlockSpec((1, TM, d), lambda i, j: (i, j, 0)),
                   pl.BlockSpec((1, TM, f), lambda i, j: (i, j, 0))],
        out_shape=[jax.ShapeDtypeStruct((b, t, d), F32), jax.ShapeDtypeStruct((b, t, f), BF16)],
        compiler_params=_cparams(2), name="mix_ffn_up",
    )(ya, ya, ya, yb, yb, yb, x, x, x, wa, wb, modarr, w_up, w_up, conv_w, conv_b.reshape(1, f))


def _ffn_down_call(act, w_down, x, modarr, modarr_next):
    b, t, d = x.shape
    f = w_down.shape[0]
    nl = (t - TM) // TM
    nt = t // TM

    def body(a_ref, wd_ref, x_ref, m_ref, mn_ref, xo_ref, h_ref):
        y = jnp.dot(a_ref[0], wd_ref[...], preferred_element_type=F32)
        x2 = x_ref[0] + m_ref[0, 0, 5:6, :] * y
        xo_ref[0] = x2
        h_ref[0] = _modulate(x2, mn_ref[0, 0, 0:1, :], mn_ref[0, 0, 1:2, :]).astype(BF16)

    return pl.pallas_call(
        body, grid=(b, nt),
        in_specs=[pl.BlockSpec((1, TM, f), lambda i, j: (i, j, 0)),
                  pl.BlockSpec((f, d), lambda i, j: (0, 0)),
                  pl.BlockSpec((1, TM, d), lambda i, j: (i, j, 0)),
                  pl.BlockSpec((1, 1, 6, d), lambda i, j: (i, j // nl, 0, 0)),
                  pl.BlockSpec((1, 1, 6, d), lambda i, j: (i, j // nl, 0, 0))],
        out_specs=[pl.BlockSpec((1, TM, d), lambda i, j: (i, j, 0)),
                   pl.BlockSpec((1, TM, d), lambda i, j: (i, j, 0))],
        out_shape=[jax.ShapeDtypeStruct((b, t, d), F32), jax.ShapeDtypeStruct((b, t, d), BF16)],
        compiler_params=_cparams(2), name="ffn_down",
    )(act, w_down, x, modarr, modarr_next)


def _ffn_last_call(act, w_down, x, modarr):
    b, t, d = x.shape
    f = w_down.shape[0]
    nl = (t - TM) // TM

    def body(a_ref, wd_ref, x_ref, m_ref, xo_ref):
        y = jnp.dot(a_ref[0], wd_ref[...], preferred_element_type=F32)
        xo_ref[0] = x_ref[0] + m_ref[0, 0, 5:6, :] * y

    return pl.pallas_call(
        body, grid=(b, nl),
        in_specs=[pl.BlockSpec((1, TM, f), lambda i, j: (i, j, 0)),
                  pl.BlockSpec((f, d), lambda i, j: (0, 0)),
                  pl.BlockSpec((1, TM, d), lambda i, j: (i, j, 0)),
                  pl.BlockSpec((1, 1, 6, d), lambda i, j: (i, 0, 0, 0))],
        out_specs=pl.BlockSpec((1, TM, d), lambda i, j: (i, j, 0)),
        out_shape=jax.ShapeDtypeStruct((b, nl * TM, d), F32),
        compiler_params=_cparams(2), name="ffn_last",
    )(act, w_down, x, modarr)


def _split_heads(q):
    lane = lax.broadcasted_iota(jnp.int32, q.shape, 1)
    zero = jnp.zeros_like(q)
    return jnp.where(lane < HEAD_DIM, q, zero), jnp.where(lane >= HEAD_DIM, q, zero)


def _flash_call(name, qt, qkv, vt, kcol, shared_kv, vrows, tk, finish, extra, out_cols, l):
    b, t, _ = qkv.shape
    c = t - l
    nl = l // TM
    n_chunks = l // tk
    n_steps = out_cols // (2 * LANES)
    n_kv = 1 if shared_kv else 2
    ns = 4
    nb = 2
    assert l % tk == 0 and n_chunks >= 2

    def body(q_ref, qn_ref, k_ref, vt_ref, *rest):
        extra_refs = rest[:-8]
        o_ref, s_ref, p_ref, sc_ref, pc_ref, acc_ref, m_ref, a_ref = rest[-8:]
        tile = pl.program_id(2)
        is_lat = tile < nl
        row = lax.broadcasted_iota(jnp.int32, (LANES, TM), 0)

        def streams(ref):
            out = []
            for blk in range(2):
                qb = ref[0, blk * LANES:(blk + 1) * LANES, :]
                out += [jnp.where(row < HEAD_DIM, qb, jnp.zeros_like(qb)),
                        jnp.where(row >= HEAD_DIM, qb, jnp.zeros_like(qb))]
            return out

        kv_of = lambda s: 0 if shared_kv else s // 2

        def scores(off, size, dst, qs):
            kcs = [k_ref[0, pl.ds(off, size), j * LANES:(j + 1) * LANES] for j in range(n_kv)]
            for s in range(ns):
                dst(s)[...] = jnp.dot(kcs[kv_of(s)], qs[s], preferred_element_type=F32)

        def softmax(src, dst, slot):
            for s in range(ns):
                st = src(s)[...]
                m = m_ref[s]
                mnew = jnp.maximum(m, jnp.max(st, axis=0, keepdims=True))
                a_ref[slot, s] = jnp.exp2(m - mnew)
                dst(s)[...] = jnp.exp2((st - mnew).astype(BF16))
                m_ref[s] = mnew

        def values(src, slot, off, size):
            vcs = [vt_ref[0, j, :, pl.ds(off, size)] for j in range(n_kv)]
            for s in range(ns):
                acc_ref[s] = a_ref[slot, s] * acc_ref[s] + jnp.dot(vcs[kv_of(s)], src(s)[...],
                                                                  preferred_element_type=F32)

        s_ctx, p_ctx = (lambda s: sc_ref.at[s]), (lambda s: pc_ref.at[s])
        s_buf = [(lambda s, i=i: s_ref.at[i, s]) for i in range(nb)]
        p_buf = [(lambda s, i=i: p_ref.at[i, s]) for i in range(nb)]

        def start():
            acc_ref[...] = jnp.zeros(acc_ref.shape, F32)
            m_ref[...] = jnp.full(m_ref.shape, NEG, F32)

        def finalize():
            outs = [acc_ref[s, 0:vrows, :] / acc_ref[s, vrows:vrows + 1, :] for s in range(ns)]
            o_ref[0] = finish(outs, *extra_refs)

        @pl.when(tile == 0)
        def _():
            qs = streams(q_ref)
            scores(l, c, s_ctx, qs)
            scores(0, tk, s_buf[0], qs)

        @pl.when(is_lat)
        def _():
            qs = streams(q_ref)
            qs_next = streams(qn_ref)
            start()
            softmax(s_ctx, p_ctx, nb)
            values(p_ctx, nb, l, c)
            scores(tk, tk, s_buf[1], qs)
            softmax(s_buf[0], p_buf[0], 0)

            for k in range(2, n_chunks):
                values(p_buf[(k - 2) % nb], (k - 2) % nb, (k - 2) * tk, tk)
                scores(k * tk, tk, s_buf[k % nb], qs)
                softmax(s_buf[(k - 1) % nb], p_buf[(k - 1) % nb], (k - 1) % nb)
            k = n_chunks
            values(p_buf[(k - 2) % nb], (k - 2) % nb, (k - 2) * tk, tk)
            scores(l, c, s_ctx, qs_next)
            softmax(s_buf[(k - 1) % nb], p_buf[(k - 1) % nb], (k - 1) % nb)
            scores(0, tk, s_buf[0], qs_next)
            values(p_buf[(k - 1) % nb], (k - 1) % nb, (k - 1) * tk, tk)
            finalize()

        @pl.when(jnp.logical_not(is_lat))
        def _():
            start()
            softmax(s_ctx, p_ctx, nb)
            values(p_ctx, nb, l, c)
            finalize()

    kw = n_kv * LANES
    extra_specs = [pl.BlockSpec(e.shape, lambda bi, h, i: (0,) * e.ndim) for e in extra]
    return pl.pallas_call(
        body, grid=(b, n_steps, t // TM),
        in_specs=[pl.BlockSpec((1, 2 * LANES, TM), lambda bi, h, i: (bi, h, i)),
                  pl.BlockSpec((1, 2 * LANES, TM), lambda bi, h, i: (bi, h, jnp.minimum(i + 1, nl))),
                  pl.BlockSpec((1, t, kw), lambda bi, h, i: (bi, 0, kcol + h)),
                  pl.BlockSpec((1, n_kv, vrows + ONES_ROWS, t), lambda bi, h, i: (bi, h, 0, 0))]
        + extra_specs,
        out_specs=pl.BlockSpec((1, TM, 2 * LANES), lambda bi, h, i: (bi, i, h)),
        out_shape=jax.ShapeDtypeStruct((b, t, out_cols), BF16),
        scratch_shapes=[pltpu.VMEM((nb, ns, tk, TM), F32), pltpu.VMEM((nb, ns, tk, TM), BF16),
                        pltpu.VMEM((ns, c, TM), F32), pltpu.VMEM((ns, c, TM), BF16),
                        pltpu.VMEM((ns, vrows + ONES_ROWS, TM), F32), pltpu.VMEM((ns, 1, TM), F32),
                        pltpu.VMEM((nb + 1, ns, 1, TM), F32)],
        compiler_params=_cparams(3), name=name,
    )(qt, qt, qkv, vt, *extra)


def _gqa_finish(outs):
    return jnp.concatenate(outs, axis=0).T.astype(BF16)


def _make_da_finish(lam_init):
    def finish(outs, lam_ref, subln_ref):
        lv = lam_ref[...]
        lam = (jnp.exp(jnp.sum(lv[0:1] * lv[1:2], keepdims=True))
               - jnp.exp(jnp.sum(lv[2:3] * lv[3:4], keepdims=True)) + lam_init)
        heads = []
        for hd in range(len(outs) // 2):
            o = (outs[2 * hd] - lam * outs[2 * hd + 1]).T
            o = o * lax.rsqrt(jnp.mean(o * o, axis=-1, keepdims=True) + EPS)
            heads.append((o * subln_ref[...] * (1.0 - lam_init)).astype(BF16))
        return jnp.concatenate(heads, axis=1)
    return finish


def _na_call(qkv, vt, bias, l):
    b, t, _ = qkv.shape
    c = t - l
    nl = l // TM
    rows = l // GRID_W
    win = NA_WROWS * GRID_W
    n_pairs = vt.shape[1] // 2

    nt = t // TM

    def window(tile):
        return pl.multiple_of(jnp.clip(tile * NA_QROWS - NA_WIN_R // 2, 0, rows - NA_WROWS) * GRID_W, 256)

    hp = 1

    def body(q0_ref, qn_ref, k_ref, vt_ref, b0_ref, bn_ref, o_ref, sw_ref, sc_ref):
        g = pl.program_id(2)

        def scores(q_ref, b_ref, tile, buf, with_win):
            for blk in range(hp):
                cols = slice(blk * LANES, (blk + 1) * LANES)
                qs = _split_heads(q_ref[0, :, cols])
                if with_win:
                    kw = k_ref[0, pl.ds(window(tile), win), cols]
                    for half in range(2):
                        sw_ref[buf, 2 * blk + half] = _dot_nt(kw, qs[half]) + b_ref[blk, half]
                kc = k_ref[0, l:l + c, cols]
                for half in range(2):
                    sc_ref[buf, 2 * blk + half] = _dot_nt(kc, qs[half])

        def finish(tile, buf, with_win):
            outs = []
            for s in range(2 * hp):
                parts = [(sc_ref[buf, s], vt_ref[0, s, :, l:l + c])]
                if with_win:
                    parts.append((sw_ref[buf, s], vt_ref[0, s, :, pl.ds(window(tile), win)]))
                m = functools.reduce(jnp.maximum, [jnp.max(sc, axis=0, keepdims=True) for sc, _ in parts])
                acc = sum(jnp.dot(v, jnp.exp2((sc - m).astype(BF16)), preferred_element_type=F32)
                          for sc, v in parts)
                outs.append(acc[:HEAD_DIM] / acc[HEAD_DIM:HEAD_DIM + 1])
            o_ref[0] = jnp.concatenate(outs, axis=0).T.astype(BF16)

        cur, nxt = g % 2, (g + 1) % 2

        @pl.when(g == 0)
        def _():
            scores(q0_ref, b0_ref, 0, 0, True)

        @pl.when(g + 1 < nl)
        def _():
            scores(qn_ref, bn_ref, g + 1, nxt, True)
            finish(g, cur, True)

        @pl.when(g + 1 == nl)
        def _():
            scores(qn_ref, bn_ref, g + 1, nxt, False)
            finish(g, cur, True)

        @pl.when(g == nl)
        def _():
            finish(g, cur, False)

    n_steps = n_pairs // hp
    wq = hp * LANES

    def bias_next(bi, h, g):
        tile = g + 1
        case = jnp.where(tile >= nl - 1, 2, 1)
        return (case * n_steps + h, 0, 0, 0)

    return pl.pallas_call(
        body, grid=(b, n_steps, nt),
        in_specs=[pl.BlockSpec((1, TM, wq), lambda bi, h, g: (bi, 0, h)),
                  pl.BlockSpec((1, TM, wq), lambda bi, h, g: (bi, jnp.minimum(g + 1, nt - 1), h)),
                  pl.BlockSpec((1, t, wq), lambda bi, h, g: (bi, 0, n_steps + h)),
                  pl.BlockSpec((1, 2 * hp, HEAD_DIM + ONES_ROWS, t), lambda bi, h, g: (bi, h, 0, 0)),
                  pl.BlockSpec((hp, 2, win, TM), lambda bi, h, g: (h, 0, 0, 0)),
                  pl.BlockSpec((hp, 2, win, TM), bias_next)],
        out_specs=pl.BlockSpec((1, TM, wq), lambda bi, h, g: (bi, g, h)),
        out_shape=jax.ShapeDtypeStruct((b, t, n_pairs * LANES), BF16),
        scratch_shapes=[pltpu.VMEM((2, 2 * hp, win, TM), F32), pltpu.VMEM((2, 2 * hp, c, TM), F32)],
        compiler_params=_cparams(3), name="na_attn",
    )(qkv, qkv, qkv, vt, bias, bias)


def _na_bias_table(rpb):
    h, n_ro, n_co = rpb.shape
    kr, qr = np.arange(NA_WROWS), np.arange(NA_QROWS)
    kc, qc = np.arange(GRID_W), np.arange(GRID_W)
    cs = np.clip(qc - NA_WIN_C // 2, 0, GRID_W - NA_WIN_C)
    col_ok = (kc[:, None] >= cs[None, :]) & (kc[:, None] < cs[None, :] + NA_WIN_C)
    co = kc[:, None] - qc[None, :] + NA_WIN_C - 1
    col_sel = (co[None] == np.arange(n_co)[:, None, None]).astype(np.float32)
    tables = []
    for d, rel in ((0, np.zeros_like(qr)), (NA_WIN_R // 2, qr), (NA_WIN_R, np.full_like(qr, NA_WIN_R // 2))):
        row_ok = (kr[:, None] >= rel[None, :]) & (kr[:, None] < rel[None, :] + NA_WIN_R)
        ro = kr[:, None] - qr[None, :] - d + NA_WIN_R - 1
        row_sel = (ro[:, :, None] == np.arange(n_ro)[None, None, :]).astype(np.float32)
        tb = jnp.einsum('kqa,hab,bcd->hkcqd', jnp.asarray(row_sel), rpb.astype(F32) * LOG2E,
                        jnp.asarray(col_sel), precision=lax.Precision.HIGHEST)
        ok = row_ok[:, None, :, None] & col_ok[None, :, None, :]
        tables.append(jnp.where(jnp.asarray(ok)[None], tb, NEG))
    tbl = jnp.stack(tables, axis=0)
    return tbl.reshape(3 * (h // 2), 2, NA_WROWS * GRID_W, TM).astype(F32)


def _hy_pre_call(qkv, conv_w, conv_b, l):
    b, t, _ = qkv.shape
    w3 = conv_w.shape[1]
    w = w3 // 3
    nl = l // TM
    nt = t // TM
    hb = 16
    r = TM // hb

    def body(u_ref, up_ref, un_ref, cw_ref, cb_ref, x0_ref, z_ref):
        j = pl.program_id(1)
        prev_ok = jnp.where((j == 0) | (j == nl), 0.0, 1.0)
        next_ok = jnp.where((j == nl - 1) | (j == nt - 1), 0.0, 1.0)
        parts = []
        for p in range(3):
            sl = slice(p * w, (p + 1) * w)
            g = u_ref[0, :, sl].astype(F32)
            gp = up_ref[0, hb - 1:hb, sl].astype(F32) * prev_ok
            gn = un_ref[0, 0:1, sl].astype(F32) * next_ok
            dn, up = _shift_rows(g, gp, gn)
            parts.append(dn * cw_ref[0:1, sl] + g * cw_ref[1:2, sl] + up * cw_ref[2:3, sl]
                         + cb_ref[0:1, sl])
        x0_ref[0] = parts[0].astype(BF16)
        z_ref[0] = (parts[2] * parts[1]).astype(BF16)

    return pl.pallas_call(
        body, grid=(b, nt),
        in_specs=[pl.BlockSpec((1, TM, w3), lambda i, j: (i, j, 0)),
                  pl.BlockSpec((1, hb, w3), lambda i, j: (i, jnp.maximum(j * r - 1, 0), 0)),
                  pl.BlockSpec((1, hb, w3), lambda i, j: (i, jnp.minimum((j + 1) * r, nt * r - 1), 0)),
                  pl.BlockSpec((3, w3), lambda i, j: (0, 0)),
                  pl.BlockSpec((1, w3), lambda i, j: (0, 0))],
        out_specs=[pl.BlockSpec((1, TM, w), lambda i, j: (i, j, 0)),
                   pl.BlockSpec((1, TM, w), lambda i, j: (i, j, 0))],
        out_shape=[jax.ShapeDtypeStruct((b, t, w), BF16), jax.ShapeDtypeStruct((b, t, w), BF16)],
        compiler_params=_cparams(2), name="hy_pre",
    )(qkv, qkv, qkv, conv_w, conv_b.reshape(1, w3))


def _hy_filter_call(length, w1, b1, w2, b2, w3, b3, w4, freq):
    order = w2.shape[0]
    w = w4.shape[1] // 2
    tl = min(length, 512)
    hi = lax.Precision.HIGHEST
    t = np.linspace(0.0, 1.0, length, dtype=np.float64)[:, None]
    bands = (HY_EMB_DIM - 1) // 2
    ang = 2.0 * math.pi * np.arange(length, dtype=np.float64)[:, None] / length
    fq = np.linspace(1e-4, bands - 1, bands, dtype=np.float64)[None, :]
    emb = np.concatenate([t, np.cos(fq * ang), -np.sin(fq * ang)], axis=-1).astype(np.float32)
    emb = np.pad(emb, ((0, 0), (0, LANES - HY_EMB_DIM)))
    max_decay = math.log(HY_DECAY_TARGET) / HY_FAST_DECAY
    min_decay = math.log(HY_DECAY_TARGET) / HY_SLOW_DECAY
    deltas = np.linspace(min_decay, max_decay, w, dtype=np.float64)
    decay = np.exp(-t * np.abs(deltas)[None, :]).astype(np.float32)
    decay2 = np.concatenate([decay, decay], axis=1)
    w1p = jnp.pad(w1, ((0, LANES - HY_EMB_DIM), (0, 0)))

    def body(e_ref, d_ref, w1_ref, b1_ref, w2_ref, b2_ref, w3_ref, b3_ref, w4_ref, f_ref,
             h_ref, s_ref):
        i = pl.program_id(0)
        fr = f_ref[...]
        hdn = jnp.sin(fr * (jnp.dot(e_ref[...], w1_ref[...], precision=hi,
                                    preferred_element_type=F32) + b1_ref[...]))
        hdn = jnp.sin(fr * (jnp.dot(hdn, w2_ref[...], precision=hi,
                                    preferred_element_type=F32) + b2_ref[...]))
        hdn = jnp.sin(fr * (jnp.dot(hdn, w3_ref[...], precision=hi,
                                    preferred_element_type=F32) + b3_ref[...]))
        taps = jnp.dot(hdn, w4_ref[...], precision=hi, preferred_element_type=F32) * d_ref[...]
        row = lax.broadcasted_iota(jnp.int32, taps.shape, 0) + i * tl
        col = lax.broadcasted_iota(jnp.int32, taps.shape, 1)
        taps = jnp.where((row == 0) & (col >= w), 0.0, taps)
        h_ref[...] = taps

        @pl.when(i == 0)
        def _():
            s_ref[...] = jnp.zeros(s_ref.shape, F32)
        s_ref[...] += jnp.sum(jnp.abs(taps), axis=0, keepdims=True)

    full = lambda a: pl.BlockSpec(a.shape, lambda i: (0,) * a.ndim)
    ops = (w1p, b1.reshape(1, order), w2, b2.reshape(1, order), w3, b3.reshape(1, order), w4,
           freq.reshape(1, order))
    return pl.pallas_call(
        body, grid=(length // tl,),
        in_specs=[pl.BlockSpec((tl, LANES), lambda i: (i, 0)),
                  pl.BlockSpec((tl, 2 * w), lambda i: (i, 0))] + [full(a) for a in ops],
        out_specs=[pl.BlockSpec((tl, 2 * w), lambda i: (i, 0)),
                   pl.BlockSpec((1, 2 * w), lambda i: (0, 0))],
        out_shape=[jax.ShapeDtypeStruct((length, 2 * w), F32), jax.ShapeDtypeStruct((1, 2 * w), F32)],
        compiler_params=_cparams(1), name="hy_filter",
    )(jnp.asarray(emb), jnp.asarray(decay2), *ops)


def _dft_tables(l1, fb):
    n1 = 2 * l1
    n = n1 * LANES
    nf = -(-(n1 // 2 + 1) // fb) * fb
    f1 = np.arange(nf, dtype=np.float64)
    wgt = np.where((f1 == 0) | (f1 == n1 // 2), 1.0, np.where(f1 < n1 // 2, 2.0, 0.0))[None, :]
    t1 = np.arange(l1, dtype=np.float64)
    th1 = 2.0 * np.pi * np.outer(f1, t1) / n1
    fwd1 = np.concatenate([np.cos(th1), -np.sin(th1)], axis=0)
    inv1 = np.concatenate([wgt * np.cos(th1).T, -wgt * np.sin(th1).T], axis=1) / n
    f2 = np.arange(LANES, dtype=np.float64)
    t2 = np.arange(LANES, dtype=np.float64)
    fr = f1[:, None, None] + n1 * f2[None, :, None]
    th2 = 2.0 * np.pi * fr * t2[None, None, :] / n
    gr, gi = np.cos(th2), -np.sin(th2)
    gb = np.concatenate([np.concatenate([gr, -gi], axis=2),
                         np.concatenate([gi, gr], axis=2)], axis=1)
    hb = np.transpose(gb, (0, 2, 1))
    as_bf = lambda a: jnp.asarray(a.astype(np.float32)).astype(BF16)
    return as_bf(fwd1), as_bf(inv1), as_bf(gb), as_bf(hb)


def _hy_stage1_call(xv, fwd1):
    bx, l1, cols = xv.shape
    n2 = fwd1.shape[0]
    tn = min(cols, 4096)

    def body(f_ref, x_ref, o_ref):
        o_ref[0] = jnp.dot(f_ref[...], x_ref[0], preferred_element_type=F32).astype(BF16)

    return pl.pallas_call(
        body, grid=(bx, cols // tn),
        in_specs=[pl.BlockSpec((n2, l1), lambda i, j: (0, 0)),
                  pl.BlockSpec((1, l1, tn), lambda i, j: (i, 0, j))],
        out_specs=pl.BlockSpec((1, n2, tn), lambda i, j: (i, 0, j)),
        out_shape=jax.ShapeDtypeStruct((bx, n2, cols), BF16),
        compiler_params=_cparams(2), name="hy_dft1",
    )(fwd1, xv)


def _hy_filter_spec_call(a5, gb, sums, fb):
    n1, w2 = a5.shape[2], a5.shape[4]
    w = w2 // 2

    def body(a_ref, g_ref, s_ref, o_ref):
        sv = s_ref[...]
        inv = 1.0 / (sv[:, :w] + sv[:, w:])
        for k in range(fb):
            a = jnp.concatenate([a_ref[0, 0, k], a_ref[0, 1, k]], axis=0)
            z = jnp.dot(g_ref[k], a, preferred_element_type=F32)
            o_ref[k, 0] = (z[:LANES, :w] + z[:LANES, w:]) * inv
            o_ref[k, 1] = (z[LANES:, :w] - z[LANES:, w:]) * inv

    return pl.pallas_call(
        body, grid=(n1 // fb,),
        in_specs=[pl.BlockSpec((1, 2, fb, LANES, w2), lambda i: (0, 0, i, 0, 0)),
                  pl.BlockSpec((fb, 2 * LANES, 2 * LANES), lambda i: (i, 0, 0)),
                  pl.BlockSpec((1, w2), lambda i: (0, 0))],
        out_specs=pl.BlockSpec((fb, 2, LANES, w), lambda i: (i, 0, 0, 0)),
        out_shape=jax.ShapeDtypeStruct((n1, 2, LANES, w), F32),
        compiler_params=_cparams(1), name="hy_fspec",
    )(a5, gb, sums)


def _hy_stage23_call(a5, gb, hb, kf, fb):
    b, _, n1, _, w = a5.shape

    def body(a_ref, g_ref, h_ref, k_ref, o_ref):
        for k in range(fb):
            a = jnp.concatenate([a_ref[0, 0, k], a_ref[0, 1, k]], axis=0)
            z = jnp.dot(g_ref[k], a, preferred_element_type=F32)
            zr, zi = z[:LANES], z[LANES:]
            kr, ki = k_ref[k, 0], k_ref[k, 1]
            y = jnp.concatenate([zr * kr - zi * ki, zr * ki + zi * kr], axis=0).astype(BF16)
            cc = jnp.dot(h_ref[k], y, preferred_element_type=F32)
            o_ref[0, 0, k] = cc[:LANES].astype(BF16)
            o_ref[0, 1, k] = cc[LANES:].astype(BF16)

    return pl.pallas_call(
        body, grid=(n1 // fb, b),
        in_specs=[pl.BlockSpec((1, 2, fb, LANES, w), lambda i, j: (j, 0, i, 0, 0)),
                  pl.BlockSpec((fb, 2 * LANES, 2 * LANES), lambda i, j: (i, 0, 0)),
                  pl.BlockSpec((fb, 2 * LANES, 2 * LANES), lambda i, j: (i, 0, 0)),
                  pl.BlockSpec((fb, 2, LANES, w), lambda i, j: (i, 0, 0, 0))],
        out_specs=pl.BlockSpec((1, 2, fb, LANES, w), lambda i, j: (j, 0, i, 0, 0)),
        out_shape=jax.ShapeDtypeStruct(a5.shape, BF16),
        compiler_params=_cparams(2), name="hy_dft23",
    )(a5, gb, hb, kf)


def _hy_stage4_call(cv, inv1, x0v, zv, skip_t):
    b, n2, cols = cv.shape
    l1 = inv1.shape[0]
    tn = skip_t.shape[1]

    def body(f_ref, c_ref, x0_ref, z_ref, s_ref, o_ref):
        y = jnp.dot(f_ref[...], c_ref[0], preferred_element_type=F32)
        z = z_ref[0].astype(F32)
        o_ref[0] = (x0_ref[0].astype(F32) * (y + s_ref[...] * z)).astype(BF16)

    return pl.pallas_call(
        body, grid=(b, cols // tn),
        in_specs=[pl.BlockSpec((l1, n2), lambda i, j: (0, 0)),
                  pl.BlockSpec((1, n2, tn), lambda i, j: (i, 0, j)),
                  pl.BlockSpec((1, l1, tn), lambda i, j: (i, 0, j)),
                  pl.BlockSpec((1, l1, tn), lambda i, j: (i, 0, j)),
                  pl.BlockSpec((1, tn), lambda i, j: (0, 0))],
        out_specs=pl.BlockSpec((1, l1, tn), lambda i, j: (i, 0, j)),
        out_shape=jax.ShapeDtypeStruct((b, l1, cols), BF16),
        compiler_params=_cparams(2), name="hy_dft4",
    )(inv1, cv, x0v, zv, skip_t)


def _hy_dense_call(x0, z, taps, sums, skip):
    b, c, w = z.shape
    n = 2 * c
    th = 2.0 * np.pi * np.outer(np.arange(n, dtype=np.float64), np.arange(c, dtype=np.float64)) / n
    fwd = jnp.asarray(np.concatenate([np.cos(th), -np.sin(th)], axis=0).astype(np.float32)).astype(BF16)
    inv = jnp.asarray((np.concatenate([np.cos(th).T, -np.sin(th).T], axis=1) / n)
                      .astype(np.float32)).astype(BF16)

    def body(f_ref, i_ref, x0_ref, z_ref, t_ref, s_ref, k_ref, o_ref):
        sv = s_ref[...]
        nrm = 1.0 / (sv[:, :w] + sv[:, w:])
        tf = jnp.dot(f_ref[...], t_ref[...].astype(BF16), preferred_element_type=F32)
        kr = (tf[:n, :w] + tf[:n, w:]) * nrm
        ki = (tf[n:, :w] - tf[n:, w:]) * nrm
        zf = jnp.dot(f_ref[...], z_ref[0], preferred_element_type=F32)
        zr, zi = zf[:n], zf[n:]
        y = jnp.concatenate([zr * kr - zi * ki, zr * ki + zi * kr], axis=0).astype(BF16)
        yt = jnp.dot(i_ref[...], y, preferred_element_type=F32)
        o_ref[0] = (x0_ref[0].astype(F32) * (yt + k_ref[...] * z_ref[0].astype(F32))).astype(BF16)

    return pl.pallas_call(
        body, grid=(b,),
        in_specs=[pl.BlockSpec((2 * n, c), lambda i: (0, 0)),
                  pl.BlockSpec((c, 2 * n), lambda i: (0, 0)),
                  pl.BlockSpec((1, c, w), lambda i: (i, 0, 0)),
                  pl.BlockSpec((1, c, w), lambda i: (i, 0, 0)),
                  pl.BlockSpec((c, 2 * w), lambda i: (0, 0)),
                  pl.BlockSpec((1, 2 * w), lambda i: (0, 0)),
                  pl.BlockSpec((1, w), lambda i: (0, 0))],
        out_specs=pl.BlockSpec((1, c, w), lambda i: (i, 0, 0)),
        out_shape=jax.ShapeDtypeStruct((b, c, w), BF16),
        compiler_params=_cparams(1), name="hy_dense",
    )(fwd, inv, x0, z, taps, sums, skip.reshape(1, w))


def _hyena_long(x0, z, fparams, skip):
    b, l, w = z.shape
    l1 = l // LANES
    n1 = 2 * l1
    fb = min(8, n1)
    fwd1, inv1, gb, hb = _dft_tables(l1, fb)
    nf = gb.shape[0]
    taps, sums = _hy_filter_call(l, *fparams)
    ta = _hy_stage1_call(taps.astype(BF16).reshape(1, l1, LANES * 2 * w), fwd1)
    kf = _hy_filter_spec_call(ta.reshape(1, 2, nf, LANES, 2 * w), gb, sums, fb)
    za = _hy_stage1_call(z.reshape(b, l1, LANES * w), fwd1)
    cc = _hy_stage23_call(za.reshape(b, 2, nf, LANES, w), gb, hb, kf, fb)
    tn = 8 * w
    skip_t = jnp.tile(skip.reshape(1, w), (1, tn // w))
    y = _hy_stage4_call(cc.reshape(b, 2 * nf, LANES * w), inv1, x0.reshape(b, l1, LANES * w),
                        z.reshape(b, l1, LANES * w), skip_t)
    return y.reshape(b, l, w)


def _hyena_short(x0, z, fparams, skip):
    taps, sums = _hy_filter_call(z.shape[1], *fparams)
    return _hy_dense_call(x0, z, taps, sums, skip)


def _rope_tables(l, c):
    t = np.arange(l)
    row = (t // GRID_W).astype(np.float64)
    col = (t % GRID_W).astype(np.float64)
    n_pairs = HEAD_DIM // 4
    inv_freq = ROPE_THETA ** (-np.arange(n_pairs, dtype=np.float64) / n_pairs)
    ang = np.concatenate([row[:, None] * inv_freq, col[:, None] * inv_freq], axis=-1)
    cos = np.repeat(np.cos(ang), 2, axis=1)
    sin = np.repeat(np.sin(ang), 2, axis=1)
    even = (np.arange(HEAD_DIM) % 2 == 0)[None, :]
    se = np.where(even, -sin, 0.0)
    so = np.where(even, 0.0, sin)
    pad = lambda a, v: np.concatenate([a, np.full((c, HEAD_DIM), v)], axis=0)
    two = lambda a: jnp.asarray(np.concatenate([a, a], axis=1).astype(np.float32))
    return two(pad(cos, 1.0)), two(pad(se, 0.0)), two(pad(so, 0.0))


def _block_diag_ones():
    i = np.arange(2 * LANES)
    return jnp.asarray((i[:, None] // HEAD_DIM == i[None, :] // HEAD_DIM).astype(np.float32)).astype(BF16)


def _gain_rows(gains, scales):
    rows = [jnp.tile(g.astype(F32) * s, 2 * LANES // HEAD_DIM) for g, s in zip(gains, scales)]
    rows += [jnp.zeros((2 * LANES,), F32)] * (8 - len(rows))
    return jnp.stack(rows, axis=0)


def kernel(x, c, ctx, c_ctx, w_ada, b_ada, w_up, ffn_conv_w, ffn_conv_b, w_down, w_in_e, w_out_e, na_q_gain, na_k_gain, na_rpb, da_q_gain, da_k_gain, da_lambda_q1, da_lambda_k1, da_lambda_q2, da_lambda_k2, da_subln_gain, w_in_o, w_out_o, gqa_q_gain, gqa_k_gain, hy_conv_w, hy_conv_b, hy_w1, hy_b1, hy_w2, hy_b2, hy_w3, hy_b3, hy_w4, hy_freq, hy_skip):
    b, l, d = x.shape
    cl = ctx.shape[1]
    t = l + cl
    depth = w_ada.shape[0]
    f = w_down.shape[1]
    assert cl == TM and (l // GRID_W) >= NA_WROWS + NA_QROWS
    assert w_in_e.shape[2] == 3072 and w_in_o.shape[2] == 2304 and d % LANES == 0
    scale = HEAD_DIM ** -0.5 * LOG2E

    rows = -(-(b + 1) // 8) * 8
    cs = jnp.zeros((rows, d), F32).at[:b].set(c).at[b].set(c_ctx)
    mods = _ada_call(cs, w_ada, b_ada)
    modarrs = []
    for layer in range(depth):
        lat = mods[layer, :b].reshape(b, 1, 6, d)
        cx = jnp.broadcast_to(mods[layer, b].reshape(1, 1, 6, d), (b, 1, 6, d))
        modarrs.append(jnp.concatenate([lat, cx], axis=1))

    ropes = _rope_tables(l, cl)
    bd = _block_diag_ones()
    x_all, h = _mod0_call(x, ctx, modarrs[0])

    for layer in range(depth):
        i = layer // 2
        if layer % 2 == 0:
            lam_init = 0.8 - 0.6 * math.exp(-0.3 * layer)
            gains = _gain_rows((na_q_gain[i], na_k_gain[i], da_q_gain[i], da_k_gain[i]),
                               (scale, 1.0, scale, 1.0))
            qkv, vat, qbt, vbt = _inproj_call(
                _inproj_even_body, "inproj_even", h, w_in_e[i].astype(BF16), bd, gains, ropes, 1536,
                ((b, 8, HEAD_DIM + ONES_ROWS, t), (b, 512, t), (b, 4, LANES + ONES_ROWS, t)))
            bias = _na_bias_table(na_rpb[i])
            ya = _na_call(qkv, vat, bias, l)
            lamv = jnp.stack([da_lambda_q1[i], da_lambda_k1[i], da_lambda_q2[i], da_lambda_k2[i]]).astype(F32)
            yb = _flash_call("da_attn", qbt, qkv, vbt, 1024 // 256, False, LANES, TK_DA,
                             _make_da_finish(lam_init),
                             (lamv, da_subln_gain[i].reshape(1, LANES).astype(F32)), 512, l)
            w_out = w_out_e[i].astype(BF16)
        else:
            gains = _gain_rows((gqa_q_gain[i], gqa_k_gain[i]), (scale, 1.0))
            qkv, qt, vt = _inproj_call(
                _inproj_odd_body, "inproj_odd", h, w_in_o[i].astype(BF16), bd, gains, ropes, 1792,
                ((b, 512, t), (b, 2, HEAD_DIM + ONES_ROWS, t)))
            ya = _flash_call("gqa_attn", qt, qkv, vt, 1536 // 128, True, HEAD_DIM, TK_GQA,
                             _gqa_finish, (), 512, l)
            x0, z = _hy_pre_call(qkv, hy_conv_w[i], hy_conv_b[i], l)
            fparams = (hy_w1[i], hy_b1[i], hy_w2[i], hy_b2[i], hy_w3[i], hy_b3[i], hy_w4[i], hy_freq[i])
            yd_l = _hyena_long(x0[:, :l], z[:, :l], fparams, hy_skip[i])
            if layer < depth - 1:
                yd_c = _hyena_short(x0[:, l:], z[:, l:], fparams, hy_skip[i])
            else:
                yd_c = jnp.zeros((b, cl, x0.shape[2]), BF16)
            yb = jnp.concatenate([yd_l, yd_c], axis=1)
            w_out = w_out_o[i].astype(BF16)
        ka = ya.shape[2]
        x_all, act = _mix_ffn_up_call(ya, yb, w_out[:ka], w_out[ka:], x_all, modarrs[layer],
                                      w_up[layer].astype(BF16), ffn_conv_w[layer], ffn_conv_b[layer])
        if layer == depth - 1:
            return _ffn_last_call(act, w_down[layer].astype(BF16), x_all, modarrs[layer])
        x_all, h = _ffn_down_call(act, w_down[layer].astype(BF16), x_all, modarrs[layer],
                                  modarrs[layer + 1])
```

```python
import functools
import math

import numpy as np
import jax
import jax.numpy as jnp
from jax import lax
from jax.experimental import pallas as pl
from jax.experimental.pallas import tpu as pltpu

F32 = jnp.float32
BF16 = jnp.bfloat16

HEAD_DIM = 64
GRID_W = 64
ROPE_THETA = 10000.0
EPS = 1e-6
NA_WIN_R = 8
NA_WIN_C = 16
HY_EMB_DIM = 33
HY_FAST_DECAY = 0.3
HY_SLOW_DECAY = 1.5
HY_DECAY_TARGET = 1e-2

LANES = 128
TM = 256
TK_DA = 512
TK_GQA = 256
NA_QROWS = TM // GRID_W
NA_WROWS = NA_QROWS + NA_WIN_R
ONES_ROWS = 16
NEG = -1e30
LOG2E = 1.4426950408889634
VMEM_LIMIT = 56 * 1024 * 1024


def _cparams(n_axes):
    return pltpu.CompilerParams(dimension_semantics=("arbitrary",) * n_axes,
                                vmem_limit_bytes=VMEM_LIMIT)


def _modulate(x, sh, sc):
    ms = jnp.mean(x * x, axis=-1, keepdims=True)
    return x * lax.rsqrt(ms + EPS) * (1.0 + sc) + sh


def _seg_norm(y, bd, gain):
    ss = jnp.dot((y * y).astype(BF16), bd, preferred_element_type=F32)
    return y * lax.rsqrt(ss * (1.0 / HEAD_DIM) + EPS) * gain


def _rope(y, c, se, so):
    return y * c + pltpu.roll(y, LANES - 1, 1) * se + pltpu.roll(y, 1, 1) * so


def _dot_nt(a, b):
    return lax.dot_general(a, b, (((1,), (1,)), ((), ())), preferred_element_type=F32)


def _ada_body(c_ref, w_ref, b_ref, o_ref):
    c = c_ref[...]
    a = (c / (1.0 + jnp.exp(-c))).astype(BF16)
    o_ref[0] = jnp.dot(a, w_ref[0].astype(BF16), preferred_element_type=F32) + b_ref[0]


def _ada_call(cs, w_ada, b_ada):
    depth, d, n6 = w_ada.shape
    rows = cs.shape[0]
    tn = 1536
    return pl.pallas_call(
        _ada_body, grid=(depth, n6 // tn),
        in_specs=[pl.BlockSpec((rows, d), lambda l, n: (0, 0)),
                  pl.BlockSpec((1, d, tn), lambda l, n: (l, 0, n)),
                  pl.BlockSpec((1, 1, tn), lambda l, n: (l, 0, n))],
        out_specs=pl.BlockSpec((1, rows, tn), lambda l, n: (l, 0, n)),
        out_shape=jax.ShapeDtypeStruct((depth, rows, n6), F32),
        compiler_params=_cparams(2), name="ada",
    )(cs, w_ada, b_ada.reshape(depth, 1, n6))


def _mod0_call(x, ctx, modarr):
    b, l, d = x.shape
    c = ctx.shape[1]
    t = l + c
    nl = l // TM

    def body(x_ref, c_ref, m_ref, xo_ref, h_ref):
        j = pl.program_id(1)
        xv = jnp.where(j < nl, x_ref[0], c_ref[0])
        xo_ref[0] = xv
        h_ref[0] = _modulate(xv, m_ref[0, 0, 0:1, :], m_ref[0, 0, 1:2, :]).astype(BF16)

    return pl.pallas_call(
        body, grid=(b, t // TM),
        in_specs=[pl.BlockSpec((1, TM, d), lambda i, j: (i, jnp.minimum(j, nl - 1), 0)),
                  pl.BlockSpec((1, TM, d), lambda i, j: (i, 0, 0)),
                  pl.BlockSpec((1, 1, 6, d), lambda i, j: (i, j // nl, 0, 0))],
        out_specs=[pl.BlockSpec((1, TM, d), lambda i, j: (i, j, 0)),
                   pl.BlockSpec((1, TM, d), lambda i, j: (i, j, 0))],
        out_shape=[jax.ShapeDtypeStruct((b, t, d), F32), jax.ShapeDtypeStruct((b, t, d), BF16)],
        compiler_params=_cparams(2), name="mod0",
    )(x, ctx, modarr)


def _inproj_even_body(a_ref, w_ref, bd_ref, g_ref, rc_ref, rse_ref, rso_ref,
                      o_ref, vat_ref, qbt_ref, vbt_ref):
    a = a_ref[0]
    bd = bd_ref[...]
    rc, rse, rso = rc_ref[...], rse_ref[...], rso_ref[...]
    plan = ((0, False, 0), (1, False, 512), (None, False, None), (2, True, None), (3, True, 1024),
            (None, False, None))
    for seg, (gain_row, rope, out_col) in enumerate(plan):
        y = jnp.dot(a, w_ref[:, seg * 512:(seg + 1) * 512], preferred_element_type=F32)
        for half in range(2):
            yy = y[:, half * 256:(half + 1) * 256]
            if gain_row is not None:
                yy = _seg_norm(yy, bd, g_ref[gain_row:gain_row + 1, :])
            for blk in range(2):
                z = yy[:, blk * LANES:(blk + 1) * LANES]
                if rope:
                    z = _rope(z, rc, rse, rso)
                m = half * 2 + blk
                if out_col is not None:
                    col = out_col + m * LANES
                    o_ref[0, :, col:col + LANES] = z.astype(BF16)
                    continue
                zt = z.T.astype(BF16)
                if seg == 2:
                    vat_ref[0, 2 * m, 0:HEAD_DIM, :] = zt[:HEAD_DIM]
                    vat_ref[0, 2 * m + 1, 0:HEAD_DIM, :] = zt[HEAD_DIM:]
                elif seg == 3:
                    qbt_ref[0, m * LANES:(m + 1) * LANES, :] = zt
                else:
                    vbt_ref[0, m, 0:LANES, :] = zt
    ones = jnp.ones((ONES_ROWS, TM), BF16)
    for hd in range(vat_ref.shape[1]):
        vat_ref[0, hd, HEAD_DIM:HEAD_DIM + ONES_ROWS, :] = ones
    for hd in range(vbt_ref.shape[1]):
        vbt_ref[0, hd, LANES:LANES + ONES_ROWS, :] = ones


def _inproj_odd_body(a_ref, ap_ref, an_ref, w_ref, bd_ref, g_ref, rc_ref, rse_ref, rso_ref,
                     cw_ref, cb_ref, o_ref, x0_ref, z_ref, qt_ref, vt_ref, *, nl, nt):
    j = pl.program_id(1)
    a = a_ref[0]
    bd = bd_ref[...]
    rc, rse, rso = rc_ref[...], rse_ref[...], rso_ref[...]
    y = jnp.dot(a, w_ref[:, 0:512], preferred_element_type=F32)
    for half in range(2):
        yy = _seg_norm(y[:, half * 256:(half + 1) * 256], bd, g_ref[0:1, :])
        for blk in range(2):
            z = _rope(yy[:, blk * LANES:(blk + 1) * LANES], rc, rse, rso)
            m = half * 2 + blk
            qt_ref[0, m * LANES:(m + 1) * LANES, :] = z.T.astype(BF16)
    y = jnp.dot(a, w_ref[:, 512:768], preferred_element_type=F32)
    k = _seg_norm(y[:, :LANES], bd_ref[0:LANES, 0:LANES], g_ref[1:2, 0:LANES])
    k = _rope(k, rc, rse, rso)
    kr = pltpu.roll(k, HEAD_DIM, 1)
    lo = lax.broadcasted_iota(jnp.int32, k.shape, 1) < HEAD_DIM
    o_ref[0, :, 0:LANES] = jnp.where(lo, k, kr).astype(BF16)
    o_ref[0, :, LANES:2 * LANES] = jnp.where(lo, kr, k).astype(BF16)
    vt = y[:, LANES:].T.astype(BF16)
    ones = jnp.ones((ONES_ROWS, TM), BF16)
    for hd in range(2):
        vt_ref[0, hd, 0:HEAD_DIM, :] = vt[hd * HEAD_DIM:(hd + 1) * HEAD_DIM]
        vt_ref[0, hd, HEAD_DIM:HEAD_DIM + ONES_ROWS, :] = ones
    hb = ap_ref.shape[1]
    zero = jnp.zeros((hb, a.shape[1]), BF16)
    ap = jnp.where((j == 0) | (j == nl), zero, ap_ref[0])
    an = jnp.where((j == nl - 1) | (j == nt - 1), zero, an_ref[0])
    a_ext = jnp.concatenate([ap, a, an], axis=0)
    conv = []
    for seg in range(3):
        sl = slice(seg * 512, (seg + 1) * 512)
        g = jnp.dot(a_ext, w_ref[:, 768 + seg * 512:768 + (seg + 1) * 512], preferred_element_type=F32)
        conv.append(g[hb - 1:hb - 1 + TM] * cw_ref[0:1, sl] + g[hb:hb + TM] * cw_ref[1:2, sl]
                    + g[hb + 1:hb + 1 + TM] * cw_ref[2:3, sl] + cb_ref[0:1, sl])
        if seg == 0:
            x0_ref[0] = conv[0].astype(BF16)
    z_ref[0] = (conv[2] * conv[1]).astype(BF16)


def _inproj_call(body, name, h, w, bd, gains, ropes, n_outs, t_shapes, extra=(), halo=False):
    b, t, d = h.shape
    n_in = w.shape[1]
    nt = t // TM
    hb = 16
    r = TM // hb
    rc, rse, rso = ropes

    def t_spec(shape):
        nd = len(shape)
        return pl.BlockSpec((1,) + tuple(shape[1:-1]) + (TM,), lambda i, j: (i,) + (0,) * (nd - 2) + (j,))

    acts, act_specs = [h], [pl.BlockSpec((1, TM, d), lambda i, j: (i, j, 0))]
    if halo:
        acts += [h, h]
        act_specs += [pl.BlockSpec((1, hb, d), lambda i, j: (i, jnp.maximum(j * r - 1, 0), 0)),
                      pl.BlockSpec((1, hb, d), lambda i, j: (i, jnp.minimum((j + 1) * r, nt * r - 1), 0))]
    return pl.pallas_call(
        body, grid=(b, nt),
        in_specs=act_specs
        + [pl.BlockSpec((d, n_in), lambda i, j: (0, 0)),
           pl.BlockSpec(bd.shape, lambda i, j: (0, 0)),
           pl.BlockSpec(gains.shape, lambda i, j: (0, 0)),
           pl.BlockSpec((TM, LANES), lambda i, j: (j, 0)),
           pl.BlockSpec((TM, LANES), lambda i, j: (j, 0)),
           pl.BlockSpec((TM, LANES), lambda i, j: (j, 0))]
        + [pl.BlockSpec(e.shape, lambda i, j: (0,) * e.ndim) for e in extra],
        out_specs=[pl.BlockSpec((1, TM, n), lambda i, j: (i, j, 0)) for n in n_outs]
        + [t_spec(s) for s in t_shapes],
        out_shape=[jax.ShapeDtypeStruct((b, t, n), BF16) for n in n_outs]
        + [jax.ShapeDtypeStruct(s, BF16) for s in t_shapes],
        compiler_params=_cparams(2), name=name,
    )(*acts, w, bd, gains, rc, rse, rso, *extra)


def _mix_ffn_up_call(ya, yb, wa, wb, x, modarr, w_up, conv_w, conv_b):
    b, t, d = x.shape
    f = w_up.shape[1] // 2
    nl = (t - TM) // TM
    nt = t // TM
    hb = 16
    r = TM // hb
    ka, kb = ya.shape[2], yb.shape[2]
    chunks = [(c0, min(2 * LANES, f - c0)) for c0 in range(0, f, 2 * LANES)]

    def body(ya_ref, yap_ref, yan_ref, yb_ref, ybp_ref, ybn_ref, x_ref, xp_ref, xn_ref,
             wa_ref, wb_ref, m_ref, wg_ref, wv_ref, cw_ref, cb_ref, xo_ref, o_ref):
        j = pl.program_id(1)
        ext = lambda p, c, n: jnp.concatenate([p[0], c[0], n[0]], axis=0)
        y = (jnp.dot(ext(yap_ref, ya_ref, yan_ref), wa_ref[...], preferred_element_type=F32)
             + jnp.dot(ext(ybp_ref, yb_ref, ybn_ref), wb_ref[...], preferred_element_type=F32))
        x1 = ext(xp_ref, x_ref, xn_ref) + m_ref[0, 0, 2:3, :] * y
        xo_ref[0] = x1[hb:hb + TM]
        hx = _modulate(x1, m_ref[0, 0, 3:4, :], m_ref[0, 0, 4:5, :]).astype(BF16)
        zero = jnp.zeros((hb, d), BF16)
        a = hx[hb:hb + TM]
        ap = jnp.where((j == 0) | (j == nl), zero, hx[0:hb])
        an = jnp.where((j == nl - 1) | (j == nt - 1), zero, hx[hb + TM:])
        a_ext = jnp.concatenate([ap, a, an], axis=0)
        for c0, cw in chunks:
            sl = slice(c0, c0 + cw)
            g = jnp.dot(a_ext, wg_ref[:, sl], preferred_element_type=F32)
            u = (g[hb - 1:hb - 1 + TM] * cw_ref[0:1, sl] + g[hb:hb + TM] * cw_ref[1:2, sl]
                 + g[hb + 1:hb + 1 + TM] * cw_ref[2:3, sl] + cb_ref[0:1, sl])
            v = jnp.dot(a, wv_ref[:, sl], preferred_element_type=F32)
            o_ref[0, :, sl] = ((u / (1.0 + jnp.exp(-u))) * v).astype(BF16)

    main = lambda w: pl.BlockSpec((1, TM, w), lambda i, j: (i, j, 0))
    prev = lambda w: pl.BlockSpec((1, hb, w), lambda i, j: (i, jnp.maximum(j * r - 1, 0), 0))
    nxt = lambda w: pl.BlockSpec((1, hb, w), lambda i, j: (i, jnp.minimum((j + 1) * r, nt * r - 1), 0))
    full = lambda a: pl.BlockSpec(a.shape, lambda i, j: (0,) * a.ndim)
    return pl.pallas_call(
        body, grid=(b, nt),
        in_specs=[main(ka), prev(ka), nxt(ka), main(kb), prev(kb), nxt(kb), main(d), prev(d), nxt(d),
                  full(wa), full(wb), pl.BlockSpec((1, 1, 6, d), lambda i, j: (i, j // nl, 0, 0)),
                  pl.BlockSpec((d, f), lambda i, j: (0, 0)), pl.BlockSpec((d, f), lambda i, j: (0, 1)),
                  pl.BlockSpec((3, f), lambda i, j: (0, 0)), pl.BlockSpec((1, f), lambda i, j: (0, 0))],
        out_specs=[pl.BlockSpec((1, TM, d), lambda i, j: (i, j, 0)),
                   pl.BlockSpec((1, TM, f), lambda i, j: (i, j, 0))],
        out_shape=[jax.ShapeDtypeStruct((b, t, d), F32), jax.ShapeDtypeStruct((b, t, f), BF16)],
        compiler_params=_cparams(2), name="mix_ffn_up",
    )(ya, ya, ya, yb, yb, yb, x, x, x, wa, wb, modarr, w_up, w_up, conv_w, conv_b.reshape(1, f))


def _ffn_down_call(act, w_down, x, modarr, modarr_next):
    b, t, d = x.shape
    f = w_down.shape[0]
    nl = (t - TM) // TM
    nt = t // TM

    def body(a_ref, wd_ref, x_ref, m_ref, mn_ref, xo_ref, h_ref):
        y = jnp.dot(a_ref[0], wd_ref[...], preferred_element_type=F32)
        x2 = x_ref[0] + m_ref[0, 0, 5:6, :] * y
        xo_ref[0] = x2
        h_ref[0] = _modulate(x2, mn_ref[0, 0, 0:1, :], mn_ref[0, 0, 1:2, :]).astype(BF16)

    return pl.pallas_call(
        body, grid=(b, nt),
        in_specs=[pl.BlockSpec((1, TM, f), lambda i, j: (i, j, 0)),
                  pl.BlockSpec((f, d), lambda i, j: (0, 0)),
                  pl.BlockSpec((1, TM, d), lambda i, j: (i, j, 0)),
                  pl.BlockSpec((1, 1, 6, d), lambda i, j: (i, j // nl, 0, 0)),
                  pl.BlockSpec((1, 1, 6, d), lambda i, j: (i, j // nl, 0, 0))],
        out_specs=[pl.BlockSpec((1, TM, d), lambda i, j: (i, j, 0)),
                   pl.BlockSpec((1, TM, d), lambda i, j: (i, j, 0))],
        out_shape=[jax.ShapeDtypeStruct((b, t, d), F32), jax.ShapeDtypeStruct((b, t, d), BF16)],
        compiler_params=_cparams(2), name="ffn_down",
    )(act, w_down, x, modarr, modarr_next)


def _ffn_last_call(act, w_down, x, modarr):
    b, t, d = x.shape
    f = w_down.shape[0]
    nl = (t - TM) // TM

    def body(a_ref, wd_ref, x_ref, m_ref, xo_ref):
        y = jnp.dot(a_ref[0], wd_ref[...], preferred_element_type=F32)
        xo_ref[0] = x_ref[0] + m_ref[0, 0, 5:6, :] * y

    return pl.pallas_call(
        body, grid=(b, nl),
        in_specs=[pl.BlockSpec((1, TM, f), lambda i, j: (i, j, 0)),
                  pl.BlockSpec((f, d), lambda i, j: (0, 0)),
                  pl.BlockSpec((1, TM, d), lambda i, j: (i, j, 0)),
                  pl.BlockSpec((1, 1, 6, d), lambda i, j: (i, 0, 0, 0))],
        out_specs=pl.BlockSpec((1, TM, d), lambda i, j: (i, j, 0)),
        out_shape=jax.ShapeDtypeStruct((b, nl * TM, d), F32),
        compiler_params=_cparams(2), name="ffn_last",
    )(act, w_down, x, modarr)


def _split_heads(q):
    lane = lax.broadcasted_iota(jnp.int32, q.shape, 1)
    zero = jnp.zeros_like(q)
    return jnp.where(lane < HEAD_DIM, q, zero), jnp.where(lane >= HEAD_DIM, q, zero)


def _flash_call(name, qt, qkv, vt, kcol, shared_kv, vrows, tk, finish, extra, out_cols, l):
    b, t, _ = qkv.shape
    c = t - l
    nl = l // TM
    n_chunks = l // tk
    n_steps = out_cols // (2 * LANES)
    n_kv = 1 if shared_kv else 2
    ns = 4
    nb = 2
    assert l % tk == 0 and n_chunks >= 2

    def body(q_ref, qn_ref, k_ref, vt_ref, *rest):
        extra_refs = rest[:-8]
        o_ref, s_ref, p_ref, sc_ref, pc_ref, acc_ref, m_ref, a_ref = rest[-8:]
        tile = pl.program_id(2)
        is_lat = tile < nl
        row = lax.broadcasted_iota(jnp.int32, (LANES, TM), 0)

        def streams(ref):
            out = []
            for blk in range(2):
                qb = ref[0, blk * LANES:(blk + 1) * LANES, :]
                out += [jnp.where(row < HEAD_DIM, qb, jnp.zeros_like(qb)),
                        jnp.where(row >= HEAD_DIM, qb, jnp.zeros_like(qb))]
            return out

        kv_of = lambda s: 0 if shared_kv else s // 2

        def scores(off, size, dst, qs):
            kcs = [k_ref[0, pl.ds(off, size), j * LANES:(j + 1) * LANES] for j in range(n_kv)]
            for s in range(ns):
                dst(s)[...] = jnp.dot(kcs[kv_of(s)], qs[s], preferred_element_type=F32)

        def softmax(src, dst, slot):
            for s in range(ns):
                st = src(s)[...]
                m = m_ref[s]
                mnew = jnp.maximum(m, jnp.max(st, axis=0, keepdims=True))
                a_ref[slot, s] = jnp.exp2(m - mnew)
                dst(s)[...] = jnp.exp2((st - mnew).astype(BF16))
                m_ref[s] = mnew

        def values(src, slot, off, size):
            vcs = [vt_ref[0, j, :, pl.ds(off, size)] for j in range(n_kv)]
            for s in range(ns):
                acc_ref[s] = a_ref[slot, s] * acc_ref[s] + jnp.dot(vcs[kv_of(s)], src(s)[...],
                                                                  preferred_element_type=F32)

        s_ctx, p_ctx = (lambda s: sc_ref.at[s]), (lambda s: pc_ref.at[s])
        s_buf = [(lambda s, i=i: s_ref.at[i, s]) for i in range(nb)]
        p_buf = [(lambda s, i=i: p_ref.at[i, s]) for i in range(nb)]

        def start():
            acc_ref[...] = jnp.zeros(acc_ref.shape, F32)
            m_ref[...] = jnp.full(m_ref.shape, NEG, F32)

        def finalize():
            outs = [acc_ref[s, 0:vrows, :] / acc_ref[s, vrows:vrows + 1, :] for s in range(ns)]
            o_ref[0] = finish(outs, *extra_refs)

        @pl.when(tile == 0)
        def _():
            qs = streams(q_ref)
            scores(l, c, s_ctx, qs)
            scores(0, tk, s_buf[0], qs)

        @pl.when(is_lat)
        def _():
            qs = streams(q_ref)
            qs_next = streams(qn_ref)
            start()
            softmax(s_ctx, p_ctx, nb)
            values(p_ctx, nb, l, c)
            scores(tk, tk, s_buf[1], qs)
            softmax(s_buf[0], p_buf[0], 0)

            for k in range(2, n_chunks):
                values(p_buf[(k - 2) % nb], (k - 2) % nb, (k - 2) * tk, tk)
                scores(k * tk, tk, s_buf[k % nb], qs)
                softmax(s_buf[(k - 1) % nb], p_buf[(k - 1) % nb], (k - 1) % nb)
            k = n_chunks
            values(p_buf[(k - 2) % nb], (k - 2) % nb, (k - 2) * tk, tk)
            scores(l, c, s_ctx, qs_next)
            softmax(s_buf[(k - 1) % nb], p_buf[(k - 1) % nb], (k - 1) % nb)
            scores(0, tk, s_buf[0], qs_next)
            values(p_buf[(k - 1) % nb], (k - 1) % nb, (k - 1) * tk, tk)
            finalize()

        @pl.when(jnp.logical_not(is_lat))
        def _():
            start()
            softmax(s_ctx, p_ctx, nb)
            values(p_ctx, nb, l, c)
            finalize()

    kw = n_kv * LANES
    extra_specs = [pl.BlockSpec(e.shape, lambda bi, h, i: (0,) * e.ndim) for e in extra]
    return pl.pallas_call(
        body, grid=(b, n_steps, t // TM),
        in_specs=[pl.BlockSpec((1, 2 * LANES, TM), lambda bi, h, i: (bi, h, i)),
                  pl.BlockSpec((1, 2 * LANES, TM), lambda bi, h, i: (bi, h, jnp.minimum(i + 1, nl))),
                  pl.BlockSpec((1, t, kw), lambda bi, h, i: (bi, 0, kcol + h)),
                  pl.BlockSpec((1, n_kv, vrows + ONES_ROWS, t), lambda bi, h, i: (bi, h, 0, 0))]
        + extra_specs,
        out_specs=pl.BlockSpec((1, TM, 2 * LANES), lambda bi, h, i: (bi, i, h)),
        out_shape=jax.ShapeDtypeStruct((b, t, out_cols), BF16),
        scratch_shapes=[pltpu.VMEM((nb, ns, tk, TM), F32), pltpu.VMEM((nb, ns, tk, TM), BF16),
                        pltpu.VMEM((ns, c, TM), F32), pltpu.VMEM((ns, c, TM), BF16),
                        pltpu.VMEM((ns, vrows + ONES_ROWS, TM), F32), pltpu.VMEM((ns, 1, TM), F32),
                        pltpu.VMEM((nb + 1, ns, 1, TM), F32)],
        compiler_params=_cparams(3), name=name,
    )(qt, qt, qkv, vt, *extra)


def _gqa_finish(outs):
    return jnp.concatenate(outs, axis=0).T.astype(BF16)


def _make_da_finish(lam_init):
    def finish(outs, lam_ref, subln_ref):
        lv = lam_ref[...]
        lam = (jnp.exp(jnp.sum(lv[0:1] * lv[1:2], keepdims=True))
               - jnp.exp(jnp.sum(lv[2:3] * lv[3:4], keepdims=True)) + lam_init)
        heads = []
        for hd in range(len(outs) // 2):
            o = (outs[2 * hd] - lam * outs[2 * hd + 1]).T
            o = o * lax.rsqrt(jnp.mean(o * o, axis=-1, keepdims=True) + EPS)
            heads.append((o * subln_ref[...] * (1.0 - lam_init)).astype(BF16))
        return jnp.concatenate(heads, axis=1)
    return finish


def _na_call(qkv, vt, bias, l):
    b, t, _ = qkv.shape
    c = t - l
    nl = l // TM
    rows = l // GRID_W
    win = NA_WROWS * GRID_W
    n_pairs = vt.shape[1] // 2

    nt = t // TM

    def window(tile):
        return pl.multiple_of(jnp.clip(tile * NA_QROWS - NA_WIN_R // 2, 0, rows - NA_WROWS) * GRID_W, 256)

    hp = 1

    def body(q0_ref, qn_ref, k_ref, vt_ref, b0_ref, bn_ref, o_ref, sw_ref, sc_ref):
        g = pl.program_id(2)

        def scores(q_ref, b_ref, tile, buf, with_win):
            for blk in range(hp):
                cols = slice(blk * LANES, (blk + 1) * LANES)
                qs = _split_heads(q_ref[0, :, cols])
                if with_win:
                    kw = k_ref[0, pl.ds(window(tile), win), cols]
                    for half in range(2):
                        sw_ref[buf, 2 * blk + half] = _dot_nt(kw, qs[half]) + b_ref[blk, half]
                kc = k_ref[0, l:l + c, cols]
                for half in range(2):
                    sc_ref[buf, 2 * blk + half] = _dot_nt(kc, qs[half])

        def finish(tile, buf, with_win):
            outs = []
            for s in range(2 * hp):
                parts = [(sc_ref[buf, s], vt_ref[0, s, :, l:l + c])]
                if with_win:
                    parts.append((sw_ref[buf, s], vt_ref[0, s, :, pl.ds(window(tile), win)]))
                m = functools.reduce(jnp.maximum, [jnp.max(sc, axis=0, keepdims=True) for sc, _ in parts])
                acc = sum(jnp.dot(v, jnp.exp2((sc - m).astype(BF16)), preferred_element_type=F32)
                          for sc, v in parts)
                outs.append(acc[:HEAD_DIM] / acc[HEAD_DIM:HEAD_DIM + 1])
            o_ref[0] = jnp.concatenate(outs, axis=0).T.astype(BF16)

        cur, nxt = g % 2, (g + 1) % 2

        @pl.when(g == 0)
        def _():
            scores(q0_ref, b0_ref, 0, 0, True)

        @pl.when(g + 1 < nl)
        def _():
            scores(qn_ref, bn_ref, g + 1, nxt, True)
            finish(g, cur, True)

        @pl.when(g + 1 == nl)
        def _():
            scores(qn_ref, bn_ref, g + 1, nxt, False)
            finish(g, cur, True)

        @pl.when(g == nl)
        def _():
            finish(g, cur, False)

    n_steps = n_pairs // hp
    wq = hp * LANES

    def bias_next(bi, h, g):
        tile = g + 1
        case = jnp.where(tile >= nl - 1, 2, 1)
        return (case * n_steps + h, 0, 0, 0)

    return pl.pallas_call(
        body, grid=(b, n_steps, nt),
        in_specs=[pl.BlockSpec((1, TM, wq), lambda bi, h, g: (bi, 0, h)),
                  pl.BlockSpec((1, TM, wq), lambda bi, h, g: (bi, jnp.minimum(g + 1, nt - 1), h)),
                  pl.BlockSpec((1, t, wq), lambda bi, h, g: (bi, 0, n_steps + h)),
                  pl.BlockSpec((1, 2 * hp, HEAD_DIM + ONES_ROWS, t), lambda bi, h, g: (bi, h, 0, 0)),
                  pl.BlockSpec((hp, 2, win, TM), lambda bi, h, g: (h, 0, 0, 0)),
                  pl.BlockSpec((hp, 2, win, TM), bias_next)],
        out_specs=pl.BlockSpec((1, TM, wq), lambda bi, h, g: (bi, g, h)),
        out_shape=jax.ShapeDtypeStruct((b, t, n_pairs * LANES), BF16),
        scratch_shapes=[pltpu.VMEM((2, 2 * hp, win, TM), F32), pltpu.VMEM((2, 2 * hp, c, TM), F32)],
        compiler_params=_cparams(3), name="na_attn",
    )(qkv, qkv, qkv, vt, bias, bias)


def _na_bias_table(rpb):
    h, n_ro, n_co = rpb.shape
    kr, qr = np.arange(NA_WROWS), np.arange(NA_QROWS)
    kc, qc = np.arange(GRID_W), np.arange(GRID_W)
    cs = np.clip(qc - NA_WIN_C // 2, 0, GRID_W - NA_WIN_C)
    col_ok = (kc[:, None] >= cs[None, :]) & (kc[:, None] < cs[None, :] + NA_WIN_C)
    co = kc[:, None] - qc[None, :] + NA_WIN_C - 1
    col_sel = (co[None] == np.arange(n_co)[:, None, None]).astype(np.float32)
    tables = []
    for d, rel in ((0, np.zeros_like(qr)), (NA_WIN_R // 2, qr), (NA_WIN_R, np.full_like(qr, NA_WIN_R // 2))):
        row_ok = (kr[:, None] >= rel[None, :]) & (kr[:, None] < rel[None, :] + NA_WIN_R)
        ro = kr[:, None] - qr[None, :] - d + NA_WIN_R - 1
        row_sel = (ro[:, :, None] == np.arange(n_ro)[None, None, :]).astype(np.float32)
        tb = jnp.einsum('kqa,hab,bcd->hkcqd', jnp.asarray(row_sel), rpb.astype(F32) * LOG2E,
                        jnp.asarray(col_sel), precision=lax.Precision.HIGHEST)
        ok = row_ok[:, None, :, None] & col_ok[None, :, None, :]
        tables.append(jnp.where(jnp.asarray(ok)[None], tb, NEG))
    tbl = jnp.stack(tables, axis=0)
    return tbl.reshape(3 * (h // 2), 2, NA_WROWS * GRID_W, TM).astype(F32)


def _hy_filter_call(length, w1, b1, w2, b2, w3, b3, w4, freq):
    order = w2.shape[0]
    w = w4.shape[1] // 2
    tl = min(length, 512)
    hi = lax.Precision.HIGHEST
    t = np.linspace(0.0, 1.0, length, dtype=np.float64)[:, None]
    bands = (HY_EMB_DIM - 1) // 2
    ang = 2.0 * math.pi * np.arange(length, dtype=np.float64)[:, None] / length
    fq = np.linspace(1e-4, bands - 1, bands, dtype=np.float64)[None, :]
    emb = np.concatenate([t, np.cos(fq * ang), -np.sin(fq * ang)], axis=-1).astype(np.float32)
    emb = np.pad(emb, ((0, 0), (0, LANES - HY_EMB_DIM)))
    max_decay = math.log(HY_DECAY_TARGET) / HY_FAST_DECAY
    min_decay = math.log(HY_DECAY_TARGET) / HY_SLOW_DECAY
    deltas = np.linspace(min_decay, max_decay, w, dtype=np.float64)
    decay = np.exp(-t * np.abs(deltas)[None, :]).astype(np.float32)
    decay2 = np.concatenate([decay, decay], axis=1)
    w1p = jnp.pad(w1, ((0, LANES - HY_EMB_DIM), (0, 0)))

    def body(e_ref, d_ref, w1_ref, b1_ref, w2_ref, b2_ref, w3_ref, b3_ref, w4_ref, f_ref,
             h_ref, s_ref):
        i = pl.program_id(0)
        fr = f_ref[...]
        hdn = jnp.sin(fr * (jnp.dot(e_ref[...], w1_ref[...], precision=hi,
                                    preferred_element_type=F32) + b1_ref[...]))
        hdn = jnp.sin(fr * (jnp.dot(hdn, w2_ref[...], precision=hi,
                                    preferred_element_type=F32) + b2_ref[...]))
        hdn = jnp.sin(fr * (jnp.dot(hdn, w3_ref[...], precision=hi,
                                    preferred_element_type=F32) + b3_ref[...]))
        taps = jnp.dot(hdn, w4_ref[...], precision=hi, preferred_element_type=F32) * d_ref[...]
        row = lax.broadcasted_iota(jnp.int32, taps.shape, 0) + i * tl
        col = lax.broadcasted_iota(jnp.int32, taps.shape, 1)
        taps = jnp.where((row == 0) & (col >= w), 0.0, taps)
        h_ref[...] = taps

        @pl.when(i == 0)
        def _():
            s_ref[...] = jnp.zeros(s_ref.shape, F32)
        s_ref[...] += jnp.sum(jnp.abs(taps), axis=0, keepdims=True)

    full = lambda a: pl.BlockSpec(a.shape, lambda i: (0,) * a.ndim)
    ops = (w1p, b1.reshape(1, order), w2, b2.reshape(1, order), w3, b3.reshape(1, order), w4,
           freq.reshape(1, order))
    return pl.pallas_call(
        body, grid=(length // tl,),
        in_specs=[pl.BlockSpec((tl, LANES), lambda i: (i, 0)),
                  pl.BlockSpec((tl, 2 * w), lambda i: (i, 0))] + [full(a) for a in ops],
        out_specs=[pl.BlockSpec((tl, 2 * w), lambda i: (i, 0)),
                   pl.BlockSpec((1, 2 * w), lambda i: (0, 0))],
        out_shape=[jax.ShapeDtypeStruct((length, 2 * w), F32), jax.ShapeDtypeStruct((1, 2 * w), F32)],
        compiler_params=_cparams(1), name="hy_filter",
    )(jnp.asarray(emb), jnp.asarray(decay2), *ops)


def _dft_tables(l1, fb):
    n1 = 2 * l1
    n = n1 * LANES
    nf = -(-(n1 // 2 + 1) // fb) * fb
    f1 = np.arange(nf, dtype=np.float64)
    wgt = np.where((f1 == 0) | (f1 == n1 // 2), 1.0, np.where(f1 < n1 // 2, 2.0, 0.0))[None, :]
    t1 = np.arange(l1, dtype=np.float64)
    th1 = 2.0 * np.pi * np.outer(f1, t1) / n1
    fwd1 = np.concatenate([np.cos(th1), -np.sin(th1)], axis=0)
    inv1 = np.concatenate([wgt * np.cos(th1).T, -wgt * np.sin(th1).T], axis=1) / n
    f2 = np.arange(LANES, dtype=np.float64)
    t2 = np.arange(LANES, dtype=np.float64)
    fr = f1[:, None, None] + n1 * f2[None, :, None]
    th2 = 2.0 * np.pi * fr * t2[None, None, :] / n
    gr, gi = np.cos(th2), -np.sin(th2)
    gb = np.concatenate([np.concatenate([gr, -gi], axis=2),
                         np.concatenate([gi, gr], axis=2)], axis=1)
    hb = np.transpose(gb, (0, 2, 1))
    as_bf = lambda a: jnp.asarray(a.astype(np.float32)).astype(BF16)
    return as_bf(fwd1), as_bf(inv1), as_bf(gb), as_bf(hb)


def _hy_stage1_call(xv, fwd1):
    bx, l1, cols = xv.shape
    n2 = fwd1.shape[0]
    tn = min(cols, 4096)

    def body(f_ref, x_ref, o_ref):
        o_ref[0] = jnp.dot(f_ref[...], x_ref[0], preferred_element_type=F32).astype(BF16)

    return pl.pallas_call(
        body, grid=(bx, cols // tn),
        in_specs=[pl.BlockSpec((n2, l1), lambda i, j: (0, 0)),
                  pl.BlockSpec((1, l1, tn), lambda i, j: (i, 0, j))],
        out_specs=pl.BlockSpec((1, n2, tn), lambda i, j: (i, 0, j)),
        out_shape=jax.ShapeDtypeStruct((bx, n2, cols), BF16),
        compiler_params=_cparams(2), name="hy_dft1",
    )(fwd1, xv)


def _hy_filter_spec_call(a5, gb, sums, fb):
    n1, w2 = a5.shape[2], a5.shape[4]
    w = w2 // 2

    def body(a_ref, g_ref, s_ref, o_ref):
        sv = s_ref[...]
        inv = 1.0 / (sv[:, :w] + sv[:, w:])
        for k in range(fb):
            a = jnp.concatenate([a_ref[0, 0, k], a_ref[0, 1, k]], axis=0)
            z = jnp.dot(g_ref[k], a, preferred_element_type=F32)
            o_ref[k, 0] = (z[:LANES, :w] + z[:LANES, w:]) * inv
            o_ref[k, 1] = (z[LANES:, :w] - z[LANES:, w:]) * inv

    return pl.pallas_call(
        body, grid=(n1 // fb,),
        in_specs=[pl.BlockSpec((1, 2, fb, LANES, w2), lambda i: (0, 0, i, 0, 0)),
                  pl.BlockSpec((fb, 2 * LANES, 2 * LANES), lambda i: (i, 0, 0)),
                  pl.BlockSpec((1, w2), lambda i: (0, 0))],
        out_specs=pl.BlockSpec((fb, 2, LANES, w), lambda i: (i, 0, 0, 0)),
        out_shape=jax.ShapeDtypeStruct((n1, 2, LANES, w), F32),
        compiler_params=_cparams(1), name="hy_fspec",
    )(a5, gb, sums)


def _hy_stage23_call(a5, gb, hb, kf, fb):
    b, _, n1, _, w = a5.shape

    def body(a_ref, g_ref, h_ref, k_ref, o_ref):
        for k in range(fb):
            a = jnp.concatenate([a_ref[0, 0, k], a_ref[0, 1, k]], axis=0)
            z = jnp.dot(g_ref[k], a, preferred_element_type=F32)
            zr, zi = z[:LANES], z[LANES:]
            kr, ki = k_ref[k, 0], k_ref[k, 1]
            y = jnp.concatenate([zr * kr - zi * ki, zr * ki + zi * kr], axis=0).astype(BF16)
            cc = jnp.dot(h_ref[k], y, preferred_element_type=F32)
            o_ref[0, 0, k] = cc[:LANES].astype(BF16)
            o_ref[0, 1, k] = cc[LANES:].astype(BF16)

    return pl.pallas_call(
        body, grid=(n1 // fb, b),
        in_specs=[pl.BlockSpec((1, 2, fb, LANES, w), lambda i, j: (j, 0, i, 0, 0)),
                  pl.BlockSpec((fb, 2 * LANES, 2 * LANES), lambda i, j: (i, 0, 0)),
                  pl.BlockSpec((fb, 2 * LANES, 2 * LANES), lambda i, j: (i, 0, 0)),
                  pl.BlockSpec((fb, 2, LANES, w), lambda i, j: (i, 0, 0, 0))],
        out_specs=pl.BlockSpec((1, 2, fb, LANES, w), lambda i, j: (j, 0, i, 0, 0)),
        out_shape=jax.ShapeDtypeStruct(a5.shape, BF16),
        compiler_params=_cparams(2), name="hy_dft23",
    )(a5, gb, hb, kf)


def _hy_stage4_call(cv, inv1, x0v, zv, skip_t):
    b, n2, cols = cv.shape
    l1 = inv1.shape[0]
    tn = skip_t.shape[1]

    def body(f_ref, c_ref, x0_ref, z_ref, s_ref, o_ref):
        y = jnp.dot(f_ref[...], c_ref[0], preferred_element_type=F32)
        z = z_ref[0].astype(F32)
        o_ref[0] = (x0_ref[0].astype(F32) * (y + s_ref[...] * z)).astype(BF16)

    return pl.pallas_call(
        body, grid=(b, cols // tn),
        in_specs=[pl.BlockSpec((l1, n2), lambda i, j: (0, 0)),
                  pl.BlockSpec((1, n2, tn), lambda i, j: (i, 0, j)),
                  pl.BlockSpec((1, l1, tn), lambda i, j: (i, 0, j)),
                  pl.BlockSpec((1, l1, tn), lambda i, j: (i, 0, j)),
                  pl.BlockSpec((1, tn), lambda i, j: (0, 0))],
        out_specs=pl.BlockSpec((1, l1, tn), lambda i, j: (i, 0, j)),
        out_shape=jax.ShapeDtypeStruct((b, l1, cols), BF16),
        compiler_params=_cparams(2), name="hy_dft4",
    )(inv1, cv, x0v, zv, skip_t)


def _hy_dense_call(x0, z, taps, sums, skip):
    b, c, w = z.shape
    n = 2 * c
    th = 2.0 * np.pi * np.outer(np.arange(n, dtype=np.float64), np.arange(c, dtype=np.float64)) / n
    fwd = jnp.asarray(np.concatenate([np.cos(th), -np.sin(th)], axis=0).astype(np.float32)).astype(BF16)
    inv = jnp.asarray((np.concatenate([np.cos(th).T, -np.sin(th).T], axis=1) / n)
                      .astype(np.float32)).astype(BF16)

    def body(f_ref, i_ref, x0_ref, z_ref, t_ref, s_ref, k_ref, o_ref):
        sv = s_ref[...]
        nrm = 1.0 / (sv[:, :w] + sv[:, w:])
        tf = jnp.dot(f_ref[...], t_ref[...].astype(BF16), preferred_element_type=F32)
        kr = (tf[:n, :w] + tf[:n, w:]) * nrm
        ki = (tf[n:, :w] - tf[n:, w:]) * nrm
        zf = jnp.dot(f_ref[...], z_ref[0], preferred_element_type=F32)
        zr, zi = zf[:n], zf[n:]
        y = jnp.concatenate([zr * kr - zi * ki, zr * ki + zi * kr], axis=0).astype(BF16)
        yt = jnp.dot(i_ref[...], y, preferred_element_type=F32)
        o_ref[0] = (x0_ref[0].astype(F32) * (yt + k_ref[...] * z_ref[0].astype(F32))).astype(BF16)

    return pl.pallas_call(
        body, grid=(b,),
        in_specs=[pl.BlockSpec((2 * n, c), lambda i: (0, 0)),
                  pl.BlockSpec((c, 2 * n), lambda i: (0, 0)),
                  pl.BlockSpec((1, c, w), lambda i: (i, 0, 0)),
                  pl.BlockSpec((1, c, w), lambda i: (i, 0, 0)),
                  pl.BlockSpec((c, 2 * w), lambda i: (0, 0)),
                  pl.BlockSpec((1, 2 * w), lambda i: (0, 0)),
                  pl.BlockSpec((1, w), lambda i: (0, 0))],
        out_specs=pl.BlockSpec((1, c, w), lambda i: (i, 0, 0)),
        out_shape=jax.ShapeDtypeStruct((b, c, w), BF16),
        compiler_params=_cparams(1), name="hy_dense",
    )(fwd, inv, x0, z, taps, sums, skip.reshape(1, w))


def _hyena_long(x0, z, fparams, skip):
    b, l, w = z.shape
    l1 = l // LANES
    n1 = 2 * l1
    fb = min(8, n1)
    fwd1, inv1, gb, hb = _dft_tables(l1, fb)
    nf = gb.shape[0]
    taps, sums = _hy_filter_call(l, *fparams)
    ta = _hy_stage1_call(taps.astype(BF16).reshape(1, l1, LANES * 2 * w), fwd1)
    kf = _hy_filter_spec_call(ta.reshape(1, 2, nf, LANES, 2 * w), gb, sums, fb)
    za = _hy_stage1_call(z.reshape(b, l1, LANES * w), fwd1)
    cc = _hy_stage23_call(za.reshape(b, 2, nf, LANES, w), gb, hb, kf, fb)
    tn = 8 * w
    skip_t = jnp.tile(skip.reshape(1, w), (1, tn // w))
    y = _hy_stage4_call(cc.reshape(b, 2 * nf, LANES * w), inv1, x0.reshape(b, l1, LANES * w),
                        z.reshape(b, l1, LANES * w), skip_t)
    return y.reshape(b, l, w)


def _hyena_short(x0, z, fparams, skip):
    taps, sums = _hy_filter_call(z.shape[1], *fparams)
    return _hy_dense_call(x0, z, taps, sums, skip)


def _rope_tables(l, c):
    t = np.arange(l)
    row = (t // GRID_W).astype(np.float64)
    col = (t % GRID_W).astype(np.float64)
    n_pairs = HEAD_DIM // 4
    inv_freq = ROPE_THETA ** (-np.arange(n_pairs, dtype=np.float64) / n_pairs)
    ang = np.concatenate([row[:, None] * inv_freq, col[:, None] * inv_freq], axis=-1)
    cos = np.repeat(np.cos(ang), 2, axis=1)
    sin = np.repeat(np.sin(ang), 2, axis=1)
    even = (np.arange(HEAD_DIM) % 2 == 0)[None, :]
    se = np.where(even, -sin, 0.0)
    so = np.where(even, 0.0, sin)
    pad = lambda a, v: np.concatenate([a, np.full((c, HEAD_DIM), v)], axis=0)
    two = lambda a: jnp.asarray(np.concatenate([a, a], axis=1).astype(np.float32))
    return two(pad(cos, 1.0)), two(pad(se, 0.0)), two(pad(so, 0.0))


def _block_diag_ones():
    i = np.arange(2 * LANES)
    return jnp.asarray((i[:, None] // HEAD_DIM == i[None, :] // HEAD_DIM).astype(np.float32)).astype(BF16)


def _gain_rows(gains, scales):
    rows = [jnp.tile(g.astype(F32) * s, 2 * LANES // HEAD_DIM) for g, s in zip(gains, scales)]
    rows += [jnp.zeros((2 * LANES,), F32)] * (8 - len(rows))
    return jnp.stack(rows, axis=0)


def kernel(x, c, ctx, c_ctx, w_ada, b_ada, w_up, ffn_conv_w, ffn_conv_b, w_down, w_in_e, w_out_e, na_q_gain, na_k_gain, na_rpb, da_q_gain, da_k_gain, da_lambda_q1, da_lambda_k1, da_lambda_q2, da_lambda_k2, da_subln_gain, w_in_o, w_out_o, gqa_q_gain, gqa_k_gain, hy_conv_w, hy_conv_b, hy_w1, hy_b1, hy_w2, hy_b2, hy_w3, hy_b3, hy_w4, hy_freq, hy_skip):
    b, l, d = x.shape
    cl = ctx.shape[1]
    t = l + cl
    depth = w_ada.shape[0]
    f = w_down.shape[1]
    assert cl == TM and (l // GRID_W) >= NA_WROWS + NA_QROWS
    assert w_in_e.shape[2] == 3072 and w_in_o.shape[2] == 2304 and d % LANES == 0
    scale = HEAD_DIM ** -0.5 * LOG2E

    rows = -(-(b + 1) // 8) * 8
    cs = jnp.zeros((rows, d), F32).at[:b].set(c).at[b].set(c_ctx)
    mods = _ada_call(cs, w_ada, b_ada)
    modarrs = []
    for layer in range(depth):
        lat = mods[layer, :b].reshape(b, 1, 6, d)
        cx = jnp.broadcast_to(mods[layer, b].reshape(1, 1, 6, d), (b, 1, 6, d))
        modarrs.append(jnp.concatenate([lat, cx], axis=1))

    ropes = _rope_tables(l, cl)
    bd = _block_diag_ones()
    x_all, h = _mod0_call(x, ctx, modarrs[0])

    for layer in range(depth):
        i = layer // 2
        if layer % 2 == 0:
            lam_init = 0.8 - 0.6 * math.exp(-0.3 * layer)
            gains = _gain_rows((na_q_gain[i], na_k_gain[i], da_q_gain[i], da_k_gain[i]),
                               (scale, 1.0, scale, 1.0))
            qkv, vat, qbt, vbt = _inproj_call(
                _inproj_even_body, "inproj_even", h, w_in_e[i].astype(BF16), bd, gains, ropes, (1536,),
                ((b, 8, HEAD_DIM + ONES_ROWS, t), (b, 512, t), (b, 4, LANES + ONES_ROWS, t)))
            bias = _na_bias_table(na_rpb[i])
            ya = _na_call(qkv, vat, bias, l)
            lamv = jnp.stack([da_lambda_q1[i], da_lambda_k1[i], da_lambda_q2[i], da_lambda_k2[i]]).astype(F32)
            yb = _flash_call("da_attn", qbt, qkv, vbt, 1024 // 256, False, LANES, TK_DA,
                             _make_da_finish(lam_init),
                             (lamv, da_subln_gain[i].reshape(1, LANES).astype(F32)), 512, l)
            w_out = w_out_e[i].astype(BF16)
        else:
            gains = _gain_rows((gqa_q_gain[i], gqa_k_gain[i]), (scale, 1.0))
            kd, x0, z, qt, vt = _inproj_call(
                functools.partial(_inproj_odd_body, nl=l // TM, nt=t // TM), "inproj_odd", h,
                w_in_o[i].astype(BF16), bd, gains, ropes, (2 * LANES, 512, 512),
                ((b, 512, t), (b, 2, HEAD_DIM + ONES_ROWS, t)),
                extra=(hy_conv_w[i], hy_conv_b[i].reshape(1, -1)), halo=True)
            ya = _flash_call("gqa_attn", qt, kd, vt, 0, True, HEAD_DIM, TK_GQA,
                             _gqa_finish, (), 512, l)
            fparams = (hy_w1[i], hy_b1[i], hy_w2[i], hy_b2[i], hy_w3[i], hy_b3[i], hy_w4[i], hy_freq[i])
            yd_l = _hyena_long(x0[:, :l], z[:, :l], fparams, hy_skip[i])
            if layer < depth - 1:
                yd_c = _hyena_short(x0[:, l:], z[:, l:], fparams, hy_skip[i])
            else:
                yd_c = jnp.zeros((b, cl, x0.shape[2]), BF16)
            yb = jnp.concatenate([yd_l, yd_c], axis=1)
            w_out = w_out_o[i].astype(BF16)
        ka = ya.shape[2]
        x_all, act = _mix_ffn_up_call(ya, yb, w_out[:ka], w_out[ka:], x_all, modarrs[layer],
                                      w_up[layer].astype(BF16), ffn_conv_w[layer], ffn_conv_b[layer])
        if layer == depth - 1:
            return _ffn_last_call(act, w_down[layer].astype(BF16), x_all, modarrs[layer])
        x_all, h = _ffn_down_call(act, w_down[layer].astype(BF16), x_all, modarrs[layer],
                                  modarrs[layer + 1])
```

```python
import functools
import math

import numpy as np
import jax
import jax.numpy as jnp
from jax import lax
from jax.experimental import pallas as pl
from jax.experimental.pallas import tpu as pltpu

F32 = jnp.float32
BF16 = jnp.bfloat16

HEAD_DIM = 64
GRID_W = 64
ROPE_THETA = 10000.0
EPS = 1e-6
NA_WIN_R = 8
NA_WIN_C = 16
HY_EMB_DIM = 33
HY_FAST_DECAY = 0.3
HY_SLOW_DECAY = 1.5
HY_DECAY_TARGET = 1e-2

LANES = 128
TM = 256
TK_DA = 256
TK_GQA = 256
NA_QROWS = TM // GRID_W
NA_WROWS = NA_QROWS + NA_WIN_R
ONES_ROWS = 16
NEG = -1e30
LOG2E = 1.4426950408889634
VMEM_LIMIT = 56 * 1024 * 1024


def _cparams(n_axes):
    return pltpu.CompilerParams(dimension_semantics=("arbitrary",) * n_axes,
                                vmem_limit_bytes=VMEM_LIMIT)


def _modulate(x, sh, sc):
    ms = jnp.mean(x * x, axis=-1, keepdims=True)
    return x * lax.rsqrt(ms + EPS) * (1.0 + sc) + sh


def _seg_norm(y, bd, gain):
    ss = jnp.dot((y * y).astype(BF16), bd, preferred_element_type=F32)
    return y * lax.rsqrt(ss * (1.0 / HEAD_DIM) + EPS) * gain


def _rope(y, c, se, so):
    return y * c + pltpu.roll(y, LANES - 1, 1) * se + pltpu.roll(y, 1, 1) * so


def _dot_nt(a, b):
    return lax.dot_general(a, b, (((1,), (1,)), ((), ())), preferred_element_type=F32)


def _ada_body(c_ref, w_ref, b_ref, o_ref):
    c = c_ref[...]
    a = (c / (1.0 + jnp.exp(-c))).astype(BF16)
    o_ref[0] = jnp.dot(a, w_ref[0].astype(BF16), preferred_element_type=F32) + b_ref[0]


def _ada_call(cs, w_ada, b_ada):
    depth, d, n6 = w_ada.shape
    rows = cs.shape[0]
    tn = 1536
    return pl.pallas_call(
        _ada_body, grid=(depth, n6 // tn),
        in_specs=[pl.BlockSpec((rows, d), lambda l, n: (0, 0)),
                  pl.BlockSpec((1, d, tn), lambda l, n: (l, 0, n)),
                  pl.BlockSpec((1, 1, tn), lambda l, n: (l, 0, n))],
        out_specs=pl.BlockSpec((1, rows, tn), lambda l, n: (l, 0, n)),
        out_shape=jax.ShapeDtypeStruct((depth, rows, n6), F32),
        compiler_params=_cparams(2), name="ada",
    )(cs, w_ada, b_ada.reshape(depth, 1, n6))


def _mod0_call(x, ctx, modarr):
    b, l, d = x.shape
    c = ctx.shape[1]
    t = l + c
    nl = l // TM

    def body(x_ref, c_ref, m_ref, xo_ref, h_ref):
        j = pl.program_id(1)
        xv = jnp.where(j < nl, x_ref[0], c_ref[0])
        xo_ref[0] = xv
        h_ref[0] = _modulate(xv, m_ref[0, 0, 0:1, :], m_ref[0, 0, 1:2, :]).astype(BF16)

    return pl.pallas_call(
        body, grid=(b, t // TM),
        in_specs=[pl.BlockSpec((1, TM, d), lambda i, j: (i, jnp.minimum(j, nl - 1), 0)),
                  pl.BlockSpec((1, TM, d), lambda i, j: (i, 0, 0)),
                  pl.BlockSpec((1, 1, 6, d), lambda i, j: (i, j // nl, 0, 0))],
        out_specs=[pl.BlockSpec((1, TM, d), lambda i, j: (i, j, 0)),
                   pl.BlockSpec((1, TM, d), lambda i, j: (i, j, 0))],
        out_shape=[jax.ShapeDtypeStruct((b, t, d), F32), jax.ShapeDtypeStruct((b, t, d), BF16)],
        compiler_params=_cparams(2), name="mod0",
    )(x, ctx, modarr)


def _inproj_even_body(a_ref, w_ref, bd_ref, g_ref, rc_ref, rse_ref, rso_ref,
                      o_ref, vat_ref, qbt_ref, vbt_ref):
    a = a_ref[0]
    bd = bd_ref[...]
    rc, rse, rso = rc_ref[...], rse_ref[...], rso_ref[...]
    plan = ((0, False, 0), (1, False, 512), (None, False, None), (2, True, None), (3, True, 1024),
            (None, False, None))
    for seg, (gain_row, rope, out_col) in enumerate(plan):
        y = jnp.dot(a, w_ref[:, seg * 512:(seg + 1) * 512], preferred_element_type=F32)
        for half in range(2):
            yy = y[:, half * 256:(half + 1) * 256]
            if gain_row is not None:
                yy = _seg_norm(yy, bd, g_ref[gain_row:gain_row + 1, :])
            for blk in range(2):
                z = yy[:, blk * LANES:(blk + 1) * LANES]
                if rope:
                    z = _rope(z, rc, rse, rso)
                m = half * 2 + blk
                if out_col is not None:
                    col = out_col + m * LANES
                    o_ref[0, :, col:col + LANES] = z.astype(BF16)
                    continue
                zt = z.T.astype(BF16)
                if seg == 2:
                    vat_ref[0, 2 * m, 0:HEAD_DIM, :] = zt[:HEAD_DIM]
                    vat_ref[0, 2 * m + 1, 0:HEAD_DIM, :] = zt[HEAD_DIM:]
                elif seg == 3:
                    qbt_ref[0, m * LANES:(m + 1) * LANES, :] = zt
                else:
                    vbt_ref[0, m, 0:LANES, :] = zt
    ones = jnp.ones((ONES_ROWS, TM), BF16)
    for hd in range(vat_ref.shape[1]):
        vat_ref[0, hd, HEAD_DIM:HEAD_DIM + ONES_ROWS, :] = ones
    for hd in range(vbt_ref.shape[1]):
        vbt_ref[0, hd, LANES:LANES + ONES_ROWS, :] = ones


def _inproj_odd_body(a_ref, ap_ref, an_ref, w_ref, bd_ref, g_ref, rc_ref, rse_ref, rso_ref,
                     cw_ref, cb_ref, o_ref, x0_ref, z_ref, qt_ref, vt_ref, *, nl, nt):
    j = pl.program_id(1)
    a = a_ref[0]
    bd = bd_ref[...]
    rc, rse, rso = rc_ref[...], rse_ref[...], rso_ref[...]
    y = jnp.dot(a, w_ref[:, 0:512], preferred_element_type=F32)
    for half in range(2):
        yy = _seg_norm(y[:, half * 256:(half + 1) * 256], bd, g_ref[0:1, :])
        for blk in range(2):
            z = _rope(yy[:, blk * LANES:(blk + 1) * LANES], rc, rse, rso)
            m = half * 2 + blk
            qt_ref[0, m * LANES:(m + 1) * LANES, :] = z.T.astype(BF16)
    y = jnp.dot(a, w_ref[:, 512:768], preferred_element_type=F32)
    k = _seg_norm(y[:, :LANES], bd_ref[0:LANES, 0:LANES], g_ref[1:2, 0:LANES])
    k = _rope(k, rc, rse, rso)
    kr = pltpu.roll(k, HEAD_DIM, 1)
    lo = lax.broadcasted_iota(jnp.int32, k.shape, 1) < HEAD_DIM
    o_ref[0, :, 0:LANES] = jnp.where(lo, k, kr).astype(BF16)
    o_ref[0, :, LANES:2 * LANES] = jnp.where(lo, kr, k).astype(BF16)
    vt = y[:, LANES:].T.astype(BF16)
    ones = jnp.ones((ONES_ROWS, TM), BF16)
    for hd in range(2):
        vt_ref[0, hd, 0:HEAD_DIM, :] = vt[hd * HEAD_DIM:(hd + 1) * HEAD_DIM]
        vt_ref[0, hd, HEAD_DIM:HEAD_DIM + ONES_ROWS, :] = ones
    hb = ap_ref.shape[1]
    zero = jnp.zeros((hb, a.shape[1]), BF16)
    ap = jnp.where((j == 0) | (j == nl), zero, ap_ref[0])
    an = jnp.where((j == nl - 1) | (j == nt - 1), zero, an_ref[0])
    a_ext = jnp.concatenate([ap, a, an], axis=0)
    conv = []
    for seg in range(3):
        sl = slice(seg * 512, (seg + 1) * 512)
        g = jnp.dot(a_ext, w_ref[:, 768 + seg * 512:768 + (seg + 1) * 512], preferred_element_type=F32)
        conv.append(g[hb - 1:hb - 1 + TM] * cw_ref[0:1, sl] + g[hb:hb + TM] * cw_ref[1:2, sl]
                    + g[hb + 1:hb + 1 + TM] * cw_ref[2:3, sl] + cb_ref[0:1, sl])
        if seg == 0:
            x0_ref[0] = conv[0].astype(BF16)
    z_ref[0] = (conv[2] * conv[1]).astype(BF16)


def _inproj_call(body, name, h, w, bd, gains, ropes, n_outs, t_shapes, extra=(), halo=False):
    b, t, d = h.shape
    n_in = w.shape[1]
    nt = t // TM
    hb = 16
    r = TM // hb
    rc, rse, rso = ropes

    def t_spec(shape):
        nd = len(shape)
        return pl.BlockSpec((1,) + tuple(shape[1:-1]) + (TM,), lambda i, j: (i,) + (0,) * (nd - 2) + (j,))

    acts, act_specs = [h], [pl.BlockSpec((1, TM, d), lambda i, j: (i, j, 0))]
    if halo:
        acts += [h, h]
        act_specs += [pl.BlockSpec((1, hb, d), lambda i, j: (i, jnp.maximum(j * r - 1, 0), 0)),
                      pl.BlockSpec((1, hb, d), lambda i, j: (i, jnp.minimum((j + 1) * r, nt * r - 1), 0))]
    return pl.pallas_call(
        body, grid=(b, nt),
        in_specs=act_specs
        + [pl.BlockSpec((d, n_in), lambda i, j: (0, 0)),
           pl.BlockSpec(bd.shape, lambda i, j: (0, 0)),
           pl.BlockSpec(gains.shape, lambda i, j: (0, 0)),
           pl.BlockSpec((TM, LANES), lambda i, j: (j, 0)),
           pl.BlockSpec((TM, LANES), lambda i, j: (j, 0)),
           pl.BlockSpec((TM, LANES), lambda i, j: (j, 0))]
        + [pl.BlockSpec(e.shape, lambda i, j: (0,) * e.ndim) for e in extra],
        out_specs=[pl.BlockSpec((1, TM, n), lambda i, j: (i, j, 0)) for n in n_outs]
        + [t_spec(s) for s in t_shapes],
        out_shape=[jax.ShapeDtypeStruct((b, t, n), BF16) for n in n_outs]
        + [jax.ShapeDtypeStruct(s, BF16) for s in t_shapes],
        compiler_params=_cparams(2), name=name,
    )(*acts, w, bd, gains, rc, rse, rso, *extra)


def _mix_ffn_up_call(ya, yb, wa, wb, x, modarr, w_up, conv_w, conv_b):
    b, t, d = x.shape
    f = w_up.shape[1] // 2
    nl = (t - TM) // TM
    nt = t // TM
    hb = 16
    r = TM // hb
    ka, kb = ya.shape[2], yb.shape[2]
    chunks = [(c0, min(2 * LANES, f - c0)) for c0 in range(0, f, 2 * LANES)]

    def body(ya_ref, yap_ref, yan_ref, yb_ref, ybp_ref, ybn_ref, x_ref, xp_ref, xn_ref,
             wa_ref, wb_ref, m_ref, wg_ref, wv_ref, cw_ref, cb_ref, xo_ref, o_ref):
        j = pl.program_id(1)
        ext = lambda p, c, n: jnp.concatenate([p[0], c[0], n[0]], axis=0)
        y = (jnp.dot(ext(yap_ref, ya_ref, yan_ref), wa_ref[...], preferred_element_type=F32)
             + jnp.dot(ext(ybp_ref, yb_ref, ybn_ref), wb_ref[...], preferred_element_type=F32))
        x1 = ext(xp_ref, x_ref, xn_ref) + m_ref[0, 0, 2:3, :] * y
        xo_ref[0] = x1[hb:hb + TM]
        hx = _modulate(x1, m_ref[0, 0, 3:4, :], m_ref[0, 0, 4:5, :]).astype(BF16)
        zero = jnp.zeros((hb, d), BF16)
        a = hx[hb:hb + TM]
        ap = jnp.where((j == 0) | (j == nl), zero, hx[0:hb])
        an = jnp.where((j == nl - 1) | (j == nt - 1), zero, hx[hb + TM:])
        a_ext = jnp.concatenate([ap, a, an], axis=0)
        for c0, cw in chunks:
            sl = slice(c0, c0 + cw)
            g = jnp.dot(a_ext, wg_ref[:, sl], preferred_element_type=F32)
            u = (g[hb - 1:hb - 1 + TM] * cw_ref[0:1, sl] + g[hb:hb + TM] * cw_ref[1:2, sl]
                 + g[hb + 1:hb + 1 + TM] * cw_ref[2:3, sl] + cb_ref[0:1, sl])
            v = jnp.dot(a, wv_ref[:, sl], preferred_element_type=F32)
            o_ref[0, :, sl] = ((u / (1.0 + jnp.exp(-u))) * v).astype(BF16)

    main = lambda w: pl.BlockSpec((1, TM, w), lambda i, j: (i, j, 0))
    prev = lambda w: pl.BlockSpec((1, hb, w), lambda i, j: (i, jnp.maximum(j * r - 1, 0), 0))
    nxt = lambda w: pl.BlockSpec((1, hb, w), lambda i, j: (i, jnp.minimum((j + 1) * r, nt * r - 1), 0))
    full = lambda a: pl.BlockSpec(a.shape, lambda i, j: (0,) * a.ndim)
    return pl.pallas_call(
        body, grid=(b, nt),
        in_specs=[main(ka), prev(ka), nxt(ka), main(kb), prev(kb), nxt(kb), main(d), prev(d), nxt(d),
                  full(wa), full(wb), pl.BlockSpec((1, 1, 6, d), lambda i, j: (i, j // nl, 0, 0)),
                  pl.BlockSpec((d, f), lambda i, j: (0, 0)), pl.BlockSpec((d, f), lambda i, j: (0, 1)),
                  pl.BlockSpec((3, f), lambda i, j: (0, 0)), pl.BlockSpec((1, f), lambda i, j: (0, 0))],
        out_specs=[pl.BlockSpec((1, TM, d), lambda i, j: (i, j, 0)),
                   pl.BlockSpec((1, TM, f), lambda i, j: (i, j, 0))],
        out_shape=[jax.ShapeDtypeStruct((b, t, d), F32), jax.ShapeDtypeStruct((b, t, f), BF16)],
        compiler_params=_cparams(2), name="mix_ffn_up",
    )(ya, ya, ya, yb, yb, yb, x, x, x, wa, wb, modarr, w_up, w_up, conv_w, conv_b.reshape(1, f))


def _ffn_down_call(act, w_down, x, modarr, modarr_next):
    b, t, d = x.shape
    f = w_down.shape[0]
    nl = (t - TM) // TM
    nt = t // TM

    def body(a_ref, wd_ref, x_ref, m_ref, mn_ref, xo_ref, h_ref):
        y = jnp.dot(a_ref[0], wd_ref[...], preferred_element_type=F32)
        x2 = x_ref[0] + m_ref[0, 0, 5:6, :] * y
        xo_ref[0] = x2
        h_ref[0] = _modulate(x2, mn_ref[0, 0, 0:1, :], mn_ref[0, 0, 1:2, :]).astype(BF16)

    return pl.pallas_call(
        body, grid=(b, nt),
        in_specs=[pl.BlockSpec((1, TM, f), lambda i, j: (i, j, 0)),
                  pl.BlockSpec((f, d), lambda i, j: (0, 0)),
                  pl.BlockSpec((1, TM, d), lambda i, j: (i, j, 0)),
                  pl.BlockSpec((1, 1, 6, d), lambda i, j: (i, j // nl, 0, 0)),
                  pl.BlockSpec((1, 1, 6, d), lambda i, j: (i, j // nl, 0, 0))],
        out_specs=[pl.BlockSpec((1, TM, d), lambda i, j: (i, j, 0)),
                   pl.BlockSpec((1, TM, d), lambda i, j: (i, j, 0))],
        out_shape=[jax.ShapeDtypeStruct((b, t, d), F32), jax.ShapeDtypeStruct((b, t, d), BF16)],
        compiler_params=_cparams(2), name="ffn_down",
    )(act, w_down, x, modarr, modarr_next)


def _ffn_last_call(act, w_down, x, modarr):
    b, t, d = x.shape
    f = w_down.shape[0]
    nl = (t - TM) // TM

    def body(a_ref, wd_ref, x_ref, m_ref, xo_ref):
        y = jnp.dot(a_ref[0], wd_ref[...], preferred_element_type=F32)
        xo_ref[0] = x_ref[0] + m_ref[0, 0, 5:6, :] * y

    return pl.pallas_call(
        body, grid=(b, nl),
        in_specs=[pl.BlockSpec((1, TM, f), lambda i, j: (i, j, 0)),
                  pl.BlockSpec((f, d), lambda i, j: (0, 0)),
                  pl.BlockSpec((1, TM, d), lambda i, j: (i, j, 0)),
                  pl.BlockSpec((1, 1, 6, d), lambda i, j: (i, 0, 0, 0))],
        out_specs=pl.BlockSpec((1, TM, d), lambda i, j: (i, j, 0)),
        out_shape=jax.ShapeDtypeStruct((b, nl * TM, d), F32),
        compiler_params=_cparams(2), name="ffn_last",
    )(act, w_down, x, modarr)


def _split_heads(q):
    lane = lax.broadcasted_iota(jnp.int32, q.shape, 1)
    zero = jnp.zeros_like(q)
    return jnp.where(lane < HEAD_DIM, q, zero), jnp.where(lane >= HEAD_DIM, q, zero)


def _flash_call(name, qt, qkv, vt, kcol, shared_kv, vrows, tk, finish, extra, out_cols, l):
    b, t, _ = qkv.shape
    c = t - l
    nl = l // TM
    n_chunks = l // tk
    n_steps = out_cols // (2 * LANES)
    n_kv = 1 if shared_kv else 2
    ns = 4
    nb = 2
    assert l % tk == 0 and n_chunks >= 2

    def body(q_ref, qn_ref, k_ref, vt_ref, *rest):
        extra_refs = rest[:-8]
        o_ref, s_ref, p_ref, sc_ref, pc_ref, acc_ref, m_ref, a_ref = rest[-8:]
        tile = pl.program_id(2)
        is_lat = tile < nl
        row = lax.broadcasted_iota(jnp.int32, (LANES, TM), 0)

        def streams(ref):
            out = []
            for blk in range(2):
                qb = ref[0, blk * LANES:(blk + 1) * LANES, :]
                out += [jnp.where(row < HEAD_DIM, qb, jnp.zeros_like(qb)),
                        jnp.where(row >= HEAD_DIM, qb, jnp.zeros_like(qb))]
            return out

        kv_of = lambda s: 0 if shared_kv else s // 2

        def scores(off, size, dst, qs):
            kcs = [k_ref[0, pl.ds(off, size), j * LANES:(j + 1) * LANES] for j in range(n_kv)]
            for s in range(ns):
                dst(s)[...] = jnp.dot(kcs[kv_of(s)], qs[s], preferred_element_type=F32)

        def softmax(src, dst, slot):
            for s in range(ns):
                st = src(s)[...]
                m = m_ref[s]
                mnew = jnp.maximum(m, jnp.max(st, axis=0, keepdims=True))
                a_ref[slot, s] = jnp.exp2(m - mnew)
                dst(s)[...] = jnp.exp2((st - mnew).astype(BF16))
                m_ref[s] = mnew

        def values(src, slot, off, size):
            vcs = [vt_ref[0, j, :, pl.ds(off, size)] for j in range(n_kv)]
            for s in range(ns):
                acc_ref[s] = a_ref[slot, s] * acc_ref[s] + jnp.dot(vcs[kv_of(s)], src(s)[...],
                                                                  preferred_element_type=F32)

        s_ctx, p_ctx = (lambda s: sc_ref.at[s]), (lambda s: pc_ref.at[s])
        s_buf = [(lambda s, i=i: s_ref.at[i, s]) for i in range(nb)]
        p_buf = [(lambda s, i=i: p_ref.at[i, s]) for i in range(nb)]

        def start():
            acc_ref[...] = jnp.zeros(acc_ref.shape, F32)
            m_ref[...] = jnp.full(m_ref.shape, NEG, F32)

        def finalize():
            outs = [acc_ref[s, 0:vrows, :] / acc_ref[s, vrows:vrows + 1, :] for s in range(ns)]
            o_ref[0] = finish(outs, *extra_refs)

        @pl.when(tile == 0)
        def _():
            qs = streams(q_ref)
            scores(l, c, s_ctx, qs)
            scores(0, tk, s_buf[0], qs)

        @pl.when(is_lat)
        def _():
            qs = streams(q_ref)
            qs_next = streams(qn_ref)
            start()
            softmax(s_ctx, p_ctx, nb)
            values(p_ctx, nb, l, c)
            scores(tk, tk, s_buf[1], qs)
            softmax(s_buf[0], p_buf[0], 0)

            for k in range(2, n_chunks):
                values(p_buf[(k - 2) % nb], (k - 2) % nb, (k - 2) * tk, tk)
                scores(k * tk, tk, s_buf[k % nb], qs)
                softmax(s_buf[(k - 1) % nb], p_buf[(k - 1) % nb], (k - 1) % nb)
            k = n_chunks
            values(p_buf[(k - 2) % nb], (k - 2) % nb, (k - 2) * tk, tk)
            scores(l, c, s_ctx, qs_next)
            softmax(s_buf[(k - 1) % nb], p_buf[(k - 1) % nb], (k - 1) % nb)
            scores(0, tk, s_buf[0], qs_next)
            values(p_buf[(k - 1) % nb], (k - 1) % nb, (k - 1) * tk, tk)
            finalize()

        @pl.when(jnp.logical_not(is_lat))
        def _():
            start()
            softmax(s_ctx, p_ctx, nb)
            values(p_ctx, nb, l, c)
            finalize()

    kw = n_kv * LANES
    extra_specs = [pl.BlockSpec(e.shape, lambda bi, h, i: (0,) * e.ndim) for e in extra]
    return pl.pallas_call(
        body, grid=(b, n_steps, t // TM),
        in_specs=[pl.BlockSpec((1, 2 * LANES, TM), lambda bi, h, i: (bi, h, i)),
                  pl.BlockSpec((1, 2 * LANES, TM), lambda bi, h, i: (bi, h, jnp.minimum(i + 1, nl))),
                  pl.BlockSpec((1, t, kw), lambda bi, h, i: (bi, 0, kcol + h)),
                  pl.BlockSpec((1, n_kv, vrows + ONES_ROWS, t), lambda bi, h, i: (bi, h, 0, 0))]
        + extra_specs,
        out_specs=pl.BlockSpec((1, TM, 2 * LANES), lambda bi, h, i: (bi, i, h)),
        out_shape=jax.ShapeDtypeStruct((b, t, out_cols), BF16),
        scratch_shapes=[pltpu.VMEM((nb, ns, tk, TM), F32), pltpu.VMEM((nb, ns, tk, TM), BF16),
                        pltpu.VMEM((ns, c, TM), F32), pltpu.VMEM((ns, c, TM), BF16),
                        pltpu.VMEM((ns, vrows + ONES_ROWS, TM), F32), pltpu.VMEM((ns, 1, TM), F32),
                        pltpu.VMEM((nb + 1, ns, 1, TM), F32)],
        compiler_params=_cparams(3), name=name,
    )(qt, qt, qkv, vt, *extra)


def _gqa_finish(outs):
    return jnp.concatenate(outs, axis=0).T.astype(BF16)


def _make_da_finish(lam_init):
    def finish(outs, lam_ref, subln_ref):
        lv = lam_ref[...]
        lam = (jnp.exp(jnp.sum(lv[0:1] * lv[1:2], keepdims=True))
               - jnp.exp(jnp.sum(lv[2:3] * lv[3:4], keepdims=True)) + lam_init)
        heads = []
        for hd in range(len(outs) // 2):
            o = (outs[2 * hd] - lam * outs[2 * hd + 1]).T
            o = o * lax.rsqrt(jnp.mean(o * o, axis=-1, keepdims=True) + EPS)
            heads.append((o * subln_ref[...] * (1.0 - lam_init)).astype(BF16))
        return jnp.concatenate(heads, axis=1)
    return finish


def _na_call(qkv, vt, bias, l):
    b, t, _ = qkv.shape
    c = t - l
    nl = l // TM
    rows = l // GRID_W
    win = NA_WROWS * GRID_W
    n_pairs = vt.shape[1] // 2

    nt = t // TM

    def window(tile):
        return pl.multiple_of(jnp.clip(tile * NA_QROWS - NA_WIN_R // 2, 0, rows - NA_WROWS) * GRID_W, 256)

    hp = 1

    def body(q0_ref, qn_ref, k_ref, vt_ref, b0_ref, bn_ref, o_ref, sw_ref, sc_ref):
        g = pl.program_id(2)

        def scores(q_ref, b_ref, tile, buf, with_win):
            for blk in range(hp):
                cols = slice(blk * LANES, (blk + 1) * LANES)
                qs = _split_heads(q_ref[0, :, cols])
                if with_win:
                    kw = k_ref[0, pl.ds(window(tile), win), cols]
                    for half in range(2):
                        sw_ref[buf, 2 * blk + half] = _dot_nt(kw, qs[half]) + b_ref[blk, half]
                kc = k_ref[0, l:l + c, cols]
                for half in range(2):
                    sc_ref[buf, 2 * blk + half] = _dot_nt(kc, qs[half])

        def finish(tile, buf, with_win):
            outs = []
            for s in range(2 * hp):
                parts = [(sc_ref[buf, s], vt_ref[0, s, :, l:l + c])]
                if with_win:
                    parts.append((sw_ref[buf, s], vt_ref[0, s, :, pl.ds(window(tile), win)]))
                m = functools.reduce(jnp.maximum, [jnp.max(sc, axis=0, keepdims=True) for sc, _ in parts])
                acc = sum(jnp.dot(v, jnp.exp2((sc - m).astype(BF16)), preferred_element_type=F32)
                          for sc, v in parts)
                outs.append(acc[:HEAD_DIM] / acc[HEAD_DIM:HEAD_DIM + 1])
            o_ref[0] = jnp.concatenate(outs, axis=0).T.astype(BF16)

        cur, nxt = g % 2, (g + 1) % 2

        @pl.when(g == 0)
        def _():
            scores(q0_ref, b0_ref, 0, 0, True)

        @pl.when(g + 1 < nl)
        def _():
            scores(qn_ref, bn_ref, g + 1, nxt, True)
            finish(g, cur, True)

        @pl.when(g + 1 == nl)
        def _():
            scores(qn_ref, bn_ref, g + 1, nxt, False)
            finish(g, cur, True)

        @pl.when(g == nl)
        def _():
            finish(g, cur, False)

    n_steps = n_pairs // hp
    wq = hp * LANES

    def bias_next(bi, h, g):
        tile = g + 1
        case = jnp.where(tile >= nl - 1, 2, 1)
        return (case * n_steps + h, 0, 0, 0)

    return pl.pallas_call(
        body, grid=(b, n_steps, nt),
        in_specs=[pl.BlockSpec((1, TM, wq), lambda bi, h, g: (bi, 0, h)),
                  pl.BlockSpec((1, TM, wq), lambda bi, h, g: (bi, jnp.minimum(g + 1, nt - 1), h)),
                  pl.BlockSpec((1, t, wq), lambda bi, h, g: (bi, 0, n_steps + h)),
                  pl.BlockSpec((1, 2 * hp, HEAD_DIM + ONES_ROWS, t), lambda bi, h, g: (bi, h, 0, 0)),
                  pl.BlockSpec((hp, 2, win, TM), lambda bi, h, g: (h, 0, 0, 0)),
                  pl.BlockSpec((hp, 2, win, TM), bias_next)],
        out_specs=pl.BlockSpec((1, TM, wq), lambda bi, h, g: (bi, g, h)),
        out_shape=jax.ShapeDtypeStruct((b, t, n_pairs * LANES), BF16),
        scratch_shapes=[pltpu.VMEM((2, 2 * hp, win, TM), F32), pltpu.VMEM((2, 2 * hp, c, TM), F32)],
        compiler_params=_cparams(3), name="na_attn",
    )(qkv, qkv, qkv, vt, bias, bias)


def _na_bias_table(rpb):
    h, n_ro, n_co = rpb.shape
    kr, qr = np.arange(NA_WROWS), np.arange(NA_QROWS)
    kc, qc = np.arange(GRID_W), np.arange(GRID_W)
    cs = np.clip(qc - NA_WIN_C // 2, 0, GRID_W - NA_WIN_C)
    col_ok = (kc[:, None] >= cs[None, :]) & (kc[:, None] < cs[None, :] + NA_WIN_C)
    co = kc[:, None] - qc[None, :] + NA_WIN_C - 1
    col_sel = (co[None] == np.arange(n_co)[:, None, None]).astype(np.float32)
    tables = []
    for d, rel in ((0, np.zeros_like(qr)), (NA_WIN_R // 2, qr), (NA_WIN_R, np.full_like(qr, NA_WIN_R // 2))):
        row_ok = (kr[:, None] >= rel[None, :]) & (kr[:, None] < rel[None, :] + NA_WIN_R)
        ro = kr[:, None] - qr[None, :] - d + NA_WIN_R - 1
        row_sel = (ro[:, :, None] == np.arange(n_ro)[None, None, :]).astype(np.float32)
        tb = jnp.einsum('kqa,hab,bcd->hkcqd', jnp.asarray(row_sel), rpb.astype(F32) * LOG2E,
                        jnp.asarray(col_sel), precision=lax.Precision.HIGHEST)
        ok = row_ok[:, None, :, None] & col_ok[None, :, None, :]
        tables.append(jnp.where(jnp.asarray(ok)[None], tb, NEG))
    tbl = jnp.stack(tables, axis=0)
    return tbl.reshape(3 * (h // 2), 2, NA_WROWS * GRID_W, TM).astype(F32)


def _hy_filter_call(length, w1, b1, w2, b2, w3, b3, w4, freq):
    order = w2.shape[0]
    w = w4.shape[1] // 2
    tl = min(length, 512)
    hi = lax.Precision.HIGHEST
    t = np.linspace(0.0, 1.0, length, dtype=np.float64)[:, None]
    bands = (HY_EMB_DIM - 1) // 2
    ang = 2.0 * math.pi * np.arange(length, dtype=np.float64)[:, None] / length
    fq = np.linspace(1e-4, bands - 1, bands, dtype=np.float64)[None, :]
    emb = np.concatenate([t, np.cos(fq * ang), -np.sin(fq * ang)], axis=-1).astype(np.float32)
    emb = np.pad(emb, ((0, 0), (0, LANES - HY_EMB_DIM)))
    max_decay = math.log(HY_DECAY_TARGET) / HY_FAST_DECAY
    min_decay = math.log(HY_DECAY_TARGET) / HY_SLOW_DECAY
    deltas = np.linspace(min_decay, max_decay, w, dtype=np.float64)
    decay = np.exp(-t * np.abs(deltas)[None, :]).astype(np.float32)
    decay2 = np.concatenate([decay, decay], axis=1)
    w1p = jnp.pad(w1, ((0, LANES - HY_EMB_DIM), (0, 0)))

    def body(e_ref, d_ref, w1_ref, b1_ref, w2_ref, b2_ref, w3_ref, b3_ref, w4_ref, f_ref,
             h_ref, s_ref):
        i = pl.program_id(0)
        fr = f_ref[...]
        hdn = jnp.sin(fr * (jnp.dot(e_ref[...], w1_ref[...], precision=hi,
                                    preferred_element_type=F32) + b1_ref[...]))
        hdn = jnp.sin(fr * (jnp.dot(hdn, w2_ref[...], precision=hi,
                                    preferred_element_type=F32) + b2_ref[...]))
        hdn = jnp.sin(fr * (jnp.dot(hdn, w3_ref[...], precision=hi,
                                    preferred_element_type=F32) + b3_ref[...]))
        taps = jnp.dot(hdn, w4_ref[...], precision=hi, preferred_element_type=F32) * d_ref[...]
        row = lax.broadcasted_iota(jnp.int32, taps.shape, 0) + i * tl
        col = lax.broadcasted_iota(jnp.int32, taps.shape, 1)
        taps = jnp.where((row == 0) & (col >= w), 0.0, taps)
        h_ref[...] = taps

        @pl.when(i == 0)
        def _():
            s_ref[...] = jnp.zeros(s_ref.shape, F32)
        s_ref[...] += jnp.sum(jnp.abs(taps), axis=0, keepdims=True)

    full = lambda a: pl.BlockSpec(a.shape, lambda i: (0,) * a.ndim)
    ops = (w1p, b1.reshape(1, order), w2, b2.reshape(1, order), w3, b3.reshape(1, order), w4,
           freq.reshape(1, order))
    return pl.pallas_call(
        body, grid=(length // tl,),
        in_specs=[pl.BlockSpec((tl, LANES), lambda i: (i, 0)),
                  pl.BlockSpec((tl, 2 * w), lambda i: (i, 0))] + [full(a) for a in ops],
        out_specs=[pl.BlockSpec((tl, 2 * w), lambda i: (i, 0)),
                   pl.BlockSpec((1, 2 * w), lambda i: (0, 0))],
        out_shape=[jax.ShapeDtypeStruct((length, 2 * w), F32), jax.ShapeDtypeStruct((1, 2 * w), F32)],
        compiler_params=_cparams(1), name="hy_filter",
    )(jnp.asarray(emb), jnp.asarray(decay2), *ops)


def _dft_tables(l1, fb):
    n1 = 2 * l1
    n = n1 * LANES
    nf = -(-(n1 // 2 + 1) // fb) * fb
    f1 = np.arange(nf, dtype=np.float64)
    wgt = np.where((f1 == 0) | (f1 == n1 // 2), 1.0, np.where(f1 < n1 // 2, 2.0, 0.0))[None, :]
    t1 = np.arange(l1, dtype=np.float64)
    th1 = 2.0 * np.pi * np.outer(f1, t1) / n1
    fwd1 = np.concatenate([np.cos(th1), -np.sin(th1)], axis=0)
    inv1 = np.concatenate([wgt * np.cos(th1).T, -wgt * np.sin(th1).T], axis=1) / n
    f2 = np.arange(LANES, dtype=np.float64)
    t2 = np.arange(LANES, dtype=np.float64)
    fr = f1[:, None, None] + n1 * f2[None, :, None]
    th2 = 2.0 * np.pi * fr * t2[None, None, :] / n
    gr, gi = np.cos(th2), -np.sin(th2)
    gb = np.concatenate([np.concatenate([gr, -gi], axis=2),
                         np.concatenate([gi, gr], axis=2)], axis=1)
    hb = np.transpose(gb, (0, 2, 1))
    as_bf = lambda a: jnp.asarray(a.astype(np.float32)).astype(BF16)
    return as_bf(fwd1), as_bf(inv1), as_bf(gb), as_bf(hb)


def _hy_stage1_call(xv, fwd1):
    bx, l1, cols = xv.shape
    n2 = fwd1.shape[0]
    tn = min(cols, 4096)

    def body(f_ref, x_ref, o_ref):
        o_ref[0] = jnp.dot(f_ref[...], x_ref[0], preferred_element_type=F32).astype(BF16)

    return pl.pallas_call(
        body, grid=(bx, cols // tn),
        in_specs=[pl.BlockSpec((n2, l1), lambda i, j: (0, 0)),
                  pl.BlockSpec((1, l1, tn), lambda i, j: (i, 0, j))],
        out_specs=pl.BlockSpec((1, n2, tn), lambda i, j: (i, 0, j)),
        out_shape=jax.ShapeDtypeStruct((bx, n2, cols), BF16),
        compiler_params=_cparams(2), name="hy_dft1",
    )(fwd1, xv)


def _hy_filter_spec_call(a5, gb, sums, fb):
    n1, w2 = a5.shape[2], a5.shape[4]
    w = w2 // 2

    def body(a_ref, g_ref, s_ref, o_ref):
        sv = s_ref[...]
        inv = 1.0 / (sv[:, :w] + sv[:, w:])
        for k in range(fb):
            a = jnp.concatenate([a_ref[0, 0, k], a_ref[0, 1, k]], axis=0)
            z = jnp.dot(g_ref[k], a, preferred_element_type=F32)
            o_ref[k, 0] = (z[:LANES, :w] + z[:LANES, w:]) * inv
            o_ref[k, 1] = (z[LANES:, :w] - z[LANES:, w:]) * inv

    return pl.pallas_call(
        body, grid=(n1 // fb,),
        in_specs=[pl.BlockSpec((1, 2, fb, LANES, w2), lambda i: (0, 0, i, 0, 0)),
                  pl.BlockSpec((fb, 2 * LANES, 2 * LANES), lambda i: (i, 0, 0)),
                  pl.BlockSpec((1, w2), lambda i: (0, 0))],
        out_specs=pl.BlockSpec((fb, 2, LANES, w), lambda i: (i, 0, 0, 0)),
        out_shape=jax.ShapeDtypeStruct((n1, 2, LANES, w), F32),
        compiler_params=_cparams(1), name="hy_fspec",
    )(a5, gb, sums)


def _hy_stage23_call(a5, gb, hb, kf, fb):
    b, _, n1, _, w = a5.shape

    def body(a_ref, g_ref, h_ref, k_ref, o_ref):
        for k in range(fb):
            a = jnp.concatenate([a_ref[0, 0, k], a_ref[0, 1, k]], axis=0)
            z = jnp.dot(g_ref[k], a, preferred_element_type=F32)
            zr, zi = z[:LANES], z[LANES:]
            kr, ki = k_ref[k, 0], k_ref[k, 1]
            y = jnp.concatenate([zr * kr - zi * ki, zr * ki + zi * kr], axis=0).astype(BF16)
            cc = jnp.dot(h_ref[k], y, preferred_element_type=F32)
            o_ref[0, 0, k] = cc[:LANES].astype(BF16)
            o_ref[0, 1, k] = cc[LANES:].astype(BF16)

    return pl.pallas_call(
        body, grid=(n1 // fb, b),
        in_specs=[pl.BlockSpec((1, 2, fb, LANES, w), lambda i, j: (j, 0, i, 0, 0)),
                  pl.BlockSpec((fb, 2 * LANES, 2 * LANES), lambda i, j: (i, 0, 0)),
                  pl.BlockSpec((fb, 2 * LANES, 2 * LANES), lambda i, j: (i, 0, 0)),
                  pl.BlockSpec((fb, 2, LANES, w), lambda i, j: (i, 0, 0, 0))],
        out_specs=pl.BlockSpec((1, 2, fb, LANES, w), lambda i, j: (j, 0, i, 0, 0)),
        out_shape=jax.ShapeDtypeStruct(a5.shape, BF16),
        compiler_params=_cparams(2), name="hy_dft23",
    )(a5, gb, hb, kf)


def _hy_stage4_call(cv, inv1, x0v, zv, skip_t):
    b, n2, cols = cv.shape
    l1 = inv1.shape[0]
    tn = skip_t.shape[1]

    def body(f_ref, c_ref, x0_ref, z_ref, s_ref, o_ref):
        y = jnp.dot(f_ref[...], c_ref[0], preferred_element_type=F32)
        z = z_ref[0].astype(F32)
        o_ref[0] = (x0_ref[0].astype(F32) * (y + s_ref[...] * z)).astype(BF16)

    return pl.pallas_call(
        body, grid=(b, cols // tn),
        in_specs=[pl.BlockSpec((l1, n2), lambda i, j: (0, 0)),
                  pl.BlockSpec((1, n2, tn), lambda i, j: (i, 0, j)),
                  pl.BlockSpec((1, l1, tn), lambda i, j: (i, 0, j)),
                  pl.BlockSpec((1, l1, tn), lambda i, j: (i, 0, j)),
                  pl.BlockSpec((1, tn), lambda i, j: (0, 0))],
        out_specs=pl.BlockSpec((1, l1, tn), lambda i, j: (i, 0, j)),
        out_shape=jax.ShapeDtypeStruct((b, l1, cols), BF16),
        compiler_params=_cparams(2), name="hy_dft4",
    )(inv1, cv, x0v, zv, skip_t)


def _hy_dense_call(x0, z, taps, sums, skip):
    b, c, w = z.shape
    n = 2 * c
    th = 2.0 * np.pi * np.outer(np.arange(n, dtype=np.float64), np.arange(c, dtype=np.float64)) / n
    fwd = jnp.asarray(np.concatenate([np.cos(th), -np.sin(th)], axis=0).astype(np.float32)).astype(BF16)
    inv = jnp.asarray((np.concatenate([np.cos(th).T, -np.sin(th).T], axis=1) / n)
                      .astype(np.float32)).astype(BF16)

    def body(f_ref, i_ref, x0_ref, z_ref, t_ref, s_ref, k_ref, o_ref):
        sv = s_ref[...]
        nrm = 1.0 / (sv[:, :w] + sv[:, w:])
        tf = jnp.dot(f_ref[...], t_ref[...].astype(BF16), preferred_element_type=F32)
        kr = (tf[:n, :w] + tf[:n, w:]) * nrm
        ki = (tf[n:, :w] - tf[n:, w:]) * nrm
        zf = jnp.dot(f_ref[...], z_ref[0], preferred_element_type=F32)
        zr, zi = zf[:n], zf[n:]
        y = jnp.concatenate([zr * kr - zi * ki, zr * ki + zi * kr], axis=0).astype(BF16)
        yt = jnp.dot(i_ref[...], y, preferred_element_type=F32)
        o_ref[0] = (x0_ref[0].astype(F32) * (yt + k_ref[...] * z_ref[0].astype(F32))).astype(BF16)

    return pl.pallas_call(
        body, grid=(b,),
        in_specs=[pl.BlockSpec((2 * n, c), lambda i: (0, 0)),
                  pl.BlockSpec((c, 2 * n), lambda i: (0, 0)),
                  pl.BlockSpec((1, c, w), lambda i: (i, 0, 0)),
                  pl.BlockSpec((1, c, w), lambda i: (i, 0, 0)),
                  pl.BlockSpec((c, 2 * w), lambda i: (0, 0)),
                  pl.BlockSpec((1, 2 * w), lambda i: (0, 0)),
                  pl.BlockSpec((1, w), lambda i: (0, 0))],
        out_specs=pl.BlockSpec((1, c, w), lambda i: (i, 0, 0)),
        out_shape=jax.ShapeDtypeStruct((b, c, w), BF16),
        compiler_params=_cparams(1), name="hy_dense",
    )(fwd, inv, x0, z, taps, sums, skip.reshape(1, w))


def _hyena_long(x0, z, fparams, skip):
    b, l, w = z.shape
    l1 = l // LANES
    n1 = 2 * l1
    fb = min(8, n1)
    fwd1, inv1, gb, hb = _dft_tables(l1, fb)
    nf = gb.shape[0]
    taps, sums = _hy_filter_call(l, *fparams)
    ta = _hy_stage1_call(taps.astype(BF16).reshape(1, l1, LANES * 2 * w), fwd1)
    kf = _hy_filter_spec_call(ta.reshape(1, 2, nf, LANES, 2 * w), gb, sums, fb)
    za = _hy_stage1_call(z.reshape(b, l1, LANES * w), fwd1)
    cc = _hy_stage23_call(za.reshape(b, 2, nf, LANES, w), gb, hb, kf, fb)
    tn = 8 * w
    skip_t = jnp.tile(skip.reshape(1, w), (1, tn // w))
    y = _hy_stage4_call(cc.reshape(b, 2 * nf, LANES * w), inv1, x0.reshape(b, l1, LANES * w),
                        z.reshape(b, l1, LANES * w), skip_t)
    return y.reshape(b, l, w)


def _hyena_short(x0, z, fparams, skip):
    taps, sums = _hy_filter_call(z.shape[1], *fparams)
    return _hy_dense_call(x0, z, taps, sums, skip)


def _rope_tables(l, c):
    t = np.arange(l)
    row = (t // GRID_W).astype(np.float64)
    col = (t % GRID_W).astype(np.float64)
    n_pairs = HEAD_DIM // 4
    inv_freq = ROPE_THETA ** (-np.arange(n_pairs, dtype=np.float64) / n_pairs)
    ang = np.concatenate([row[:, None] * inv_freq, col[:, None] * inv_freq], axis=-1)
    cos = np.repeat(np.cos(ang), 2, axis=1)
    sin = np.repeat(np.sin(ang), 2, axis=1)
    even = (np.arange(HEAD_DIM) % 2 == 0)[None, :]
    se = np.where(even, -sin, 0.0)
    so = np.where(even, 0.0, sin)
    pad = lambda a, v: np.concatenate([a, np.full((c, HEAD_DIM), v)], axis=0)
    two = lambda a: jnp.asarray(np.concatenate([a, a], axis=1).astype(np.float32))
    return two(pad(cos, 1.0)), two(pad(se, 0.0)), two(pad(so, 0.0))


def _block_diag_ones():
    i = np.arange(2 * LANES)
    return jnp.asarray((i[:, None] // HEAD_DIM == i[None, :] // HEAD_DIM).astype(np.float32)).astype(BF16)


def _gain_rows(gains, scales):
    rows = [jnp.tile(g.astype(F32) * s, 2 * LANES // HEAD_DIM) for g, s in zip(gains, scales)]
    rows += [jnp.zeros((2 * LANES,), F32)] * (8 - len(rows))
    return jnp.stack(rows, axis=0)


def kernel(x, c, ctx, c_ctx, w_ada, b_ada, w_up, ffn_conv_w, ffn_conv_b, w_down, w_in_e, w_out_e, na_q_gain, na_k_gain, na_rpb, da_q_gain, da_k_gain, da_lambda_q1, da_lambda_k1, da_lambda_q2, da_lambda_k2, da_subln_gain, w_in_o, w_out_o, gqa_q_gain, gqa_k_gain, hy_conv_w, hy_conv_b, hy_w1, hy_b1, hy_w2, hy_b2, hy_w3, hy_b3, hy_w4, hy_freq, hy_skip):
    b, l, d = x.shape
    cl = ctx.shape[1]
    t = l + cl
    depth = w_ada.shape[0]
    f = w_down.shape[1]
    assert cl == TM and (l // GRID_W) >= NA_WROWS + NA_QROWS
    assert w_in_e.shape[2] == 3072 and w_in_o.shape[2] == 2304 and d % LANES == 0
    scale = HEAD_DIM ** -0.5 * LOG2E

    rows = -(-(b + 1) // 8) * 8
    cs = jnp.zeros((rows, d), F32).at[:b].set(c).at[b].set(c_ctx)
    mods = _ada_call(cs, w_ada, b_ada)
    modarrs = []
    for layer in range(depth):
        lat = mods[layer, :b].reshape(b, 1, 6, d)
        cx = jnp.broadcast_to(mods[layer, b].reshape(1, 1, 6, d), (b, 1, 6, d))
        modarrs.append(jnp.concatenate([lat, cx], axis=1))

    ropes = _rope_tables(l, cl)
    bd = _block_diag_ones()
    x_all, h = _mod0_call(x, ctx, modarrs[0])

    for layer in range(depth):
        i = layer // 2
        if layer % 2 == 0:
            lam_init = 0.8 - 0.6 * math.exp(-0.3 * layer)
            gains = _gain_rows((na_q_gain[i], na_k_gain[i], da_q_gain[i], da_k_gain[i]),
                               (scale, 1.0, scale, 1.0))
            qkv, vat, qbt, vbt = _inproj_call(
                _inproj_even_body, "inproj_even", h, w_in_e[i].astype(BF16), bd, gains, ropes, (1536,),
                ((b, 8, HEAD_DIM + ONES_ROWS, t), (b, 512, t), (b, 4, LANES + ONES_ROWS, t)))
            bias = _na_bias_table(na_rpb[i])
            ya = _na_call(qkv, vat, bias, l)
            lamv = jnp.stack([da_lambda_q1[i], da_lambda_k1[i], da_lambda_q2[i], da_lambda_k2[i]]).astype(F32)
            yb = _flash_call("da_attn", qbt, qkv, vbt, 1024 // 256, False, LANES, TK_DA,
                             _make_da_finish(lam_init),
                             (lamv, da_subln_gain[i].reshape(1, LANES).astype(F32)), 512, l)
            w_out = w_out_e[i].astype(BF16)
        else:
            gains = _gain_rows((gqa_q_gain[i], gqa_k_gain[i]), (scale, 1.0))
            kd, x0, z, qt, vt = _inproj_call(
                functools.partial(_inproj_odd_body, nl=l // TM, nt=t // TM), "inproj_odd", h,
                w_in_o[i].astype(BF16), bd, gains, ropes, (2 * LANES, 512, 512),
                ((b, 512, t), (b, 2, HEAD_DIM + ONES_ROWS, t)),
                extra=(hy_conv_w[i], hy_conv_b[i].reshape(1, -1)), halo=True)
            ya = _flash_call("gqa_attn", qt, kd, vt, 0, True, HEAD_DIM, TK_GQA,
                             _gqa_finish, (), 512, l)
            fparams = (hy_w1[i], hy_b1[i], hy_w2[i], hy_b2[i], hy_w3[i], hy_b3[i], hy_w4[i], hy_freq[i])
            yd_l = _hyena_long(x0[:, :l], z[:, :l], fparams, hy_skip[i])
            if layer < depth - 1:
                yd_c = _hyena_short(x0[:, l:], z[:, l:], fparams, hy_skip[i])
            else:
                yd_c = jnp.zeros((b, cl, x0.shape[2]), BF16)
            yb = jnp.concatenate([yd_l, yd_c], axis=1)
            w_out = w_out_o[i].astype(BF16)
        ka = ya.shape[2]
        x_all, act = _mix_ffn_up_call(ya, yb, w_out[:ka], w_out[ka:], x_all, modarrs[layer],
                                      w_up[layer].astype(BF16), ffn_conv_w[layer], ffn_conv_b[layer])
        if layer == depth - 1:
            return _ffn_last_call(act, w_down[layer].astype(BF16), x_all, modarrs[layer])
        x_all, h = _ffn_down_call(act, w_down[layer].astype(BF16), x_all, modarrs[layer],
                                  modarrs[layer + 1])
```

```python
import functools
import math

import numpy as np
import jax
import jax.numpy as jnp
from jax import lax
from jax.experimental import pallas as pl
from jax.experimental.pallas import tpu as pltpu

F32 = jnp.float32
BF16 = jnp.bfloat16

HEAD_DIM = 64
GRID_W = 64
ROPE_THETA = 10000.0
EPS = 1e-6
NA_WIN_R = 8
NA_WIN_C = 16
HY_EMB_DIM = 33
HY_FAST_DECAY = 0.3
HY_SLOW_DECAY = 1.5
HY_DECAY_TARGET = 1e-2

LANES = 128
TM = 256
TK_DA = 1024
TK_GQA = 256
NA_QROWS = TM // GRID_W
NA_WROWS = NA_QROWS + NA_WIN_R
ONES_ROWS = 16
NEG = -1e30
LOG2E = 1.4426950408889634
VMEM_LIMIT = 56 * 1024 * 1024


def _cparams(n_axes):
    return pltpu.CompilerParams(dimension_semantics=("arbitrary",) * n_axes,
                                vmem_limit_bytes=VMEM_LIMIT)


def _modulate(x, sh, sc):
    ms = jnp.mean(x * x, axis=-1, keepdims=True)
    return x * lax.rsqrt(ms + EPS) * (1.0 + sc) + sh


def _seg_norm(y, bd, gain):
    ss = jnp.dot((y * y).astype(BF16), bd, preferred_element_type=F32)
    return y * lax.rsqrt(ss * (1.0 / HEAD_DIM) + EPS) * gain


def _rope(y, c, se, so):
    return y * c + pltpu.roll(y, LANES - 1, 1) * se + pltpu.roll(y, 1, 1) * so


def _dot_nt(a, b):
    return lax.dot_general(a, b, (((1,), (1,)), ((), ())), preferred_element_type=F32)


def _ada_body(c_ref, w_ref, b_ref, o_ref):
    c = c_ref[...]
    a = (c / (1.0 + jnp.exp(-c))).astype(BF16)
    o_ref[0] = jnp.dot(a, w_ref[0].astype(BF16), preferred_element_type=F32) + b_ref[0]


def _ada_call(cs, w_ada, b_ada):
    depth, d, n6 = w_ada.shape
    rows = cs.shape[0]
    tn = 1536
    return pl.pallas_call(
        _ada_body, grid=(depth, n6 // tn),
        in_specs=[pl.BlockSpec((rows, d), lambda l, n: (0, 0)),
                  pl.BlockSpec((1, d, tn), lambda l, n: (l, 0, n)),
                  pl.BlockSpec((1, 1, tn), lambda l, n: (l, 0, n))],
        out_specs=pl.BlockSpec((1, rows, tn), lambda l, n: (l, 0, n)),
        out_shape=jax.ShapeDtypeStruct((depth, rows, n6), F32),
        compiler_params=_cparams(2), name="ada",
    )(cs, w_ada, b_ada.reshape(depth, 1, n6))


def _mod0_call(x, ctx, modarr):
    b, l, d = x.shape
    c = ctx.shape[1]
    t = l + c
    nl = l // TM

    def body(x_ref, c_ref, m_ref, xo_ref, h_ref):
        j = pl.program_id(1)
        xv = jnp.where(j < nl, x_ref[0], c_ref[0])
        xo_ref[0] = xv
        h_ref[0] = _modulate(xv, m_ref[0, 0, 0:1, :], m_ref[0, 0, 1:2, :]).astype(BF16)

    return pl.pallas_call(
        body, grid=(b, t // TM),
        in_specs=[pl.BlockSpec((1, TM, d), lambda i, j: (i, jnp.minimum(j, nl - 1), 0)),
                  pl.BlockSpec((1, TM, d), lambda i, j: (i, 0, 0)),
                  pl.BlockSpec((1, 1, 6, d), lambda i, j: (i, j // nl, 0, 0))],
        out_specs=[pl.BlockSpec((1, TM, d), lambda i, j: (i, j, 0)),
                   pl.BlockSpec((1, TM, d), lambda i, j: (i, j, 0))],
        out_shape=[jax.ShapeDtypeStruct((b, t, d), F32), jax.ShapeDtypeStruct((b, t, d), BF16)],
        compiler_params=_cparams(2), name="mod0",
    )(x, ctx, modarr)


def _inproj_even_body(a_ref, w_ref, bd_ref, g_ref, rc_ref, rse_ref, rso_ref,
                      o_ref, vat_ref, qbt_ref, vbt_ref):
    a = a_ref[0]
    bd = bd_ref[...]
    rc, rse, rso = rc_ref[...], rse_ref[...], rso_ref[...]
    plan = ((0, False, 0), (1, False, 512), (None, False, None), (2, True, None), (3, True, 1024),
            (None, False, None))
    for seg, (gain_row, rope, out_col) in enumerate(plan):
        y = jnp.dot(a, w_ref[:, seg * 512:(seg + 1) * 512], preferred_element_type=F32)
        for half in range(2):
            yy = y[:, half * 256:(half + 1) * 256]
            if gain_row is not None:
                yy = _seg_norm(yy, bd, g_ref[gain_row:gain_row + 1, :])
            for blk in range(2):
                z = yy[:, blk * LANES:(blk + 1) * LANES]
                if rope:
                    z = _rope(z, rc, rse, rso)
                m = half * 2 + blk
                if out_col is not None:
                    col = out_col + m * LANES
                    o_ref[0, :, col:col + LANES] = z.astype(BF16)
                    continue
                zt = z.T.astype(BF16)
                if seg == 2:
                    vat_ref[0, 2 * m, 0:HEAD_DIM, :] = zt[:HEAD_DIM]
                    vat_ref[0, 2 * m + 1, 0:HEAD_DIM, :] = zt[HEAD_DIM:]
                elif seg == 3:
                    qbt_ref[0, m * LANES:(m + 1) * LANES, :] = zt
                else:
                    vbt_ref[0, m, 0:LANES, :] = zt
    ones = jnp.ones((ONES_ROWS, TM), BF16)
    for hd in range(vat_ref.shape[1]):
        vat_ref[0, hd, HEAD_DIM:HEAD_DIM + ONES_ROWS, :] = ones
    for hd in range(vbt_ref.shape[1]):
        vbt_ref[0, hd, LANES:LANES + ONES_ROWS, :] = ones


def _inproj_odd_body(a_ref, ap_ref, an_ref, w_ref, bd_ref, g_ref, rc_ref, rse_ref, rso_ref,
                     cw_ref, cb_ref, o_ref, x0_ref, z_ref, qt_ref, vt_ref, *, nl, nt):
    j = pl.program_id(1)
    a = a_ref[0]
    bd = bd_ref[...]
    rc, rse, rso = rc_ref[...], rse_ref[...], rso_ref[...]
    y = jnp.dot(a, w_ref[:, 0:512], preferred_element_type=F32)
    for half in range(2):
        yy = _seg_norm(y[:, half * 256:(half + 1) * 256], bd, g_ref[0:1, :])
        for blk in range(2):
            z = _rope(yy[:, blk * LANES:(blk + 1) * LANES], rc, rse, rso)
            m = half * 2 + blk
            qt_ref[0, m * LANES:(m + 1) * LANES, :] = z.T.astype(BF16)
    y = jnp.dot(a, w_ref[:, 512:768], preferred_element_type=F32)
    k = _seg_norm(y[:, :LANES], bd_ref[0:LANES, 0:LANES], g_ref[1:2, 0:LANES])
    k = _rope(k, rc, rse, rso)
    kr = pltpu.roll(k, HEAD_DIM, 1)
    lo = lax.broadcasted_iota(jnp.int32, k.shape, 1) < HEAD_DIM
    o_ref[0, :, 0:LANES] = jnp.where(lo, k, kr).astype(BF16)
    o_ref[0, :, LANES:2 * LANES] = jnp.where(lo, kr, k).astype(BF16)
    vt = y[:, LANES:].T.astype(BF16)
    ones = jnp.ones((ONES_ROWS, TM), BF16)
    for hd in range(2):
        vt_ref[0, hd, 0:HEAD_DIM, :] = vt[hd * HEAD_DIM:(hd + 1) * HEAD_DIM]
        vt_ref[0, hd, HEAD_DIM:HEAD_DIM + ONES_ROWS, :] = ones
    hb = ap_ref.shape[1]
    zero = jnp.zeros((hb, a.shape[1]), BF16)
    ap = jnp.where((j == 0) | (j == nl), zero, ap_ref[0])
    an = jnp.where((j == nl - 1) | (j == nt - 1), zero, an_ref[0])
    a_ext = jnp.concatenate([ap, a, an], axis=0)
    conv = []
    for seg in range(3):
        sl = slice(seg * 512, (seg + 1) * 512)
        g = jnp.dot(a_ext, w_ref[:, 768 + seg * 512:768 + (seg + 1) * 512], preferred_element_type=F32)
        conv.append(g[hb - 1:hb - 1 + TM] * cw_ref[0:1, sl] + g[hb:hb + TM] * cw_ref[1:2, sl]
                    + g[hb + 1:hb + 1 + TM] * cw_ref[2:3, sl] + cb_ref[0:1, sl])
        if seg == 0:
            x0_ref[0] = conv[0].astype(BF16)
    z_ref[0] = (conv[2] * conv[1]).astype(BF16)


def _inproj_call(body, name, h, w, bd, gains, ropes, n_outs, t_shapes, extra=(), halo=False):
    b, t, d = h.shape
    n_in = w.shape[1]
    nt = t // TM
    hb = 16
    r = TM // hb
    rc, rse, rso = ropes

    def t_spec(shape):
        nd = len(shape)
        return pl.BlockSpec((1,) + tuple(shape[1:-1]) + (TM,), lambda i, j: (i,) + (0,) * (nd - 2) + (j,))

    acts, act_specs = [h], [pl.BlockSpec((1, TM, d), lambda i, j: (i, j, 0))]
    if halo:
        acts += [h, h]
        act_specs += [pl.BlockSpec((1, hb, d), lambda i, j: (i, jnp.maximum(j * r - 1, 0), 0)),
                      pl.BlockSpec((1, hb, d), lambda i, j: (i, jnp.minimum((j + 1) * r, nt * r - 1), 0))]
    return pl.pallas_call(
        body, grid=(b, nt),
        in_specs=act_specs
        + [pl.BlockSpec((d, n_in), lambda i, j: (0, 0)),
           pl.BlockSpec(bd.shape, lambda i, j: (0, 0)),
           pl.BlockSpec(gains.shape, lambda i, j: (0, 0)),
           pl.BlockSpec((TM, LANES), lambda i, j: (j, 0)),
           pl.BlockSpec((TM, LANES), lambda i, j: (j, 0)),
           pl.BlockSpec((TM, LANES), lambda i, j: (j, 0))]
        + [pl.BlockSpec(e.shape, lambda i, j: (0,) * e.ndim) for e in extra],
        out_specs=[pl.BlockSpec((1, TM, n), lambda i, j: (i, j, 0)) for n in n_outs]
        + [t_spec(s) for s in t_shapes],
        out_shape=[jax.ShapeDtypeStruct((b, t, n), BF16) for n in n_outs]
        + [jax.ShapeDtypeStruct(s, BF16) for s in t_shapes],
        compiler_params=_cparams(2), name=name,
    )(*acts, w, bd, gains, rc, rse, rso, *extra)


def _mix_ffn_up_call(ya, yb, wa, wb, x, modarr, w_up, conv_w, conv_b):
    b, t, d = x.shape
    f = w_up.shape[1] // 2
    nl = (t - TM) // TM
    nt = t // TM
    hb = 16
    r = TM // hb
    ka, kb = ya.shape[2], yb.shape[2]
    chunks = [(c0, min(2 * LANES, f - c0)) for c0 in range(0, f, 2 * LANES)]

    def body(ya_ref, yap_ref, yan_ref, yb_ref, ybp_ref, ybn_ref, x_ref, xp_ref, xn_ref,
             wa_ref, wb_ref, m_ref, wg_ref, wv_ref, cw_ref, cb_ref, xo_ref, o_ref):
        j = pl.program_id(1)
        ext = lambda p, c, n: jnp.concatenate([p[0], c[0], n[0]], axis=0)
        y = (jnp.dot(ext(yap_ref, ya_ref, yan_ref), wa_ref[...], preferred_element_type=F32)
             + jnp.dot(ext(ybp_ref, yb_ref, ybn_ref), wb_ref[...], preferred_element_type=F32))
        x1 = ext(xp_ref, x_ref, xn_ref) + m_ref[0, 0, 2:3, :] * y
        xo_ref[0] = x1[hb:hb + TM]
        hx = _modulate(x1, m_ref[0, 0, 3:4, :], m_ref[0, 0, 4:5, :]).astype(BF16)
        zero = jnp.zeros((hb, d), BF16)
        a = hx[hb:hb + TM]
        ap = jnp.where((j == 0) | (j == nl), zero, hx[0:hb])
        an = jnp.where((j == nl - 1) | (j == nt - 1), zero, hx[hb + TM:])
        a_ext = jnp.concatenate([ap, a, an], axis=0)
        for c0, cw in chunks:
            sl = slice(c0, c0 + cw)
            g = jnp.dot(a_ext, wg_ref[:, sl], preferred_element_type=F32)
            u = (g[hb - 1:hb - 1 + TM] * cw_ref[0:1, sl] + g[hb:hb + TM] * cw_ref[1:2, sl]
                 + g[hb + 1:hb + 1 + TM] * cw_ref[2:3, sl] + cb_ref[0:1, sl])
            v = jnp.dot(a, wv_ref[:, sl], preferred_element_type=F32)
            o_ref[0, :, sl] = ((u / (1.0 + jnp.exp(-u))) * v).astype(BF16)

    main = lambda w: pl.BlockSpec((1, TM, w), lambda i, j: (i, j, 0))
    prev = lambda w: pl.BlockSpec((1, hb, w), lambda i, j: (i, jnp.maximum(j * r - 1, 0), 0))
    nxt = lambda w: pl.BlockSpec((1, hb, w), lambda i, j: (i, jnp.minimum((j + 1) * r, nt * r - 1), 0))
    full = lambda a: pl.BlockSpec(a.shape, lambda i, j: (0,) * a.ndim)
    return pl.pallas_call(
        body, grid=(b, nt),
        in_specs=[main(ka), prev(ka), nxt(ka), main(kb), prev(kb), nxt(kb), main(d), prev(d), nxt(d),
                  full(wa), full(wb), pl.BlockSpec((1, 1, 6, d), lambda i, j: (i, j // nl, 0, 0)),
                  pl.BlockSpec((d, f), lambda i, j: (0, 0)), pl.BlockSpec((d, f), lambda i, j: (0, 1)),
                  pl.BlockSpec((3, f), lambda i, j: (0, 0)), pl.BlockSpec((1, f), lambda i, j: (0, 0))],
        out_specs=[pl.BlockSpec((1, TM, d), lambda i, j: (i, j, 0)),
                   pl.BlockSpec((1, TM, f), lambda i, j: (i, j, 0))],
        out_shape=[jax.ShapeDtypeStruct((b, t, d), F32), jax.ShapeDtypeStruct((b, t, f), BF16)],
        compiler_params=_cparams(2), name="mix_ffn_up",
    )(ya, ya, ya, yb, yb, yb, x, x, x, wa, wb, modarr, w_up, w_up, conv_w, conv_b.reshape(1, f))


def _ffn_down_call(act, w_down, x, modarr, modarr_next):
    b, t, d = x.shape
    f = w_down.shape[0]
    nl = (t - TM) // TM
    nt = t // TM

    def body(a_ref, wd_ref, x_ref, m_ref, mn_ref, xo_ref, h_ref):
        y = jnp.dot(a_ref[0], wd_ref[...], preferred_element_type=F32)
        x2 = x_ref[0] + m_ref[0, 0, 5:6, :] * y
        xo_ref[0] = x2
        h_ref[0] = _modulate(x2, mn_ref[0, 0, 0:1, :], mn_ref[0, 0, 1:2, :]).astype(BF16)

    return pl.pallas_call(
        body, grid=(b, nt),
        in_specs=[pl.BlockSpec((1, TM, f), lambda i, j: (i, j, 0)),
                  pl.BlockSpec((f, d), lambda i, j: (0, 0)),
                  pl.BlockSpec((1, TM, d), lambda i, j: (i, j, 0)),
                  pl.BlockSpec((1, 1, 6, d), lambda i, j: (i, j // nl, 0, 0)),
                  pl.BlockSpec((1, 1, 6, d), lambda i, j: (i, j // nl, 0, 0))],
        out_specs=[pl.BlockSpec((1, TM, d), lambda i, j: (i, j, 0)),
                   pl.BlockSpec((1, TM, d), lambda i, j: (i, j, 0))],
        out_shape=[jax.ShapeDtypeStruct((b, t, d), F32), jax.ShapeDtypeStruct((b, t, d), BF16)],
        compiler_params=_cparams(2), name="ffn_down",
    )(act, w_down, x, modarr, modarr_next)


def _ffn_last_call(act, w_down, x, modarr):
    b, t, d = x.shape
    f = w_down.shape[0]
    nl = (t - TM) // TM

    def body(a_ref, wd_ref, x_ref, m_ref, xo_ref):
        y = jnp.dot(a_ref[0], wd_ref[...], preferred_element_type=F32)
        xo_ref[0] = x_ref[0] + m_ref[0, 0, 5:6, :] * y

    return pl.pallas_call(
        body, grid=(b, nl),
        in_specs=[pl.BlockSpec((1, TM, f), lambda i, j: (i, j, 0)),
                  pl.BlockSpec((f, d), lambda i, j: (0, 0)),
                  pl.BlockSpec((1, TM, d), lambda i, j: (i, j, 0)),
                  pl.BlockSpec((1, 1, 6, d), lambda i, j: (i, 0, 0, 0))],
        out_specs=pl.BlockSpec((1, TM, d), lambda i, j: (i, j, 0)),
        out_shape=jax.ShapeDtypeStruct((b, nl * TM, d), F32),
        compiler_params=_cparams(2), name="ffn_last",
    )(act, w_down, x, modarr)


def _split_heads(q):
    lane = lax.broadcasted_iota(jnp.int32, q.shape, 1)
    zero = jnp.zeros_like(q)
    return jnp.where(lane < HEAD_DIM, q, zero), jnp.where(lane >= HEAD_DIM, q, zero)


def _flash_call(name, qt, qkv, vt, kcol, shared_kv, vrows, tk, finish, extra, out_cols, l):
    b, t, _ = qkv.shape
    c = t - l
    nl = l // TM
    n_chunks = l // tk
    n_steps = out_cols // (2 * LANES)
    n_kv = 1 if shared_kv else 2
    ns = 4
    nb = 2
    assert l % tk == 0 and n_chunks >= 2

    def body(q_ref, qn_ref, k_ref, vt_ref, *rest):
        extra_refs = rest[:-8]
        o_ref, s_ref, p_ref, sc_ref, pc_ref, acc_ref, m_ref, a_ref = rest[-8:]
        tile = pl.program_id(2)
        is_lat = tile < nl
        row = lax.broadcasted_iota(jnp.int32, (LANES, TM), 0)

        def streams(ref):
            out = []
            for blk in range(2):
                qb = ref[0, blk * LANES:(blk + 1) * LANES, :]
                out += [jnp.where(row < HEAD_DIM, qb, jnp.zeros_like(qb)),
                        jnp.where(row >= HEAD_DIM, qb, jnp.zeros_like(qb))]
            return out

        kv_of = lambda s: 0 if shared_kv else s // 2

        def scores(off, size, dst, qs):
            kcs = [k_ref[0, pl.ds(off, size), j * LANES:(j + 1) * LANES] for j in range(n_kv)]
            for s in range(ns):
                dst(s)[...] = jnp.dot(kcs[kv_of(s)], qs[s], preferred_element_type=F32)

        def softmax(src, dst, slot):
            for s in range(ns):
                st = src(s)[...]
                m = m_ref[s]
                mnew = jnp.maximum(m, jnp.max(st, axis=0, keepdims=True))
                a_ref[slot, s] = jnp.exp2(m - mnew)
                dst(s)[...] = jnp.exp2((st - mnew).astype(BF16))
                m_ref[s] = mnew

        def values(src, slot, off, size):
            vcs = [vt_ref[0, j, :, pl.ds(off, size)] for j in range(n_kv)]
            for s in range(ns):
                acc_ref[s] = a_ref[slot, s] * acc_ref[s] + jnp.dot(vcs[kv_of(s)], src(s)[...],
                                                                  preferred_element_type=F32)

        s_ctx, p_ctx = (lambda s: sc_ref.at[s]), (lambda s: pc_ref.at[s])
        s_buf = [(lambda s, i=i: s_ref.at[i, s]) for i in range(nb)]
        p_buf = [(lambda s, i=i: p_ref.at[i, s]) for i in range(nb)]

        def start():
            acc_ref[...] = jnp.zeros(acc_ref.shape, F32)
            m_ref[...] = jnp.full(m_ref.shape, NEG, F32)

        def finalize():
            outs = [acc_ref[s, 0:vrows, :] / acc_ref[s, vrows:vrows + 1, :] for s in range(ns)]
            o_ref[0] = finish(outs, *extra_refs)

        @pl.when(tile == 0)
        def _():
            qs = streams(q_ref)
            scores(l, c, s_ctx, qs)
            scores(0, tk, s_buf[0], qs)

        @pl.when(is_lat)
        def _():
            qs = streams(q_ref)
            qs_next = streams(qn_ref)
            start()
            softmax(s_ctx, p_ctx, nb)
            values(p_ctx, nb, l, c)
            scores(tk, tk, s_buf[1], qs)
            softmax(s_buf[0], p_buf[0], 0)

            for k in range(2, n_chunks):
                values(p_buf[(k - 2) % nb], (k - 2) % nb, (k - 2) * tk, tk)
                scores(k * tk, tk, s_buf[k % nb], qs)
                softmax(s_buf[(k - 1) % nb], p_buf[(k - 1) % nb], (k - 1) % nb)
            k = n_chunks
            values(p_buf[(k - 2) % nb], (k - 2) % nb, (k - 2) * tk, tk)
            scores(l, c, s_ctx, qs_next)
            softmax(s_buf[(k - 1) % nb], p_buf[(k - 1) % nb], (k - 1) % nb)
            scores(0, tk, s_buf[0], qs_next)
            values(p_buf[(k - 1) % nb], (k - 1) % nb, (k - 1) * tk, tk)
            finalize()

        @pl.when(jnp.logical_not(is_lat))
        def _():
            start()
            softmax(s_ctx, p_ctx, nb)
            values(p_ctx, nb, l, c)
            finalize()

    kw = n_kv * LANES
    extra_specs = [pl.BlockSpec(e.shape, lambda bi, h, i: (0,) * e.ndim) for e in extra]
    return pl.pallas_call(
        body, grid=(b, n_steps, t // TM),
        in_specs=[pl.BlockSpec((1, 2 * LANES, TM), lambda bi, h, i: (bi, h, i)),
                  pl.BlockSpec((1, 2 * LANES, TM), lambda bi, h, i: (bi, h, jnp.minimum(i + 1, nl))),
                  pl.BlockSpec((1, t, kw), lambda bi, h, i: (bi, 0, kcol + h)),
                  pl.BlockSpec((1, n_kv, vrows + ONES_ROWS, t), lambda bi, h, i: (bi, h, 0, 0))]
        + extra_specs,
        out_specs=pl.BlockSpec((1, TM, 2 * LANES), lambda bi, h, i: (bi, i, h)),
        out_shape=jax.ShapeDtypeStruct((b, t, out_cols), BF16),
        scratch_shapes=[pltpu.VMEM((nb, ns, tk, TM), F32), pltpu.VMEM((nb, ns, tk, TM), BF16),
                        pltpu.VMEM((ns, c, TM), F32), pltpu.VMEM((ns, c, TM), BF16),
                        pltpu.VMEM((ns, vrows + ONES_ROWS, TM), F32), pltpu.VMEM((ns, 1, TM), F32),
                        pltpu.VMEM((nb + 1, ns, 1, TM), F32)],
        compiler_params=_cparams(3), name=name,
    )(qt, qt, qkv, vt, *extra)


def _gqa_finish(outs):
    return jnp.concatenate(outs, axis=0).T.astype(BF16)


def _make_da_finish(lam_init):
    def finish(outs, lam_ref, subln_ref):
        lv = lam_ref[...]
        lam = (jnp.exp(jnp.sum(lv[0:1] * lv[1:2], keepdims=True))
               - jnp.exp(jnp.sum(lv[2:3] * lv[3:4], keepdims=True)) + lam_init)
        heads = []
        for hd in range(len(outs) // 2):
            o = (outs[2 * hd] - lam * outs[2 * hd + 1]).T
            o = o * lax.rsqrt(jnp.mean(o * o, axis=-1, keepdims=True) + EPS)
            heads.append((o * subln_ref[...] * (1.0 - lam_init)).astype(BF16))
        return jnp.concatenate(heads, axis=1)
    return finish


def _na_call(qkv, vt, bias, l):
    b, t, _ = qkv.shape
    c = t - l
    nl = l // TM
    rows = l // GRID_W
    win = NA_WROWS * GRID_W
    n_pairs = vt.shape[1] // 2

    nt = t // TM

    def window(tile):
        return pl.multiple_of(jnp.clip(tile * NA_QROWS - NA_WIN_R // 2, 0, rows - NA_WROWS) * GRID_W, 256)

    hp = 1

    def body(q0_ref, qn_ref, k_ref, vt_ref, b0_ref, bn_ref, o_ref, sw_ref, sc_ref):
        g = pl.program_id(2)

        def scores(q_ref, b_ref, tile, buf, with_win):
            for blk in range(hp):
                cols = slice(blk * LANES, (blk + 1) * LANES)
                qs = _split_heads(q_ref[0, :, cols])
                if with_win:
                    kw = k_ref[0, pl.ds(window(tile), win), cols]
                    for half in range(2):
                        sw_ref[buf, 2 * blk + half] = _dot_nt(kw, qs[half]) + b_ref[blk, half]
                kc = k_ref[0, l:l + c, cols]
                for half in range(2):
                    sc_ref[buf, 2 * blk + half] = _dot_nt(kc, qs[half])

        def finish(tile, buf, with_win):
            outs = []
            for s in range(2 * hp):
                parts = [(sc_ref[buf, s], vt_ref[0, s, :, l:l + c])]
                if with_win:
                    parts.append((sw_ref[buf, s], vt_ref[0, s, :, pl.ds(window(tile), win)]))
                m = functools.reduce(jnp.maximum, [jnp.max(sc, axis=0, keepdims=True) for sc, _ in parts])
                acc = sum(jnp.dot(v, jnp.exp2((sc - m).astype(BF16)), preferred_element_type=F32)
                          for sc, v in parts)
                outs.append(acc[:HEAD_DIM] / acc[HEAD_DIM:HEAD_DIM + 1])
            o_ref[0] = jnp.concatenate(outs, axis=0).T.astype(BF16)

        cur, nxt = g % 2, (g + 1) % 2

        @pl.when(g == 0)
        def _():
            scores(q0_ref, b0_ref, 0, 0, True)

        @pl.when(g + 1 < nl)
        def _():
            scores(qn_ref, bn_ref, g + 1, nxt, True)
            finish(g, cur, True)

        @pl.when(g + 1 == nl)
        def _():
            scores(qn_ref, bn_ref, g + 1, nxt, False)
            finish(g, cur, True)

        @pl.when(g == nl)
        def _():
            finish(g, cur, False)

    n_steps = n_pairs // hp
    wq = hp * LANES

    def bias_next(bi, h, g):
        tile = g + 1
        case = jnp.where(tile >= nl - 1, 2, 1)
        return (case * n_steps + h, 0, 0, 0)

    return pl.pallas_call(
        body, grid=(b, n_steps, nt),
        in_specs=[pl.BlockSpec((1, TM, wq), lambda bi, h, g: (bi, 0, h)),
                  pl.BlockSpec((1, TM, wq), lambda bi, h, g: (bi, jnp.minimum(g + 1, nt - 1), h)),
                  pl.BlockSpec((1, t, wq), lambda bi, h, g: (bi, 0, n_steps + h)),
                  pl.BlockSpec((1, 2 * hp, HEAD_DIM + ONES_ROWS, t), lambda bi, h, g: (bi, h, 0, 0)),
                  pl.BlockSpec((hp, 2, win, TM), lambda bi, h, g: (h, 0, 0, 0)),
                  pl.BlockSpec((hp, 2, win, TM), bias_next)],
        out_specs=pl.BlockSpec((1, TM, wq), lambda bi, h, g: (bi, g, h)),
        out_shape=jax.ShapeDtypeStruct((b, t, n_pairs * LANES), BF16),
        scratch_shapes=[pltpu.VMEM((2, 2 * hp, win, TM), F32), pltpu.VMEM((2, 2 * hp, c, TM), F32)],
        compiler_params=_cparams(3), name="na_attn",
    )(qkv, qkv, qkv, vt, bias, bias)


def _na_bias_table(rpb):
    h, n_ro, n_co = rpb.shape
    kr, qr = np.arange(NA_WROWS), np.arange(NA_QROWS)
    kc, qc = np.arange(GRID_W), np.arange(GRID_W)
    cs = np.clip(qc - NA_WIN_C // 2, 0, GRID_W - NA_WIN_C)
    col_ok = (kc[:, None] >= cs[None, :]) & (kc[:, None] < cs[None, :] + NA_WIN_C)
    co = kc[:, None] - qc[None, :] + NA_WIN_C - 1
    col_sel = (co[None] == np.arange(n_co)[:, None, None]).astype(np.float32)
    tables = []
    for d, rel in ((0, np.zeros_like(qr)), (NA_WIN_R // 2, qr), (NA_WIN_R, np.full_like(qr, NA_WIN_R // 2))):
        row_ok = (kr[:, None] >= rel[None, :]) & (kr[:, None] < rel[None, :] + NA_WIN_R)
        ro = kr[:, None] - qr[None, :] - d + NA_WIN_R - 1
        row_sel = (ro[:, :, None] == np.arange(n_ro)[None, None, :]).astype(np.float32)
        tb = jnp.einsum('kqa,hab,bcd->hkcqd', jnp.asarray(row_sel), rpb.astype(F32) * LOG2E,
                        jnp.asarray(col_sel), precision=lax.Precision.HIGHEST)
        ok = row_ok[:, None, :, None] & col_ok[None, :, None, :]
        tables.append(jnp.where(jnp.asarray(ok)[None], tb, NEG))
    tbl = jnp.stack(tables, axis=0)
    return tbl.reshape(3 * (h // 2), 2, NA_WROWS * GRID_W, TM).astype(F32)


def _hy_filter_call(length, w1, b1, w2, b2, w3, b3, w4, freq):
    order = w2.shape[0]
    w = w4.shape[1] // 2
    tl = min(length, 512)
    hi = lax.Precision.HIGHEST
    t = np.linspace(0.0, 1.0, length, dtype=np.float64)[:, None]
    bands = (HY_EMB_DIM - 1) // 2
    ang = 2.0 * math.pi * np.arange(length, dtype=np.float64)[:, None] / length
    fq = np.linspace(1e-4, bands - 1, bands, dtype=np.float64)[None, :]
    emb = np.concatenate([t, np.cos(fq * ang), -np.sin(fq * ang)], axis=-1).astype(np.float32)
    emb = np.pad(emb, ((0, 0), (0, LANES - HY_EMB_DIM)))
    max_decay = math.log(HY_DECAY_TARGET) / HY_FAST_DECAY
    min_decay = math.log(HY_DECAY_TARGET) / HY_SLOW_DECAY
    deltas = np.linspace(min_decay, max_decay, w, dtype=np.float64)
    decay = np.exp(-t * np.abs(deltas)[None, :]).astype(np.float32)
    decay2 = np.concatenate([decay, decay], axis=1)
    w1p = jnp.pad(w1, ((0, LANES - HY_EMB_DIM), (0, 0)))

    def body(e_ref, d_ref, w1_ref, b1_ref, w2_ref, b2_ref, w3_ref, b3_ref, w4_ref, f_ref,
             h_ref, s_ref):
        i = pl.program_id(0)
        fr = f_ref[...]
        hdn = jnp.sin(fr * (jnp.dot(e_ref[...], w1_ref[...], precision=hi,
                                    preferred_element_type=F32) + b1_ref[...]))
        hdn = jnp.sin(fr * (jnp.dot(hdn, w2_ref[...], precision=hi,
                                    preferred_element_type=F32) + b2_ref[...]))
        hdn = jnp.sin(fr * (jnp.dot(hdn, w3_ref[...], precision=hi,
                                    preferred_element_type=F32) + b3_ref[...]))
        taps = jnp.dot(hdn, w4_ref[...], precision=hi, preferred_element_type=F32) * d_ref[...]
        row = lax.broadcasted_iota(jnp.int32, taps.shape, 0) + i * tl
        col = lax.broadcasted_iota(jnp.int32, taps.shape, 1)
        taps = jnp.where((row == 0) & (col >= w), 0.0, taps)
        h_ref[...] = taps

        @pl.when(i == 0)
        def _():
            s_ref[...] = jnp.zeros(s_ref.shape, F32)
        s_ref[...] += jnp.sum(jnp.abs(taps), axis=0, keepdims=True)

    full = lambda a: pl.BlockSpec(a.shape, lambda i: (0,) * a.ndim)
    ops = (w1p, b1.reshape(1, order), w2, b2.reshape(1, order), w3, b3.reshape(1, order), w4,
           freq.reshape(1, order))
    return pl.pallas_call(
        body, grid=(length // tl,),
        in_specs=[pl.BlockSpec((tl, LANES), lambda i: (i, 0)),
                  pl.BlockSpec((tl, 2 * w), lambda i: (i, 0))] + [full(a) for a in ops],
        out_specs=[pl.BlockSpec((tl, 2 * w), lambda i: (i, 0)),
                   pl.BlockSpec((1, 2 * w), lambda i: (0, 0))],
        out_shape=[jax.ShapeDtypeStruct((length, 2 * w), F32), jax.ShapeDtypeStruct((1, 2 * w), F32)],
        compiler_params=_cparams(1), name="hy_filter",
    )(jnp.asarray(emb), jnp.asarray(decay2), *ops)


def _dft_tables(l1, fb):
    n1 = 2 * l1
    n = n1 * LANES
    nf = -(-(n1 // 2 + 1) // fb) * fb
    f1 = np.arange(nf, dtype=np.float64)
    wgt = np.where((f1 == 0) | (f1 == n1 // 2), 1.0, np.where(f1 < n1 // 2, 2.0, 0.0))[None, :]
    t1 = np.arange(l1, dtype=np.float64)
    th1 = 2.0 * np.pi * np.outer(f1, t1) / n1
    fwd1 = np.concatenate([np.cos(th1), -np.sin(th1)], axis=0)
    inv1 = np.concatenate([wgt * np.cos(th1).T, -wgt * np.sin(th1).T], axis=1) / n
    f2 = np.arange(LANES, dtype=np.float64)
    t2 = np.arange(LANES, dtype=np.float64)
    fr = f1[:, None, None] + n1 * f2[None, :, None]
    th2 = 2.0 * np.pi * fr * t2[None, None, :] / n
    gr, gi = np.cos(th2), -np.sin(th2)
    gb = np.concatenate([np.concatenate([gr, -gi], axis=2),
                         np.concatenate([gi, gr], axis=2)], axis=1)
    hb = np.transpose(gb, (0, 2, 1))
    as_bf = lambda a: jnp.asarray(a.astype(np.float32)).astype(BF16)
    return as_bf(fwd1), as_bf(inv1), as_bf(gb), as_bf(hb)


def _hy_stage1_call(xv, fwd1):
    bx, l1, cols = xv.shape
    n2 = fwd1.shape[0]
    tn = min(cols, 4096)

    def body(f_ref, x_ref, o_ref):
        o_ref[0] = jnp.dot(f_ref[...], x_ref[0], preferred_element_type=F32).astype(BF16)

    return pl.pallas_call(
        body, grid=(bx, cols // tn),
        in_specs=[pl.BlockSpec((n2, l1), lambda i, j: (0, 0)),
                  pl.BlockSpec((1, l1, tn), lambda i, j: (i, 0, j))],
        out_specs=pl.BlockSpec((1, n2, tn), lambda i, j: (i, 0, j)),
        out_shape=jax.ShapeDtypeStruct((bx, n2, cols), BF16),
        compiler_params=_cparams(2), name="hy_dft1",
    )(fwd1, xv)


def _hy_filter_spec_call(a5, gb, sums, fb):
    n1, w2 = a5.shape[2], a5.shape[4]
    w = w2 // 2

    def body(a_ref, g_ref, s_ref, o_ref):
        sv = s_ref[...]
        inv = 1.0 / (sv[:, :w] + sv[:, w:])
        for k in range(fb):
            a = jnp.concatenate([a_ref[0, 0, k], a_ref[0, 1, k]], axis=0)
            z = jnp.dot(g_ref[k], a, preferred_element_type=F32)
            o_ref[k, 0] = (z[:LANES, :w] + z[:LANES, w:]) * inv
            o_ref[k, 1] = (z[LANES:, :w] - z[LANES:, w:]) * inv

    return pl.pallas_call(
        body, grid=(n1 // fb,),
        in_specs=[pl.BlockSpec((1, 2, fb, LANES, w2), lambda i: (0, 0, i, 0, 0)),
                  pl.BlockSpec((fb, 2 * LANES, 2 * LANES), lambda i: (i, 0, 0)),
                  pl.BlockSpec((1, w2), lambda i: (0, 0))],
        out_specs=pl.BlockSpec((fb, 2, LANES, w), lambda i: (i, 0, 0, 0)),
        out_shape=jax.ShapeDtypeStruct((n1, 2, LANES, w), F32),
        compiler_params=_cparams(1), name="hy_fspec",
    )(a5, gb, sums)


def _hy_stage23_call(a5, gb, hb, kf, fb):
    b, _, n1, _, w = a5.shape

    def body(a_ref, g_ref, h_ref, k_ref, o_ref):
        for k in range(fb):
            a = jnp.concatenate([a_ref[0, 0, k], a_ref[0, 1, k]], axis=0)
            z = jnp.dot(g_ref[k], a, preferred_element_type=F32)
            zr, zi = z[:LANES], z[LANES:]
            kr, ki = k_ref[k, 0], k_ref[k, 1]
            y = jnp.concatenate([zr * kr - zi * ki, zr * ki + zi * kr], axis=0).astype(BF16)
            cc = jnp.dot(h_ref[k], y, preferred_element_type=F32)
            o_ref[0, 0, k] = cc[:LANES].astype(BF16)
            o_ref[0, 1, k] = cc[LANES:].astype(BF16)

    return pl.pallas_call(
        body, grid=(n1 // fb, b),
        in_specs=[pl.BlockSpec((1, 2, fb, LANES, w), lambda i, j: (j, 0, i, 0, 0)),
                  pl.BlockSpec((fb, 2 * LANES, 2 * LANES), lambda i, j: (i, 0, 0)),
                  pl.BlockSpec((fb, 2 * LANES, 2 * LANES), lambda i, j: (i, 0, 0)),
                  pl.BlockSpec((fb, 2, LANES, w), lambda i, j: (i, 0, 0, 0))],
        out_specs=pl.BlockSpec((1, 2, fb, LANES, w), lambda i, j: (j, 0, i, 0, 0)),
        out_shape=jax.ShapeDtypeStruct(a5.shape, BF16),
        compiler_params=_cparams(2), name="hy_dft23",
    )(a5, gb, hb, kf)


def _hy_stage4_call(cv, inv1, x0v, zv, skip_t):
    b, n2, cols = cv.shape
    l1 = inv1.shape[0]
    tn = skip_t.shape[1]

    def body(f_ref, c_ref, x0_ref, z_ref, s_ref, o_ref):
        y = jnp.dot(f_ref[...], c_ref[0], preferred_element_type=F32)
        z = z_ref[0].astype(F32)
        o_ref[0] = (x0_ref[0].astype(F32) * (y + s_ref[...] * z)).astype(BF16)

    return pl.pallas_call(
        body, grid=(b, cols // tn),
        in_specs=[pl.BlockSpec((l1, n2), lambda i, j: (0, 0)),
                  pl.BlockSpec((1, n2, tn), lambda i, j: (i, 0, j)),
                  pl.BlockSpec((1, l1, tn), lambda i, j: (i, 0, j)),
                  pl.BlockSpec((1, l1, tn), lambda i, j: (i, 0, j)),
                  pl.BlockSpec((1, tn), lambda i, j: (0, 0))],
        out_specs=pl.BlockSpec((1, l1, tn), lambda i, j: (i, 0, j)),
        out_shape=jax.ShapeDtypeStruct((b, l1, cols), BF16),
        compiler_params=_cparams(2), name="hy_dft4",
    )(inv1, cv, x0v, zv, skip_t)


def _hy_dense_call(x0, z, taps, sums, skip):
    b, c, w = z.shape
    n = 2 * c
    th = 2.0 * np.pi * np.outer(np.arange(n, dtype=np.float64), np.arange(c, dtype=np.float64)) / n
    fwd = jnp.asarray(np.concatenate([np.cos(th), -np.sin(th)], axis=0).astype(np.float32)).astype(BF16)
    inv = jnp.asarray((np.concatenate([np.cos(th).T, -np.sin(th).T], axis=1) / n)
                      .astype(np.float32)).astype(BF16)

    def body(f_ref, i_ref, x0_ref, z_ref, t_ref, s_ref, k_ref, o_ref):
        sv = s_ref[...]
        nrm = 1.0 / (sv[:, :w] + sv[:, w:])
        tf = jnp.dot(f_ref[...], t_ref[...].astype(BF16), preferred_element_type=F32)
        kr = (tf[:n, :w] + tf[:n, w:]) * nrm
        ki = (tf[n:, :w] - tf[n:, w:]) * nrm
        zf = jnp.dot(f_ref[...], z_ref[0], preferred_element_type=F32)
        zr, zi = zf[:n], zf[n:]
        y = jnp.concatenate([zr * kr - zi * ki, zr * ki + zi * kr], axis=0).astype(BF16)
        yt = jnp.dot(i_ref[...], y, preferred_element_type=F32)
        o_ref[0] = (x0_ref[0].astype(F32) * (yt + k_ref[...] * z_ref[0].astype(F32))).astype(BF16)

    return pl.pallas_call(
        body, grid=(b,),
        in_specs=[pl.BlockSpec((2 * n, c), lambda i: (0, 0)),
                  pl.BlockSpec((c, 2 * n), lambda i: (0, 0)),
                  pl.BlockSpec((1, c, w), lambda i: (i, 0, 0)),
                  pl.BlockSpec((1, c, w), lambda i: (i, 0, 0)),
                  pl.BlockSpec((c, 2 * w), lambda i: (0, 0)),
                  pl.BlockSpec((1, 2 * w), lambda i: (0, 0)),
                  pl.BlockSpec((1, w), lambda i: (0, 0))],
        out_specs=pl.BlockSpec((1, c, w), lambda i: (i, 0, 0)),
        out_shape=jax.ShapeDtypeStruct((b, c, w), BF16),
        compiler_params=_cparams(1), name="hy_dense",
    )(fwd, inv, x0, z, taps, sums, skip.reshape(1, w))


def _hyena_long(x0, z, fparams, skip):
    b, l, w = z.shape
    l1 = l // LANES
    n1 = 2 * l1
    fb = min(8, n1)
    fwd1, inv1, gb, hb = _dft_tables(l1, fb)
    nf = gb.shape[0]
    taps, sums = _hy_filter_call(l, *fparams)
    ta = _hy_stage1_call(taps.astype(BF16).reshape(1, l1, LANES * 2 * w), fwd1)
    kf = _hy_filter_spec_call(ta.reshape(1, 2, nf, LANES, 2 * w), gb, sums, fb)
    za = _hy_stage1_call(z.reshape(b, l1, LANES * w), fwd1)
    cc = _hy_stage23_call(za.reshape(b, 2, nf, LANES, w), gb, hb, kf, fb)
    tn = 8 * w
    skip_t = jnp.tile(skip.reshape(1, w), (1, tn // w))
    y = _hy_stage4_call(cc.reshape(b, 2 * nf, LANES * w), inv1, x0.reshape(b, l1, LANES * w),
                        z.reshape(b, l1, LANES * w), skip_t)
    return y.reshape(b, l, w)


def _hyena_short(x0, z, fparams, skip):
    taps, sums = _hy_filter_call(z.shape[1], *fparams)
    return _hy_dense_call(x0, z, taps, sums, skip)


def _rope_tables(l, c):
    t = np.arange(l)
    row = (t // GRID_W).astype(np.float64)
    col = (t % GRID_W).astype(np.float64)
    n_pairs = HEAD_DIM // 4
    inv_freq = ROPE_THETA ** (-np.arange(n_pairs, dtype=np.float64) / n_pairs)
    ang = np.concatenate([row[:, None] * inv_freq, col[:, None] * inv_freq], axis=-1)
    cos = np.repeat(np.cos(ang), 2, axis=1)
    sin = np.repeat(np.sin(ang), 2, axis=1)
    even = (np.arange(HEAD_DIM) % 2 == 0)[None, :]
    se = np.where(even, -sin, 0.0)
    so = np.where(even, 0.0, sin)
    pad = lambda a, v: np.concatenate([a, np.full((c, HEAD_DIM), v)], axis=0)
    two = lambda a: jnp.asarray(np.concatenate([a, a], axis=1).astype(np.float32))
    return two(pad(cos, 1.0)), two(pad(se, 0.0)), two(pad(so, 0.0))


def _block_diag_ones():
    i = np.arange(2 * LANES)
    return jnp.asarray((i[:, None] // HEAD_DIM == i[None, :] // HEAD_DIM).astype(np.float32)).astype(BF16)


def _gain_rows(gains, scales):
    rows = [jnp.tile(g.astype(F32) * s, 2 * LANES // HEAD_DIM) for g, s in zip(gains, scales)]
    rows += [jnp.zeros((2 * LANES,), F32)] * (8 - len(rows))
    return jnp.stack(rows, axis=0)


def kernel(x, c, ctx, c_ctx, w_ada, b_ada, w_up, ffn_conv_w, ffn_conv_b, w_down, w_in_e, w_out_e, na_q_gain, na_k_gain, na_rpb, da_q_gain, da_k_gain, da_lambda_q1, da_lambda_k1, da_lambda_q2, da_lambda_k2, da_subln_gain, w_in_o, w_out_o, gqa_q_gain, gqa_k_gain, hy_conv_w, hy_conv_b, hy_w1, hy_b1, hy_w2, hy_b2, hy_w3, hy_b3, hy_w4, hy_freq, hy_skip):
    b, l, d = x.shape
    cl = ctx.shape[1]
    t = l + cl
    depth = w_ada.shape[0]
    f = w_down.shape[1]
    assert cl == TM and (l // GRID_W) >= NA_WROWS + NA_QROWS
    assert w_in_e.shape[2] == 3072 and w_in_o.shape[2] == 2304 and d % LANES == 0
    scale = HEAD_DIM ** -0.5 * LOG2E

    rows = -(-(b + 1) // 8) * 8
    cs = jnp.zeros((rows, d), F32).at[:b].set(c).at[b].set(c_ctx)
    mods = _ada_call(cs, w_ada, b_ada)
    modarrs = []
    for layer in range(depth):
        lat = mods[layer, :b].reshape(b, 1, 6, d)
        cx = jnp.broadcast_to(mods[layer, b].reshape(1, 1, 6, d), (b, 1, 6, d))
        modarrs.append(jnp.concatenate([lat, cx], axis=1))

    ropes = _rope_tables(l, cl)
    bd = _block_diag_ones()
    x_all, h = _mod0_call(x, ctx, modarrs[0])

    for layer in range(depth):
        i = layer // 2
        if layer % 2 == 0:
            lam_init = 0.8 - 0.6 * math.exp(-0.3 * layer)
            gains = _gain_rows((na_q_gain[i], na_k_gain[i], da_q_gain[i], da_k_gain[i]),
                               (scale, 1.0, scale, 1.0))
            qkv, vat, qbt, vbt = _inproj_call(
                _inproj_even_body, "inproj_even", h, w_in_e[i].astype(BF16), bd, gains, ropes, (1536,),
                ((b, 8, HEAD_DIM + ONES_ROWS, t), (b, 512, t), (b, 4, LANES + ONES_ROWS, t)))
            bias = _na_bias_table(na_rpb[i])
            ya = _na_call(qkv, vat, bias, l)
            lamv = jnp.stack([da_lambda_q1[i], da_lambda_k1[i], da_lambda_q2[i], da_lambda_k2[i]]).astype(F32)
            yb = _flash_call("da_attn", qbt, qkv, vbt, 1024 // 256, False, LANES, TK_DA,
                             _make_da_finish(lam_init),
                             (lamv, da_subln_gain[i].reshape(1, LANES).astype(F32)), 512, l)
            w_out = w_out_e[i].astype(BF16)
        else:
            gains = _gain_rows((gqa_q_gain[i], gqa_k_gain[i]), (scale, 1.0))
            kd, x0, z, qt, vt = _inproj_call(
                functools.partial(_inproj_odd_body, nl=l // TM, nt=t // TM), "inproj_odd", h,
                w_in_o[i].astype(BF16), bd, gains, ropes, (2 * LANES, 512, 512),
                ((b, 512, t), (b, 2, HEAD_DIM + ONES_ROWS, t)),
                extra=(hy_conv_w[i], hy_conv_b[i].reshape(1, -1)), halo=True)
            ya = _flash_call("gqa_attn", qt, kd, vt, 0, True, HEAD_DIM, TK_GQA,
                             _gqa_finish, (), 512, l)
            fparams = (hy_w1[i], hy_b1[i], hy_w2[i], hy_b2[i], hy_w3[i], hy_b3[i], hy_w4[i], hy_freq[i])
            yd_l = _hyena_long(x0[:, :l], z[:, :l], fparams, hy_skip[i])
            if layer < depth - 1:
                yd_c = _hyena_short(x0[:, l:], z[:, l:], fparams, hy_skip[i])
            else:
                yd_c = jnp.zeros((b, cl, x0.shape[2]), BF16)
            yb = jnp.concatenate([yd_l, yd_c], axis=1)
            w_out = w_out_o[i].astype(BF16)
        ka = ya.shape[2]
        x_all, act = _mix_ffn_up_call(ya, yb, w_out[:ka], w_out[ka:], x_all, modarrs[layer],
                                      w_up[layer].astype(BF16), ffn_conv_w[layer], ffn_conv_b[layer])
        if layer == depth - 1:
            return _ffn_last_call(act, w_down[layer].astype(BF16), x_all, modarrs[layer])
        x_all, h = _ffn_down_call(act, w_down[layer].astype(BF16), x_all, modarrs[layer],
                                  modarrs[layer + 1])
```

```python
import functools
import math

import numpy as np
import jax
import jax.numpy as jnp
from jax import lax
from jax.experimental import pallas as pl
from jax.experimental.pallas import tpu as pltpu

F32 = jnp.float32
BF16 = jnp.bfloat16

HEAD_DIM = 64
GRID_W = 64
ROPE_THETA = 10000.0
EPS = 1e-6
NA_WIN_R = 8
NA_WIN_C = 16
HY_EMB_DIM = 33
HY_FAST_DECAY = 0.3
HY_SLOW_DECAY = 1.5
HY_DECAY_TARGET = 1e-2

LANES = 128
TM = 256
TK_DA = 512
TK_GQA = 256
NA_QROWS = TM // GRID_W
NA_WROWS = NA_QROWS + NA_WIN_R
ONES_ROWS = 16
NEG = -1e30
LOG2E = 1.4426950408889634
VMEM_LIMIT = 56 * 1024 * 1024


def _cparams(n_axes):
    return pltpu.CompilerParams(dimension_semantics=("arbitrary",) * n_axes,
                                vmem_limit_bytes=VMEM_LIMIT)


def _modulate(x, sh, sc):
    ms = jnp.mean(x * x, axis=-1, keepdims=True)
    return x * lax.rsqrt(ms + EPS) * (1.0 + sc) + sh


def _seg_norm(y, bd, gain):
    ss = jnp.dot((y * y).astype(BF16), bd, preferred_element_type=F32)
    return y * lax.rsqrt(ss * (1.0 / HEAD_DIM) + EPS) * gain


def _rope(y, c, se, so):
    return y * c + pltpu.roll(y, LANES - 1, 1) * se + pltpu.roll(y, 1, 1) * so


def _dot_nt(a, b):
    return lax.dot_general(a, b, (((1,), (1,)), ((), ())), preferred_element_type=F32)


def _ada_body(c_ref, w_ref, b_ref, o_ref):
    c = c_ref[...]
    a = (c / (1.0 + jnp.exp(-c))).astype(BF16)
    o_ref[0] = jnp.dot(a, w_ref[0].astype(BF16), preferred_element_type=F32) + b_ref[0]


def _ada_call(cs, w_ada, b_ada):
    depth, d, n6 = w_ada.shape
    rows = cs.shape[0]
    tn = 1536
    return pl.pallas_call(
        _ada_body, grid=(depth, n6 // tn),
        in_specs=[pl.BlockSpec((rows, d), lambda l, n: (0, 0)),
                  pl.BlockSpec((1, d, tn), lambda l, n: (l, 0, n)),
                  pl.BlockSpec((1, 1, tn), lambda l, n: (l, 0, n))],
        out_specs=pl.BlockSpec((1, rows, tn), lambda l, n: (l, 0, n)),
        out_shape=jax.ShapeDtypeStruct((depth, rows, n6), F32),
        compiler_params=_cparams(2), name="ada",
    )(cs, w_ada, b_ada.reshape(depth, 1, n6))


def _mod0_call(x, ctx, modarr):
    b, l, d = x.shape
    c = ctx.shape[1]
    t = l + c
    nl = l // TM

    def body(x_ref, c_ref, m_ref, xo_ref, h_ref):
        j = pl.program_id(1)
        xv = jnp.where(j < nl, x_ref[0], c_ref[0])
        xo_ref[0] = xv
        h_ref[0] = _modulate(xv, m_ref[0, 0, 0:1, :], m_ref[0, 0, 1:2, :]).astype(BF16)

    return pl.pallas_call(
        body, grid=(b, t // TM),
        in_specs=[pl.BlockSpec((1, TM, d), lambda i, j: (i, jnp.minimum(j, nl - 1), 0)),
                  pl.BlockSpec((1, TM, d), lambda i, j: (i, 0, 0)),
                  pl.BlockSpec((1, 1, 6, d), lambda i, j: (i, j // nl, 0, 0))],
        out_specs=[pl.BlockSpec((1, TM, d), lambda i, j: (i, j, 0)),
                   pl.BlockSpec((1, TM, d), lambda i, j: (i, j, 0))],
        out_shape=[jax.ShapeDtypeStruct((b, t, d), F32), jax.ShapeDtypeStruct((b, t, d), BF16)],
        compiler_params=_cparams(2), name="mod0",
    )(x, ctx, modarr)


def _inproj_even_body(a_ref, w_ref, bd_ref, g_ref, rc_ref, rse_ref, rso_ref,
                      o_ref, vat_ref, qbt_ref, vbt_ref):
    a = a_ref[0]
    bd = bd_ref[...]
    rc, rse, rso = rc_ref[...], rse_ref[...], rso_ref[...]
    plan = ((0, False, 0), (1, False, 512), (None, False, None), (2, True, None), (3, True, 1024),
            (None, False, None))
    for seg, (gain_row, rope, out_col) in enumerate(plan):
        y = jnp.dot(a, w_ref[:, seg * 512:(seg + 1) * 512], preferred_element_type=F32)
        for half in range(2):
            yy = y[:, half * 256:(half + 1) * 256]
            if gain_row is not None:
                yy = _seg_norm(yy, bd, g_ref[gain_row:gain_row + 1, :])
            for blk in range(2):
                z = yy[:, blk * LANES:(blk + 1) * LANES]
                if rope:
                    z = _rope(z, rc, rse, rso)
                m = half * 2 + blk
                if out_col is not None:
                    col = out_col + m * LANES
                    o_ref[0, :, col:col + LANES] = z.astype(BF16)
                    continue
                zt = z.T.astype(BF16)
                if seg == 2:
                    vat_ref[0, 2 * m, 0:HEAD_DIM, :] = zt[:HEAD_DIM]
                    vat_ref[0, 2 * m + 1, 0:HEAD_DIM, :] = zt[HEAD_DIM:]
                elif seg == 3:
                    qbt_ref[0, m * LANES:(m + 1) * LANES, :] = zt
                else:
                    vbt_ref[0, m, 0:LANES, :] = zt
    ones = jnp.ones((ONES_ROWS, TM), BF16)
    for hd in range(vat_ref.shape[1]):
        vat_ref[0, hd, HEAD_DIM:HEAD_DIM + ONES_ROWS, :] = ones
    for hd in range(vbt_ref.shape[1]):
        vbt_ref[0, hd, LANES:LANES + ONES_ROWS, :] = ones


def _inproj_odd_body(a_ref, ap_ref, an_ref, w_ref, bd_ref, g_ref, rc_ref, rse_ref, rso_ref,
                     cw_ref, cb_ref, o_ref, x0_ref, z_ref, qt_ref, vt_ref, *, nl, nt):
    j = pl.program_id(1)
    a = a_ref[0]
    bd = bd_ref[...]
    rc, rse, rso = rc_ref[...], rse_ref[...], rso_ref[...]
    y = jnp.dot(a, w_ref[:, 0:512], preferred_element_type=F32)
    for half in range(2):
        yy = _seg_norm(y[:, half * 256:(half + 1) * 256], bd, g_ref[0:1, :])
        for blk in range(2):
            z = _rope(yy[:, blk * LANES:(blk + 1) * LANES], rc, rse, rso)
            m = half * 2 + blk
            qt_ref[0, m * LANES:(m + 1) * LANES, :] = z.T.astype(BF16)
    y = jnp.dot(a, w_ref[:, 512:768], preferred_element_type=F32)
    k = _seg_norm(y[:, :LANES], bd_ref[0:LANES, 0:LANES], g_ref[1:2, 0:LANES])
    k = _rope(k, rc, rse, rso)
    kr = pltpu.roll(k, HEAD_DIM, 1)
    lo = lax.broadcasted_iota(jnp.int32, k.shape, 1) < HEAD_DIM
    o_ref[0, :, 0:LANES] = jnp.where(lo, k, kr).astype(BF16)
    o_ref[0, :, LANES:2 * LANES] = jnp.where(lo, kr, k).astype(BF16)
    vt = y[:, LANES:].T.astype(BF16)
    ones = jnp.ones((ONES_ROWS, TM), BF16)
    for hd in range(2):
        vt_ref[0, hd, 0:HEAD_DIM, :] = vt[hd * HEAD_DIM:(hd + 1) * HEAD_DIM]
        vt_ref[0, hd, HEAD_DIM:HEAD_DIM + ONES_ROWS, :] = ones
    hb = ap_ref.shape[1]
    zero = jnp.zeros((hb, a.shape[1]), BF16)
    ap = jnp.where((j == 0) | (j == nl), zero, ap_ref[0])
    an = jnp.where((j == nl - 1) | (j == nt - 1), zero, an_ref[0])
    a_ext = jnp.concatenate([ap, a, an], axis=0)
    conv = []
    for seg in range(3):
        sl = slice(seg * 512, (seg + 1) * 512)
        g = jnp.dot(a_ext, w_ref[:, 768 + seg * 512:768 + (seg + 1) * 512], preferred_element_type=F32)
        conv.append(g[hb - 1:hb - 1 + TM] * cw_ref[0:1, sl] + g[hb:hb + TM] * cw_ref[1:2, sl]
                    + g[hb + 1:hb + 1 + TM] * cw_ref[2:3, sl] + cb_ref[0:1, sl])
        if seg == 0:
            x0_ref[0] = conv[0].astype(BF16)
    z_ref[0] = (conv[2] * conv[1]).astype(BF16)


def _inproj_call(body, name, h, w, bd, gains, ropes, n_outs, t_shapes, extra=(), halo=False):
    b, t, d = h.shape
    n_in = w.shape[1]
    nt = t // TM
    hb = 16
    r = TM // hb
    rc, rse, rso = ropes

    def t_spec(shape):
        nd = len(shape)
        return pl.BlockSpec((1,) + tuple(shape[1:-1]) + (TM,), lambda i, j: (i,) + (0,) * (nd - 2) + (j,))

    acts, act_specs = [h], [pl.BlockSpec((1, TM, d), lambda i, j: (i, j, 0))]
    if halo:
        acts += [h, h]
        act_specs += [pl.BlockSpec((1, hb, d), lambda i, j: (i, jnp.maximum(j * r - 1, 0), 0)),
                      pl.BlockSpec((1, hb, d), lambda i, j: (i, jnp.minimum((j + 1) * r, nt * r - 1), 0))]
    return pl.pallas_call(
        body, grid=(b, nt),
        in_specs=act_specs
        + [pl.BlockSpec((d, n_in), lambda i, j: (0, 0)),
           pl.BlockSpec(bd.shape, lambda i, j: (0, 0)),
           pl.BlockSpec(gains.shape, lambda i, j: (0, 0)),
           pl.BlockSpec((TM, LANES), lambda i, j: (j, 0)),
           pl.BlockSpec((TM, LANES), lambda i, j: (j, 0)),
           pl.BlockSpec((TM, LANES), lambda i, j: (j, 0))]
        + [pl.BlockSpec(e.shape, lambda i, j: (0,) * e.ndim) for e in extra],
        out_specs=[pl.BlockSpec((1, TM, n), lambda i, j: (i, j, 0)) for n in n_outs]
        + [t_spec(s) for s in t_shapes],
        out_shape=[jax.ShapeDtypeStruct((b, t, n), BF16) for n in n_outs]
        + [jax.ShapeDtypeStruct(s, BF16) for s in t_shapes],
        compiler_params=_cparams(2), name=name,
    )(*acts, w, bd, gains, rc, rse, rso, *extra)


def _mix_ffn_up_call(ya, yb, wa, wb, x, modarr, w_up, conv_w, conv_b):
    b, t, d = x.shape
    f = w_up.shape[1] // 2
    nl = (t - TM) // TM
    nt = t // TM
    hb = 16
    r = TM // hb
    ka, kb = ya.shape[2], yb.shape[2]
    chunks = [(c0, min(2 * LANES, f - c0)) for c0 in range(0, f, 2 * LANES)]

    def body(ya_ref, yap_ref, yan_ref, yb_ref, ybp_ref, ybn_ref, x_ref, xp_ref, xn_ref,
             wa_ref, wb_ref, m_ref, wg_ref, wv_ref, cw_ref, cb_ref, xo_ref, o_ref):
        j = pl.program_id(1)
        ext = lambda p, c, n: jnp.concatenate([p[0], c[0], n[0]], axis=0)
        y = (jnp.dot(ext(yap_ref, ya_ref, yan_ref), wa_ref[...], preferred_element_type=F32)
             + jnp.dot(ext(ybp_ref, yb_ref, ybn_ref), wb_ref[...], preferred_element_type=F32))
        x1 = ext(xp_ref, x_ref, xn_ref) + m_ref[0, 0, 2:3, :] * y
        xo_ref[0] = x1[hb:hb + TM]
        hx = _modulate(x1, m_ref[0, 0, 3:4, :], m_ref[0, 0, 4:5, :]).astype(BF16)
        zero = jnp.zeros((hb, d), BF16)
        a = hx[hb:hb + TM]
        ap = jnp.where((j == 0) | (j == nl), zero, hx[0:hb])
        an = jnp.where((j == nl - 1) | (j == nt - 1), zero, hx[hb + TM:])
        a_ext = jnp.concatenate([ap, a, an], axis=0)
        for c0, cw in chunks:
            sl = slice(c0, c0 + cw)
            g = jnp.dot(a_ext, wg_ref[:, sl], preferred_element_type=F32)
            u = (g[hb - 1:hb - 1 + TM] * cw_ref[0:1, sl] + g[hb:hb + TM] * cw_ref[1:2, sl]
                 + g[hb + 1:hb + 1 + TM] * cw_ref[2:3, sl] + cb_ref[0:1, sl])
            v = jnp.dot(a, wv_ref[:, sl], preferred_element_type=F32)
            o_ref[0, :, sl] = ((u / (1.0 + jnp.exp(-u))) * v).astype(BF16)

    main = lambda w: pl.BlockSpec((1, TM, w), lambda i, j: (i, j, 0))
    prev = lambda w: pl.BlockSpec((1, hb, w), lambda i, j: (i, jnp.maximum(j * r - 1, 0), 0))
    nxt = lambda w: pl.BlockSpec((1, hb, w), lambda i, j: (i, jnp.minimum((j + 1) * r, nt * r - 1), 0))
    full = lambda a: pl.BlockSpec(a.shape, lambda i, j: (0,) * a.ndim)
    return pl.pallas_call(
        body, grid=(b, nt),
        in_specs=[main(ka), prev(ka), nxt(ka), main(kb), prev(kb), nxt(kb), main(d), prev(d), nxt(d),
                  full(wa), full(wb), pl.BlockSpec((1, 1, 6, d), lambda i, j: (i, j // nl, 0, 0)),
                  pl.BlockSpec((d, f), lambda i, j: (0, 0)), pl.BlockSpec((d, f), lambda i, j: (0, 1)),
                  pl.BlockSpec((3, f), lambda i, j: (0, 0)), pl.BlockSpec((1, f), lambda i, j: (0, 0))],
        out_specs=[pl.BlockSpec((1, TM, d), lambda i, j: (i, j, 0)),
                   pl.BlockSpec((1, TM, f), lambda i, j: (i, j, 0))],
        out_shape=[jax.ShapeDtypeStruct((b, t, d), F32), jax.ShapeDtypeStruct((b, t, f), BF16)],
        compiler_params=_cparams(2), name="mix_ffn_up",
    )(ya, ya, ya, yb, yb, yb, x, x, x, wa, wb, modarr, w_up, w_up, conv_w, conv_b.reshape(1, f))


def _ffn_down_call(act, w_down, x, modarr, modarr_next):
    b, t, d = x.shape
    f = w_down.shape[0]
    nl = (t - TM) // TM
    nt = t // TM

    gs = 2 if b % 2 == 0 else 1

    def body(a_ref, wd_ref, x_ref, m_ref, mn_ref, xo_ref, h_ref):
        a = jnp.concatenate([a_ref[gi] for gi in range(gs)], axis=0)
        y = jnp.dot(a, wd_ref[...], preferred_element_type=F32)
        for gi in range(gs):
            x2 = x_ref[gi] + m_ref[gi, 0, 5:6, :] * y[gi * TM:(gi + 1) * TM]
            xo_ref[gi] = x2
            h_ref[gi] = _modulate(x2, mn_ref[gi, 0, 0:1, :], mn_ref[gi, 0, 1:2, :]).astype(BF16)

    return pl.pallas_call(
        body, grid=(b // gs, nt),
        in_specs=[pl.BlockSpec((gs, TM, f), lambda i, j: (i, j, 0)),
                  pl.BlockSpec((f, d), lambda i, j: (0, 0)),
                  pl.BlockSpec((gs, TM, d), lambda i, j: (i, j, 0)),
                  pl.BlockSpec((gs, 1, 6, d), lambda i, j: (i, j // nl, 0, 0)),
                  pl.BlockSpec((gs, 1, 6, d), lambda i, j: (i, j // nl, 0, 0))],
        out_specs=[pl.BlockSpec((gs, TM, d), lambda i, j: (i, j, 0)),
                   pl.BlockSpec((gs, TM, d), lambda i, j: (i, j, 0))],
        out_shape=[jax.ShapeDtypeStruct((b, t, d), F32), jax.ShapeDtypeStruct((b, t, d), BF16)],
        compiler_params=_cparams(2), name="ffn_down",
    )(act, w_down, x, modarr, modarr_next)


def _ffn_last_call(act, w_down, x, modarr):
    b, t, d = x.shape
    f = w_down.shape[0]
    nl = (t - TM) // TM

    gs = 2 if b % 2 == 0 else 1

    def body(a_ref, wd_ref, x_ref, m_ref, xo_ref):
        a = jnp.concatenate([a_ref[gi] for gi in range(gs)], axis=0)
        y = jnp.dot(a, wd_ref[...], preferred_element_type=F32)
        for gi in range(gs):
            xo_ref[gi] = x_ref[gi] + m_ref[gi, 0, 5:6, :] * y[gi * TM:(gi + 1) * TM]

    return pl.pallas_call(
        body, grid=(b // gs, nl),
        in_specs=[pl.BlockSpec((gs, TM, f), lambda i, j: (i, j, 0)),
                  pl.BlockSpec((f, d), lambda i, j: (0, 0)),
                  pl.BlockSpec((gs, TM, d), lambda i, j: (i, j, 0)),
                  pl.BlockSpec((gs, 1, 6, d), lambda i, j: (i, 0, 0, 0))],
        out_specs=pl.BlockSpec((gs, TM, d), lambda i, j: (i, j, 0)),
        out_shape=jax.ShapeDtypeStruct((b, nl * TM, d), F32),
        compiler_params=_cparams(2), name="ffn_last",
    )(act, w_down, x, modarr)


def _split_heads(q):
    lane = lax.broadcasted_iota(jnp.int32, q.shape, 1)
    zero = jnp.zeros_like(q)
    return jnp.where(lane < HEAD_DIM, q, zero), jnp.where(lane >= HEAD_DIM, q, zero)


def _flash_call(name, qt, qkv, vt, kcol, shared_kv, vrows, tk, finish, extra, out_cols, l):
    b, t, _ = qkv.shape
    c = t - l
    nl = l // TM
    n_chunks = l // tk
    n_steps = out_cols // (2 * LANES)
    n_kv = 1 if shared_kv else 2
    ns = 4
    nb = 2
    assert l % tk == 0 and n_chunks >= 2

    def body(q_ref, qn_ref, k_ref, vt_ref, *rest):
        extra_refs = rest[:-8]
        o_ref, s_ref, p_ref, sc_ref, pc_ref, acc_ref, m_ref, a_ref = rest[-8:]
        tile = pl.program_id(2)
        is_lat = tile < nl
        row = lax.broadcasted_iota(jnp.int32, (LANES, TM), 0)

        def streams(ref):
            out = []
            for blk in range(2):
                qb = ref[0, blk * LANES:(blk + 1) * LANES, :]
                out += [jnp.where(row < HEAD_DIM, qb, jnp.zeros_like(qb)),
                        jnp.where(row >= HEAD_DIM, qb, jnp.zeros_like(qb))]
            return out

        kv_of = lambda s: 0 if shared_kv else s // 2

        def scores(off, size, dst, qs):
            kcs = [k_ref[0, pl.ds(off, size), j * LANES:(j + 1) * LANES] for j in range(n_kv)]
            for s in range(ns):
                dst(s)[...] = jnp.dot(kcs[kv_of(s)], qs[s], preferred_element_type=F32)

        def softmax(src, dst, slot):
            for s in range(ns):
                st = src(s)[...]
                m = m_ref[s]
                mnew = jnp.maximum(m, jnp.max(st, axis=0, keepdims=True))
                a_ref[slot, s] = jnp.exp2(m - mnew)
                dst(s)[...] = jnp.exp2((st - mnew).astype(BF16))
                m_ref[s] = mnew

        def values(src, slot, off, size):
            vcs = [vt_ref[0, j, :, pl.ds(off, size)] for j in range(n_kv)]
            for s in range(ns):
                acc_ref[s] = a_ref[slot, s] * acc_ref[s] + jnp.dot(vcs[kv_of(s)], src(s)[...],
                                                                  preferred_element_type=F32)

        s_ctx, p_ctx = (lambda s: sc_ref.at[s]), (lambda s: pc_ref.at[s])
        s_buf = [(lambda s, i=i: s_ref.at[i, s]) for i in range(nb)]
        p_buf = [(lambda s, i=i: p_ref.at[i, s]) for i in range(nb)]

        def start():
            acc_ref[...] = jnp.zeros(acc_ref.shape, F32)
            m_ref[...] = jnp.full(m_ref.shape, NEG, F32)

        def finalize():
            outs = [acc_ref[s, 0:vrows, :] / acc_ref[s, vrows:vrows + 1, :] for s in range(ns)]
            o_ref[0] = finish(outs, *extra_refs)

        @pl.when(tile == 0)
        def _():
            qs = streams(q_ref)
            scores(l, c, s_ctx, qs)
            scores(0, tk, s_buf[0], qs)

        @pl.when(is_lat)
        def _():
            qs = streams(q_ref)
            qs_next = streams(qn_ref)
            start()
            softmax(s_ctx, p_ctx, nb)
            values(p_ctx, nb, l, c)
            scores(tk, tk, s_buf[1], qs)
            softmax(s_buf[0], p_buf[0], 0)

            for k in range(2, n_chunks):
                values(p_buf[(k - 2) % nb], (k - 2) % nb, (k - 2) * tk, tk)
                scores(k * tk, tk, s_buf[k % nb], qs)
                softmax(s_buf[(k - 1) % nb], p_buf[(k - 1) % nb], (k - 1) % nb)
            k = n_chunks
            values(p_buf[(k - 2) % nb], (k - 2) % nb, (k - 2) * tk, tk)
            scores(l, c, s_ctx, qs_next)
            softmax(s_buf[(k - 1) % nb], p_buf[(k - 1) % nb], (k - 1) % nb)
            scores(0, tk, s_buf[0], qs_next)
            values(p_buf[(k - 1) % nb], (k - 1) % nb, (k - 1) * tk, tk)
            finalize()

        @pl.when(jnp.logical_not(is_lat))
        def _():
            start()
            softmax(s_ctx, p_ctx, nb)
            values(p_ctx, nb, l, c)
            finalize()

    kw = n_kv * LANES
    extra_specs = [pl.BlockSpec(e.shape, lambda bi, h, i: (0,) * e.ndim) for e in extra]
    return pl.pallas_call(
        body, grid=(b, n_steps, t // TM),
        in_specs=[pl.BlockSpec((1, 2 * LANES, TM), lambda bi, h, i: (bi, h, i)),
                  pl.BlockSpec((1, 2 * LANES, TM), lambda bi, h, i: (bi, h, jnp.minimum(i + 1, nl))),
                  pl.BlockSpec((1, t, kw), lambda bi, h, i: (bi, 0, kcol + h)),
                  pl.BlockSpec((1, n_kv, vrows + ONES_ROWS, t), lambda bi, h, i: (bi, h, 0, 0))]
        + extra_specs,
        out_specs=pl.BlockSpec((1, TM, 2 * LANES), lambda bi, h, i: (bi, i, h)),
        out_shape=jax.ShapeDtypeStruct((b, t, out_cols), BF16),
        scratch_shapes=[pltpu.VMEM((nb, ns, tk, TM), F32), pltpu.VMEM((nb, ns, tk, TM), BF16),
                        pltpu.VMEM((ns, c, TM), F32), pltpu.VMEM((ns, c, TM), BF16),
                        pltpu.VMEM((ns, vrows + ONES_ROWS, TM), F32), pltpu.VMEM((ns, 1, TM), F32),
                        pltpu.VMEM((nb + 1, ns, 1, TM), F32)],
        compiler_params=_cparams(3), name=name,
    )(qt, qt, qkv, vt, *extra)


def _gqa_finish(outs):
    return jnp.concatenate(outs, axis=0).T.astype(BF16)


def _make_da_finish(lam_init):
    def finish(outs, lam_ref, subln_ref):
        lv = lam_ref[...]
        lam = (jnp.exp(jnp.sum(lv[0:1] * lv[1:2], keepdims=True))
               - jnp.exp(jnp.sum(lv[2:3] * lv[3:4], keepdims=True)) + lam_init)
        heads = []
        for hd in range(len(outs) // 2):
            o = (outs[2 * hd] - lam * outs[2 * hd + 1]).T
            o = o * lax.rsqrt(jnp.mean(o * o, axis=-1, keepdims=True) + EPS)
            heads.append((o * subln_ref[...] * (1.0 - lam_init)).astype(BF16))
        return jnp.concatenate(heads, axis=1)
    return finish


def _na_call(qkv, vt, bias, l):
    b, t, _ = qkv.shape
    c = t - l
    nl = l // TM
    rows = l // GRID_W
    win = NA_WROWS * GRID_W
    n_pairs = vt.shape[1] // 2

    nt = t // TM

    def window(tile):
        return pl.multiple_of(jnp.clip(tile * NA_QROWS - NA_WIN_R // 2, 0, rows - NA_WROWS) * GRID_W, 256)

    hp = 1

    def body(q0_ref, qn_ref, k_ref, vt_ref, b0_ref, bn_ref, o_ref, sw_ref, sc_ref):
        g = pl.program_id(2)

        def scores(q_ref, b_ref, tile, buf, with_win):
            for blk in range(hp):
                cols = slice(blk * LANES, (blk + 1) * LANES)
                qs = _split_heads(q_ref[0, :, cols])
                if with_win:
                    kw = k_ref[0, pl.ds(window(tile), win), cols]
                    for half in range(2):
                        sw_ref[buf, 2 * blk + half] = _dot_nt(kw, qs[half]) + b_ref[blk, half]
                kc = k_ref[0, l:l + c, cols]
                for half in range(2):
                    sc_ref[buf, 2 * blk + half] = _dot_nt(kc, qs[half])

        def finish(tile, buf, with_win):
            outs = []
            for s in range(2 * hp):
                parts = [(sc_ref[buf, s], vt_ref[0, s, :, l:l + c])]
                if with_win:
                    parts.append((sw_ref[buf, s], vt_ref[0, s, :, pl.ds(window(tile), win)]))
                m = functools.reduce(jnp.maximum, [jnp.max(sc, axis=0, keepdims=True) for sc, _ in parts])
                acc = sum(jnp.dot(v, jnp.exp2((sc - m).astype(BF16)), preferred_element_type=F32)
                          for sc, v in parts)
                outs.append(acc[:HEAD_DIM] / acc[HEAD_DIM:HEAD_DIM + 1])
            o_ref[0] = jnp.concatenate(outs, axis=0).T.astype(BF16)

        cur, nxt = g % 2, (g + 1) % 2

        @pl.when(g == 0)
        def _():
            scores(q0_ref, b0_ref, 0, 0, True)

        @pl.when(g + 1 < nl)
        def _():
            scores(qn_ref, bn_ref, g + 1, nxt, True)
            finish(g, cur, True)

        @pl.when(g + 1 == nl)
        def _():
            scores(qn_ref, bn_ref, g + 1, nxt, False)
            finish(g, cur, True)

        @pl.when(g == nl)
        def _():
            finish(g, cur, False)

    n_steps = n_pairs // hp
    wq = hp * LANES

    def bias_next(bi, h, g):
        tile = g + 1
        case = jnp.where(tile >= nl - 1, 2, 1)
        return (case * n_steps + h, 0, 0, 0)

    return pl.pallas_call(
        body, grid=(b, n_steps, nt),
        in_specs=[pl.BlockSpec((1, TM, wq), lambda bi, h, g: (bi, 0, h)),
                  pl.BlockSpec((1, TM, wq), lambda bi, h, g: (bi, jnp.minimum(g + 1, nt - 1), h)),
                  pl.BlockSpec((1, t, wq), lambda bi, h, g: (bi, 0, n_steps + h)),
                  pl.BlockSpec((1, 2 * hp, HEAD_DIM + ONES_ROWS, t), lambda bi, h, g: (bi, h, 0, 0)),
                  pl.BlockSpec((hp, 2, win, TM), lambda bi, h, g: (h, 0, 0, 0)),
                  pl.BlockSpec((hp, 2, win, TM), bias_next)],
        out_specs=pl.BlockSpec((1, TM, wq), lambda bi, h, g: (bi, g, h)),
        out_shape=jax.ShapeDtypeStruct((b, t, n_pairs * LANES), BF16),
        scratch_shapes=[pltpu.VMEM((2, 2 * hp, win, TM), F32), pltpu.VMEM((2, 2 * hp, c, TM), F32)],
        compiler_params=_cparams(3), name="na_attn",
    )(qkv, qkv, qkv, vt, bias, bias)


def _na_bias_table(rpb):
    h, n_ro, n_co = rpb.shape
    kr, qr = np.arange(NA_WROWS), np.arange(NA_QROWS)
    kc, qc = np.arange(GRID_W), np.arange(GRID_W)
    cs = np.clip(qc - NA_WIN_C // 2, 0, GRID_W - NA_WIN_C)
    col_ok = (kc[:, None] >= cs[None, :]) & (kc[:, None] < cs[None, :] + NA_WIN_C)
    co = kc[:, None] - qc[None, :] + NA_WIN_C - 1
    col_sel = (co[None] == np.arange(n_co)[:, None, None]).astype(np.float32)
    tables = []
    for d, rel in ((0, np.zeros_like(qr)), (NA_WIN_R // 2, qr), (NA_WIN_R, np.full_like(qr, NA_WIN_R // 2))):
        row_ok = (kr[:, None] >= rel[None, :]) & (kr[:, None] < rel[None, :] + NA_WIN_R)
        ro = kr[:, None] - qr[None, :] - d + NA_WIN_R - 1
        row_sel = (ro[:, :, None] == np.arange(n_ro)[None, None, :]).astype(np.float32)
        tb = jnp.einsum('kqa,hab,bcd->hkcqd', jnp.asarray(row_sel), rpb.astype(F32) * LOG2E,
                        jnp.asarray(col_sel), precision=lax.Precision.HIGHEST)
        ok = row_ok[:, None, :, None] & col_ok[None, :, None, :]
        tables.append(jnp.where(jnp.asarray(ok)[None], tb, NEG))
    tbl = jnp.stack(tables, axis=0)
    return tbl.reshape(3 * (h // 2), 2, NA_WROWS * GRID_W, TM).astype(F32)


def _hy_filter_call(length, w1, b1, w2, b2, w3, b3, w4, freq):
    order = w2.shape[0]
    w = w4.shape[1] // 2
    tl = min(length, 512)
    hi = lax.Precision.HIGHEST
    t = np.linspace(0.0, 1.0, length, dtype=np.float64)[:, None]
    bands = (HY_EMB_DIM - 1) // 2
    ang = 2.0 * math.pi * np.arange(length, dtype=np.float64)[:, None] / length
    fq = np.linspace(1e-4, bands - 1, bands, dtype=np.float64)[None, :]
    emb = np.concatenate([t, np.cos(fq * ang), -np.sin(fq * ang)], axis=-1).astype(np.float32)
    emb = np.pad(emb, ((0, 0), (0, LANES - HY_EMB_DIM)))
    max_decay = math.log(HY_DECAY_TARGET) / HY_FAST_DECAY
    min_decay = math.log(HY_DECAY_TARGET) / HY_SLOW_DECAY
    deltas = np.linspace(min_decay, max_decay, w, dtype=np.float64)
    decay = np.exp(-t * np.abs(deltas)[None, :]).astype(np.float32)
    decay2 = np.concatenate([decay, decay], axis=1)
    w1p = jnp.pad(w1, ((0, LANES - HY_EMB_DIM), (0, 0)))

    def body(e_ref, d_ref, w1_ref, b1_ref, w2_ref, b2_ref, w3_ref, b3_ref, w4_ref, f_ref,
             h_ref, s_ref):
        i = pl.program_id(0)
        fr = f_ref[...]
        hdn = jnp.sin(fr * (jnp.dot(e_ref[...], w1_ref[...], precision=hi,
                                    preferred_element_type=F32) + b1_ref[...]))
        hdn = jnp.sin(fr * (jnp.dot(hdn, w2_ref[...], precision=hi,
                                    preferred_element_type=F32) + b2_ref[...]))
        hdn = jnp.sin(fr * (jnp.dot(hdn, w3_ref[...], precision=hi,
                                    preferred_element_type=F32) + b3_ref[...]))
        taps = jnp.dot(hdn, w4_ref[...], precision=hi, preferred_element_type=F32) * d_ref[...]
        row = lax.broadcasted_iota(jnp.int32, taps.shape, 0) + i * tl
        col = lax.broadcasted_iota(jnp.int32, taps.shape, 1)
        taps = jnp.where((row == 0) & (col >= w), 0.0, taps)
        h_ref[...] = taps

        @pl.when(i == 0)
        def _():
            s_ref[...] = jnp.zeros(s_ref.shape, F32)
        s_ref[...] += jnp.sum(jnp.abs(taps), axis=0, keepdims=True)

    full = lambda a: pl.BlockSpec(a.shape, lambda i: (0,) * a.ndim)
    ops = (w1p, b1.reshape(1, order), w2, b2.reshape(1, order), w3, b3.reshape(1, order), w4,
           freq.reshape(1, order))
    return pl.pallas_call(
        body, grid=(length // tl,),
        in_specs=[pl.BlockSpec((tl, LANES), lambda i: (i, 0)),
                  pl.BlockSpec((tl, 2 * w), lambda i: (i, 0))] + [full(a) for a in ops],
        out_specs=[pl.BlockSpec((tl, 2 * w), lambda i: (i, 0)),
                   pl.BlockSpec((1, 2 * w), lambda i: (0, 0))],
        out_shape=[jax.ShapeDtypeStruct((length, 2 * w), F32), jax.ShapeDtypeStruct((1, 2 * w), F32)],
        compiler_params=_cparams(1), name="hy_filter",
    )(jnp.asarray(emb), jnp.asarray(decay2), *ops)


def _dft_tables(l1, fb):
    n1 = 2 * l1
    n = n1 * LANES
    nf = -(-(n1 // 2 + 1) // fb) * fb
    f1 = np.arange(nf, dtype=np.float64)
    wgt = np.where((f1 == 0) | (f1 == n1 // 2), 1.0, np.where(f1 < n1 // 2, 2.0, 0.0))[None, :]
    t1 = np.arange(l1, dtype=np.float64)
    th1 = 2.0 * np.pi * np.outer(f1, t1) / n1
    fwd1 = np.concatenate([np.cos(th1), -np.sin(th1)], axis=0)
    inv1 = np.concatenate([wgt * np.cos(th1).T, -wgt * np.sin(th1).T], axis=1) / n
    f2 = np.arange(LANES, dtype=np.float64)
    t2 = np.arange(LANES, dtype=np.float64)
    fr = f1[:, None, None] + n1 * f2[None, :, None]
    th2 = 2.0 * np.pi * fr * t2[None, None, :] / n
    gr, gi = np.cos(th2), -np.sin(th2)
    gb = np.concatenate([np.concatenate([gr, -gi], axis=2),
                         np.concatenate([gi, gr], axis=2)], axis=1)
    hb = np.transpose(gb, (0, 2, 1))
    as_bf = lambda a: jnp.asarray(a.astype(np.float32)).astype(BF16)
    return as_bf(fwd1), as_bf(inv1), as_bf(gb), as_bf(hb)


def _hy_stage1_call(xv, fwd1):
    bx, l1, cols = xv.shape
    n2 = fwd1.shape[0]
    tn = min(cols, 4096)

    def body(f_ref, x_ref, o_ref):
        o_ref[0] = jnp.dot(f_ref[...], x_ref[0], preferred_element_type=F32).astype(BF16)

    return pl.pallas_call(
        body, grid=(bx, cols // tn),
        in_specs=[pl.BlockSpec((n2, l1), lambda i, j: (0, 0)),
                  pl.BlockSpec((1, l1, tn), lambda i, j: (i, 0, j))],
        out_specs=pl.BlockSpec((1, n2, tn), lambda i, j: (i, 0, j)),
        out_shape=jax.ShapeDtypeStruct((bx, n2, cols), BF16),
        compiler_params=_cparams(2), name="hy_dft1",
    )(fwd1, xv)


def _hy_filter_spec_call(a5, gb, sums, fb):
    n1, w2 = a5.shape[2], a5.shape[4]
    w = w2 // 2

    def body(a_ref, g_ref, s_ref, o_ref):
        sv = s_ref[...]
        inv = 1.0 / (sv[:, :w] + sv[:, w:])
        for k in range(fb):
            a = jnp.concatenate([a_ref[0, 0, k], a_ref[0, 1, k]], axis=0)
            z = jnp.dot(g_ref[k], a, preferred_element_type=F32)
            o_ref[k, 0] = (z[:LANES, :w] + z[:LANES, w:]) * inv
            o_ref[k, 1] = (z[LANES:, :w] - z[LANES:, w:]) * inv

    return pl.pallas_call(
        body, grid=(n1 // fb,),
        in_specs=[pl.BlockSpec((1, 2, fb, LANES, w2), lambda i: (0, 0, i, 0, 0)),
                  pl.BlockSpec((fb, 2 * LANES, 2 * LANES), lambda i: (i, 0, 0)),
                  pl.BlockSpec((1, w2), lambda i: (0, 0))],
        out_specs=pl.BlockSpec((fb, 2, LANES, w), lambda i: (i, 0, 0, 0)),
        out_shape=jax.ShapeDtypeStruct((n1, 2, LANES, w), F32),
        compiler_params=_cparams(1), name="hy_fspec",
    )(a5, gb, sums)


def _hy_stage23_call(a5, gb, hb, kf, fb):
    b, _, n1, _, w = a5.shape

    def body(a_ref, g_ref, h_ref, k_ref, o_ref):
        for k in range(fb):
            a = jnp.concatenate([a_ref[0, 0, k], a_ref[0, 1, k]], axis=0)
            z = jnp.dot(g_ref[k], a, preferred_element_type=F32)
            zr, zi = z[:LANES], z[LANES:]
            kr, ki = k_ref[k, 0], k_ref[k, 1]
            y = jnp.concatenate([zr * kr - zi * ki, zr * ki + zi * kr], axis=0).astype(BF16)
            cc = jnp.dot(h_ref[k], y, preferred_element_type=F32)
            o_ref[0, 0, k] = cc[:LANES].astype(BF16)
            o_ref[0, 1, k] = cc[LANES:].astype(BF16)

    return pl.pallas_call(
        body, grid=(n1 // fb, b),
        in_specs=[pl.BlockSpec((1, 2, fb, LANES, w), lambda i, j: (j, 0, i, 0, 0)),
                  pl.BlockSpec((fb, 2 * LANES, 2 * LANES), lambda i, j: (i, 0, 0)),
                  pl.BlockSpec((fb, 2 * LANES, 2 * LANES), lambda i, j: (i, 0, 0)),
                  pl.BlockSpec((fb, 2, LANES, w), lambda i, j: (i, 0, 0, 0))],
        out_specs=pl.BlockSpec((1, 2, fb, LANES, w), lambda i, j: (j, 0, i, 0, 0)),
        out_shape=jax.ShapeDtypeStruct(a5.shape, BF16),
        compiler_params=_cparams(2), name="hy_dft23",
    )(a5, gb, hb, kf)


def _hy_stage4_call(cv, inv1, x0v, zv, skip_t):
    b, n2, cols = cv.shape
    l1 = inv1.shape[0]
    tn = skip_t.shape[1]

    def body(f_ref, c_ref, x0_ref, z_ref, s_ref, o_ref):
        y = jnp.dot(f_ref[...], c_ref[0], preferred_element_type=F32)
        z = z_ref[0].astype(F32)
        o_ref[0] = (x0_ref[0].astype(F32) * (y + s_ref[...] * z)).astype(BF16)

    return pl.pallas_call(
        body, grid=(b, cols // tn),
        in_specs=[pl.BlockSpec((l1, n2), lambda i, j: (0, 0)),
                  pl.BlockSpec((1, n2, tn), lambda i, j: (i, 0, j)),
                  pl.BlockSpec((1, l1, tn), lambda i, j: (i, 0, j)),
                  pl.BlockSpec((1, l1, tn), lambda i, j: (i, 0, j)),
                  pl.BlockSpec((1, tn), lambda i, j: (0, 0))],
        out_specs=pl.BlockSpec((1, l1, tn), lambda i, j: (i, 0, j)),
        out_shape=jax.ShapeDtypeStruct((b, l1, cols), BF16),
        compiler_params=_cparams(2), name="hy_dft4",
    )(inv1, cv, x0v, zv, skip_t)


def _hy_dense_call(x0, z, taps, sums, skip):
    b, c, w = z.shape
    n = 2 * c
    th = 2.0 * np.pi * np.outer(np.arange(n, dtype=np.float64), np.arange(c, dtype=np.float64)) / n
    fwd = jnp.asarray(np.concatenate([np.cos(th), -np.sin(th)], axis=0).astype(np.float32)).astype(BF16)
    inv = jnp.asarray((np.concatenate([np.cos(th).T, -np.sin(th).T], axis=1) / n)
                      .astype(np.float32)).astype(BF16)

    def body(f_ref, i_ref, x0_ref, z_ref, t_ref, s_ref, k_ref, o_ref):
        sv = s_ref[...]
        nrm = 1.0 / (sv[:, :w] + sv[:, w:])
        tf = jnp.dot(f_ref[...], t_ref[...].astype(BF16), preferred_element_type=F32)
        kr = (tf[:n, :w] + tf[:n, w:]) * nrm
        ki = (tf[n:, :w] - tf[n:, w:]) * nrm
        zf = jnp.dot(f_ref[...], z_ref[0], preferred_element_type=F32)
        zr, zi = zf[:n], zf[n:]
        y = jnp.concatenate([zr * kr - zi * ki, zr * ki + zi * kr], axis=0).astype(BF16)
        yt = jnp.dot(i_ref[...], y, preferred_element_type=F32)
        o_ref[0] = (x0_ref[0].astype(F32) * (yt + k_ref[...] * z_ref[0].astype(F32))).astype(BF16)

    return pl.pallas_call(
        body, grid=(b,),
        in_specs=[pl.BlockSpec((2 * n, c), lambda i: (0, 0)),
                  pl.BlockSpec((c, 2 * n), lambda i: (0, 0)),
                  pl.BlockSpec((1, c, w), lambda i: (i, 0, 0)),
                  pl.BlockSpec((1, c, w), lambda i: (i, 0, 0)),
                  pl.BlockSpec((c, 2 * w), lambda i: (0, 0)),
                  pl.BlockSpec((1, 2 * w), lambda i: (0, 0)),
                  pl.BlockSpec((1, w), lambda i: (0, 0))],
        out_specs=pl.BlockSpec((1, c, w), lambda i: (i, 0, 0)),
        out_shape=jax.ShapeDtypeStruct((b, c, w), BF16),
        compiler_params=_cparams(1), name="hy_dense",
    )(fwd, inv, x0, z, taps, sums, skip.reshape(1, w))


def _hyena_long(x0, z, fparams, skip):
    b, l, w = z.shape
    l1 = l // LANES
    n1 = 2 * l1
    fb = min(8, n1)
    fwd1, inv1, gb, hb = _dft_tables(l1, fb)
    nf = gb.shape[0]
    taps, sums = _hy_filter_call(l, *fparams)
    ta = _hy_stage1_call(taps.astype(BF16).reshape(1, l1, LANES * 2 * w), fwd1)
    kf = _hy_filter_spec_call(ta.reshape(1, 2, nf, LANES, 2 * w), gb, sums, fb)
    za = _hy_stage1_call(z.reshape(b, l1, LANES * w), fwd1)
    cc = _hy_stage23_call(za.reshape(b, 2, nf, LANES, w), gb, hb, kf, fb)
    tn = 8 * w
    skip_t = jnp.tile(skip.reshape(1, w), (1, tn // w))
    y = _hy_stage4_call(cc.reshape(b, 2 * nf, LANES * w), inv1, x0.reshape(b, l1, LANES * w),
                        z.reshape(b, l1, LANES * w), skip_t)
    return y.reshape(b, l, w)


def _hyena_short(x0, z, fparams, skip):
    taps, sums = _hy_filter_call(z.shape[1], *fparams)
    return _hy_dense_call(x0, z, taps, sums, skip)


def _rope_tables(l, c):
    t = np.arange(l)
    row = (t // GRID_W).astype(np.float64)
    col = (t % GRID_W).astype(np.float64)
    n_pairs = HEAD_DIM // 4
    inv_freq = ROPE_THETA ** (-np.arange(n_pairs, dtype=np.float64) / n_pairs)
    ang = np.concatenate([row[:, None] * inv_freq, col[:, None] * inv_freq], axis=-1)
    cos = np.repeat(np.cos(ang), 2, axis=1)
    sin = np.repeat(np.sin(ang), 2, axis=1)
    even = (np.arange(HEAD_DIM) % 2 == 0)[None, :]
    se = np.where(even, -sin, 0.0)
    so = np.where(even, 0.0, sin)
    pad = lambda a, v: np.concatenate([a, np.full((c, HEAD_DIM), v)], axis=0)
    two = lambda a: jnp.asarray(np.concatenate([a, a], axis=1).astype(np.float32))
    return two(pad(cos, 1.0)), two(pad(se, 0.0)), two(pad(so, 0.0))


def _block_diag_ones():
    i = np.arange(2 * LANES)
    return jnp.asarray((i[:, None] // HEAD_DIM == i[None, :] // HEAD_DIM).astype(np.float32)).astype(BF16)


def _gain_rows(gains, scales):
    rows = [jnp.tile(g.astype(F32) * s, 2 * LANES // HEAD_DIM) for g, s in zip(gains, scales)]
    rows += [jnp.zeros((2 * LANES,), F32)] * (8 - len(rows))
    return jnp.stack(rows, axis=0)


def kernel(x, c, ctx, c_ctx, w_ada, b_ada, w_up, ffn_conv_w, ffn_conv_b, w_down, w_in_e, w_out_e, na_q_gain, na_k_gain, na_rpb, da_q_gain, da_k_gain, da_lambda_q1, da_lambda_k1, da_lambda_q2, da_lambda_k2, da_subln_gain, w_in_o, w_out_o, gqa_q_gain, gqa_k_gain, hy_conv_w, hy_conv_b, hy_w1, hy_b1, hy_w2, hy_b2, hy_w3, hy_b3, hy_w4, hy_freq, hy_skip):
    b, l, d = x.shape
    cl = ctx.shape[1]
    t = l + cl
    depth = w_ada.shape[0]
    f = w_down.shape[1]
    assert cl == TM and (l // GRID_W) >= NA_WROWS + NA_QROWS
    assert w_in_e.shape[2] == 3072 and w_in_o.shape[2] == 2304 and d % LANES == 0
    scale = HEAD_DIM ** -0.5 * LOG2E

    rows = -(-(b + 1) // 8) * 8
    cs = jnp.zeros((rows, d), F32).at[:b].set(c).at[b].set(c_ctx)
    mods = _ada_call(cs, w_ada, b_ada)
    modarrs = []
    for layer in range(depth):
        lat = mods[layer, :b].reshape(b, 1, 6, d)
        cx = jnp.broadcast_to(mods[layer, b].reshape(1, 1, 6, d), (b, 1, 6, d))
        modarrs.append(jnp.concatenate([lat, cx], axis=1))

    ropes = _rope_tables(l, cl)
    bd = _block_diag_ones()
    x_all, h = _mod0_call(x, ctx, modarrs[0])

    for layer in range(depth):
        i = layer // 2
        if layer % 2 == 0:
            lam_init = 0.8 - 0.6 * math.exp(-0.3 * layer)
            gains = _gain_rows((na_q_gain[i], na_k_gain[i], da_q_gain[i], da_k_gain[i]),
                               (scale, 1.0, scale, 1.0))
            qkv, vat, qbt, vbt = _inproj_call(
                _inproj_even_body, "inproj_even", h, w_in_e[i].astype(BF16), bd, gains, ropes, (1536,),
                ((b, 8, HEAD_DIM + ONES_ROWS, t), (b, 512, t), (b, 4, LANES + ONES_ROWS, t)))
            bias = _na_bias_table(na_rpb[i])
            ya = _na_call(qkv, vat, bias, l)
            lamv = jnp.stack([da_lambda_q1[i], da_lambda_k1[i], da_lambda_q2[i], da_lambda_k2[i]]).astype(F32)
            yb = _flash_call("da_attn", qbt, qkv, vbt, 1024 // 256, False, LANES, TK_DA,
                             _make_da_finish(lam_init),
                             (lamv, da_subln_gain[i].reshape(1, LANES).astype(F32)), 512, l)
            w_out = w_out_e[i].astype(BF16)
        else:
            gains = _gain_rows((gqa_q_gain[i], gqa_k_gain[i]), (scale, 1.0))
            kd, x0, z, qt, vt = _inproj_call(
                functools.partial(_inproj_odd_body, nl=l // TM, nt=t // TM), "inproj_odd", h,
                w_in_o[i].astype(BF16), bd, gains, ropes, (2 * LANES, 512, 512),
                ((b, 512, t), (b, 2, HEAD_DIM + ONES_ROWS, t)),
                extra=(hy_conv_w[i], hy_conv_b[i].reshape(1, -1)), halo=True)
            ya = _flash_call("gqa_attn", qt, kd, vt, 0, True, HEAD_DIM, TK_GQA,
                             _gqa_finish, (), 512, l)
            fparams = (hy_w1[i], hy_b1[i], hy_w2[i], hy_b2[i], hy_w3[i], hy_b3[i], hy_w4[i], hy_freq[i])
            yd_l = _hyena_long(x0[:, :l], z[:, :l], fparams, hy_skip[i])
            if layer < depth - 1:
                yd_c = _hyena_short(x0[:, l:], z[:, l:], fparams, hy_skip[i])
            else:
                yd_c = jnp.zeros((b, cl, x0.shape[2]), BF16)
            yb = jnp.concatenate([yd_l, yd_c], axis=1)
            w_out = w_out_o[i].astype(BF16)
        ka = ya.shape[2]
        x_all, act = _mix_ffn_up_call(ya, yb, w_out[:ka], w_out[ka:], x_all, modarrs[layer],
                                      w_up[layer].astype(BF16), ffn_conv_w[layer], ffn_conv_b[layer])
        if layer == depth - 1:
            return _ffn_last_call(act, w_down[layer].astype(BF16), x_all, modarrs[layer])
        x_all, h = _ffn_down_call(act, w_down[layer].astype(BF16), x_all, modarrs[layer],
                                  modarrs[layer + 1])
```

```python
import functools
import math

import numpy as np
import jax
import jax.numpy as jnp
from jax import lax
from jax.experimental import pallas as pl
from jax.experimental.pallas import tpu as pltpu

F32 = jnp.float32
BF16 = jnp.bfloat16

HEAD_DIM = 64
GRID_W = 64
ROPE_THETA = 10000.0
EPS = 1e-6
NA_WIN_R = 8
NA_WIN_C = 16
HY_EMB_DIM = 33
HY_FAST_DECAY = 0.3
HY_SLOW_DECAY = 1.5
HY_DECAY_TARGET = 1e-2

LANES = 128
TM = 256
TK_DA = 512
TK_GQA = 256
NA_QROWS = TM // GRID_W
NA_WROWS = NA_QROWS + NA_WIN_R
ONES_ROWS = 16
NEG = -1e30
LOG2E = 1.4426950408889634
VMEM_LIMIT = 56 * 1024 * 1024


def _cparams(n_axes):
    return pltpu.CompilerParams(dimension_semantics=("arbitrary",) * n_axes,
                                vmem_limit_bytes=VMEM_LIMIT)


def _modulate(x, sh, sc):
    ms = jnp.mean(x * x, axis=-1, keepdims=True)
    return x * lax.rsqrt(ms + EPS) * (1.0 + sc) + sh


def _seg_norm(y, bd, gain):
    ss = jnp.dot((y * y).astype(BF16), bd, preferred_element_type=F32)
    return y * lax.rsqrt(ss * (1.0 / HEAD_DIM) + EPS) * gain


def _rope(y, c, se, so):
    return y * c + pltpu.roll(y, LANES - 1, 1) * se + pltpu.roll(y, 1, 1) * so


def _dot_nt(a, b):
    return lax.dot_general(a, b, (((1,), (1,)), ((), ())), preferred_element_type=F32)


def _ada_body(c_ref, w_ref, b_ref, o_ref):
    c = c_ref[...]
    a = (c / (1.0 + jnp.exp(-c))).astype(BF16)
    o_ref[0] = jnp.dot(a, w_ref[0].astype(BF16), preferred_element_type=F32) + b_ref[0]


def _ada_call(cs, w_ada, b_ada):
    depth, d, n6 = w_ada.shape
    rows = cs.shape[0]
    tn = 1536
    return pl.pallas_call(
        _ada_body, grid=(depth, n6 // tn),
        in_specs=[pl.BlockSpec((rows, d), lambda l, n: (0, 0)),
                  pl.BlockSpec((1, d, tn), lambda l, n: (l, 0, n)),
                  pl.BlockSpec((1, 1, tn), lambda l, n: (l, 0, n))],
        out_specs=pl.BlockSpec((1, rows, tn), lambda l, n: (l, 0, n)),
        out_shape=jax.ShapeDtypeStruct((depth, rows, n6), F32),
        compiler_params=_cparams(2), name="ada",
    )(cs, w_ada, b_ada.reshape(depth, 1, n6))


def _mod0_call(x, ctx, modarr):
    b, l, d = x.shape
    c = ctx.shape[1]
    t = l + c
    nl = l // TM

    def body(x_ref, c_ref, m_ref, xo_ref, h_ref):
        j = pl.program_id(1)
        xv = jnp.where(j < nl, x_ref[0], c_ref[0])
        xo_ref[0] = xv
        h_ref[0] = _modulate(xv, m_ref[0, 0, 0:1, :], m_ref[0, 0, 1:2, :]).astype(BF16)

    return pl.pallas_call(
        body, grid=(b, t // TM),
        in_specs=[pl.BlockSpec((1, TM, d), lambda i, j: (i, jnp.minimum(j, nl - 1), 0)),
                  pl.BlockSpec((1, TM, d), lambda i, j: (i, 0, 0)),
                  pl.BlockSpec((1, 1, 6, d), lambda i, j: (i, j // nl, 0, 0))],
        out_specs=[pl.BlockSpec((1, TM, d), lambda i, j: (i, j, 0)),
                   pl.BlockSpec((1, TM, d), lambda i, j: (i, j, 0))],
        out_shape=[jax.ShapeDtypeStruct((b, t, d), F32), jax.ShapeDtypeStruct((b, t, d), BF16)],
        compiler_params=_cparams(2), name="mod0",
    )(x, ctx, modarr)


def _inproj_even_body(a_ref, w_ref, bd_ref, g_ref, rc_ref, rse_ref, rso_ref,
                      o_ref, vat_ref, qbt_ref, vbt_ref):
    a = a_ref[0]
    bd = bd_ref[...]
    rc, rse, rso = rc_ref[...], rse_ref[...], rso_ref[...]
    plan = ((0, False, 0), (1, False, 512), (None, False, None), (2, True, None), (3, True, 1024),
            (None, False, None))
    for seg, (gain_row, rope, out_col) in enumerate(plan):
        y = jnp.dot(a, w_ref[:, seg * 512:(seg + 1) * 512], preferred_element_type=F32)
        for half in range(2):
            yy = y[:, half * 256:(half + 1) * 256]
            if gain_row is not None:
                yy = _seg_norm(yy, bd, g_ref[gain_row:gain_row + 1, :])
            for blk in range(2):
                z = yy[:, blk * LANES:(blk + 1) * LANES]
                if rope:
                    z = _rope(z, rc, rse, rso)
                m = half * 2 + blk
                if out_col is not None:
                    col = out_col + m * LANES
                    o_ref[0, :, col:col + LANES] = z.astype(BF16)
                    continue
                zt = z.T.astype(BF16)
                if seg == 2:
                    vat_ref[0, 2 * m, 0:HEAD_DIM, :] = zt[:HEAD_DIM]
                    vat_ref[0, 2 * m + 1, 0:HEAD_DIM, :] = zt[HEAD_DIM:]
                elif seg == 3:
                    qbt_ref[0, m * LANES:(m + 1) * LANES, :] = zt
                else:
                    vbt_ref[0, m, 0:LANES, :] = zt
    ones = jnp.ones((ONES_ROWS, TM), BF16)
    for hd in range(vat_ref.shape[1]):
        vat_ref[0, hd, HEAD_DIM:HEAD_DIM + ONES_ROWS, :] = ones
    for hd in range(vbt_ref.shape[1]):
        vbt_ref[0, hd, LANES:LANES + ONES_ROWS, :] = ones


def _inproj_odd_body(a_ref, ap_ref, an_ref, w_ref, bd_ref, g_ref, rc_ref, rse_ref, rso_ref,
                     cw_ref, cb_ref, o_ref, x0_ref, z_ref, qt_ref, vt_ref, *, nl, nt):
    j = pl.program_id(1)
    a = a_ref[0]
    bd = bd_ref[...]
    rc, rse, rso = rc_ref[...], rse_ref[...], rso_ref[...]
    y = jnp.dot(a, w_ref[:, 0:512], preferred_element_type=F32)
    for half in range(2):
        yy = _seg_norm(y[:, half * 256:(half + 1) * 256], bd, g_ref[0:1, :])
        for blk in range(2):
            z = _rope(yy[:, blk * LANES:(blk + 1) * LANES], rc, rse, rso)
            m = half * 2 + blk
            qt_ref[0, m * LANES:(m + 1) * LANES, :] = z.T.astype(BF16)
    y = jnp.dot(a, w_ref[:, 512:768], preferred_element_type=F32)
    k = _seg_norm(y[:, :LANES], bd_ref[0:LANES, 0:LANES], g_ref[1:2, 0:LANES])
    k = _rope(k, rc, rse, rso)
    kr = pltpu.roll(k, HEAD_DIM, 1)
    lo = lax.broadcasted_iota(jnp.int32, k.shape, 1) < HEAD_DIM
    o_ref[0, :, 0:LANES] = jnp.where(lo, k, kr).astype(BF16)
    o_ref[0, :, LANES:2 * LANES] = jnp.where(lo, kr, k).astype(BF16)
    vt = y[:, LANES:].T.astype(BF16)
    ones = jnp.ones((ONES_ROWS, TM), BF16)
    for hd in range(2):
        vt_ref[0, hd, 0:HEAD_DIM, :] = vt[hd * HEAD_DIM:(hd + 1) * HEAD_DIM]
        vt_ref[0, hd, HEAD_DIM:HEAD_DIM + ONES_ROWS, :] = ones
    hb = ap_ref.shape[1]
    zero = jnp.zeros((hb, a.shape[1]), BF16)
    ap = jnp.where((j == 0) | (j == nl), zero, ap_ref[0])
    an = jnp.where((j == nl - 1) | (j == nt - 1), zero, an_ref[0])
    a_ext = jnp.concatenate([ap, a, an], axis=0)
    conv = []
    for seg in range(3):
        sl = slice(seg * 512, (seg + 1) * 512)
        g = jnp.dot(a_ext, w_ref[:, 768 + seg * 512:768 + (seg + 1) * 512], preferred_element_type=F32)
        conv.append(g[hb - 1:hb - 1 + TM] * cw_ref[0:1, sl] + g[hb:hb + TM] * cw_ref[1:2, sl]
                    + g[hb + 1:hb + 1 + TM] * cw_ref[2:3, sl] + cb_ref[0:1, sl])
        if seg == 0:
            x0_ref[0] = conv[0].astype(BF16)
    z_ref[0] = (conv[2] * conv[1]).astype(BF16)


def _inproj_call(body, name, h, w, bd, gains, ropes, n_outs, t_shapes, extra=(), halo=False):
    b, t, d = h.shape
    n_in = w.shape[1]
    nt = t // TM
    hb = 16
    r = TM // hb
    rc, rse, rso = ropes

    def t_spec(shape):
        nd = len(shape)
        return pl.BlockSpec((1,) + tuple(shape[1:-1]) + (TM,), lambda i, j: (i,) + (0,) * (nd - 2) + (j,))

    acts, act_specs = [h], [pl.BlockSpec((1, TM, d), lambda i, j: (i, j, 0))]
    if halo:
        acts += [h, h]
        act_specs += [pl.BlockSpec((1, hb, d), lambda i, j: (i, jnp.maximum(j * r - 1, 0), 0)),
                      pl.BlockSpec((1, hb, d), lambda i, j: (i, jnp.minimum((j + 1) * r, nt * r - 1), 0))]
    return pl.pallas_call(
        body, grid=(b, nt),
        in_specs=act_specs
        + [pl.BlockSpec((d, n_in), lambda i, j: (0, 0)),
           pl.BlockSpec(bd.shape, lambda i, j: (0, 0)),
           pl.BlockSpec(gains.shape, lambda i, j: (0, 0)),
           pl.BlockSpec((TM, LANES), lambda i, j: (j, 0)),
           pl.BlockSpec((TM, LANES), lambda i, j: (j, 0)),
           pl.BlockSpec((TM, LANES), lambda i, j: (j, 0))]
        + [pl.BlockSpec(e.shape, lambda i, j: (0,) * e.ndim) for e in extra],
        out_specs=[pl.BlockSpec((1, TM, n), lambda i, j: (i, j, 0)) for n in n_outs]
        + [t_spec(s) for s in t_shapes],
        out_shape=[jax.ShapeDtypeStruct((b, t, n), BF16) for n in n_outs]
        + [jax.ShapeDtypeStruct(s, BF16) for s in t_shapes],
        compiler_params=_cparams(2), name=name,
    )(*acts, w, bd, gains, rc, rse, rso, *extra)


def _mix_ffn_up_call(ya, yb, wa, wb, x, modarr, w_up, conv_w, conv_b):
    b, t, d = x.shape
    f = w_up.shape[1] // 2
    nl = (t - TM) // TM
    nt = t // TM
    hb = 16
    r = TM // hb
    ka, kb = ya.shape[2], yb.shape[2]
    chunks = [(c0, min(2 * LANES, f - c0)) for c0 in range(0, f, 2 * LANES)]

    def body(ya_ref, yap_ref, yan_ref, yb_ref, ybp_ref, ybn_ref, x_ref, xp_ref, xn_ref,
             wa_ref, wb_ref, m_ref, wg_ref, wv_ref, cw_ref, cb_ref, xo_ref, o_ref):
        j = pl.program_id(1)
        ext = lambda p, c, n: jnp.concatenate([p[0], c[0], n[0]], axis=0)
        y = (jnp.dot(ext(yap_ref, ya_ref, yan_ref), wa_ref[...], preferred_element_type=F32)
             + jnp.dot(ext(ybp_ref, yb_ref, ybn_ref), wb_ref[...], preferred_element_type=F32))
        x1 = ext(xp_ref, x_ref, xn_ref) + m_ref[0, 0, 2:3, :] * y
        xo_ref[0] = x1[hb:hb + TM]
        hx = _modulate(x1, m_ref[0, 0, 3:4, :], m_ref[0, 0, 4:5, :]).astype(BF16)
        zero = jnp.zeros((hb, d), BF16)
        a = hx[hb:hb + TM]
        ap = jnp.where((j == 0) | (j == nl), zero, hx[0:hb])
        an = jnp.where((j == nl - 1) | (j == nt - 1), zero, hx[hb + TM:])
        a_ext = jnp.concatenate([ap, a, an], axis=0)
        for c0, cw in chunks:
            sl = slice(c0, c0 + cw)
            g = jnp.dot(a_ext, wg_ref[:, sl], preferred_element_type=F32)
            u = (g[hb - 1:hb - 1 + TM] * cw_ref[0:1, sl] + g[hb:hb + TM] * cw_ref[1:2, sl]
                 + g[hb + 1:hb + 1 + TM] * cw_ref[2:3, sl] + cb_ref[0:1, sl])
            v = jnp.dot(a, wv_ref[:, sl], preferred_element_type=F32)
            o_ref[0, :, sl] = ((u / (1.0 + jnp.exp(-u))) * v).astype(BF16)

    main = lambda w: pl.BlockSpec((1, TM, w), lambda i, j: (i, j, 0))
    prev = lambda w: pl.BlockSpec((1, hb, w), lambda i, j: (i, jnp.maximum(j * r - 1, 0), 0))
    nxt = lambda w: pl.BlockSpec((1, hb, w), lambda i, j: (i, jnp.minimum((j + 1) * r, nt * r - 1), 0))
    full = lambda a: pl.BlockSpec(a.shape, lambda i, j: (0,) * a.ndim)
    return pl.pallas_call(
        body, grid=(b, nt),
        in_specs=[main(ka), prev(ka), nxt(ka), main(kb), prev(kb), nxt(kb), main(d), prev(d), nxt(d),
                  full(wa), full(wb), pl.BlockSpec((1, 1, 6, d), lambda i, j: (i, j // nl, 0, 0)),
                  pl.BlockSpec((d, f), lambda i, j: (0, 0)), pl.BlockSpec((d, f), lambda i, j: (0, 1)),
                  pl.BlockSpec((3, f), lambda i, j: (0, 0)), pl.BlockSpec((1, f), lambda i, j: (0, 0))],
        out_specs=[pl.BlockSpec((1, TM, d), lambda i, j: (i, j, 0)),
                   pl.BlockSpec((1, TM, f), lambda i, j: (i, j, 0))],
        out_shape=[jax.ShapeDtypeStruct((b, t, d), F32), jax.ShapeDtypeStruct((b, t, f), BF16)],
        compiler_params=_cparams(2), name="mix_ffn_up",
    )(ya, ya, ya, yb, yb, yb, x, x, x, wa, wb, modarr, w_up, w_up, conv_w, conv_b.reshape(1, f))


def _ffn_down_call(act, w_down, x, modarr, modarr_next):
    b, t, d = x.shape
    f = w_down.shape[0]
    nl = (t - TM) // TM
    nt = t // TM

    gs = 4 if b % 4 == 0 else (2 if b % 2 == 0 else 1)

    def body(a_ref, wd_ref, x_ref, m_ref, mn_ref, xo_ref, h_ref):
        a = jnp.concatenate([a_ref[gi] for gi in range(gs)], axis=0)
        y = jnp.dot(a, wd_ref[...], preferred_element_type=F32)
        for gi in range(gs):
            x2 = x_ref[gi] + m_ref[gi, 0, 5:6, :] * y[gi * TM:(gi + 1) * TM]
            xo_ref[gi] = x2
            h_ref[gi] = _modulate(x2, mn_ref[gi, 0, 0:1, :], mn_ref[gi, 0, 1:2, :]).astype(BF16)

    return pl.pallas_call(
        body, grid=(b // gs, nt),
        in_specs=[pl.BlockSpec((gs, TM, f), lambda i, j: (i, j, 0)),
                  pl.BlockSpec((f, d), lambda i, j: (0, 0)),
                  pl.BlockSpec((gs, TM, d), lambda i, j: (i, j, 0)),
                  pl.BlockSpec((gs, 1, 6, d), lambda i, j: (i, j // nl, 0, 0)),
                  pl.BlockSpec((gs, 1, 6, d), lambda i, j: (i, j // nl, 0, 0))],
        out_specs=[pl.BlockSpec((gs, TM, d), lambda i, j: (i, j, 0)),
                   pl.BlockSpec((gs, TM, d), lambda i, j: (i, j, 0))],
        out_shape=[jax.ShapeDtypeStruct((b, t, d), F32), jax.ShapeDtypeStruct((b, t, d), BF16)],
        compiler_params=_cparams(2), name="ffn_down",
    )(act, w_down, x, modarr, modarr_next)


def _ffn_last_call(act, w_down, x, modarr):
    b, t, d = x.shape
    f = w_down.shape[0]
    nl = (t - TM) // TM

    gs = 4 if b % 4 == 0 else (2 if b % 2 == 0 else 1)

    def body(a_ref, wd_ref, x_ref, m_ref, xo_ref):
        a = jnp.concatenate([a_ref[gi] for gi in range(gs)], axis=0)
        y = jnp.dot(a, wd_ref[...], preferred_element_type=F32)
        for gi in range(gs):
            xo_ref[gi] = x_ref[gi] + m_ref[gi, 0, 5:6, :] * y[gi * TM:(gi + 1) * TM]

    return pl.pallas_call(
        body, grid=(b // gs, nl),
        in_specs=[pl.BlockSpec((gs, TM, f), lambda i, j: (i, j, 0)),
                  pl.BlockSpec((f, d), lambda i, j: (0, 0)),
                  pl.BlockSpec((gs, TM, d), lambda i, j: (i, j, 0)),
                  pl.BlockSpec((gs, 1, 6, d), lambda i, j: (i, 0, 0, 0))],
        out_specs=pl.BlockSpec((gs, TM, d), lambda i, j: (i, j, 0)),
        out_shape=jax.ShapeDtypeStruct((b, nl * TM, d), F32),
        compiler_params=_cparams(2), name="ffn_last",
    )(act, w_down, x, modarr)


def _split_heads(q):
    lane = lax.broadcasted_iota(jnp.int32, q.shape, 1)
    zero = jnp.zeros_like(q)
    return jnp.where(lane < HEAD_DIM, q, zero), jnp.where(lane >= HEAD_DIM, q, zero)


def _flash_call(name, qt, qkv, vt, kcol, shared_kv, vrows, tk, finish, extra, out_cols, l):
    b, t, _ = qkv.shape
    c = t - l
    nl = l // TM
    n_chunks = l // tk
    n_steps = out_cols // (2 * LANES)
    n_kv = 1 if shared_kv else 2
    ns = 4
    nb = 2
    assert l % tk == 0 and n_chunks >= 2

    def body(q_ref, qn_ref, k_ref, vt_ref, *rest):
        extra_refs = rest[:-8]
        o_ref, s_ref, p_ref, sc_ref, pc_ref, acc_ref, m_ref, a_ref = rest[-8:]
        tile = pl.program_id(2)
        is_lat = tile < nl
        row = lax.broadcasted_iota(jnp.int32, (LANES, TM), 0)

        def streams(ref):
            out = []
            for blk in range(2):
                qb = ref[0, blk * LANES:(blk + 1) * LANES, :]
                out += [jnp.where(row < HEAD_DIM, qb, jnp.zeros_like(qb)),
                        jnp.where(row >= HEAD_DIM, qb, jnp.zeros_like(qb))]
            return out

        kv_of = lambda s: 0 if shared_kv else s // 2

        def scores(off, size, dst, qs):
            kcs = [k_ref[0, pl.ds(off, size), j * LANES:(j + 1) * LANES] for j in range(n_kv)]
            for s in range(ns):
                dst(s)[...] = jnp.dot(kcs[kv_of(s)], qs[s], preferred_element_type=F32)

        def softmax(src, dst, slot):
            for s in range(ns):
                st = src(s)[...]
                m = m_ref[s]
                mnew = jnp.maximum(m, jnp.max(st, axis=0, keepdims=True))
                a_ref[slot, s] = jnp.exp2(m - mnew)
                dst(s)[...] = jnp.exp2((st - mnew).astype(BF16))
                m_ref[s] = mnew

        def values(src, slot, off, size):
            vcs = [vt_ref[0, j, :, pl.ds(off, size)] for j in range(n_kv)]
            for s in range(ns):
                acc_ref[s] = a_ref[slot, s] * acc_ref[s] + jnp.dot(vcs[kv_of(s)], src(s)[...],
                                                                  preferred_element_type=F32)

        s_ctx, p_ctx = (lambda s: sc_ref.at[s]), (lambda s: pc_ref.at[s])
        s_buf = [(lambda s, i=i: s_ref.at[i, s]) for i in range(nb)]
        p_buf = [(lambda s, i=i: p_ref.at[i, s]) for i in range(nb)]

        def start():
            acc_ref[...] = jnp.zeros(acc_ref.shape, F32)
            m_ref[...] = jnp.full(m_ref.shape, NEG, F32)

        def finalize():
            outs = [acc_ref[s, 0:vrows, :] / acc_ref[s, vrows:vrows + 1, :] for s in range(ns)]
            o_ref[0] = finish(outs, *extra_refs)

        @pl.when(tile == 0)
        def _():
            qs = streams(q_ref)
            scores(l, c, s_ctx, qs)
            scores(0, tk, s_buf[0], qs)

        @pl.when(is_lat)
        def _():
            qs = streams(q_ref)
            qs_next = streams(qn_ref)
            start()
            softmax(s_ctx, p_ctx, nb)
            values(p_ctx, nb, l, c)
            scores(tk, tk, s_buf[1], qs)
            softmax(s_buf[0], p_buf[0], 0)

            for k in range(2, n_chunks):
                values(p_buf[(k - 2) % nb], (k - 2) % nb, (k - 2) * tk, tk)
                scores(k * tk, tk, s_buf[k % nb], qs)
                softmax(s_buf[(k - 1) % nb], p_buf[(k - 1) % nb], (k - 1) % nb)
            k = n_chunks
            values(p_buf[(k - 2) % nb], (k - 2) % nb, (k - 2) * tk, tk)
            scores(l, c, s_ctx, qs_next)
            softmax(s_buf[(k - 1) % nb], p_buf[(k - 1) % nb], (k - 1) % nb)
            scores(0, tk, s_buf[0], qs_next)
            values(p_buf[(k - 1) % nb], (k - 1) % nb, (k - 1) * tk, tk)
            finalize()

        @pl.when(jnp.logical_not(is_lat))
        def _():
            start()
            softmax(s_ctx, p_ctx, nb)
            values(p_ctx, nb, l, c)
            finalize()

    kw = n_kv * LANES
    extra_specs = [pl.BlockSpec(e.shape, lambda bi, h, i: (0,) * e.ndim) for e in extra]
    return pl.pallas_call(
        body, grid=(b, n_steps, t // TM),
        in_specs=[pl.BlockSpec((1, 2 * LANES, TM), lambda bi, h, i: (bi, h, i)),
                  pl.BlockSpec((1, 2 * LANES, TM), lambda bi, h, i: (bi, h, jnp.minimum(i + 1, nl))),
                  pl.BlockSpec((1, t, kw), lambda bi, h, i: (bi, 0, kcol + h)),
                  pl.BlockSpec((1, n_kv, vrows + ONES_ROWS, t), lambda bi, h, i: (bi, h, 0, 0))]
        + extra_specs,
        out_specs=pl.BlockSpec((1, TM, 2 * LANES), lambda bi, h, i: (bi, i, h)),
        out_shape=jax.ShapeDtypeStruct((b, t, out_cols), BF16),
        scratch_shapes=[pltpu.VMEM((nb, ns, tk, TM), F32), pltpu.VMEM((nb, ns, tk, TM), BF16),
                        pltpu.VMEM((ns, c, TM), F32), pltpu.VMEM((ns, c, TM), BF16),
                        pltpu.VMEM((ns, vrows + ONES_ROWS, TM), F32), pltpu.VMEM((ns, 1, TM), F32),
                        pltpu.VMEM((nb + 1, ns, 1, TM), F32)],
        compiler_params=_cparams(3), name=name,
    )(qt, qt, qkv, vt, *extra)


def _gqa_finish(outs):
    return jnp.concatenate(outs, axis=0).T.astype(BF16)


def _make_da_finish(lam_init):
    def finish(outs, lam_ref, subln_ref):
        lv = lam_ref[...]
        lam = (jnp.exp(jnp.sum(lv[0:1] * lv[1:2], keepdims=True))
               - jnp.exp(jnp.sum(lv[2:3] * lv[3:4], keepdims=True)) + lam_init)
        heads = []
        for hd in range(len(outs) // 2):
            o = (outs[2 * hd] - lam * outs[2 * hd + 1]).T
            o = o * lax.rsqrt(jnp.mean(o * o, axis=-1, keepdims=True) + EPS)
            heads.append((o * subln_ref[...] * (1.0 - lam_init)).astype(BF16))
        return jnp.concatenate(heads, axis=1)
    return finish


def _na_call(qkv, vt, bias, l):
    b, t, _ = qkv.shape
    c = t - l
    nl = l // TM
    rows = l // GRID_W
    win = NA_WROWS * GRID_W
    n_pairs = vt.shape[1] // 2

    nt = t // TM

    def window(tile):
        return pl.multiple_of(jnp.clip(tile * NA_QROWS - NA_WIN_R // 2, 0, rows - NA_WROWS) * GRID_W, 256)

    hp = 1

    def body(q0_ref, qn_ref, k_ref, vt_ref, b0_ref, bn_ref, o_ref, sw_ref, sc_ref):
        g = pl.program_id(2)

        def scores(q_ref, b_ref, tile, buf, with_win):
            for blk in range(hp):
                cols = slice(blk * LANES, (blk + 1) * LANES)
                qs = _split_heads(q_ref[0, :, cols])
                if with_win:
                    kw = k_ref[0, pl.ds(window(tile), win), cols]
                    for half in range(2):
                        sw_ref[buf, 2 * blk + half] = _dot_nt(kw, qs[half]) + b_ref[blk, half]
                kc = k_ref[0, l:l + c, cols]
                for half in range(2):
                    sc_ref[buf, 2 * blk + half] = _dot_nt(kc, qs[half])

        def finish(tile, buf, with_win):
            outs = []
            for s in range(2 * hp):
                parts = [(sc_ref[buf, s], vt_ref[0, s, :, l:l + c])]
                if with_win:
                    parts.append((sw_ref[buf, s], vt_ref[0, s, :, pl.ds(window(tile), win)]))
                m = functools.reduce(jnp.maximum, [jnp.max(sc, axis=0, keepdims=True) for sc, _ in parts])
                acc = sum(jnp.dot(v, jnp.exp2((sc - m).astype(BF16)), preferred_element_type=F32)
                          for sc, v in parts)
                outs.append(acc[:HEAD_DIM] / acc[HEAD_DIM:HEAD_DIM + 1])
            o_ref[0] = jnp.concatenate(outs, axis=0).T.astype(BF16)

        cur, nxt = g % 2, (g + 1) % 2

        @pl.when(g == 0)
        def _():
            scores(q0_ref, b0_ref, 0, 0, True)

        @pl.when(g + 1 < nl)
        def _():
            scores(qn_ref, bn_ref, g + 1, nxt, True)
            finish(g, cur, True)

        @pl.when(g + 1 == nl)
        def _():
            scores(qn_ref, bn_ref, g + 1, nxt, False)
            finish(g, cur, True)

        @pl.when(g == nl)
        def _():
            finish(g, cur, False)

    n_steps = n_pairs // hp
    wq = hp * LANES

    def bias_next(bi, h, g):
        tile = g + 1
        case = jnp.where(tile >= nl - 1, 2, 1)
        return (case * n_steps + h, 0, 0, 0)

    return pl.pallas_call(
        body, grid=(b, n_steps, nt),
        in_specs=[pl.BlockSpec((1, TM, wq), lambda bi, h, g: (bi, 0, h)),
                  pl.BlockSpec((1, TM, wq), lambda bi, h, g: (bi, jnp.minimum(g + 1, nt - 1), h)),
                  pl.BlockSpec((1, t, wq), lambda bi, h, g: (bi, 0, n_steps + h)),
                  pl.BlockSpec((1, 2 * hp, HEAD_DIM + ONES_ROWS, t), lambda bi, h, g: (bi, h, 0, 0)),
                  pl.BlockSpec((hp, 2, win, TM), lambda bi, h, g: (h, 0, 0, 0)),
                  pl.BlockSpec((hp, 2, win, TM), bias_next)],
        out_specs=pl.BlockSpec((1, TM, wq), lambda bi, h, g: (bi, g, h)),
        out_shape=jax.ShapeDtypeStruct((b, t, n_pairs * LANES), BF16),
        scratch_shapes=[pltpu.VMEM((2, 2 * hp, win, TM), F32), pltpu.VMEM((2, 2 * hp, c, TM), F32)],
        compiler_params=_cparams(3), name="na_attn",
    )(qkv, qkv, qkv, vt, bias, bias)


def _na_bias_table(rpb):
    h, n_ro, n_co = rpb.shape
    kr, qr = np.arange(NA_WROWS), np.arange(NA_QROWS)
    kc, qc = np.arange(GRID_W), np.arange(GRID_W)
    cs = np.clip(qc - NA_WIN_C // 2, 0, GRID_W - NA_WIN_C)
    col_ok = (kc[:, None] >= cs[None, :]) & (kc[:, None] < cs[None, :] + NA_WIN_C)
    co = kc[:, None] - qc[None, :] + NA_WIN_C - 1
    col_sel = (co[None] == np.arange(n_co)[:, None, None]).astype(np.float32)
    tables = []
    for d, rel in ((0, np.zeros_like(qr)), (NA_WIN_R // 2, qr), (NA_WIN_R, np.full_like(qr, NA_WIN_R // 2))):
        row_ok = (kr[:, None] >= rel[None, :]) & (kr[:, None] < rel[None, :] + NA_WIN_R)
        ro = kr[:, None] - qr[None, :] - d + NA_WIN_R - 1
        row_sel = (ro[:, :, None] == np.arange(n_ro)[None, None, :]).astype(np.float32)
        tb = jnp.einsum('kqa,hab,bcd->hkcqd', jnp.asarray(row_sel), rpb.astype(F32) * LOG2E,
                        jnp.asarray(col_sel), precision=lax.Precision.HIGHEST)
        ok = row_ok[:, None, :, None] & col_ok[None, :, None, :]
        tables.append(jnp.where(jnp.asarray(ok)[None], tb, NEG))
    tbl = jnp.stack(tables, axis=0)
    return tbl.reshape(3 * (h // 2), 2, NA_WROWS * GRID_W, TM).astype(F32)


def _hy_filter_call(length, w1, b1, w2, b2, w3, b3, w4, freq):
    order = w2.shape[0]
    w = w4.shape[1] // 2
    tl = min(length, 512)
    hi = lax.Precision.HIGHEST
    t = np.linspace(0.0, 1.0, length, dtype=np.float64)[:, None]
    bands = (HY_EMB_DIM - 1) // 2
    ang = 2.0 * math.pi * np.arange(length, dtype=np.float64)[:, None] / length
    fq = np.linspace(1e-4, bands - 1, bands, dtype=np.float64)[None, :]
    emb = np.concatenate([t, np.cos(fq * ang), -np.sin(fq * ang)], axis=-1).astype(np.float32)
    emb = np.pad(emb, ((0, 0), (0, LANES - HY_EMB_DIM)))
    max_decay = math.log(HY_DECAY_TARGET) / HY_FAST_DECAY
    min_decay = math.log(HY_DECAY_TARGET) / HY_SLOW_DECAY
    deltas = np.linspace(min_decay, max_decay, w, dtype=np.float64)
    decay = np.exp(-t * np.abs(deltas)[None, :]).astype(np.float32)
    decay2 = np.concatenate([decay, decay], axis=1)
    w1p = jnp.pad(w1, ((0, LANES - HY_EMB_DIM), (0, 0)))

    def body(e_ref, d_ref, w1_ref, b1_ref, w2_ref, b2_ref, w3_ref, b3_ref, w4_ref, f_ref,
             h_ref, s_ref):
        i = pl.program_id(0)
        fr = f_ref[...]
        hdn = jnp.sin(fr * (jnp.dot(e_ref[...], w1_ref[...], precision=hi,
                                    preferred_element_type=F32) + b1_ref[...]))
        hdn = jnp.sin(fr * (jnp.dot(hdn, w2_ref[...], precision=hi,
                                    preferred_element_type=F32) + b2_ref[...]))
        hdn = jnp.sin(fr * (jnp.dot(hdn, w3_ref[...], precision=hi,
                                    preferred_element_type=F32) + b3_ref[...]))
        taps = jnp.dot(hdn, w4_ref[...], precision=hi, preferred_element_type=F32) * d_ref[...]
        row = lax.broadcasted_iota(jnp.int32, taps.shape, 0) + i * tl
        col = lax.broadcasted_iota(jnp.int32, taps.shape, 1)
        taps = jnp.where((row == 0) & (col >= w), 0.0, taps)
        h_ref[...] = taps

        @pl.when(i == 0)
        def _():
            s_ref[...] = jnp.zeros(s_ref.shape, F32)
        s_ref[...] += jnp.sum(jnp.abs(taps), axis=0, keepdims=True)

    full = lambda a: pl.BlockSpec(a.shape, lambda i: (0,) * a.ndim)
    ops = (w1p, b1.reshape(1, order), w2, b2.reshape(1, order), w3, b3.reshape(1, order), w4,
           freq.reshape(1, order))
    return pl.pallas_call(
        body, grid=(length // tl,),
        in_specs=[pl.BlockSpec((tl, LANES), lambda i: (i, 0)),
                  pl.BlockSpec((tl, 2 * w), lambda i: (i, 0))] + [full(a) for a in ops],
        out_specs=[pl.BlockSpec((tl, 2 * w), lambda i: (i, 0)),
                   pl.BlockSpec((1, 2 * w), lambda i: (0, 0))],
        out_shape=[jax.ShapeDtypeStruct((length, 2 * w), F32), jax.ShapeDtypeStruct((1, 2 * w), F32)],
        compiler_params=_cparams(1), name="hy_filter",
    )(jnp.asarray(emb), jnp.asarray(decay2), *ops)


def _dft_tables(l1, fb):
    n1 = 2 * l1
    n = n1 * LANES
    nf = -(-(n1 // 2 + 1) // fb) * fb
    f1 = np.arange(nf, dtype=np.float64)
    wgt = np.where((f1 == 0) | (f1 == n1 // 2), 1.0, np.where(f1 < n1 // 2, 2.0, 0.0))[None, :]
    t1 = np.arange(l1, dtype=np.float64)
    th1 = 2.0 * np.pi * np.outer(f1, t1) / n1
    fwd1 = np.concatenate([np.cos(th1), -np.sin(th1)], axis=0)
    inv1 = np.concatenate([wgt * np.cos(th1).T, -wgt * np.sin(th1).T], axis=1) / n
    f2 = np.arange(LANES, dtype=np.float64)
    t2 = np.arange(LANES, dtype=np.float64)
    fr = f1[:, None, None] + n1 * f2[None, :, None]
    th2 = 2.0 * np.pi * fr * t2[None, None, :] / n
    gr, gi = np.cos(th2), -np.sin(th2)
    gb = np.concatenate([np.concatenate([gr, -gi], axis=2),
                         np.concatenate([gi, gr], axis=2)], axis=1)
    hb = np.transpose(gb, (0, 2, 1))
    as_bf = lambda a: jnp.asarray(a.astype(np.float32)).astype(BF16)
    return as_bf(fwd1), as_bf(inv1), as_bf(gb), as_bf(hb)


def _hy_stage1_call(xv, fwd1):
    bx, l1, cols = xv.shape
    n2 = fwd1.shape[0]
    tn = min(cols, 4096)

    def body(f_ref, x_ref, o_ref):
        o_ref[0] = jnp.dot(f_ref[...], x_ref[0], preferred_element_type=F32).astype(BF16)

    return pl.pallas_call(
        body, grid=(bx, cols // tn),
        in_specs=[pl.BlockSpec((n2, l1), lambda i, j: (0, 0)),
                  pl.BlockSpec((1, l1, tn), lambda i, j: (i, 0, j))],
        out_specs=pl.BlockSpec((1, n2, tn), lambda i, j: (i, 0, j)),
        out_shape=jax.ShapeDtypeStruct((bx, n2, cols), BF16),
        compiler_params=_cparams(2), name="hy_dft1",
    )(fwd1, xv)


def _hy_filter_spec_call(a5, gb, sums, fb):
    n1, w2 = a5.shape[2], a5.shape[4]
    w = w2 // 2

    def body(a_ref, g_ref, s_ref, o_ref):
        sv = s_ref[...]
        inv = 1.0 / (sv[:, :w] + sv[:, w:])
        for k in range(fb):
            a = jnp.concatenate([a_ref[0, 0, k], a_ref[0, 1, k]], axis=0)
            z = jnp.dot(g_ref[k], a, preferred_element_type=F32)
            o_ref[k, 0] = (z[:LANES, :w] + z[:LANES, w:]) * inv
            o_ref[k, 1] = (z[LANES:, :w] - z[LANES:, w:]) * inv

    return pl.pallas_call(
        body, grid=(n1 // fb,),
        in_specs=[pl.BlockSpec((1, 2, fb, LANES, w2), lambda i: (0, 0, i, 0, 0)),
                  pl.BlockSpec((fb, 2 * LANES, 2 * LANES), lambda i: (i, 0, 0)),
                  pl.BlockSpec((1, w2), lambda i: (0, 0))],
        out_specs=pl.BlockSpec((fb, 2, LANES, w), lambda i: (i, 0, 0, 0)),
        out_shape=jax.ShapeDtypeStruct((n1, 2, LANES, w), F32),
        compiler_params=_cparams(1), name="hy_fspec",
    )(a5, gb, sums)


def _hy_stage23_call(a5, gb, hb, kf, fb):
    b, _, n1, _, w = a5.shape

    def body(a_ref, g_ref, h_ref, k_ref, o_ref):
        for k in range(fb):
            a = jnp.concatenate([a_ref[0, 0, k], a_ref[0, 1, k]], axis=0)
            z = jnp.dot(g_ref[k], a, preferred_element_type=F32)
            zr, zi = z[:LANES], z[LANES:]
            kr, ki = k_ref[k, 0], k_ref[k, 1]
            y = jnp.concatenate([zr * kr - zi * ki, zr * ki + zi * kr], axis=0).astype(BF16)
            cc = jnp.dot(h_ref[k], y, preferred_element_type=F32)
            o_ref[0, 0, k] = cc[:LANES].astype(BF16)
            o_ref[0, 1, k] = cc[LANES:].astype(BF16)

    return pl.pallas_call(
        body, grid=(n1 // fb, b),
        in_specs=[pl.BlockSpec((1, 2, fb, LANES, w), lambda i, j: (j, 0, i, 0, 0)),
                  pl.BlockSpec((fb, 2 * LANES, 2 * LANES), lambda i, j: (i, 0, 0)),
                  pl.BlockSpec((fb, 2 * LANES, 2 * LANES), lambda i, j: (i, 0, 0)),
                  pl.BlockSpec((fb, 2, LANES, w), lambda i, j: (i, 0, 0, 0))],
        out_specs=pl.BlockSpec((1, 2, fb, LANES, w), lambda i, j: (j, 0, i, 0, 0)),
        out_shape=jax.ShapeDtypeStruct(a5.shape, BF16),
        compiler_params=_cparams(2), name="hy_dft23",
    )(a5, gb, hb, kf)


def _hy_stage4_call(cv, inv1, x0v, zv, skip_t):
    b, n2, cols = cv.shape
    l1 = inv1.shape[0]
    tn = skip_t.shape[1]

    def body(f_ref, c_ref, x0_ref, z_ref, s_ref, o_ref):
        y = jnp.dot(f_ref[...], c_ref[0], preferred_element_type=F32)
        z = z_ref[0].astype(F32)
        o_ref[0] = (x0_ref[0].astype(F32) * (y + s_ref[...] * z)).astype(BF16)

    return pl.pallas_call(
        body, grid=(b, cols // tn),
        in_specs=[pl.BlockSpec((l1, n2), lambda i, j: (0, 0)),
                  pl.BlockSpec((1, n2, tn), lambda i, j: (i, 0, j)),
                  pl.BlockSpec((1, l1, tn), lambda i, j: (i, 0, j)),
                  pl.BlockSpec((1, l1, tn), lambda i, j: (i, 0, j)),
                  pl.BlockSpec((1, tn), lambda i, j: (0, 0))],
        out_specs=pl.BlockSpec((1, l1, tn), lambda i, j: (i, 0, j)),
        out_shape=jax.ShapeDtypeStruct((b, l1, cols), BF16),
        compiler_params=_cparams(2), name="hy_dft4",
    )(inv1, cv, x0v, zv, skip_t)


def _hy_dense_call(x0, z, taps, sums, skip):
    b, c, w = z.shape
    n = 2 * c
    th = 2.0 * np.pi * np.outer(np.arange(n, dtype=np.float64), np.arange(c, dtype=np.float64)) / n
    fwd = jnp.asarray(np.concatenate([np.cos(th), -np.sin(th)], axis=0).astype(np.float32)).astype(BF16)
    inv = jnp.asarray((np.concatenate([np.cos(th).T, -np.sin(th).T], axis=1) / n)
                      .astype(np.float32)).astype(BF16)

    def body(f_ref, i_ref, x0_ref, z_ref, t_ref, s_ref, k_ref, o_ref):
        sv = s_ref[...]
        nrm = 1.0 / (sv[:, :w] + sv[:, w:])
        tf = jnp.dot(f_ref[...], t_ref[...].astype(BF16), preferred_element_type=F32)
        kr = (tf[:n, :w] + tf[:n, w:]) * nrm
        ki = (tf[n:, :w] - tf[n:, w:]) * nrm
        zf = jnp.dot(f_ref[...], z_ref[0], preferred_element_type=F32)
        zr, zi = zf[:n], zf[n:]
        y = jnp.concatenate([zr * kr - zi * ki, zr * ki + zi * kr], axis=0).astype(BF16)
        yt = jnp.dot(i_ref[...], y, preferred_element_type=F32)
        o_ref[0] = (x0_ref[0].astype(F32) * (yt + k_ref[...] * z_ref[0].astype(F32))).astype(BF16)

    return pl.pallas_call(
        body, grid=(b,),
        in_specs=[pl.BlockSpec((2 * n, c), lambda i: (0, 0)),
                  pl.BlockSpec((c, 2 * n), lambda i: (0, 0)),
                  pl.BlockSpec((1, c, w), lambda i: (i, 0, 0)),
                  pl.BlockSpec((1, c, w), lambda i: (i, 0, 0)),
                  pl.BlockSpec((c, 2 * w), lambda i: (0, 0)),
                  pl.BlockSpec((1, 2 * w), lambda i: (0, 0)),
                  pl.BlockSpec((1, w), lambda i: (0, 0))],
        out_specs=pl.BlockSpec((1, c, w), lambda i: (i, 0, 0)),
        out_shape=jax.ShapeDtypeStruct((b, c, w), BF16),
        compiler_params=_cparams(1), name="hy_dense",
    )(fwd, inv, x0, z, taps, sums, skip.reshape(1, w))


def _hyena_long(x0, z, fparams, skip):
    b, l, w = z.shape
    l1 = l // LANES
    n1 = 2 * l1
    fb = min(8, n1)
    fwd1, inv1, gb, hb = _dft_tables(l1, fb)
    nf = gb.shape[0]
    taps, sums = _hy_filter_call(l, *fparams)
    ta = _hy_stage1_call(taps.astype(BF16).reshape(1, l1, LANES * 2 * w), fwd1)
    kf = _hy_filter_spec_call(ta.reshape(1, 2, nf, LANES, 2 * w), gb, sums, fb)
    za = _hy_stage1_call(z.reshape(b, l1, LANES * w), fwd1)
    cc = _hy_stage23_call(za.reshape(b, 2, nf, LANES, w), gb, hb, kf, fb)
    tn = 8 * w
    skip_t = jnp.tile(skip.reshape(1, w), (1, tn // w))
    y = _hy_stage4_call(cc.reshape(b, 2 * nf, LANES * w), inv1, x0.reshape(b, l1, LANES * w),
                        z.reshape(b, l1, LANES * w), skip_t)
    return y.reshape(b, l, w)


def _hyena_short(x0, z, fparams, skip):
    taps, sums = _hy_filter_call(z.shape[1], *fparams)
    return _hy_dense_call(x0, z, taps, sums, skip)


def _rope_tables(l, c):
    t = np.arange(l)
    row = (t // GRID_W).astype(np.float64)
    col = (t % GRID_W).astype(np.float64)
    n_pairs = HEAD_DIM // 4
    inv_freq = ROPE_THETA ** (-np.arange(n_pairs, dtype=np.float64) / n_pairs)
    ang = np.concatenate([row[:, None] * inv_freq, col[:, None] * inv_freq], axis=-1)
    cos = np.repeat(np.cos(ang), 2, axis=1)
    sin = np.repeat(np.sin(ang), 2, axis=1)
    even = (np.arange(HEAD_DIM) % 2 == 0)[None, :]
    se = np.where(even, -sin, 0.0)
    so = np.where(even, 0.0, sin)
    pad = lambda a, v: np.concatenate([a, np.full((c, HEAD_DIM), v)], axis=0)
    two = lambda a: jnp.asarray(np.concatenate([a, a], axis=1).astype(np.float32))
    return two(pad(cos, 1.0)), two(pad(se, 0.0)), two(pad(so, 0.0))


def _block_diag_ones():
    i = np.arange(2 * LANES)
    return jnp.asarray((i[:, None] // HEAD_DIM == i[None, :] // HEAD_DIM).astype(np.float32)).astype(BF16)


def _gain_rows(gains, scales):
    rows = [jnp.tile(g.astype(F32) * s, 2 * LANES // HEAD_DIM) for g, s in zip(gains, scales)]
    rows += [jnp.zeros((2 * LANES,), F32)] * (8 - len(rows))
    return jnp.stack(rows, axis=0)


def kernel(x, c, ctx, c_ctx, w_ada, b_ada, w_up, ffn_conv_w, ffn_conv_b, w_down, w_in_e, w_out_e, na_q_gain, na_k_gain, na_rpb, da_q_gain, da_k_gain, da_lambda_q1, da_lambda_k1, da_lambda_q2, da_lambda_k2, da_subln_gain, w_in_o, w_out_o, gqa_q_gain, gqa_k_gain, hy_conv_w, hy_conv_b, hy_w1, hy_b1, hy_w2, hy_b2, hy_w3, hy_b3, hy_w4, hy_freq, hy_skip):
    b, l, d = x.shape
    cl = ctx.shape[1]
    t = l + cl
    depth = w_ada.shape[0]
    f = w_down.shape[1]
    assert cl == TM and (l // GRID_W) >= NA_WROWS + NA_QROWS
    assert w_in_e.shape[2] == 3072 and w_in_o.shape[2] == 2304 and d % LANES == 0
    scale = HEAD_DIM ** -0.5 * LOG2E

    rows = -(-(b + 1) // 8) * 8
    cs = jnp.zeros((rows, d), F32).at[:b].set(c).at[b].set(c_ctx)
    mods = _ada_call(cs, w_ada, b_ada)
    modarrs = []
    for layer in range(depth):
        lat = mods[layer, :b].reshape(b, 1, 6, d)
        cx = jnp.broadcast_to(mods[layer, b].reshape(1, 1, 6, d), (b, 1, 6, d))
        modarrs.append(jnp.concatenate([lat, cx], axis=1))

    ropes = _rope_tables(l, cl)
    bd = _block_diag_ones()
    x_all, h = _mod0_call(x, ctx, modarrs[0])

    for layer in range(depth):
        i = layer // 2
        if layer % 2 == 0:
            lam_init = 0.8 - 0.6 * math.exp(-0.3 * layer)
            gains = _gain_rows((na_q_gain[i], na_k_gain[i], da_q_gain[i], da_k_gain[i]),
                               (scale, 1.0, scale, 1.0))
            qkv, vat, qbt, vbt = _inproj_call(
                _inproj_even_body, "inproj_even", h, w_in_e[i].astype(BF16), bd, gains, ropes, (1536,),
                ((b, 8, HEAD_DIM + ONES_ROWS, t), (b, 512, t), (b, 4, LANES + ONES_ROWS, t)))
            bias = _na_bias_table(na_rpb[i])
            ya = _na_call(qkv, vat, bias, l)
            lamv = jnp.stack([da_lambda_q1[i], da_lambda_k1[i], da_lambda_q2[i], da_lambda_k2[i]]).astype(F32)
            yb = _flash_call("da_attn", qbt, qkv, vbt, 1024 // 256, False, LANES, TK_DA,
                             _make_da_finish(lam_init),
                             (lamv, da_subln_gain[i].reshape(1, LANES).astype(F32)), 512, l)
            w_out = w_out_e[i].astype(BF16)
        else:
            gains = _gain_rows((gqa_q_gain[i], gqa_k_gain[i]), (scale, 1.0))
            kd, x0, z, qt, vt = _inproj_call(
                functools.partial(_inproj_odd_body, nl=l // TM, nt=t // TM), "inproj_odd", h,
                w_in_o[i].astype(BF16), bd, gains, ropes, (2 * LANES, 512, 512),
                ((b, 512, t), (b, 2, HEAD_DIM + ONES_ROWS, t)),
                extra=(hy_conv_w[i], hy_conv_b[i].reshape(1, -1)), halo=True)
            ya = _flash_call("gqa_attn", qt, kd, vt, 0, True, HEAD_DIM, TK_GQA,
                             _gqa_finish, (), 512, l)
            fparams = (hy_w1[i], hy_b1[i], hy_w2[i], hy_b2[i], hy_w3[i], hy_b3[i], hy_w4[i], hy_freq[i])
            yd_l = _hyena_long(x0[:, :l], z[:, :l], fparams, hy_skip[i])
            if layer < depth - 1:
                yd_c = _hyena_short(x0[:, l:], z[:, l:], fparams, hy_skip[i])
            else:
                yd_c = jnp.zeros((b, cl, x0.shape[2]), BF16)
            yb = jnp.concatenate([yd_l, yd_c], axis=1)
            w_out = w_out_o[i].astype(BF16)
        ka = ya.shape[2]
        x_all, act = _mix_ffn_up_call(ya, yb, w_out[:ka], w_out[ka:], x_all, modarrs[layer],
                                      w_up[layer].astype(BF16), ffn_conv_w[layer], ffn_conv_b[layer])
        if layer == depth - 1:
            return _ffn_last_call(act, w_down[layer].astype(BF16), x_all, modarrs[layer])
        x_all, h = _ffn_down_call(act, w_down[layer].astype(BF16), x_all, modarrs[layer],
                                  modarrs[layer + 1])
```
